```python
import math
import jax, jax.numpy as jnp
from jax import lax
import numpy as np

D_MODEL = 1024
BATCH = 8
SEQ = 2048
DEPTH = 1
DEC_BATCH = 128
DEC_SEQ = 1
PAST_LEN = 16384
PAGE_SIZE = 128

D_MIX = 2 * D_MODEL
HGRN_HEADS = 8
HGRN_DK = 128
HGRN_DV = 128
D_HGRN = HGRN_HEADS * HGRN_DV
HGRN_CHUNK = 64
SSM_HEADS = 16
SSM_HEAD_DIM = 64
D_SSM = SSM_HEADS * SSM_HEAD_DIM
SSM_STATE = 128
SSM_GROUPS = 2
SSM_CONV = 4
SSM_CHUNK = 128
CONV_DIM = D_SSM + 2 * SSM_GROUPS * SSM_STATE
DT_MIN = 0.001
DT_MAX = 0.1
D_FF = 2816
FFN_CONV = 3
EPS = 1e-6
IN_SIZES = (HGRN_HEADS * HGRN_DK, HGRN_HEADS * HGRN_DK, D_HGRN, D_HGRN, D_SSM, CONV_DIM, SSM_HEADS)
D_IN = sum(IN_SIZES)

kernel_name = 'hymba_hgrn2_mamba2_convffn_step'


def _chunk_len(L, C):
    return C if L % C == 0 else L


def rmsnorm(x, w):
    xf = x.astype(jnp.float32)
    y = xf * lax.rsqrt(jnp.mean(xf * xf, axis=-1, keepdims=True) + EPS)
    return (y * w.astype(jnp.float32)).astype(x.dtype)


def causal_dwconv(x, buf, w, b):
    L = x.shape[1]
    xe = jnp.concatenate([buf.astype(x.dtype), x], axis=1)
    y = b.astype(x.dtype)
    for j in range(w.shape[0]):
        y = y + xe[:, j:j + L] * w[j].astype(x.dtype)
    return y, xe[:, L:]


def hgrn2_chunk(S, inp):
    q, k, v, g = inp
    C = q.shape[2]
    b = jnp.cumsum(g, axis=2)
    o = jnp.einsum('bhtk,bhkv->bhtv', q * jnp.exp(b), S)
    causal = jnp.tril(jnp.ones((C, C), dtype=bool))[:, :, None]
    diff = b[:, :, :, None, :] - b[:, :, None, :, :]
    decay = jnp.exp(jnp.where(causal, diff, -jnp.inf))
    A = jnp.einsum('bhtsk,bhsk->bhts', decay * q[:, :, :, None, :], k)
    o = o + jnp.einsum('bhts,bhsv->bhtv', A, v)
    b_last = b[:, :, -1:]
    S_new = jnp.exp(b_last[:, :, 0])[..., None] * S + jnp.einsum('bhsk,bhsv->bhkv', k * jnp.exp(b_last - b), v)
    return S_new, o


def hgrn2_mix(q, k, v, log_f, S0):
    Bsz, L, H, _ = q.shape
    C = _chunk_len(L, HGRN_CHUNK)
    nc = L // C

    def to_chunks(t):
        return t.reshape(Bsz, nc, C, H, t.shape[-1]).transpose(1, 0, 3, 2, 4)

    S_T, o = lax.scan(hgrn2_chunk, S0, (to_chunks(q), to_chunks(k), to_chunks(v), to_chunks(log_f)))
    o = o.transpose(1, 0, 3, 2, 4).reshape(Bsz, L, H, -1)
    return o, S_T


def segsum(x):
    T = x.shape[-1]
    cs = jnp.cumsum(x, axis=-1)
    d = cs[..., :, None] - cs[..., None, :]
    return jnp.where(jnp.tril(jnp.ones((T, T), dtype=bool)), d, -jnp.inf)


def ssd_scan(X, A, Bh, Ch, S0):
    Bsz, L, H, P = X.shape
    T = _chunk_len(L, SSM_CHUNK)
    nc = L // T
    X = X.reshape(Bsz, nc, T, H, P)
    Bh = Bh.reshape(Bsz, nc, T, H, -1)
    Ch = Ch.reshape(Bsz, nc, T, H, -1)
    A = A.reshape(Bsz, nc, T, H).transpose(0, 3, 1, 2)
    A_cs = jnp.cumsum(A, axis=-1)
    Lmat = jnp.exp(segsum(A))
    y_diag = jnp.einsum('bclhn,bcshn,bhcls,bcshp->bclhp', Ch, Bh, Lmat, X)
    decay_states = jnp.exp(A_cs[..., -1:] - A_cs)
    states = jnp.einsum('bclhn,bhcl,bclhp->bchpn', Bh, decay_states, X)
    states = jnp.concatenate([S0[:, None], states], axis=1)
    decay_chunk = jnp.exp(segsum(jnp.pad(A_cs[..., -1], ((0, 0), (0, 0), (1, 0)))))
    states = jnp.einsum('bhzc,bchpn->bzhpn', decay_chunk, states)
    y_off = jnp.einsum('bclhn,bchpn,bhcl->bclhp', Ch, states[:, :-1], jnp.exp(A_cs))
    y = (y_diag + y_off).reshape(Bsz, L, H, P)
    return y, states[:, -1]


def decoder_layer(x, S_h, S_s, buf_s, buf_f, lb, w):
    f32 = jnp.float32
    Bsz, L, _ = x.shape
    dt_ = x.dtype
    h = rmsnorm(x, w['norm1_w'])
    proj = h @ w['w_in']
    splits = np.cumsum(IN_SIZES)[:-1].tolist()
    q_r, f_r, i_r, g_r, z, xbc_r, dt_r = jnp.split(proj, splits, axis=-1)

    f = lb + (1.0 - lb) * jax.nn.sigmoid(f_r.astype(f32))
    log_f = jnp.log(f)
    k = 1.0 - f
    q = jax.nn.silu(q_r.astype(f32))

    def heads(t):
        return t.reshape(Bsz, L, HGRN_HEADS, -1)

    o_a, S_h_new = hgrn2_mix(heads(q), heads(k), heads(i_r.astype(f32)), heads(log_f), S_h.astype(f32))
    o_a = rmsnorm(o_a, w['hgrn_norm_w']) * jax.nn.silu(heads(g_r).astype(f32))
    o_a = o_a.reshape(Bsz, L, D_HGRN).astype(dt_)

    xbc, buf_s_new = causal_dwconv(xbc_r, buf_s, w['ssm_conv_w'], w['ssm_conv_b'])
    xbc = jax.nn.silu(xbc.astype(f32))
    xs, Bm, Cm = jnp.split(xbc, [D_SSM, D_SSM + SSM_GROUPS * SSM_STATE], axis=-1)
    dt = jax.nn.softplus(dt_r.astype(f32) + w['ssm_dt_bias'].astype(f32))
    A = -jnp.exp(w['ssm_a_log'].astype(f32))
    X = xs.reshape(Bsz, L, SSM_HEADS, SSM_HEAD_DIM)
    rep = SSM_HEADS // SSM_GROUPS
    Bh = jnp.repeat(Bm.reshape(Bsz, L, SSM_GROUPS, SSM_STATE), rep, axis=2)
    Ch = jnp.repeat(Cm.reshape(Bsz, L, SSM_GROUPS, SSM_STATE), rep, axis=2)
    y, S_s_new = ssd_scan(X * dt[..., None], A * dt, Bh, Ch, S_s.astype(f32))
    y = y + w['ssm_d'].astype(f32)[:, None] * X
    y = y.reshape(Bsz, L, D_SSM) * jax.nn.silu(z.astype(f32))
    y = rmsnorm(y.reshape(Bsz, L, SSM_GROUPS, -1), w['ssm_norm_w'].reshape(SSM_GROUPS, -1))
    y = y.reshape(Bsz, L, D_SSM).astype(dt_)

    x = x + jnp.concatenate([o_a, y], axis=-1) @ w['w_out']

    h = rmsnorm(x, w['norm2_w'])
    gate, val = jnp.split(h @ w['w_up'], 2, axis=-1)
    gate, buf_f_new = causal_dwconv(gate, buf_f, w['ffn_conv_w'], w['ffn_conv_b'])
    x = x + (jax.nn.silu(gate) * val) @ w['w_down']
    return (x, S_h_new.astype(S_h.dtype), S_s_new.astype(S_s.dtype),
            buf_s_new.astype(buf_s.dtype), buf_f_new.astype(buf_f.dtype))


def run_trunk(x, st_hgrn, st_ssm, st_cs, st_cf, hgrn_lb, layer_w, final_norm_w):
    lb_all = jnp.cumsum(jax.nn.softmax(hgrn_lb.astype(jnp.float32), axis=0), axis=0)
    new_h, new_s, new_cs, new_cf = [], [], [], []
    for l in range(DEPTH):
        w = {name: arr[l] for name, arr in layer_w.items()}
        x, sh, ss, scs, scf = decoder_layer(x, st_hgrn[l], st_ssm[l], st_cs[l], st_cf[l], lb_all[l], w)
        new_h.append(sh)
        new_s.append(ss)
        new_cs.append(scs)
        new_cf.append(scf)
    y = rmsnorm(x, final_norm_w)
    return y, jnp.stack(new_h), jnp.stack(new_s), jnp.stack(new_cs), jnp.stack(new_cf)


def setup_inputs(seed: int = 0) -> dict:
    key = jax.random.key(seed)
    k = jax.random.split(key, 24)
    f32 = jnp.float32

    def nrm(i, shape, scale):
        return jax.random.normal(k[i], shape, f32) * scale

    x_prompt = nrm(0, (BATCH, SEQ, D_MODEL), 1.0)
    x_sample = nrm(1, (DEC_BATCH, DEC_SEQ, D_MODEL), 1.0)
    state_hgrn = nrm(2, (DEPTH, DEC_BATCH, HGRN_HEADS, HGRN_DK, HGRN_DV), 0.5)
    state_ssm = nrm(3, (DEPTH, DEC_BATCH, SSM_HEADS, SSM_HEAD_DIM, SSM_STATE), 0.3)
    state_conv_ssm = nrm(4, (DEPTH, DEC_BATCH, SSM_CONV - 1, CONV_DIM), 1.0)
    state_conv_ffn = nrm(5, (DEPTH, DEC_BATCH, FFN_CONV - 1, D_FF), 1.0)
    norm1_w = 1.0 + nrm(6, (DEPTH, D_MODEL), 0.02)
    w_in = nrm(7, (DEPTH, D_MODEL, D_IN), D_MODEL ** -0.5)
    hgrn_lb = nrm(8, (DEPTH + 1, HGRN_HEADS * HGRN_DK), 0.1)
    hgrn_norm_w = 1.0 + nrm(9, (DEPTH, HGRN_DV), 0.02)
    ssm_conv_w = nrm(10, (DEPTH, SSM_CONV, CONV_DIM), SSM_CONV ** -0.5)
    ssm_conv_b = nrm(11, (DEPTH, CONV_DIM), 0.02)
    dt0 = jnp.exp(jax.random.uniform(k[12], (DEPTH, SSM_HEADS), f32, math.log(DT_MIN), math.log(DT_MAX)))
    ssm_dt_bias = dt0 + jnp.log(-jnp.expm1(-dt0))
    ssm_a_log = jnp.log(jax.random.uniform(k[13], (DEPTH, SSM_HEADS), f32, 1.0, 16.0))
    ssm_d = 1.0 + nrm(14, (DEPTH, SSM_HEADS), 0.02)
    ssm_norm_w = 1.0 + nrm(15, (DEPTH, D_SSM), 0.02)
    w_out = nrm(16, (DEPTH, D_MIX, D_MODEL), D_MIX ** -0.5)
    norm2_w = 1.0 + nrm(17, (DEPTH, D_MODEL), 0.02)
    w_up = nrm(18, (DEPTH, D_MODEL, 2 * D_FF), D_MODEL ** -0.5)
    ffn_conv_w = nrm(19, (DEPTH, FFN_CONV, D_FF), FFN_CONV ** -0.5)
    ffn_conv_b = nrm(20, (DEPTH, D_FF), 0.02)
    w_down = nrm(21, (DEPTH, D_FF, D_MODEL), D_FF ** -0.5)
    final_norm_w = 1.0 + nrm(22, (D_MODEL,), 0.02)
    return {'x_prompt': x_prompt, 'x_sample': x_sample,
            'state_hgrn': state_hgrn, 'state_ssm': state_ssm,
            'state_conv_ssm': state_conv_ssm, 'state_conv_ffn': state_conv_ffn,
            'norm1_w': norm1_w, 'w_in': w_in, 'hgrn_lb': hgrn_lb, 'hgrn_norm_w': hgrn_norm_w,
            'ssm_conv_w': ssm_conv_w, 'ssm_conv_b': ssm_conv_b, 'ssm_dt_bias': ssm_dt_bias,
            'ssm_a_log': ssm_a_log, 'ssm_d': ssm_d, 'ssm_norm_w': ssm_norm_w, 'w_out': w_out,
            'norm2_w': norm2_w, 'w_up': w_up, 'ffn_conv_w': ffn_conv_w, 'ffn_conv_b': ffn_conv_b,
            'w_down': w_down, 'final_norm_w': final_norm_w}


def reference(x_prompt, x_sample, state_hgrn, state_ssm, state_conv_ssm, state_conv_ffn,
              norm1_w, w_in, hgrn_lb, hgrn_norm_w, ssm_conv_w, ssm_conv_b, ssm_dt_bias,
              ssm_a_log, ssm_d, ssm_norm_w, w_out, norm2_w, w_up, ffn_conv_w, ffn_conv_b,
              w_down, final_norm_w):
    layer_w = {'norm1_w': norm1_w, 'w_in': w_in, 'hgrn_norm_w': hgrn_norm_w,
               'ssm_conv_w': ssm_conv_w, 'ssm_conv_b': ssm_conv_b, 'ssm_dt_bias': ssm_dt_bias,
               'ssm_a_log': ssm_a_log, 'ssm_d': ssm_d, 'ssm_norm_w': ssm_norm_w, 'w_out': w_out,
               'norm2_w': norm2_w, 'w_up': w_up, 'ffn_conv_w': ffn_conv_w, 'ffn_conv_b': ffn_conv_b,
               'w_down': w_down}
    bp = x_prompt.shape[0]
    dtp = x_prompt.dtype
    z_hgrn = jnp.zeros((DEPTH, bp, HGRN_HEADS, HGRN_DK, HGRN_DV), dtp)
    z_ssm = jnp.zeros((DEPTH, bp, SSM_HEADS, SSM_HEAD_DIM, SSM_STATE), dtp)
    z_cs = jnp.zeros((DEPTH, bp, SSM_CONV - 1, CONV_DIM), dtp)
    z_cf = jnp.zeros((DEPTH, bp, FFN_CONV - 1, D_FF), dtp)
    y_prompt, hgrn_p, ssm_p, cs_p, cf_p = run_trunk(
        x_prompt, z_hgrn, z_ssm, z_cs, z_cf, hgrn_lb, layer_w, final_norm_w)
    y_sample, hgrn_s, ssm_s, cs_s, cf_s = run_trunk(
        x_sample, state_hgrn, state_ssm, state_conv_ssm, state_conv_ffn, hgrn_lb, layer_w, final_norm_w)
    return (y_prompt, y_sample, hgrn_p, hgrn_s, ssm_p, ssm_s, cs_p, cs_s, cf_p, cf_s)
```

```python
import functools

import numpy as np
import jax
import jax.numpy as jnp
from jax import lax
from jax.experimental import pallas as pl
from jax.experimental.pallas import tpu as pltpu

F32 = jnp.float32
BF16 = jnp.bfloat16
EPS = 1e-6

LANES = 128
SUBLANES = 8

D_MODEL = 1024
HGRN_HEADS = 8
HGRN_DK = 128
HGRN_DV = 128
D_HGRN = HGRN_HEADS * HGRN_DV
SSM_HEADS = 16
SSM_HEAD_DIM = 64
D_SSM = SSM_HEADS * SSM_HEAD_DIM
SSM_STATE = 128
SSM_GROUPS = 2
SSM_CONV = 4
CONV_DIM = D_SSM + 2 * SSM_GROUPS * SSM_STATE
D_FF = 2816
FFN_CONV = 3
D_MAIN = 4 * D_HGRN + D_SSM + CONV_DIM
OFF_Q, OFF_F, OFF_I, OFF_G = 0, 1024, 2048, 3072
OFF_Z, OFF_XS, OFF_BC = 4096, 5120, 6144

CHUNK = 128
GROUP_W = D_SSM // SSM_GROUPS
VMEM_LIMIT = 56 * 1024 * 1024


def _cp(sem):
    return pltpu.CompilerParams(dimension_semantics=sem, vmem_limit_bytes=VMEM_LIMIT)


def _dot(a, b):
    return jnp.dot(a, b, preferred_element_type=F32)


def _dot_nt(a, b):
    return lax.dot_general(a, b, (((1,), (1,)), ((), ())), preferred_element_type=F32)


def _split3(x):
    h = x.astype(BF16)
    r = x - h.astype(F32)
    m = r.astype(BF16)
    lo = (r - m.astype(F32)).astype(BF16)
    return h, m, lo


def _dot_exact_lhs01(m01, x):
    h, m, lo = _split3(x)
    return _dot(m01, h) + _dot(m01, m) + _dot(m01, lo)


def _dot_exact_rhs01(x, m01):
    h, m, lo = _split3(x)
    return _dot(h, m01) + _dot(m, m01) + _dot(lo, m01)


def _sigmoid(x):
    return 1.0 / (1.0 + jnp.exp(-x))


def _silu(x):
    return x * _sigmoid(x)


def _rms(x, w):
    ms = jnp.mean(x * x, axis=-1, keepdims=True)
    return x * lax.rsqrt(ms + EPS) * w


def _inproj_kernel(x_ref, nw_ref, w_ref, wdt_ref, out_ref, dt_ref, h_scr):
    @pl.when(pl.program_id(1) == 0)
    def _():
        hb = _rms(x_ref[...], nw_ref[...]).astype(BF16)
        h_scr[...] = hb
        dt_ref[...] = _dot(hb, wdt_ref[...])

    out_ref[...] = _dot(h_scr[...], w_ref[...])


def _inproj(x2d, norm_w, w_main, w_dt, tm):
    n = x2d.shape[0]
    tn = 512
    return pl.pallas_call(
        _inproj_kernel,
        grid=(n // tm, D_MAIN // tn),
        in_specs=[
            pl.BlockSpec((tm, D_MODEL), lambda i, j: (i, 0)),
            pl.BlockSpec((1, D_MODEL), lambda i, j: (0, 0)),
            pl.BlockSpec((D_MODEL, tn), lambda i, j: (0, j)),
            pl.BlockSpec((D_MODEL, LANES), lambda i, j: (0, 0)),
        ],
        out_specs=[
            pl.BlockSpec((tm, tn), lambda i, j: (i, j)),
            pl.BlockSpec((tm, LANES), lambda i, j: (i, 0)),
        ],
        out_shape=[
            jax.ShapeDtypeStruct((n, D_MAIN), F32),
            jax.ShapeDtypeStruct((n, LANES), F32),
        ],
        scratch_shapes=[pltpu.VMEM((tm, D_MODEL), BF16)],
        compiler_params=_cp(("arbitrary", "arbitrary")),
        name="inproj",
    )(x2d, norm_w, w_main, w_dt)


N_LEVELS = 7


def _hgrn_const():
    c = CHUNK
    t = np.arange(c)[:, None]
    j = np.arange(c)[None, :]
    blocks = [(j <= t), (j > t)]
    h = c // 2
    while h >= 1:
        mid = (t // (2 * h)) * (2 * h) + h
        upper = (t >= mid) & (j >= mid) & (j <= t)
        lower = (t < mid) & (j > t) & (j < mid)
        blocks.append(upper | lower)
        h //= 2
    return np.concatenate(blocks, axis=0).astype(np.float32)


def _hgrn_lb(lb_raw):
    mx = jnp.max(lb_raw, axis=0, keepdims=True)
    e = jnp.exp(lb_raw - mx)
    return e[0:1, :] / jnp.sum(e, axis=0, keepdims=True)


def _hgrn_gates(q_r, f_r, lb_raw):
    lb = _hgrn_lb(lb_raw)
    f = lb + (1.0 - lb) * _sigmoid(f_r)
    return _silu(q_r), f, 1.0 - f


def _level_map():
    t = lax.broadcasted_iota(jnp.int32, (CHUNK, CHUNK), 0)
    s = lax.broadcasted_iota(jnp.int32, (CHUNK, CHUNK), 1)
    bitlen = 32 - lax.clz(t ^ s)
    return jnp.where(t > s, bitlen, jnp.where(t == s, 0, -1))


def _hgrn_prompt_kernel(q_ref, f_ref, i_ref, g_ref, lb_ref, nw_ref, mc_ref,
                        o_ref, s_out_ref, st_scr):
    c = pl.program_id(2)

    @pl.when(c == 0)
    def _():
        st_scr[...] = jnp.zeros_like(st_scr)

    q, f, k = _hgrn_gates(q_ref[...], f_ref[...], lb_ref[...])
    v = i_ref[...]
    g = jnp.log(f)
    e_all = _dot_exact_lhs01(mc_ref[...], g)
    b = e_all[0:CHUNK]
    to_end = e_all[CHUNK:2 * CHUNK]

    st = st_scr[...]
    o = _dot_nt((q * jnp.exp(b)).astype(BF16), st.astype(BF16))

    lev = _level_map()
    qb = q.astype(BF16)
    kb = k.astype(BF16)
    a = jnp.where(lev == 0, _dot_nt(qb, kb), 0.0)
    for l in range(N_LEVELS):
        w = jnp.exp(e_all[(2 + l) * CHUNK:(3 + l) * CHUNK])
        p = _dot_nt((q * w).astype(BF16), (k * w).astype(BF16))
        a = jnp.where(lev == N_LEVELS - l, p, a)
    vb = v.astype(BF16)
    o = o + _dot(a.astype(BF16), vb)

    ks = (k * jnp.exp(to_end)).astype(BF16)
    st_new = st * jnp.exp(b[CHUNK - 1:CHUNK, :]) + _dot(v.T.astype(BF16), ks)
    st_scr[...] = st_new

    o_ref[...] = _rms(o, nw_ref[...]) * _silu(g_ref[...])

    @pl.when(c == pl.num_programs(2) - 1)
    def _():
        s_out_ref[0, 0] = st_new.T


def _hgrn_prompt(proj, lb_raw, norm_w, mconst, batch, seq):
    nc = seq // CHUNK
    hb = lambda off: off // HGRN_DK

    def col(off):
        return pl.BlockSpec((CHUNK, HGRN_DK), lambda b, h, c: (b * nc + c, hb(off) + h))

    return pl.pallas_call(
        _hgrn_prompt_kernel,
        grid=(batch, HGRN_HEADS, nc),
        in_specs=[
            col(OFF_Q), col(OFF_F), col(OFF_I), col(OFF_G),
            pl.BlockSpec((lb_raw.shape[0], HGRN_DK), lambda b, h, c: (0, h)),
            pl.BlockSpec((1, HGRN_DV), lambda b, h, c: (0, 0)),
            pl.BlockSpec(mconst.shape, lambda b, h, c: (0, 0)),
        ],
        out_specs=[
            pl.BlockSpec((CHUNK, HGRN_DV), lambda b, h, c: (b * nc + c, h)),
            pl.BlockSpec((1, 1, HGRN_DK, HGRN_DV), lambda b, h, c: (b, h, 0, 0)),
        ],
        out_shape=[
            jax.ShapeDtypeStruct((batch * seq, D_HGRN), F32),
            jax.ShapeDtypeStruct((batch, HGRN_HEADS, HGRN_DK, HGRN_DV), F32),
        ],
        scratch_shapes=[pltpu.VMEM((HGRN_DV, HGRN_DK), F32)],
        compiler_params=_cp(("arbitrary", "arbitrary", "arbitrary")),
        name="hgrn_prompt",
    )(proj, proj, proj, proj, lb_raw, norm_w, mconst)


def _hgrn_step_kernel(q_ref, f_ref, i_ref, g_ref, lb_ref, nw_ref, s_ref,
                      o_ref, s_out_ref, o_scr):
    nb = q_ref.shape[0]
    q, f, k = _hgrn_gates(q_ref[...], f_ref[...], lb_ref[...])
    v = i_ref[...]
    q_t, f_t, k_t = q.T, f.T, k.T
    for t in range(nb):
        s_new = f_t[:, t:t + 1] * s_ref[t, 0] + k_t[:, t:t + 1] * v[t:t + 1, :]
        s_out_ref[t, 0] = s_new
        o_scr[t:t + 1, :] = jnp.sum(q_t[:, t:t + 1] * s_new, axis=0, keepdims=True)
    o_ref[...] = _rms(o_scr[...], nw_ref[...]) * _silu(g_ref[...])


def _hgrn_step(proj, lb_raw, norm_w, state):
    nb = state.shape[0]
    hb = lambda off: off // HGRN_DK

    def col(off):
        return pl.BlockSpec((nb, HGRN_DK), lambda h: (0, hb(off) + h))

    st_spec = pl.BlockSpec((nb, 1, HGRN_DK, HGRN_DV), lambda h: (0, h, 0, 0))
    return pl.pallas_call(
        _hgrn_step_kernel,
        grid=(HGRN_HEADS,),
        in_specs=[
            col(OFF_Q), col(OFF_F), col(OFF_I), col(OFF_G),
            pl.BlockSpec((lb_raw.shape[0], HGRN_DK), lambda h: (0, h)),
            pl.BlockSpec((1, HGRN_DV), lambda h: (0, 0)),
            st_spec,
        ],
        out_specs=[pl.BlockSpec((nb, HGRN_DV), lambda h: (0, h)), st_spec],
        out_shape=[
            jax.ShapeDtypeStruct((nb, D_HGRN), F32),
            jax.ShapeDtypeStruct(state.shape, F32),
        ],
        scratch_shapes=[pltpu.VMEM((nb, HGRN_DV), F32)],
        compiler_params=_cp(("arbitrary",)),
        name="hgrn_step",
    )(proj, proj, proj, proj, lb_raw, norm_w, state)


def _head_expand():
    e = np.zeros((LANES, D_SSM), np.float32)
    for h in range(SSM_HEADS):
        e[h, h * SSM_HEAD_DIM:(h + 1) * SSM_HEAD_DIM] = 1.0
    return e


def _softplus(x):
    return jnp.maximum(x, 0.0) + jnp.log(1.0 + jnp.exp(-jnp.abs(x)))


def _ssm_gate_norm(y, z, nw):
    y = y * _silu(z)
    parts = [_rms(y[:, g * GROUP_W:(g + 1) * GROUP_W], nw[:, g * GROUP_W:(g + 1) * GROUP_W])
             for g in range(SSM_GROUPS)]
    return jnp.concatenate(parts, axis=-1)


def _ssd_prompt_kernel(z_ref, xs_ref, bc_ref, dt_ref, cw_ref, cb_ref, dtb_ref, alog_ref,
                       dvec_ref, nw_ref, tri_ref, exp_ref,
                       y_ref, st_out_ref, xe_scr, st_scr, yd_scr):
    c = pl.program_id(1)
    t = CHUNK
    pad = SUBLANES

    @pl.when(c == 0)
    def _():
        st_scr[...] = jnp.zeros_like(st_scr)
        xe_scr[0:pad, :] = jnp.zeros((pad, CONV_DIM), F32)

    @pl.when(c > 0)
    def _():
        xe_scr[0:pad, :] = xe_scr[t:t + pad, :]

    xe_scr[pad:, 0:D_SSM] = xs_ref[...]
    xe_scr[pad:, D_SSM:] = bc_ref[...]

    acc = cb_ref[...] + cw_ref[SSM_CONV - 1:SSM_CONV, :] * xe_scr[pad:, :]
    for d in range(1, SSM_CONV):
        acc = acc + cw_ref[SSM_CONV - 1 - d:SSM_CONV - d, :] * xe_scr[pad - d:pad - d + t, :]
    xbc = _silu(acc)
    xs = xbc[:, 0:D_SSM]

    dt = _softplus(dt_ref[...] + dtb_ref[...])
    da = dt * (-jnp.exp(alog_ref[...]))
    cs = _dot_exact_lhs01(tri_ref[...], da)
    ex = exp_ref[...]
    dt_full = _dot_exact_rhs01(dt, ex)
    cs_full = _dot_exact_rhs01(cs, ex)
    cs_last_full = cs_full[t - 1:t, :]
    x_dt = xs * dt_full
    x_end = (x_dt * jnp.exp(cs_last_full - cs_full)).astype(BF16)
    cs_t = cs.T

    row = lax.broadcasted_iota(jnp.int32, (t, t), 0)
    colm = lax.broadcasted_iota(jnp.int32, (t, t), 1)
    causal = row >= colm
    lane = lax.broadcasted_iota(jnp.int32, (1, LANES), 1)
    heads_per_group = SSM_HEADS // SSM_GROUPS
    pair_w = 2 * SSM_HEAD_DIM

    for g in range(SSM_GROUPS):
        b_g = xbc[:, D_SSM + g * SSM_STATE:D_SSM + (g + 1) * SSM_STATE]
        c_off = D_SSM + SSM_GROUPS * SSM_STATE + g * SSM_STATE
        c_g = xbc[:, c_off:c_off + SSM_STATE].astype(BF16)
        gmat = _dot_nt(c_g, b_g.astype(BF16))
        for pp in range(heads_per_group // 2):
            h0 = g * heads_per_group + 2 * pp
            xp = x_dt[:, h0 * SSM_HEAD_DIM:h0 * SSM_HEAD_DIM + pair_w]
            yp = None
            for sub in range(2):
                h = h0 + sub
                diff = cs[:, h:h + 1] - cs_t[h:h + 1, :]
                w = jnp.where(causal, jnp.exp(jnp.minimum(diff, 0.0)), 0.0) * gmat
                keep = (lane >= sub * SSM_HEAD_DIM) & (lane < (sub + 1) * SSM_HEAD_DIM)
                xm = jnp.where(keep, xp, 0.0).astype(BF16)
                part = _dot(w.astype(BF16), xm)
                yp = part if yp is None else yp + part
            yd_scr[:, h0 * SSM_HEAD_DIM:h0 * SSM_HEAD_DIM + pair_w] = yp

        sl = slice(g * GROUP_W, (g + 1) * GROUP_W)
        st_g = st_scr[:, sl]
        y_off = _dot(c_g, st_g.astype(BF16)) * jnp.exp(cs_full[:, sl])
        yd_scr[:, sl] = yd_scr[:, sl] + y_off
        st_scr[:, sl] = (st_g * jnp.exp(cs_last_full[:, sl])
                         + _dot(b_g.T.astype(BF16), x_end[:, sl]))

    y = yd_scr[...] + dvec_ref[...] * xs
    y_ref[...] = _ssm_gate_norm(y, z_ref[...], nw_ref[...])

    @pl.when(c == pl.num_programs(1) - 1)
    def _():
        for j in range(D_SSM // LANES):
            st_out_ref[0, j * LANES:(j + 1) * LANES, :] = st_scr[:, j * LANES:(j + 1) * LANES].T


def _ssd_prompt(proj, dtr, conv_w, conv_b, dt_bias, a_log, d_full, norm_w, tri, expand,
                batch, seq):
    nc = seq // CHUNK
    const = lambda shape: pl.BlockSpec(shape, lambda b, c: (0, 0))
    return pl.pallas_call(
        _ssd_prompt_kernel,
        grid=(batch, nc),
        in_specs=[
            pl.BlockSpec((CHUNK, D_SSM), lambda b, c: (b * nc + c, OFF_Z // D_SSM)),
            pl.BlockSpec((CHUNK, D_SSM), lambda b, c: (b * nc + c, OFF_XS // D_SSM)),
            pl.BlockSpec((CHUNK, 512), lambda b, c: (b * nc + c, OFF_BC // 512)),
            pl.BlockSpec((CHUNK, LANES), lambda b, c: (b * nc + c, 0)),
            const((SSM_CONV, CONV_DIM)), const((1, CONV_DIM)),
            const((1, LANES)), const((1, LANES)),
            const((1, D_SSM)), const((1, D_SSM)),
            const((CHUNK, CHUNK)), const((LANES, D_SSM)),
        ],
        out_specs=[
            pl.BlockSpec((CHUNK, D_SSM), lambda b, c: (b * nc + c, 0)),
            pl.BlockSpec((1, D_SSM, SSM_STATE), lambda b, c: (b, 0, 0)),
        ],
        out_shape=[
            jax.ShapeDtypeStruct((batch * seq, D_SSM), F32),
            jax.ShapeDtypeStruct((batch, D_SSM, SSM_STATE), F32),
        ],
        scratch_shapes=[
            pltpu.VMEM((CHUNK + SUBLANES, CONV_DIM), F32),
            pltpu.VMEM((SSM_STATE, D_SSM), F32),
            pltpu.VMEM((CHUNK, D_SSM), F32),
        ],
        compiler_params=_cp(("arbitrary", "arbitrary")),
        name="ssd_prompt",
    )(proj, proj, proj, dtr, conv_w, conv_b, dt_bias, a_log, d_full, norm_w, tri, expand)


def _ssd_step_kernel(z_ref, xs_ref, bc_ref, dt_ref, b0_ref, b1_ref, b2_ref, cw_ref, cb_ref,
                     dtb_ref, alog_ref, dvec_ref, nw_ref, exp_ref, st_ref,
                     y_ref, st_out_ref, xt_scr, at_scr, xs_scr, bc_scr, y_scr):
    p = pl.program_id(0)
    nb = z_ref.shape[0]
    pair_w = 2 * SSM_HEAD_DIM
    pairs_per_group = SSM_HEADS // SSM_GROUPS // 2

    @pl.when(p == 0)
    def _():
        x_new = jnp.concatenate([xs_ref[...], bc_ref[...]], axis=-1)
        acc = (cb_ref[...] + cw_ref[0:1, :] * b0_ref[...] + cw_ref[1:2, :] * b1_ref[...]
               + cw_ref[2:3, :] * b2_ref[...] + cw_ref[3:4, :] * x_new)
        xbc = _silu(acc)
        xs = xbc[:, 0:D_SSM]
        dt = _softplus(dt_ref[...] + dtb_ref[...])
        da = dt * (-jnp.exp(alog_ref[...]))
        ex = exp_ref[...]
        x_dt = xs * _dot_exact_rhs01(dt, ex)
        decay = jnp.exp(_dot_exact_rhs01(da, ex))
        xs_scr[...] = xs
        bc_scr[...] = xbc[:, D_SSM:]
        for j in range(D_SSM // LANES):
            sl = slice(j * LANES, (j + 1) * LANES)
            xt_scr[sl, :] = x_dt[:, sl].T
            at_scr[sl, :] = decay[:, sl].T

    g_is_1 = p >= pairs_per_group
    row0 = pl.multiple_of(p * pair_w, pair_w)
    x_t = xt_scr[pl.ds(row0, pair_w), :]
    a_t = at_scr[pl.ds(row0, pair_w), :]
    bc = bc_scr[...]
    b_all = jnp.where(g_is_1, bc[:, SSM_STATE:2 * SSM_STATE], bc[:, 0:SSM_STATE])
    c_all = jnp.where(g_is_1, bc[:, 3 * SSM_STATE:4 * SSM_STATE],
                      bc[:, 2 * SSM_STATE:3 * SSM_STATE]).astype(BF16)
    for t in range(nb):
        st = st_ref[t].reshape(pair_w, SSM_STATE)
        new = a_t[:, t:t + 1] * st + x_t[:, t:t + 1] * b_all[t:t + 1, :]
        st_out_ref[t] = new.reshape(2, SSM_HEAD_DIM, SSM_STATE)
        c_rows = jnp.broadcast_to(c_all[t:t + 1, :], (SUBLANES, SSM_STATE))
        y_scr[p, t:t + 1, :] = _dot_nt(c_rows, new.astype(BF16))[0:1, :]

    @pl.when(p == pl.num_programs(0) - 1)
    def _():
        y_mix = jnp.concatenate([y_scr[j] for j in range(SSM_HEADS // 2)], axis=-1)
        y = y_mix + dvec_ref[...] * xs_scr[...]
        y_ref[...] = _ssm_gate_norm(y, z_ref[...], nw_ref[...])


def _ssd_step(proj, dtr, buf, conv_w, conv_b, dt_bias, a_log, d_full, norm_w, expand, state):
    nb = state.shape[0]
    n_pairs = SSM_HEADS // 2
    const = lambda shape: pl.BlockSpec(shape, lambda p: (0, 0))
    st_spec = pl.BlockSpec((nb, 2, SSM_HEAD_DIM, SSM_STATE), lambda p: (0, p, 0, 0))
    return pl.pallas_call(
        _ssd_step_kernel,
        grid=(n_pairs,),
        in_specs=[
            pl.BlockSpec((nb, D_SSM), lambda p: (0, OFF_Z // D_SSM)),
            pl.BlockSpec((nb, D_SSM), lambda p: (0, OFF_XS // D_SSM)),
            pl.BlockSpec((nb, 512), lambda p: (0, OFF_BC // 512)),
            const((nb, LANES)),
            const((nb, CONV_DIM)), const((nb, CONV_DIM)), const((nb, CONV_DIM)),
            const((SSM_CONV, CONV_DIM)), const((1, CONV_DIM)),
            const((1, LANES)), const((1, LANES)),
            const((1, D_SSM)), const((1, D_SSM)),
            const((LANES, D_SSM)),
            st_spec,
        ],
        out_specs=[const((nb, D_SSM)), st_spec],
        out_shape=[
            jax.ShapeDtypeStruct((nb, D_SSM), F32),
            jax.ShapeDtypeStruct(state.shape, F32),
        ],
        scratch_shapes=[
            pltpu.VMEM((D_SSM, nb), F32),
            pltpu.VMEM((D_SSM, nb), F32),
            pltpu.VMEM((nb, D_SSM), F32),
            pltpu.VMEM((nb, 2 * SSM_GROUPS * SSM_STATE), F32),
            pltpu.VMEM((n_pairs, nb, 2 * SSM_HEAD_DIM), F32),
        ],
        compiler_params=_cp(("arbitrary",)),
        name="ssd_step",
    )(proj, proj, proj, dtr, buf[:, 0], buf[:, 1], buf[:, 2], conv_w, conv_b, dt_bias, a_log,
      d_full, norm_w, expand, state)


def _outproj_kernel(oa_ref, ys_ref, x_ref, wa_ref, ws_ref, nw_ref, x1_ref, h2_ref):
    x1 = (x_ref[...] + _dot(oa_ref[...].astype(BF16), wa_ref[...])
          + _dot(ys_ref[...].astype(BF16), ws_ref[...]))
    x1_ref[...] = x1
    h2_ref[...] = _rms(x1, nw_ref[...]).astype(BF16)


def _outproj(o_a, y_s, x2d, w_a, w_s, norm_w, tm):
    n = x2d.shape[0]
    row = lambda w: pl.BlockSpec((tm, w), lambda i: (i, 0))
    const = lambda shape: pl.BlockSpec(shape, lambda i: (0, 0))
    return pl.pallas_call(
        _outproj_kernel,
        grid=(n // tm,),
        in_specs=[row(D_HGRN), row(D_SSM), row(D_MODEL),
                  const((D_HGRN, D_MODEL)), const((D_SSM, D_MODEL)), const((1, D_MODEL))],
        out_specs=[row(D_MODEL), row(D_MODEL)],
        out_shape=[jax.ShapeDtypeStruct((n, D_MODEL), F32),
                   jax.ShapeDtypeStruct((n, D_MODEL), BF16)],
        compiler_params=_cp(("arbitrary",)),
        name="outproj",
    )(o_a, y_s, x2d, w_a, w_s, norm_w)


FF_BLOCK = 256


def _ffn_finish(j, contrib, x1_ref, fnw_ref, y_ref, acc_scr):
    @pl.when(j == 0)
    def _():
        acc_scr[...] = contrib

    @pl.when(j > 0)
    def _():
        acc_scr[...] = acc_scr[...] + contrib

    @pl.when(j == pl.num_programs(1) - 1)
    def _():
        y_ref[...] = _rms(x1_ref[...] + acc_scr[...], fnw_ref[...])


def _ffn_prompt_kernel(h2_ref, x1_ref, wg_ref, wv_ref, wd_ref, cw_ref, cb_ref, fnw_ref,
                       y_ref, tail_ref, acc_scr, ge_scr, carry_scr, *, tiles_per_seq):
    i = pl.program_id(0)
    j = pl.program_id(1)
    tm = h2_ref.shape[0]
    pad = SUBLANES
    h2 = h2_ref[...]
    gate = _dot(h2, wg_ref[...])
    val = _dot(h2, wv_ref[...])

    seq_start = lax.rem(i, tiles_per_seq) == 0
    ge_scr[0:pad, :] = jnp.where(seq_start, 0.0, carry_scr[j])
    ge_scr[pad:, :] = gate
    tail = gate[tm - pad:, :]
    carry_scr[j] = tail
    tail_ref[0] = tail

    conv = (cb_ref[...] + cw_ref[2:3, :] * gate + cw_ref[1:2, :] * ge_scr[pad - 1:pad - 1 + tm, :]
            + cw_ref[0:1, :] * ge_scr[pad - 2:pad - 2 + tm, :])
    act = (_silu(conv) * val).astype(BF16)
    _ffn_finish(j, _dot(act, wd_ref[...]), x1_ref, fnw_ref, y_ref, acc_scr)


def _ffn_prompt(h2, x1, w_up, w_down, conv_w, conv_b, fnorm_w, tm, seq):
    n = h2.shape[0]
    nj = D_FF // FF_BLOCK
    kern = functools.partial(_ffn_prompt_kernel, tiles_per_seq=seq // tm)
    return pl.pallas_call(
        kern,
        grid=(n // tm, nj),
        in_specs=[
            pl.BlockSpec((tm, D_MODEL), lambda i, j: (i, 0)),
            pl.BlockSpec((tm, D_MODEL), lambda i, j: (i, 0)),
            pl.BlockSpec((D_MODEL, FF_BLOCK), lambda i, j: (0, j)),
            pl.BlockSpec((D_MODEL, FF_BLOCK), lambda i, j: (0, nj + j)),
            pl.BlockSpec((FF_BLOCK, D_MODEL), lambda i, j: (j, 0)),
            pl.BlockSpec((FFN_CONV, FF_BLOCK), lambda i, j: (0, j)),
            pl.BlockSpec((1, FF_BLOCK), lambda i, j: (0, j)),
            pl.BlockSpec((1, D_MODEL), lambda i, j: (0, 0)),
        ],
        out_specs=[
            pl.BlockSpec((tm, D_MODEL), lambda i, j: (i, 0)),
            pl.BlockSpec((1, SUBLANES, FF_BLOCK), lambda i, j: (i, 0, j)),
        ],
        out_shape=[
            jax.ShapeDtypeStruct((n, D_MODEL), F32),
            jax.ShapeDtypeStruct((n // tm, SUBLANES, D_FF), F32),
        ],
        scratch_shapes=[
            pltpu.VMEM((tm, D_MODEL), F32),
            pltpu.VMEM((tm + SUBLANES, FF_BLOCK), F32),
            pltpu.VMEM((nj, SUBLANES, FF_BLOCK), F32),
        ],
        compiler_params=_cp(("arbitrary", "arbitrary")),
        name="ffn_prompt",
    )(h2, x1, w_up, w_up, w_down, conv_w, conv_b, fnorm_w)


def _ffn_step_kernel(h2_ref, x1_ref, wg_ref, wv_ref, wd_ref, cw_ref, cb_ref, fnw_ref,
                     b0_ref, b1_ref, y_ref, gate_ref, acc_scr):
    j = pl.program_id(1)
    h2 = h2_ref[...]
    gate = _dot(h2, wg_ref[...])
    val = _dot(h2, wv_ref[...])
    gate_ref[...] = gate
    conv = (cb_ref[...] + cw_ref[2:3, :] * gate + cw_ref[1:2, :] * b1_ref[...]
            + cw_ref[0:1, :] * b0_ref[...])
    act = (_silu(conv) * val).astype(BF16)
    _ffn_finish(j, _dot(act, wd_ref[...]), x1_ref, fnw_ref, y_ref, acc_scr)


def _ffn_step(h2, x1, w_up, w_down, conv_w, conv_b, fnorm_w, buf):
    n = h2.shape[0]
    nj = D_FF // FF_BLOCK
    return pl.pallas_call(
        _ffn_step_kernel,
        grid=(1, nj),
        in_specs=[
            pl.BlockSpec((n, D_MODEL), lambda i, j: (0, 0)),
            pl.BlockSpec((n, D_MODEL), lambda i, j: (0, 0)),
            pl.BlockSpec((D_MODEL, FF_BLOCK), lambda i, j: (0, j)),
            pl.BlockSpec((D_MODEL, FF_BLOCK), lambda i, j: (0, nj + j)),
            pl.BlockSpec((FF_BLOCK, D_MODEL), lambda i, j: (j, 0)),
            pl.BlockSpec((FFN_CONV, FF_BLOCK), lambda i, j: (0, j)),
            pl.BlockSpec((1, FF_BLOCK), lambda i, j: (0, j)),
            pl.BlockSpec((1, D_MODEL), lambda i, j: (0, 0)),
            pl.BlockSpec((n, FF_BLOCK), lambda i, j: (0, j)),
            pl.BlockSpec((n, FF_BLOCK), lambda i, j: (0, j)),
        ],
        out_specs=[
            pl.BlockSpec((n, D_MODEL), lambda i, j: (0, 0)),
            pl.BlockSpec((n, FF_BLOCK), lambda i, j: (0, j)),
        ],
        out_shape=[
            jax.ShapeDtypeStruct((n, D_MODEL), F32),
            jax.ShapeDtypeStruct((n, D_FF), F32),
        ],
        scratch_shapes=[pltpu.VMEM((n, D_MODEL), F32)],
        compiler_params=_cp(("arbitrary", "arbitrary")),
        name="ffn_step",
    )(h2, x1, w_up, w_up, w_down, conv_w, conv_b, fnorm_w, buf[:, 0], buf[:, 1])


def _row(v):
    return v.reshape(1, -1).astype(F32)


def _pad_lanes(v):
    return jnp.pad(v.astype(F32), (0, LANES - v.shape[0])).reshape(1, LANES)


def _row_tile(n):
    for tm in (1024, 512, 256, 128):
        if n % tm == 0:
            return tm
    raise ValueError(f"token count {n} is not a multiple of 128")


def kernel(x_prompt, x_sample, state_hgrn, state_ssm, state_conv_ssm, state_conv_ffn, norm1_w, w_in, hgrn_lb, hgrn_norm_w, ssm_conv_w, ssm_conv_b, ssm_dt_bias, ssm_a_log, ssm_d, ssm_norm_w, w_out, norm2_w, w_up, ffn_conv_w, ffn_conv_b, w_down, final_norm_w):
    depth = w_in.shape[0]
    assert depth == 1, "single-layer trunk"
    l = 0
    batch, seq, _ = x_prompt.shape
    dec_batch, dec_seq, _ = x_sample.shape
    assert dec_seq == 1 and seq % CHUNK == 0 and seq >= SSM_CONV

    w_main = w_in[l][:, :D_MAIN].astype(BF16)
    w_dt = jnp.pad(w_in[l][:, D_MAIN:], ((0, 0), (0, LANES - SSM_HEADS))).astype(BF16)
    w_oa = w_out[l][:D_HGRN].astype(BF16)
    w_os = w_out[l][D_HGRN:].astype(BF16)
    w_upb = w_up[l].astype(BF16)
    w_dnb = w_down[l].astype(BF16)
    d_full = jnp.repeat(ssm_d[l].astype(F32), SSM_HEAD_DIM).reshape(1, D_SSM)
    dt_bias = _pad_lanes(ssm_dt_bias[l])
    a_log = _pad_lanes(ssm_a_log[l])
    mconst = jnp.asarray(_hgrn_const(), BF16)
    tri = jnp.asarray(np.tril(np.ones((CHUNK, CHUNK), np.float32)), BF16)
    expand = jnp.asarray(_head_expand(), BF16)
    lb_raw = hgrn_lb.astype(F32)

    def dense_tail(x2d, o_a, y_s, tm):
        return _outproj(o_a, y_s, x2d, w_oa, w_os, _row(norm2_w[l]), tm)

    xp = x_prompt.reshape(batch * seq, D_MODEL)
    tm_p = _row_tile(seq)
    proj_p, dt_p = _inproj(xp, _row(norm1_w[l]), w_main, w_dt, tm_p)
    oa_p, hgrn_p = _hgrn_prompt(proj_p, lb_raw, _row(hgrn_norm_w[l]), mconst, batch, seq)
    ys_p, ssm_p = _ssd_prompt(proj_p, dt_p, ssm_conv_w[l], _row(ssm_conv_b[l]), dt_bias, a_log,
                              d_full, _row(ssm_norm_w[l]), tri, expand, batch, seq)
    x1_p, h2_p = dense_tail(xp, oa_p, ys_p, tm_p)
    y_p, tail_p = _ffn_prompt(h2_p, x1_p, w_upb, w_dnb, ffn_conv_w[l], _row(ffn_conv_b[l]),
                              _row(final_norm_w), tm_p, seq)
    proj_p3 = proj_p.reshape(batch, seq, D_MAIN)
    cs_p = proj_p3[:, seq - (SSM_CONV - 1):, OFF_XS:OFF_XS + CONV_DIM]
    tails = tail_p.reshape(batch, seq // tm_p, SUBLANES, D_FF)
    cf_p = tails[:, -1, SUBLANES - (FFN_CONV - 1):, :]

    xs_ = x_sample.reshape(dec_batch, D_MODEL)
    proj_s, dt_s = _inproj(xs_, _row(norm1_w[l]), w_main, w_dt, dec_batch)
    oa_s, hgrn_s = _hgrn_step(proj_s, lb_raw, _row(hgrn_norm_w[l]), state_hgrn[l])
    ys_s, ssm_s = _ssd_step(proj_s, dt_s, state_conv_ssm[l], ssm_conv_w[l], _row(ssm_conv_b[l]),
                            dt_bias, a_log, d_full, _row(ssm_norm_w[l]), expand, state_ssm[l])
    x1_s, h2_s = dense_tail(xs_, oa_s, ys_s, dec_batch)
    y_s, gate_s = _ffn_step(h2_s, x1_s, w_upb, w_dnb, ffn_conv_w[l], _row(ffn_conv_b[l]),
                            _row(final_norm_w), state_conv_ffn[l])
    cs_s = jnp.concatenate([state_conv_ssm[l][:, 1:], proj_s[:, None, OFF_XS:OFF_XS + CONV_DIM]],
                           axis=1)
    cf_s = jnp.concatenate([state_conv_ffn[l][:, 1:], gate_s[:, None, :]], axis=1)

    dt_ = x_prompt.dtype
    return (y_p.reshape(batch, seq, D_MODEL).astype(dt_),
            y_s.reshape(dec_batch, 1, D_MODEL).astype(dt_),
            hgrn_p[None].astype(dt_),
            hgrn_s[None].astype(dt_),
            ssm_p.reshape(1, batch, SSM_HEADS, SSM_HEAD_DIM, SSM_STATE).astype(dt_),
            ssm_s[None].astype(dt_),
            cs_p[None].astype(dt_),
            cs_s[None].astype(dt_),
            cf_p[None].astype(dt_),
            cf_s[None].astype(dt_))
```

```python
import functools

import numpy as np
import jax
import jax.numpy as jnp
from jax import lax
from jax.experimental import pallas as pl
from jax.experimental.pallas import tpu as pltpu

F32 = jnp.float32
BF16 = jnp.bfloat16
EPS = 1e-6

LANES = 128
SUBLANES = 8

D_MODEL = 1024
HGRN_HEADS = 8
HGRN_DK = 128
HGRN_DV = 128
D_HGRN = HGRN_HEADS * HGRN_DV
SSM_HEADS = 16
SSM_HEAD_DIM = 64
D_SSM = SSM_HEADS * SSM_HEAD_DIM
SSM_STATE = 128
SSM_GROUPS = 2
SSM_CONV = 4
CONV_DIM = D_SSM + 2 * SSM_GROUPS * SSM_STATE
D_FF = 2816
FFN_CONV = 3
D_MAIN = 4 * D_HGRN + D_SSM + CONV_DIM
OFF_Q, OFF_F, OFF_I, OFF_G = 0, 1024, 2048, 3072
OFF_Z, OFF_XS, OFF_BC = 4096, 5120, 6144

CHUNK = 128
GROUP_W = D_SSM // SSM_GROUPS
VMEM_LIMIT = 56 * 1024 * 1024


def _cp(sem):
    return pltpu.CompilerParams(dimension_semantics=sem, vmem_limit_bytes=VMEM_LIMIT)


def _dot(a, b):
    return jnp.dot(a, b, preferred_element_type=F32)


def _dot_nt(a, b):
    return lax.dot_general(a, b, (((1,), (1,)), ((), ())), preferred_element_type=F32)


def _split3(x):
    h = x.astype(BF16)
    r = x - h.astype(F32)
    m = r.astype(BF16)
    lo = (r - m.astype(F32)).astype(BF16)
    return h, m, lo


def _dot_exact_lhs01(m01, x):
    h, m, lo = _split3(x)
    return _dot(m01, h) + _dot(m01, m) + _dot(m01, lo)


def _dot_exact_rhs01(x, m01):
    h, m, lo = _split3(x)
    return _dot(h, m01) + _dot(m, m01) + _dot(lo, m01)


def _dot_split_lhs01(m01, x):
    h = x.astype(BF16)
    lo = (x - h.astype(F32)).astype(BF16)
    return _dot(m01, h) + _dot(m01, lo)


def _sigmoid(x):
    return 1.0 / (1.0 + jnp.exp(-x))


def _silu(x):
    return x * _sigmoid(x)


def _rms(x, w):
    ms = jnp.mean(x * x, axis=-1, keepdims=True)
    return x * lax.rsqrt(ms + EPS) * w


def _inproj_kernel(x_ref, nw_ref, w_ref, wdt_ref, out_ref, dt_ref, h_scr):
    @pl.when(pl.program_id(1) == 0)
    def _():
        hb = _rms(x_ref[...], nw_ref[...]).astype(BF16)
        h_scr[...] = hb
        dt_ref[...] = _dot(hb, wdt_ref[...])

    out_ref[...] = _dot(h_scr[...], w_ref[...])


def _inproj(x2d, norm_w, w_main, w_dt, tm):
    n = x2d.shape[0]
    tn = 512
    return pl.pallas_call(
        _inproj_kernel,
        grid=(n // tm, D_MAIN // tn),
        in_specs=[
            pl.BlockSpec((tm, D_MODEL), lambda i, j: (i, 0)),
            pl.BlockSpec((1, D_MODEL), lambda i, j: (0, 0)),
            pl.BlockSpec((D_MODEL, tn), lambda i, j: (0, j)),
            pl.BlockSpec((D_MODEL, LANES), lambda i, j: (0, 0)),
        ],
        out_specs=[
            pl.BlockSpec((tm, tn), lambda i, j: (i, j)),
            pl.BlockSpec((tm, LANES), lambda i, j: (i, 0)),
        ],
        out_shape=[
            jax.ShapeDtypeStruct((n, D_MAIN), F32),
            jax.ShapeDtypeStruct((n, LANES), F32),
        ],
        scratch_shapes=[pltpu.VMEM((tm, D_MODEL), BF16)],
        compiler_params=_cp(("arbitrary", "arbitrary")),
        name="inproj",
    )(x2d, norm_w, w_main, w_dt)


N_LEVELS = 7
MXU_LEVEL_HALVES = (4, 2)


def _hgrn_const():
    c = CHUNK
    t = np.arange(c)[:, None]
    j = np.arange(c)[None, :]
    blocks = [(j <= t)]
    for h in MXU_LEVEL_HALVES:
        mid = (t // (2 * h)) * (2 * h) + h
        upper = (t >= mid) & (j >= mid) & (j <= t)
        lower = (t < mid) & (j > t) & (j < mid)
        blocks.append(upper | lower)
    return np.concatenate(blocks, axis=0).astype(np.float32)


def _midpoint_rows(b, h):
    pieces = []
    for start in range(0, CHUNK, 2 * h):
        mid = start + h
        pieces.append(jnp.broadcast_to(b[mid - 1:mid, :], (2 * h, b.shape[1])))
    return pieces[0] if len(pieces) == 1 else jnp.concatenate(pieces, axis=0)


def _mix_rows(q, k, h):
    pieces = []
    for start in range(0, CHUNK, 2 * h):
        pieces.append(k[start:start + h])
        pieces.append(q[start + h:start + 2 * h])
    return jnp.concatenate(pieces, axis=0)


def _hgrn_lb(lb_raw):
    mx = jnp.max(lb_raw, axis=0, keepdims=True)
    e = jnp.exp(lb_raw - mx)
    return e[0:1, :] / jnp.sum(e, axis=0, keepdims=True)


def _hgrn_gates(q_r, f_r, lb_raw):
    lb = _hgrn_lb(lb_raw)
    f = lb + (1.0 - lb) * _sigmoid(f_r)
    return _silu(q_r), f, 1.0 - f


def _level_map():
    t = lax.broadcasted_iota(jnp.int32, (CHUNK, CHUNK), 0)
    s = lax.broadcasted_iota(jnp.int32, (CHUNK, CHUNK), 1)
    bitlen = 32 - lax.clz(t ^ s)
    return jnp.where(t > s, bitlen, jnp.where(t == s, 0, -1))


def _hgrn_prompt_kernel(q_ref, f_ref, i_ref, g_ref, lb_ref, nw_ref, mc_ref,
                        o_ref, s_out_ref, st_scr):
    c = pl.program_id(1)

    @pl.when(c == 0)
    def _():
        st_scr[...] = jnp.zeros_like(st_scr)

    q_all, f_all, k_all = _hgrn_gates(q_ref[...], f_ref[...], lb_ref[...])
    e_heads = _dot_split_lhs01(mc_ref[...], jnp.log(f_all))
    lev = _level_map()
    row = lax.broadcasted_iota(jnp.int32, (CHUNK, HGRN_DK), 0)

    for h in range(HGRN_HEADS):
        sl = slice(h * HGRN_DK, (h + 1) * HGRN_DK)
        q, k, v = q_all[:, sl], k_all[:, sl], i_ref[:, sl]
        b = e_heads[0:CHUNK, sl]
        b_last = b[CHUNK - 1:CHUNK, :]

        st = st_scr[h]
        o = _dot_nt((q * jnp.exp(b)).astype(BF16), st.astype(BF16))

        a = jnp.where(lev == 0, _dot_nt(q.astype(BF16), k.astype(BF16)), 0.0)
        half = CHUNK // 2
        while half >= 1:
            if half >= SUBLANES:
                w = jnp.exp(-jnp.abs(b - _midpoint_rows(b, half)))
                x = _mix_rows(q, k, half) * w
            else:
                upper = (row & half) != 0
                if half in MXU_LEVEL_HALVES:
                    blk = 1 + MXU_LEVEL_HALVES.index(half)
                    w = jnp.exp(e_heads[blk * CHUNK:(blk + 1) * CHUNK, sl])
                    x = jnp.where(upper, q, k) * w
                else:
                    x = jnp.where(upper, q * f_all[:, sl], k)
            xb = x.astype(BF16)
            a = jnp.where(lev == half.bit_length(), _dot_nt(xb, xb), a)
            half //= 2
        o = o + _dot(a.astype(BF16), v.astype(BF16))

        ks = (k * jnp.exp(b_last - b)).astype(BF16)
        st_new = st * jnp.exp(b_last) + _dot(v.T.astype(BF16), ks)
        st_scr[h] = st_new

        o_ref[:, sl] = _rms(o, nw_ref[...]) * _silu(g_ref[:, sl])

    @pl.when(c == pl.num_programs(1) - 1)
    def _():
        for h in range(HGRN_HEADS):
            s_out_ref[0, h] = st_scr[h].T


def _hgrn_prompt(proj, lb_raw, norm_w, mconst, batch, seq):
    nc = seq // CHUNK

    def col(off):
        return pl.BlockSpec((CHUNK, D_HGRN), lambda b, c: (b * nc + c, off // D_HGRN))

    return pl.pallas_call(
        _hgrn_prompt_kernel,
        grid=(batch, nc),
        in_specs=[
            col(OFF_Q), col(OFF_F), col(OFF_I), col(OFF_G),
            pl.BlockSpec((lb_raw.shape[0], D_HGRN), lambda b, c: (0, 0)),
            pl.BlockSpec((1, HGRN_DV), lambda b, c: (0, 0)),
            pl.BlockSpec(mconst.shape, lambda b, c: (0, 0)),
        ],
        out_specs=[
            pl.BlockSpec((CHUNK, D_HGRN), lambda b, c: (b * nc + c, 0)),
            pl.BlockSpec((1, HGRN_HEADS, HGRN_DK, HGRN_DV), lambda b, c: (b, 0, 0, 0)),
        ],
        out_shape=[
            jax.ShapeDtypeStruct((batch * seq, D_HGRN), F32),
            jax.ShapeDtypeStruct((batch, HGRN_HEADS, HGRN_DK, HGRN_DV), F32),
        ],
        scratch_shapes=[pltpu.VMEM((HGRN_HEADS, HGRN_DV, HGRN_DK), F32)],
        compiler_params=_cp(("arbitrary", "arbitrary")),
        name="hgrn_prompt",
    )(proj, proj, proj, proj, lb_raw, norm_w, mconst)


def _hgrn_step_kernel(q_ref, f_ref, i_ref, g_ref, lb_ref, nw_ref, s_ref,
                      o_ref, s_out_ref, o_scr):
    nb = q_ref.shape[0]
    q, f, k = _hgrn_gates(q_ref[...], f_ref[...], lb_ref[...])
    v = i_ref[...]
    q_t, f_t, k_t = q.T, f.T, k.T
    for t in range(nb):
        s_new = f_t[:, t:t + 1] * s_ref[t, 0] + k_t[:, t:t + 1] * v[t:t + 1, :]
        s_out_ref[t, 0] = s_new
        o_scr[t:t + 1, :] = jnp.sum(q_t[:, t:t + 1] * s_new, axis=0, keepdims=True)
    o_ref[...] = _rms(o_scr[...], nw_ref[...]) * _silu(g_ref[...])


def _hgrn_step(proj, lb_raw, norm_w, state):
    nb = state.shape[0]
    hb = lambda off: off // HGRN_DK

    def col(off):
        return pl.BlockSpec((nb, HGRN_DK), lambda h: (0, hb(off) + h))

    st_spec = pl.BlockSpec((nb, 1, HGRN_DK, HGRN_DV), lambda h: (0, h, 0, 0))
    return pl.pallas_call(
        _hgrn_step_kernel,
        grid=(HGRN_HEADS,),
        in_specs=[
            col(OFF_Q), col(OFF_F), col(OFF_I), col(OFF_G),
            pl.BlockSpec((lb_raw.shape[0], HGRN_DK), lambda h: (0, h)),
            pl.BlockSpec((1, HGRN_DV), lambda h: (0, 0)),
            st_spec,
        ],
        out_specs=[pl.BlockSpec((nb, HGRN_DV), lambda h: (0, h)), st_spec],
        out_shape=[
            jax.ShapeDtypeStruct((nb, D_HGRN), F32),
            jax.ShapeDtypeStruct(state.shape, F32),
        ],
        scratch_shapes=[pltpu.VMEM((nb, HGRN_DV), F32)],
        compiler_params=_cp(("arbitrary",)),
        name="hgrn_step",
    )(proj, proj, proj, proj, lb_raw, norm_w, state)


def _head_expand():
    e = np.zeros((LANES, D_SSM), np.float32)
    for h in range(SSM_HEADS):
        e[h, h * SSM_HEAD_DIM:(h + 1) * SSM_HEAD_DIM] = 1.0
    return e


def _softplus(x):
    return jnp.maximum(x, 0.0) + jnp.log(1.0 + jnp.exp(-jnp.abs(x)))


def _ssm_gate_norm(y, z, nw):
    y = y * _silu(z)
    parts = [_rms(y[:, g * GROUP_W:(g + 1) * GROUP_W], nw[:, g * GROUP_W:(g + 1) * GROUP_W])
             for g in range(SSM_GROUPS)]
    return jnp.concatenate(parts, axis=-1)


def _ssd_prompt_kernel(z_ref, xs_ref, bc_ref, dt_ref, cw_ref, cb_ref, dtb_ref, alog_ref,
                       dvec_ref, nw_ref, tri_ref, exp_ref,
                       y_ref, st_out_ref, xe_scr, st_scr, yd_scr):
    c = pl.program_id(1)
    t = CHUNK
    pad = SUBLANES

    @pl.when(c == 0)
    def _():
        st_scr[...] = jnp.zeros_like(st_scr)
        xe_scr[0:pad, :] = jnp.zeros((pad, CONV_DIM), F32)

    @pl.when(c > 0)
    def _():
        xe_scr[0:pad, :] = xe_scr[t:t + pad, :]

    xe_scr[pad:, 0:D_SSM] = xs_ref[...]
    xe_scr[pad:, D_SSM:] = bc_ref[...]

    acc = cb_ref[...] + cw_ref[SSM_CONV - 1:SSM_CONV, :] * xe_scr[pad:, :]
    for d in range(1, SSM_CONV):
        acc = acc + cw_ref[SSM_CONV - 1 - d:SSM_CONV - d, :] * xe_scr[pad - d:pad - d + t, :]
    xbc = _silu(acc)
    xs = xbc[:, 0:D_SSM]

    dt = _softplus(dt_ref[...] + dtb_ref[...])
    da = dt * (-jnp.exp(alog_ref[...]))
    cs = _dot_exact_lhs01(tri_ref[...], da)
    ex = exp_ref[...]
    dt_full = _dot_exact_rhs01(dt, ex)
    cs_full = _dot_exact_rhs01(cs, ex)
    cs_last_full = cs_full[t - 1:t, :]
    x_dt = xs * dt_full
    x_end = (x_dt * jnp.exp(cs_last_full - cs_full)).astype(BF16)
    cs_t = cs.T

    row = lax.broadcasted_iota(jnp.int32, (t, t), 0)
    colm = lax.broadcasted_iota(jnp.int32, (t, t), 1)
    causal = row >= colm
    lane = lax.broadcasted_iota(jnp.int32, (1, LANES), 1)
    heads_per_group = SSM_HEADS // SSM_GROUPS
    pair_w = 2 * SSM_HEAD_DIM

    for g in range(SSM_GROUPS):
        b_g = xbc[:, D_SSM + g * SSM_STATE:D_SSM + (g + 1) * SSM_STATE]
        c_off = D_SSM + SSM_GROUPS * SSM_STATE + g * SSM_STATE
        c_g = xbc[:, c_off:c_off + SSM_STATE].astype(BF16)
        gmat = _dot_nt(c_g, b_g.astype(BF16))
        for pp in range(heads_per_group // 2):
            h0 = g * heads_per_group + 2 * pp
            xp = x_dt[:, h0 * SSM_HEAD_DIM:h0 * SSM_HEAD_DIM + pair_w]
            yp = None
            for sub in range(2):
                h = h0 + sub
                diff = cs[:, h:h + 1] - cs_t[h:h + 1, :]
                w = jnp.where(causal, jnp.exp(jnp.minimum(diff, 0.0)), 0.0) * gmat
                keep = (lane >= sub * SSM_HEAD_DIM) & (lane < (sub + 1) * SSM_HEAD_DIM)
                xm = jnp.where(keep, xp, 0.0).astype(BF16)
                part = _dot(w.astype(BF16), xm)
                yp = part if yp is None else yp + part
            yd_scr[:, h0 * SSM_HEAD_DIM:h0 * SSM_HEAD_DIM + pair_w] = yp

        sl = slice(g * GROUP_W, (g + 1) * GROUP_W)
        st_g = st_scr[:, sl]
        y_off = _dot(c_g, st_g.astype(BF16)) * jnp.exp(cs_full[:, sl])
        yd_scr[:, sl] = yd_scr[:, sl] + y_off
        st_scr[:, sl] = (st_g * jnp.exp(cs_last_full[:, sl])
                         + _dot(b_g.T.astype(BF16), x_end[:, sl]))

    y = yd_scr[...] + dvec_ref[...] * xs
    y_ref[...] = _ssm_gate_norm(y, z_ref[...], nw_ref[...])

    @pl.when(c == pl.num_programs(1) - 1)
    def _():
        for j in range(D_SSM // LANES):
            st_out_ref[0, j * LANES:(j + 1) * LANES, :] = st_scr[:, j * LANES:(j + 1) * LANES].T


def _ssd_prompt(proj, dtr, conv_w, conv_b, dt_bias, a_log, d_full, norm_w, tri, expand,
                batch, seq):
    nc = seq // CHUNK
    const = lambda shape: pl.BlockSpec(shape, lambda b, c: (0, 0))
    return pl.pallas_call(
        _ssd_prompt_kernel,
        grid=(batch, nc),
        in_specs=[
            pl.BlockSpec((CHUNK, D_SSM), lambda b, c: (b * nc + c, OFF_Z // D_SSM)),
            pl.BlockSpec((CHUNK, D_SSM), lambda b, c: (b * nc + c, OFF_XS // D_SSM)),
            pl.BlockSpec((CHUNK, 512), lambda b, c: (b * nc + c, OFF_BC // 512)),
            pl.BlockSpec((CHUNK, LANES), lambda b, c: (b * nc + c, 0)),
            const((SSM_CONV, CONV_DIM)), const((1, CONV_DIM)),
            const((1, LANES)), const((1, LANES)),
            const((1, D_SSM)), const((1, D_SSM)),
            const((CHUNK, CHUNK)), const((LANES, D_SSM)),
        ],
        out_specs=[
            pl.BlockSpec((CHUNK, D_SSM), lambda b, c: (b * nc + c, 0)),
            pl.BlockSpec((1, D_SSM, SSM_STATE), lambda b, c: (b, 0, 0)),
        ],
        out_shape=[
            jax.ShapeDtypeStruct((batch * seq, D_SSM), F32),
            jax.ShapeDtypeStruct((batch, D_SSM, SSM_STATE), F32),
        ],
        scratch_shapes=[
            pltpu.VMEM((CHUNK + SUBLANES, CONV_DIM), F32),
            pltpu.VMEM((SSM_STATE, D_SSM), F32),
            pltpu.VMEM((CHUNK, D_SSM), F32),
        ],
        compiler_params=_cp(("arbitrary", "arbitrary")),
        name="ssd_prompt",
    )(proj, proj, proj, dtr, conv_w, conv_b, dt_bias, a_log, d_full, norm_w, tri, expand)


def _ssd_step_kernel(z_ref, xs_ref, bc_ref, dt_ref, b0_ref, b1_ref, b2_ref, cw_ref, cb_ref,
                     dtb_ref, alog_ref, dvec_ref, nw_ref, exp_ref, st_ref,
                     y_ref, st_out_ref, xt_scr, at_scr, xs_scr, bc_scr, y_scr):
    p = pl.program_id(0)
    nb = z_ref.shape[0]
    pair_w = 2 * SSM_HEAD_DIM
    pairs_per_group = SSM_HEADS // SSM_GROUPS // 2

    @pl.when(p == 0)
    def _():
        x_new = jnp.concatenate([xs_ref[...], bc_ref[...]], axis=-1)
        acc = (cb_ref[...] + cw_ref[0:1, :] * b0_ref[...] + cw_ref[1:2, :] * b1_ref[...]
               + cw_ref[2:3, :] * b2_ref[...] + cw_ref[3:4, :] * x_new)
        xbc = _silu(acc)
        xs = xbc[:, 0:D_SSM]
        dt = _softplus(dt_ref[...] + dtb_ref[...])
        da = dt * (-jnp.exp(alog_ref[...]))
        ex = exp_ref[...]
        x_dt = xs * _dot_exact_rhs01(dt, ex)
        decay = jnp.exp(_dot_exact_rhs01(da, ex))
        xs_scr[...] = xs
        bc_scr[...] = xbc[:, D_SSM:]
        for j in range(D_SSM // LANES):
            sl = slice(j * LANES, (j + 1) * LANES)
            xt_scr[sl, :] = x_dt[:, sl].T
            at_scr[sl, :] = decay[:, sl].T

    g_is_1 = p >= pairs_per_group
    row0 = pl.multiple_of(p * pair_w, pair_w)
    x_t = xt_scr[pl.ds(row0, pair_w), :]
    a_t = at_scr[pl.ds(row0, pair_w), :]
    bc = bc_scr[...]
    b_all = jnp.where(g_is_1, bc[:, SSM_STATE:2 * SSM_STATE], bc[:, 0:SSM_STATE])
    c_all = jnp.where(g_is_1, bc[:, 3 * SSM_STATE:4 * SSM_STATE],
                      bc[:, 2 * SSM_STATE:3 * SSM_STATE]).astype(BF16)
    for t in range(nb):
        st = st_ref[t].reshape(pair_w, SSM_STATE)
        new = a_t[:, t:t + 1] * st + x_t[:, t:t + 1] * b_all[t:t + 1, :]
        st_out_ref[t] = new.reshape(2, SSM_HEAD_DIM, SSM_STATE)
        c_rows = jnp.broadcast_to(c_all[t:t + 1, :], (SUBLANES, SSM_STATE))
        y_scr[p, t:t + 1, :] = _dot_nt(c_rows, new.astype(BF16))[0:1, :]

    @pl.when(p == pl.num_programs(0) - 1)
    def _():
        y_mix = jnp.concatenate([y_scr[j] for j in range(SSM_HEADS // 2)], axis=-1)
        y = y_mix + dvec_ref[...] * xs_scr[...]
        y_ref[...] = _ssm_gate_norm(y, z_ref[...], nw_ref[...])


def _ssd_step(proj, dtr, buf, conv_w, conv_b, dt_bias, a_log, d_full, norm_w, expand, state):
    nb = state.shape[0]
    n_pairs = SSM_HEADS // 2
    const = lambda shape: pl.BlockSpec(shape, lambda p: (0, 0))
    st_spec = pl.BlockSpec((nb, 2, SSM_HEAD_DIM, SSM_STATE), lambda p: (0, p, 0, 0))
    return pl.pallas_call(
        _ssd_step_kernel,
        grid=(n_pairs,),
        in_specs=[
            pl.BlockSpec((nb, D_SSM), lambda p: (0, OFF_Z // D_SSM)),
            pl.BlockSpec((nb, D_SSM), lambda p: (0, OFF_XS // D_SSM)),
            pl.BlockSpec((nb, 512), lambda p: (0, OFF_BC // 512)),
            const((nb, LANES)),
            const((nb, CONV_DIM)), const((nb, CONV_DIM)), const((nb, CONV_DIM)),
            const((SSM_CONV, CONV_DIM)), const((1, CONV_DIM)),
            const((1, LANES)), const((1, LANES)),
            const((1, D_SSM)), const((1, D_SSM)),
            const((LANES, D_SSM)),
            st_spec,
        ],
        out_specs=[const((nb, D_SSM)), st_spec],
        out_shape=[
            jax.ShapeDtypeStruct((nb, D_SSM), F32),
            jax.ShapeDtypeStruct(state.shape, F32),
        ],
        scratch_shapes=[
            pltpu.VMEM((D_SSM, nb), F32),
            pltpu.VMEM((D_SSM, nb), F32),
            pltpu.VMEM((nb, D_SSM), F32),
            pltpu.VMEM((nb, 2 * SSM_GROUPS * SSM_STATE), F32),
            pltpu.VMEM((n_pairs, nb, 2 * SSM_HEAD_DIM), F32),
        ],
        compiler_params=_cp(("arbitrary",)),
        name="ssd_step",
    )(proj, proj, proj, dtr, buf[:, 0], buf[:, 1], buf[:, 2], conv_w, conv_b, dt_bias, a_log,
      d_full, norm_w, expand, state)


def _outproj_kernel(oa_ref, ys_ref, x_ref, wa_ref, ws_ref, nw_ref, x1_ref, h2_ref):
    x1 = (x_ref[...] + _dot(oa_ref[...].astype(BF16), wa_ref[...])
          + _dot(ys_ref[...].astype(BF16), ws_ref[...]))
    x1_ref[...] = x1
    h2_ref[...] = _rms(x1, nw_ref[...]).astype(BF16)


def _outproj(o_a, y_s, x2d, w_a, w_s, norm_w, tm):
    n = x2d.shape[0]
    row = lambda w: pl.BlockSpec((tm, w), lambda i: (i, 0))
    const = lambda shape: pl.BlockSpec(shape, lambda i: (0, 0))
    return pl.pallas_call(
        _outproj_kernel,
        grid=(n // tm,),
        in_specs=[row(D_HGRN), row(D_SSM), row(D_MODEL),
                  const((D_HGRN, D_MODEL)), const((D_SSM, D_MODEL)), const((1, D_MODEL))],
        out_specs=[row(D_MODEL), row(D_MODEL)],
        out_shape=[jax.ShapeDtypeStruct((n, D_MODEL), F32),
                   jax.ShapeDtypeStruct((n, D_MODEL), BF16)],
        compiler_params=_cp(("arbitrary",)),
        name="outproj",
    )(o_a, y_s, x2d, w_a, w_s, norm_w)


FF_BLOCK = 256


def _ffn_finish(j, contrib, x1_ref, fnw_ref, y_ref, acc_scr):
    @pl.when(j == 0)
    def _():
        acc_scr[...] = contrib

    @pl.when(j > 0)
    def _():
        acc_scr[...] = acc_scr[...] + contrib

    @pl.when(j == pl.num_programs(1) - 1)
    def _():
        y_ref[...] = _rms(x1_ref[...] + acc_scr[...], fnw_ref[...])


def _ffn_prompt_kernel(h2_ref, x1_ref, wg_ref, wv_ref, wd_ref, cw_ref, cb_ref, fnw_ref,
                       y_ref, tail_ref, acc_scr, ge_scr, carry_scr, *, tiles_per_seq):
    i = pl.program_id(0)
    j = pl.program_id(1)
    tm = h2_ref.shape[0]
    pad = SUBLANES
    h2 = h2_ref[...]
    gate = _dot(h2, wg_ref[...])
    val = _dot(h2, wv_ref[...])

    seq_start = lax.rem(i, tiles_per_seq) == 0
    ge_scr[0:pad, :] = jnp.where(seq_start, 0.0, carry_scr[j])
    ge_scr[pad:, :] = gate
    tail = gate[tm - pad:, :]
    carry_scr[j] = tail
    tail_ref[0] = tail

    conv = (cb_ref[...] + cw_ref[2:3, :] * gate + cw_ref[1:2, :] * ge_scr[pad - 1:pad - 1 + tm, :]
            + cw_ref[0:1, :] * ge_scr[pad - 2:pad - 2 + tm, :])
    act = (_silu(conv) * val).astype(BF16)
    _ffn_finish(j, _dot(act, wd_ref[...]), x1_ref, fnw_ref, y_ref, acc_scr)


def _ffn_prompt(h2, x1, w_up, w_down, conv_w, conv_b, fnorm_w, tm, seq):
    n = h2.shape[0]
    nj = D_FF // FF_BLOCK
    kern = functools.partial(_ffn_prompt_kernel, tiles_per_seq=seq // tm)
    return pl.pallas_call(
        kern,
        grid=(n // tm, nj),
        in_specs=[
            pl.BlockSpec((tm, D_MODEL), lambda i, j: (i, 0)),
            pl.BlockSpec((tm, D_MODEL), lambda i, j: (i, 0)),
            pl.BlockSpec((D_MODEL, FF_BLOCK), lambda i, j: (0, j)),
            pl.BlockSpec((D_MODEL, FF_BLOCK), lambda i, j: (0, nj + j)),
            pl.BlockSpec((FF_BLOCK, D_MODEL), lambda i, j: (j, 0)),
            pl.BlockSpec((FFN_CONV, FF_BLOCK), lambda i, j: (0, j)),
            pl.BlockSpec((1, FF_BLOCK), lambda i, j: (0, j)),
            pl.BlockSpec((1, D_MODEL), lambda i, j: (0, 0)),
        ],
        out_specs=[
            pl.BlockSpec((tm, D_MODEL), lambda i, j: (i, 0)),
            pl.BlockSpec((1, SUBLANES, FF_BLOCK), lambda i, j: (i, 0, j)),
        ],
        out_shape=[
            jax.ShapeDtypeStruct((n, D_MODEL), F32),
            jax.ShapeDtypeStruct((n // tm, SUBLANES, D_FF), F32),
        ],
        scratch_shapes=[
            pltpu.VMEM((tm, D_MODEL), F32),
            pltpu.VMEM((tm + SUBLANES, FF_BLOCK), F32),
            pltpu.VMEM((nj, SUBLANES, FF_BLOCK), F32),
        ],
        compiler_params=_cp(("arbitrary", "arbitrary")),
        name="ffn_prompt",
    )(h2, x1, w_up, w_up, w_down, conv_w, conv_b, fnorm_w)


def _ffn_step_kernel(h2_ref, x1_ref, wg_ref, wv_ref, wd_ref, cw_ref, cb_ref, fnw_ref,
                     b0_ref, b1_ref, y_ref, gate_ref, acc_scr):
    j = pl.program_id(1)
    h2 = h2_ref[...]
    gate = _dot(h2, wg_ref[...])
    val = _dot(h2, wv_ref[...])
    gate_ref[...] = gate
    conv = (cb_ref[...] + cw_ref[2:3, :] * gate + cw_ref[1:2, :] * b1_ref[...]
            + cw_ref[0:1, :] * b0_ref[...])
    act = (_silu(conv) * val).astype(BF16)
    _ffn_finish(j, _dot(act, wd_ref[...]), x1_ref, fnw_ref, y_ref, acc_scr)


def _ffn_step(h2, x1, w_up, w_down, conv_w, conv_b, fnorm_w, buf):
    n = h2.shape[0]
    nj = D_FF // FF_BLOCK
    return pl.pallas_call(
        _ffn_step_kernel,
        grid=(1, nj),
        in_specs=[
            pl.BlockSpec((n, D_MODEL), lambda i, j: (0, 0)),
            pl.BlockSpec((n, D_MODEL), lambda i, j: (0, 0)),
            pl.BlockSpec((D_MODEL, FF_BLOCK), lambda i, j: (0, j)),
            pl.BlockSpec((D_MODEL, FF_BLOCK), lambda i, j: (0, nj + j)),
            pl.BlockSpec((FF_BLOCK, D_MODEL), lambda i, j: (j, 0)),
            pl.BlockSpec((FFN_CONV, FF_BLOCK), lambda i, j: (0, j)),
            pl.BlockSpec((1, FF_BLOCK), lambda i, j: (0, j)),
            pl.BlockSpec((1, D_MODEL), lambda i, j: (0, 0)),
            pl.BlockSpec((n, FF_BLOCK), lambda i, j: (0, j)),
            pl.BlockSpec((n, FF_BLOCK), lambda i, j: (0, j)),
        ],
        out_specs=[
            pl.BlockSpec((n, D_MODEL), lambda i, j: (0, 0)),
            pl.BlockSpec((n, FF_BLOCK), lambda i, j: (0, j)),
        ],
        out_shape=[
            jax.ShapeDtypeStruct((n, D_MODEL), F32),
            jax.ShapeDtypeStruct((n, D_FF), F32),
        ],
        scratch_shapes=[pltpu.VMEM((n, D_MODEL), F32)],
        compiler_params=_cp(("arbitrary", "arbitrary")),
        name="ffn_step",
    )(h2, x1, w_up, w_up, w_down, conv_w, conv_b, fnorm_w, buf[:, 0], buf[:, 1])


def _row(v):
    return v.reshape(1, -1).astype(F32)


def _pad_lanes(v):
    return jnp.pad(v.astype(F32), (0, LANES - v.shape[0])).reshape(1, LANES)


def _row_tile(n):
    for tm in (1024, 512, 256, 128):
        if n % tm == 0:
            return tm
    raise ValueError(f"token count {n} is not a multiple of 128")


def kernel(x_prompt, x_sample, state_hgrn, state_ssm, state_conv_ssm, state_conv_ffn, norm1_w, w_in, hgrn_lb, hgrn_norm_w, ssm_conv_w, ssm_conv_b, ssm_dt_bias, ssm_a_log, ssm_d, ssm_norm_w, w_out, norm2_w, w_up, ffn_conv_w, ffn_conv_b, w_down, final_norm_w):
    depth = w_in.shape[0]
    assert depth == 1, "single-layer trunk"
    l = 0
    batch, seq, _ = x_prompt.shape
    dec_batch, dec_seq, _ = x_sample.shape
    assert dec_seq == 1 and seq % CHUNK == 0 and seq >= SSM_CONV

    w_main = w_in[l][:, :D_MAIN].astype(BF16)
    w_dt = jnp.pad(w_in[l][:, D_MAIN:], ((0, 0), (0, LANES - SSM_HEADS))).astype(BF16)
    w_oa = w_out[l][:D_HGRN].astype(BF16)
    w_os = w_out[l][D_HGRN:].astype(BF16)
    w_upb = w_up[l].astype(BF16)
    w_dnb = w_down[l].astype(BF16)
    d_full = jnp.repeat(ssm_d[l].astype(F32), SSM_HEAD_DIM).reshape(1, D_SSM)
    dt_bias = _pad_lanes(ssm_dt_bias[l])
    a_log = _pad_lanes(ssm_a_log[l])
    mconst = jnp.asarray(_hgrn_const(), BF16)
    tri = jnp.asarray(np.tril(np.ones((CHUNK, CHUNK), np.float32)), BF16)
    expand = jnp.asarray(_head_expand(), BF16)
    lb_raw = hgrn_lb.astype(F32)

    def dense_tail(x2d, o_a, y_s, tm):
        return _outproj(o_a, y_s, x2d, w_oa, w_os, _row(norm2_w[l]), tm)

    xp = x_prompt.reshape(batch * seq, D_MODEL)
    tm_p = _row_tile(seq)
    proj_p, dt_p = _inproj(xp, _row(norm1_w[l]), w_main, w_dt, tm_p)
    oa_p, hgrn_p = _hgrn_prompt(proj_p, lb_raw, _row(hgrn_norm_w[l]), mconst, batch, seq)
    ys_p, ssm_p = _ssd_prompt(proj_p, dt_p, ssm_conv_w[l], _row(ssm_conv_b[l]), dt_bias, a_log,
                              d_full, _row(ssm_norm_w[l]), tri, expand, batch, seq)
    x1_p, h2_p = dense_tail(xp, oa_p, ys_p, tm_p)
    y_p, tail_p = _ffn_prompt(h2_p, x1_p, w_upb, w_dnb, ffn_conv_w[l], _row(ffn_conv_b[l]),
                              _row(final_norm_w), tm_p, seq)
    proj_p3 = proj_p.reshape(batch, seq, D_MAIN)
    cs_p = proj_p3[:, seq - (SSM_CONV - 1):, OFF_XS:OFF_XS + CONV_DIM]
    tails = tail_p.reshape(batch, seq // tm_p, SUBLANES, D_FF)
    cf_p = tails[:, -1, SUBLANES - (FFN_CONV - 1):, :]

    xs_ = x_sample.reshape(dec_batch, D_MODEL)
    proj_s, dt_s = _inproj(xs_, _row(norm1_w[l]), w_main, w_dt, dec_batch)
    oa_s, hgrn_s = _hgrn_step(proj_s, lb_raw, _row(hgrn_norm_w[l]), state_hgrn[l])
    ys_s, ssm_s = _ssd_step(proj_s, dt_s, state_conv_ssm[l], ssm_conv_w[l], _row(ssm_conv_b[l]),
                            dt_bias, a_log, d_full, _row(ssm_norm_w[l]), expand, state_ssm[l])
    x1_s, h2_s = dense_tail(xs_, oa_s, ys_s, dec_batch)
    y_s, gate_s = _ffn_step(h2_s, x1_s, w_upb, w_dnb, ffn_conv_w[l], _row(ffn_conv_b[l]),
                            _row(final_norm_w), state_conv_ffn[l])
    cs_s = jnp.concatenate([state_conv_ssm[l][:, 1:], proj_s[:, None, OFF_XS:OFF_XS + CONV_DIM]],
                           axis=1)
    cf_s = jnp.concatenate([state_conv_ffn[l][:, 1:], gate_s[:, None, :]], axis=1)

    dt_ = x_prompt.dtype
    return (y_p.reshape(batch, seq, D_MODEL).astype(dt_),
            y_s.reshape(dec_batch, 1, D_MODEL).astype(dt_),
            hgrn_p[None].astype(dt_),
            hgrn_s[None].astype(dt_),
            ssm_p.reshape(1, batch, SSM_HEADS, SSM_HEAD_DIM, SSM_STATE).astype(dt_),
            ssm_s[None].astype(dt_),
            cs_p[None].astype(dt_),
            cs_s[None].astype(dt_),
            cf_p[None].astype(dt_),
            cf_s[None].astype(dt_))
```

```python
import functools

import numpy as np
import jax
import jax.numpy as jnp
from jax import lax
from jax.experimental import pallas as pl
from jax.experimental.pallas import tpu as pltpu

F32 = jnp.float32
BF16 = jnp.bfloat16
EPS = 1e-6

LANES = 128
SUBLANES = 8

D_MODEL = 1024
HGRN_HEADS = 8
HGRN_DK = 128
HGRN_DV = 128
D_HGRN = HGRN_HEADS * HGRN_DV
SSM_HEADS = 16
SSM_HEAD_DIM = 64
D_SSM = SSM_HEADS * SSM_HEAD_DIM
SSM_STATE = 128
SSM_GROUPS = 2
SSM_CONV = 4
CONV_DIM = D_SSM + 2 * SSM_GROUPS * SSM_STATE
D_FF = 2816
FFN_CONV = 3
D_MAIN = 4 * D_HGRN + D_SSM + CONV_DIM
OFF_Q, OFF_F, OFF_I, OFF_G = 0, 1024, 2048, 3072
OFF_Z, OFF_XS, OFF_BC = 4096, 5120, 6144

CHUNK = 128
GROUP_W = D_SSM // SSM_GROUPS
VMEM_LIMIT = 56 * 1024 * 1024


def _cp(sem):
    return pltpu.CompilerParams(dimension_semantics=sem, vmem_limit_bytes=VMEM_LIMIT)


def _dot(a, b):
    return jnp.dot(a, b, preferred_element_type=F32)


def _dot_nt(a, b):
    return lax.dot_general(a, b, (((1,), (1,)), ((), ())), preferred_element_type=F32)


def _split3(x):
    h = x.astype(BF16)
    r = x - h.astype(F32)
    m = r.astype(BF16)
    lo = (r - m.astype(F32)).astype(BF16)
    return h, m, lo


def _dot_exact_lhs01(m01, x):
    h, m, lo = _split3(x)
    return _dot(m01, h) + _dot(m01, m) + _dot(m01, lo)


def _dot_exact_rhs01(x, m01):
    h, m, lo = _split3(x)
    return _dot(h, m01) + _dot(m, m01) + _dot(lo, m01)


def _dot_split_lhs01(m01, x):
    h = x.astype(BF16)
    lo = (x - h.astype(F32)).astype(BF16)
    return _dot(m01, h) + _dot(m01, lo)


def _sigmoid(x):
    return 1.0 / (1.0 + jnp.exp(-x))


def _silu(x):
    return x * _sigmoid(x)


def _rms(x, w):
    ms = jnp.mean(x * x, axis=-1, keepdims=True)
    return x * lax.rsqrt(ms + EPS) * w


def _inproj_kernel(x_ref, nw_ref, w_ref, wdt_ref, lb_ref, dtb_ref, mix_ref, lg_ref, dt_ref):
    hb = _rms(x_ref[...], nw_ref[...]).astype(BF16)

    def sec(off, width=D_HGRN):
        return _dot(hb, w_ref[:, off:off + width])

    def put(off, val):
        mix_ref[:, off:off + val.shape[1]] = val.astype(BF16)

    lb = _hgrn_lb(lb_ref[...])
    f = lb + (1.0 - lb) * _sigmoid(sec(OFF_F))
    lg_ref[...] = jnp.log(f)
    put(OFF_F, 1.0 - f)
    put(OFF_Q, _silu(sec(OFF_Q)))
    put(OFF_I, sec(OFF_I))
    put(OFF_G, _silu(sec(OFF_G)))
    put(OFF_Z, _silu(sec(OFF_Z, D_SSM)))
    put(OFF_XS, sec(OFF_XS, CONV_DIM))
    dt_ref[...] = _softplus(_dot(hb, wdt_ref[...]) + dtb_ref[...])


INPROJ_ROW_TILE = 512


def _inproj(x2d, norm_w, w_main, w_dt, lb_raw, dt_bias):
    n = x2d.shape[0]
    tm = min(INPROJ_ROW_TILE, n)
    assert n % tm == 0
    row = lambda w: pl.BlockSpec((tm, w), lambda i: (i, 0))
    resident = lambda shape: pl.BlockSpec(shape, lambda i: (0, 0), pipeline_mode=pl.Buffered(1))
    return pl.pallas_call(
        _inproj_kernel,
        grid=(n // tm,),
        in_specs=[
            row(D_MODEL), resident((1, D_MODEL)),
            resident((D_MODEL, D_MAIN)), resident((D_MODEL, LANES)),
            resident(lb_raw.shape), resident((1, LANES)),
        ],
        out_specs=[row(D_MAIN), row(D_HGRN), row(LANES)],
        out_shape=[
            jax.ShapeDtypeStruct((n, D_MAIN), BF16),
            jax.ShapeDtypeStruct((n, D_HGRN), F32),
            jax.ShapeDtypeStruct((n, LANES), F32),
        ],
        compiler_params=_cp(("arbitrary",)),
        name="inproj",
    )(x2d, norm_w, w_main, w_dt, lb_raw, dt_bias)


N_LEVELS = 7
MXU_LEVEL_HALVES = (4, 2)


def _hgrn_const():
    c = CHUNK
    t = np.arange(c)[:, None]
    j = np.arange(c)[None, :]
    blocks = [(j <= t)]
    for h in MXU_LEVEL_HALVES:
        mid = (t // (2 * h)) * (2 * h) + h
        upper = (t >= mid) & (j >= mid) & (j <= t)
        lower = (t < mid) & (j > t) & (j < mid)
        blocks.append(upper | lower)
    return np.concatenate(blocks, axis=0).astype(np.float32)


def _midpoint_rows(b, h):
    pieces = []
    for start in range(0, CHUNK, 2 * h):
        mid = start + h
        pieces.append(jnp.broadcast_to(b[mid - 1:mid, :], (2 * h, b.shape[1])))
    return pieces[0] if len(pieces) == 1 else jnp.concatenate(pieces, axis=0)


def _mix_rows(q, k, h):
    pieces = []
    for start in range(0, CHUNK, 2 * h):
        pieces.append(k[start:start + h])
        pieces.append(q[start + h:start + 2 * h])
    return jnp.concatenate(pieces, axis=0)


def _hgrn_lb(lb_raw):
    mx = jnp.max(lb_raw, axis=0, keepdims=True)
    e = jnp.exp(lb_raw - mx)
    return e[0:1, :] / jnp.sum(e, axis=0, keepdims=True)


def _level_map():
    t = lax.broadcasted_iota(jnp.int32, (CHUNK, CHUNK), 0)
    s = lax.broadcasted_iota(jnp.int32, (CHUNK, CHUNK), 1)
    bitlen = 32 - lax.clz(t ^ s)
    return jnp.where(t > s, bitlen, jnp.where(t == s, 0, -1))


def _hgrn_prompt_kernel(q_ref, k_ref, i_ref, g_ref, lg_ref, nw_ref, mc_ref,
                        o_ref, s_out_ref, st_scr):
    c = pl.program_id(1)

    @pl.when(c == 0)
    def _():
        st_scr[...] = jnp.zeros_like(st_scr)

    e_heads = _dot_split_lhs01(mc_ref[...], lg_ref[...])
    lev = _level_map()
    row = lax.broadcasted_iota(jnp.int32, (CHUNK, HGRN_DK), 0)

    for h in range(HGRN_HEADS):
        sl = slice(h * HGRN_DK, (h + 1) * HGRN_DK)
        qb, kb, vb = q_ref[:, sl], k_ref[:, sl], i_ref[:, sl]
        q, k = qb.astype(F32), kb.astype(F32)
        b = e_heads[0:CHUNK, sl]
        b_last = b[CHUNK - 1:CHUNK, :]

        st = st_scr[h]
        o = _dot_nt((q * jnp.exp(b)).astype(BF16), st.astype(BF16))

        a = jnp.where(lev == 0, _dot_nt(qb, kb), 0.0)
        half = CHUNK // 2
        while half >= 1:
            if half >= SUBLANES:
                w = jnp.exp(-jnp.abs(b - _midpoint_rows(b, half)))
                x = _mix_rows(q, k, half) * w
            else:
                upper = (row & half) != 0
                if half in MXU_LEVEL_HALVES:
                    blk = 1 + MXU_LEVEL_HALVES.index(half)
                    w = jnp.exp(e_heads[blk * CHUNK:(blk + 1) * CHUNK, sl])
                    x = jnp.where(upper, q, k) * w
                else:
                    x = jnp.where(upper, q * (1.0 - k), k)
            xb = x.astype(BF16)
            a = jnp.where(lev == half.bit_length(), _dot_nt(xb, xb), a)
            half //= 2
        o = o + _dot(a.astype(BF16), vb)

        ks = (k * jnp.exp(b_last - b)).astype(BF16)
        st_new = st * jnp.exp(b_last) + _dot(vb.astype(F32).T.astype(BF16), ks)
        st_scr[h] = st_new

        o_ref[:, sl] = (_rms(o, nw_ref[...]) * g_ref[:, sl].astype(F32)).astype(BF16)

    @pl.when(c == pl.num_programs(1) - 1)
    def _():
        for h in range(HGRN_HEADS):
            s_out_ref[0, h] = st_scr[h].T


def _hgrn_prompt(mix, lg, norm_w, mconst, batch, seq):
    nc = seq // CHUNK

    def col(off):
        return pl.BlockSpec((CHUNK, D_HGRN), lambda b, c: (b * nc + c, off // D_HGRN))

    return pl.pallas_call(
        _hgrn_prompt_kernel,
        grid=(batch, nc),
        in_specs=[
            col(OFF_Q), col(OFF_F), col(OFF_I), col(OFF_G), col(0),
            pl.BlockSpec((1, HGRN_DV), lambda b, c: (0, 0)),
            pl.BlockSpec(mconst.shape, lambda b, c: (0, 0)),
        ],
        out_specs=[
            pl.BlockSpec((CHUNK, D_HGRN), lambda b, c: (b * nc + c, 0)),
            pl.BlockSpec((1, HGRN_HEADS, HGRN_DK, HGRN_DV), lambda b, c: (b, 0, 0, 0)),
        ],
        out_shape=[
            jax.ShapeDtypeStruct((batch * seq, D_HGRN), BF16),
            jax.ShapeDtypeStruct((batch, HGRN_HEADS, HGRN_DK, HGRN_DV), F32),
        ],
        scratch_shapes=[pltpu.VMEM((HGRN_HEADS, HGRN_DV, HGRN_DK), F32)],
        compiler_params=_cp(("arbitrary", "arbitrary")),
        name="hgrn_prompt",
    )(mix, mix, mix, mix, lg, norm_w, mconst)


def _hgrn_step_kernel(q_ref, i_ref, g_ref, lg_ref, nw_ref, s_ref,
                      o_ref, s_out_ref, o_scr):
    nb = q_ref.shape[0]
    q = q_ref[...].astype(F32)
    f = jnp.exp(lg_ref[...])
    k = 1.0 - f
    v = i_ref[...].astype(F32)
    q_t, f_t, k_t = q.T, f.T, k.T
    for t in range(nb):
        s_new = f_t[:, t:t + 1] * s_ref[t, 0] + k_t[:, t:t + 1] * v[t:t + 1, :]
        s_out_ref[t, 0] = s_new
        o_scr[t:t + 1, :] = jnp.sum(q_t[:, t:t + 1] * s_new, axis=0, keepdims=True)
    o_ref[...] = (_rms(o_scr[...], nw_ref[...]) * g_ref[...].astype(F32)).astype(BF16)


def _hgrn_step(mix, lg, norm_w, state):
    nb = state.shape[0]
    hb = lambda off: off // HGRN_DK

    def col(off):
        return pl.BlockSpec((nb, HGRN_DK), lambda h: (0, hb(off) + h))

    st_spec = pl.BlockSpec((nb, 1, HGRN_DK, HGRN_DV), lambda h: (0, h, 0, 0))
    return pl.pallas_call(
        _hgrn_step_kernel,
        grid=(HGRN_HEADS,),
        in_specs=[
            col(OFF_Q), col(OFF_I), col(OFF_G), col(0),
            pl.BlockSpec((1, HGRN_DV), lambda h: (0, 0)),
            st_spec,
        ],
        out_specs=[pl.BlockSpec((nb, HGRN_DV), lambda h: (0, h)), st_spec],
        out_shape=[
            jax.ShapeDtypeStruct((nb, D_HGRN), BF16),
            jax.ShapeDtypeStruct(state.shape, F32),
        ],
        scratch_shapes=[pltpu.VMEM((nb, HGRN_DV), F32)],
        compiler_params=_cp(("arbitrary",)),
        name="hgrn_step",
    )(mix, mix, mix, lg, norm_w, state)


def _head_expand():
    e = np.zeros((LANES, D_SSM), np.float32)
    for h in range(SSM_HEADS):
        e[h, h * SSM_HEAD_DIM:(h + 1) * SSM_HEAD_DIM] = 1.0
    return e


def _softplus(x):
    return jnp.maximum(x, 0.0) + jnp.log(1.0 + jnp.exp(-jnp.abs(x)))


def _ssm_gate_norm(y, z_gate, nw):
    y = y * z_gate.astype(F32)
    parts = [_rms(y[:, g * GROUP_W:(g + 1) * GROUP_W], nw[:, g * GROUP_W:(g + 1) * GROUP_W])
             for g in range(SSM_GROUPS)]
    return jnp.concatenate(parts, axis=-1)


def _ssd_prompt_kernel(z_ref, xs_ref, bc_ref, dt_ref, cw_ref, cb_ref, alog_ref,
                       dvec_ref, nw_ref, tri_ref, exp_ref,
                       y_ref, st_out_ref, xe_scr, st_scr, yd_scr):
    c = pl.program_id(1)
    t = CHUNK
    pad = SUBLANES

    @pl.when(c == 0)
    def _():
        st_scr[...] = jnp.zeros_like(st_scr)
        xe_scr[0:pad, :] = jnp.zeros((pad, CONV_DIM), F32)

    @pl.when(c > 0)
    def _():
        xe_scr[0:pad, :] = xe_scr[t:t + pad, :]

    xe_scr[pad:, 0:D_SSM] = xs_ref[...].astype(F32)
    xe_scr[pad:, D_SSM:] = bc_ref[...].astype(F32)

    acc = cb_ref[...] + cw_ref[SSM_CONV - 1:SSM_CONV, :] * xe_scr[pad:, :]
    for d in range(1, SSM_CONV):
        acc = acc + cw_ref[SSM_CONV - 1 - d:SSM_CONV - d, :] * xe_scr[pad - d:pad - d + t, :]
    xbc = _silu(acc)
    xs = xbc[:, 0:D_SSM]

    dt = dt_ref[...]
    da = dt * (-jnp.exp(alog_ref[...]))
    cs = _dot_exact_lhs01(tri_ref[...], da)
    ex = exp_ref[...]
    dt_full = _dot_exact_rhs01(dt, ex)
    cs_full = _dot_exact_rhs01(cs, ex)
    cs_last_full = cs_full[t - 1:t, :]
    x_dt = xs * dt_full
    x_end = (x_dt * jnp.exp(cs_last_full - cs_full)).astype(BF16)
    cs_t = cs.T

    row = lax.broadcasted_iota(jnp.int32, (t, t), 0)
    colm = lax.broadcasted_iota(jnp.int32, (t, t), 1)
    causal = row >= colm
    lane = lax.broadcasted_iota(jnp.int32, (1, LANES), 1)
    heads_per_group = SSM_HEADS // SSM_GROUPS
    pair_w = 2 * SSM_HEAD_DIM

    for g in range(SSM_GROUPS):
        b_g = xbc[:, D_SSM + g * SSM_STATE:D_SSM + (g + 1) * SSM_STATE]
        c_off = D_SSM + SSM_GROUPS * SSM_STATE + g * SSM_STATE
        c_g = xbc[:, c_off:c_off + SSM_STATE].astype(BF16)
        gmat = _dot_nt(c_g, b_g.astype(BF16))
        for pp in range(heads_per_group // 2):
            h0 = g * heads_per_group + 2 * pp
            xp = x_dt[:, h0 * SSM_HEAD_DIM:h0 * SSM_HEAD_DIM + pair_w]
            yp = None
            for sub in range(2):
                h = h0 + sub
                diff = cs[:, h:h + 1] - cs_t[h:h + 1, :]
                w = jnp.where(causal, jnp.exp(jnp.minimum(diff, 0.0)), 0.0) * gmat
                keep = (lane >= sub * SSM_HEAD_DIM) & (lane < (sub + 1) * SSM_HEAD_DIM)
                xm = jnp.where(keep, xp, 0.0).astype(BF16)
                part = _dot(w.astype(BF16), xm)
                yp = part if yp is None else yp + part
            yd_scr[:, h0 * SSM_HEAD_DIM:h0 * SSM_HEAD_DIM + pair_w] = yp

        sl = slice(g * GROUP_W, (g + 1) * GROUP_W)
        st_g = st_scr[:, sl]
        y_off = _dot(c_g, st_g.astype(BF16)) * jnp.exp(cs_full[:, sl])
        yd_scr[:, sl] = yd_scr[:, sl] + y_off
        st_scr[:, sl] = (st_g * jnp.exp(cs_last_full[:, sl])
                         + _dot(b_g.T.astype(BF16), x_end[:, sl]))

    y = yd_scr[...] + dvec_ref[...] * xs
    y_ref[...] = _ssm_gate_norm(y, z_ref[...], nw_ref[...]).astype(BF16)

    @pl.when(c == pl.num_programs(1) - 1)
    def _():
        for j in range(D_SSM // LANES):
            st_out_ref[0, j * LANES:(j + 1) * LANES, :] = st_scr[:, j * LANES:(j + 1) * LANES].T


def _ssd_prompt(mix, dt, conv_w, conv_b, a_log, d_full, norm_w, tri, expand, batch, seq):
    nc = seq // CHUNK
    const = lambda shape: pl.BlockSpec(shape, lambda b, c: (0, 0))
    return pl.pallas_call(
        _ssd_prompt_kernel,
        grid=(batch, nc),
        in_specs=[
            pl.BlockSpec((CHUNK, D_SSM), lambda b, c: (b * nc + c, OFF_Z // D_SSM)),
            pl.BlockSpec((CHUNK, D_SSM), lambda b, c: (b * nc + c, OFF_XS // D_SSM)),
            pl.BlockSpec((CHUNK, 512), lambda b, c: (b * nc + c, OFF_BC // 512)),
            pl.BlockSpec((CHUNK, LANES), lambda b, c: (b * nc + c, 0)),
            const((SSM_CONV, CONV_DIM)), const((1, CONV_DIM)),
            const((1, LANES)),
            const((1, D_SSM)), const((1, D_SSM)),
            const((CHUNK, CHUNK)), const((LANES, D_SSM)),
        ],
        out_specs=[
            pl.BlockSpec((CHUNK, D_SSM), lambda b, c: (b * nc + c, 0)),
            pl.BlockSpec((1, D_SSM, SSM_STATE), lambda b, c: (b, 0, 0)),
        ],
        out_shape=[
            jax.ShapeDtypeStruct((batch * seq, D_SSM), BF16),
            jax.ShapeDtypeStruct((batch, D_SSM, SSM_STATE), F32),
        ],
        scratch_shapes=[
            pltpu.VMEM((CHUNK + SUBLANES, CONV_DIM), F32),
            pltpu.VMEM((SSM_STATE, D_SSM), F32),
            pltpu.VMEM((CHUNK, D_SSM), F32),
        ],
        compiler_params=_cp(("arbitrary", "arbitrary")),
        name="ssd_prompt",
    )(mix, mix, mix, dt, conv_w, conv_b, a_log, d_full, norm_w, tri, expand)


def _ssd_step_kernel(z_ref, xs_ref, bc_ref, dt_ref, b0_ref, b1_ref, b2_ref, cw_ref, cb_ref,
                     alog_ref, dvec_ref, nw_ref, exp_ref, st_ref,
                     y_ref, st_out_ref, xt_scr, at_scr, xs_scr, bc_scr, y_scr):
    p = pl.program_id(0)
    nb = z_ref.shape[0]
    pair_w = 2 * SSM_HEAD_DIM
    pairs_per_group = SSM_HEADS // SSM_GROUPS // 2

    @pl.when(p == 0)
    def _():
        x_new = jnp.concatenate([xs_ref[...], bc_ref[...]], axis=-1).astype(F32)
        acc = (cb_ref[...] + cw_ref[0:1, :] * b0_ref[...] + cw_ref[1:2, :] * b1_ref[...]
               + cw_ref[2:3, :] * b2_ref[...] + cw_ref[3:4, :] * x_new)
        xbc = _silu(acc)
        xs = xbc[:, 0:D_SSM]
        dt = dt_ref[...]
        da = dt * (-jnp.exp(alog_ref[...]))
        ex = exp_ref[...]
        x_dt = xs * _dot_exact_rhs01(dt, ex)
        decay = jnp.exp(_dot_exact_rhs01(da, ex))
        xs_scr[...] = xs
        bc_scr[...] = xbc[:, D_SSM:]
        for j in range(D_SSM // LANES):
            sl = slice(j * LANES, (j + 1) * LANES)
            xt_scr[sl, :] = x_dt[:, sl].T
            at_scr[sl, :] = decay[:, sl].T

    g_is_1 = p >= pairs_per_group
    row0 = pl.multiple_of(p * pair_w, pair_w)
    x_t = xt_scr[pl.ds(row0, pair_w), :]
    a_t = at_scr[pl.ds(row0, pair_w), :]
    bc = bc_scr[...]
    b_all = jnp.where(g_is_1, bc[:, SSM_STATE:2 * SSM_STATE], bc[:, 0:SSM_STATE])
    c_all = jnp.where(g_is_1, bc[:, 3 * SSM_STATE:4 * SSM_STATE],
                      bc[:, 2 * SSM_STATE:3 * SSM_STATE]).astype(BF16)
    for t in range(nb):
        st = st_ref[t].reshape(pair_w, SSM_STATE)
        new = a_t[:, t:t + 1] * st + x_t[:, t:t + 1] * b_all[t:t + 1, :]
        st_out_ref[t] = new.reshape(2, SSM_HEAD_DIM, SSM_STATE)
        c_rows = jnp.broadcast_to(c_all[t:t + 1, :], (SUBLANES, SSM_STATE))
        y_scr[p, t:t + 1, :] = _dot_nt(c_rows, new.astype(BF16))[0:1, :]

    @pl.when(p == pl.num_programs(0) - 1)
    def _():
        y_mix = jnp.concatenate([y_scr[j] for j in range(SSM_HEADS // 2)], axis=-1)
        y = y_mix + dvec_ref[...] * xs_scr[...]
        y_ref[...] = _ssm_gate_norm(y, z_ref[...], nw_ref[...]).astype(BF16)


def _ssd_step(mix, dt, buf, conv_w, conv_b, a_log, d_full, norm_w, expand, state):
    nb = state.shape[0]
    n_pairs = SSM_HEADS // 2
    const = lambda shape: pl.BlockSpec(shape, lambda p: (0, 0))
    st_spec = pl.BlockSpec((nb, 2, SSM_HEAD_DIM, SSM_STATE), lambda p: (0, p, 0, 0))
    return pl.pallas_call(
        _ssd_step_kernel,
        grid=(n_pairs,),
        in_specs=[
            pl.BlockSpec((nb, D_SSM), lambda p: (0, OFF_Z // D_SSM)),
            pl.BlockSpec((nb, D_SSM), lambda p: (0, OFF_XS // D_SSM)),
            pl.BlockSpec((nb, 512), lambda p: (0, OFF_BC // 512)),
            const((nb, LANES)),
            const((nb, CONV_DIM)), const((nb, CONV_DIM)), const((nb, CONV_DIM)),
            const((SSM_CONV, CONV_DIM)), const((1, CONV_DIM)),
            const((1, LANES)),
            const((1, D_SSM)), const((1, D_SSM)),
            const((LANES, D_SSM)),
            st_spec,
        ],
        out_specs=[const((nb, D_SSM)), st_spec],
        out_shape=[
            jax.ShapeDtypeStruct((nb, D_SSM), BF16),
            jax.ShapeDtypeStruct(state.shape, F32),
        ],
        scratch_shapes=[
            pltpu.VMEM((D_SSM, nb), F32),
            pltpu.VMEM((D_SSM, nb), F32),
            pltpu.VMEM((nb, D_SSM), F32),
            pltpu.VMEM((nb, 2 * SSM_GROUPS * SSM_STATE), F32),
            pltpu.VMEM((n_pairs, nb, 2 * SSM_HEAD_DIM), F32),
        ],
        compiler_params=_cp(("arbitrary",)),
        name="ssd_step",
    )(mix, mix, mix, dt, buf[:, 0], buf[:, 1], buf[:, 2], conv_w, conv_b, a_log,
      d_full, norm_w, expand, state)


def _outproj_kernel(oa_ref, ys_ref, x_ref, wa_ref, ws_ref, nw_ref, x1_ref, h2_ref):
    x1 = (x_ref[...] + _dot(oa_ref[...].astype(BF16), wa_ref[...])
          + _dot(ys_ref[...].astype(BF16), ws_ref[...]))
    x1_ref[...] = x1
    h2_ref[...] = _rms(x1, nw_ref[...]).astype(BF16)


def _outproj(o_a, y_s, x2d, w_a, w_s, norm_w, tm):
    n = x2d.shape[0]
    row = lambda w: pl.BlockSpec((tm, w), lambda i: (i, 0))
    const = lambda shape: pl.BlockSpec(shape, lambda i: (0, 0))
    return pl.pallas_call(
        _outproj_kernel,
        grid=(n // tm,),
        in_specs=[row(D_HGRN), row(D_SSM), row(D_MODEL),
                  const((D_HGRN, D_MODEL)), const((D_SSM, D_MODEL)), const((1, D_MODEL))],
        out_specs=[row(D_MODEL), row(D_MODEL)],
        out_shape=[jax.ShapeDtypeStruct((n, D_MODEL), F32),
                   jax.ShapeDtypeStruct((n, D_MODEL), BF16)],
        compiler_params=_cp(("arbitrary",)),
        name="outproj",
    )(o_a, y_s, x2d, w_a, w_s, norm_w)


FF_BLOCK = 256


def _ffn_finish(j, contrib, x1_ref, fnw_ref, y_ref, acc_scr):
    @pl.when(j == 0)
    def _():
        acc_scr[...] = contrib

    @pl.when(j > 0)
    def _():
        acc_scr[...] = acc_scr[...] + contrib

    @pl.when(j == pl.num_programs(1) - 1)
    def _():
        y_ref[...] = _rms(x1_ref[...] + acc_scr[...], fnw_ref[...])


def _ffn_prompt_kernel(h2_ref, x1_ref, wup_ref, wd_ref, cw_ref, cb_ref, fnw_ref,
                       y_ref, tail_ref, ge_scr, *, tiles_per_seq):
    i = pl.program_id(0)
    tm = h2_ref.shape[0]
    pad = SUBLANES
    h2 = h2_ref[...]

    seq_start = lax.rem(i, tiles_per_seq) == 0

    @pl.when(seq_start)
    def _():
        ge_scr[0:pad, :] = jnp.zeros((pad, D_FF), F32)

    @pl.when(jnp.logical_not(seq_start))
    def _():
        ge_scr[0:pad, :] = ge_scr[tm:tm + pad, :]

    acc = None
    for j in range(D_FF // FF_BLOCK):
        c0, c1 = j * FF_BLOCK, (j + 1) * FF_BLOCK
        gate = _dot(h2, wup_ref[:, c0:c1])
        val = _dot(h2, wup_ref[:, D_FF + c0:D_FF + c1])
        ge_scr[pad:, c0:c1] = gate
        tail_ref[0, :, c0:c1] = gate[tm - pad:, :]
        conv = (cb_ref[:, c0:c1] + cw_ref[2:3, c0:c1] * gate
                + cw_ref[1:2, c0:c1] * ge_scr[pad - 1:pad - 1 + tm, c0:c1]
                + cw_ref[0:1, c0:c1] * ge_scr[pad - 2:pad - 2 + tm, c0:c1])
        act = (_silu(conv) * val).astype(BF16)
        part = _dot(act, wd_ref[c0:c1, :])
        acc = part if acc is None else acc + part
    y_ref[...] = _rms(x1_ref[...] + acc, fnw_ref[...])


FFN_ROW_TILE = 512


def _ffn_prompt(h2, x1, w_up, w_down, conv_w, conv_b, fnorm_w, seq):
    n = h2.shape[0]
    tm = FFN_ROW_TILE
    assert seq % tm == 0
    kern = functools.partial(_ffn_prompt_kernel, tiles_per_seq=seq // tm)
    row = lambda w: pl.BlockSpec((tm, w), lambda i: (i, 0))
    resident = lambda shape: pl.BlockSpec(shape, lambda i: (0, 0), pipeline_mode=pl.Buffered(1))
    return pl.pallas_call(
        kern,
        grid=(n // tm,),
        in_specs=[
            row(D_MODEL), row(D_MODEL),
            resident((D_MODEL, 2 * D_FF)), resident((D_FF, D_MODEL)),
            resident((FFN_CONV, D_FF)), resident((1, D_FF)), resident((1, D_MODEL)),
        ],
        out_specs=[
            row(D_MODEL),
            pl.BlockSpec((1, SUBLANES, D_FF), lambda i: (i, 0, 0)),
        ],
        out_shape=[
            jax.ShapeDtypeStruct((n, D_MODEL), F32),
            jax.ShapeDtypeStruct((n // tm, SUBLANES, D_FF), F32),
        ],
        scratch_shapes=[pltpu.VMEM((tm + SUBLANES, D_FF), F32)],
        compiler_params=_cp(("arbitrary",)),
        name="ffn_prompt",
    )(h2, x1, w_up, w_down, conv_w, conv_b, fnorm_w)


def _ffn_step_kernel(h2_ref, x1_ref, wg_ref, wv_ref, wd_ref, cw_ref, cb_ref, fnw_ref,
                     b0_ref, b1_ref, y_ref, gate_ref, acc_scr):
    j = pl.program_id(1)
    h2 = h2_ref[...]
    gate = _dot(h2, wg_ref[...])
    val = _dot(h2, wv_ref[...])
    gate_ref[...] = gate
    conv = (cb_ref[...] + cw_ref[2:3, :] * gate + cw_ref[1:2, :] * b1_ref[...]
            + cw_ref[0:1, :] * b0_ref[...])
    act = (_silu(conv) * val).astype(BF16)
    _ffn_finish(j, _dot(act, wd_ref[...]), x1_ref, fnw_ref, y_ref, acc_scr)


def _ffn_step(h2, x1, w_up, w_down, conv_w, conv_b, fnorm_w, buf):
    n = h2.shape[0]
    nj = D_FF // FF_BLOCK
    return pl.pallas_call(
        _ffn_step_kernel,
        grid=(1, nj),
        in_specs=[
            pl.BlockSpec((n, D_MODEL), lambda i, j: (0, 0)),
            pl.BlockSpec((n, D_MODEL), lambda i, j: (0, 0)),
            pl.BlockSpec((D_MODEL, FF_BLOCK), lambda i, j: (0, j)),
            pl.BlockSpec((D_MODEL, FF_BLOCK), lambda i, j: (0, nj + j)),
            pl.BlockSpec((FF_BLOCK, D_MODEL), lambda i, j: (j, 0)),
            pl.BlockSpec((FFN_CONV, FF_BLOCK), lambda i, j: (0, j)),
            pl.BlockSpec((1, FF_BLOCK), lambda i, j: (0, j)),
            pl.BlockSpec((1, D_MODEL), lambda i, j: (0, 0)),
            pl.BlockSpec((n, FF_BLOCK), lambda i, j: (0, j)),
            pl.BlockSpec((n, FF_BLOCK), lambda i, j: (0, j)),
        ],
        out_specs=[
            pl.BlockSpec((n, D_MODEL), lambda i, j: (0, 0)),
            pl.BlockSpec((n, FF_BLOCK), lambda i, j: (0, j)),
        ],
        out_shape=[
            jax.ShapeDtypeStruct((n, D_MODEL), F32),
            jax.ShapeDtypeStruct((n, D_FF), F32),
        ],
        scratch_shapes=[pltpu.VMEM((n, D_MODEL), F32)],
        compiler_params=_cp(("arbitrary", "arbitrary")),
        name="ffn_step",
    )(h2, x1, w_up, w_up, w_down, conv_w, conv_b, fnorm_w, buf[:, 0], buf[:, 1])


def _row(v):
    return v.reshape(1, -1).astype(F32)


def _pad_lanes(v):
    return jnp.pad(v.astype(F32), (0, LANES - v.shape[0])).reshape(1, LANES)


def _row_tile(n):
    for tm in (1024, 512, 256, 128):
        if n % tm == 0:
            return tm
    raise ValueError(f"token count {n} is not a multiple of 128")


def kernel(x_prompt, x_sample, state_hgrn, state_ssm, state_conv_ssm, state_conv_ffn, norm1_w, w_in, hgrn_lb, hgrn_norm_w, ssm_conv_w, ssm_conv_b, ssm_dt_bias, ssm_a_log, ssm_d, ssm_norm_w, w_out, norm2_w, w_up, ffn_conv_w, ffn_conv_b, w_down, final_norm_w):
    depth = w_in.shape[0]
    assert depth == 1, "single-layer trunk"
    l = 0
    batch, seq, _ = x_prompt.shape
    dec_batch, dec_seq, _ = x_sample.shape
    assert dec_seq == 1 and seq % CHUNK == 0 and seq >= SSM_CONV

    w_main = w_in[l][:, :D_MAIN].astype(BF16)
    w_dt = jnp.pad(w_in[l][:, D_MAIN:], ((0, 0), (0, LANES - SSM_HEADS))).astype(BF16)
    w_oa = w_out[l][:D_HGRN].astype(BF16)
    w_os = w_out[l][D_HGRN:].astype(BF16)
    w_upb = w_up[l].astype(BF16)
    w_dnb = w_down[l].astype(BF16)
    d_full = jnp.repeat(ssm_d[l].astype(F32), SSM_HEAD_DIM).reshape(1, D_SSM)
    dt_bias = _pad_lanes(ssm_dt_bias[l])
    a_log = _pad_lanes(ssm_a_log[l])
    mconst = jnp.asarray(_hgrn_const(), BF16)
    tri = jnp.asarray(np.tril(np.ones((CHUNK, CHUNK), np.float32)), BF16)
    expand = jnp.asarray(_head_expand(), BF16)
    lb_raw = hgrn_lb.astype(F32)

    def dense_tail(x2d, o_a, y_s, tm):
        return _outproj(o_a, y_s, x2d, w_oa, w_os, _row(norm2_w[l]), tm)

    xp = x_prompt.reshape(batch * seq, D_MODEL)
    tm_p = _row_tile(seq)
    proj_p, lg_p, dt_p = _inproj(xp, _row(norm1_w[l]), w_main, w_dt, lb_raw, dt_bias)
    oa_p, hgrn_p = _hgrn_prompt(proj_p, lg_p, _row(hgrn_norm_w[l]), mconst, batch, seq)
    ys_p, ssm_p = _ssd_prompt(proj_p, dt_p, ssm_conv_w[l], _row(ssm_conv_b[l]), a_log,
                              d_full, _row(ssm_norm_w[l]), tri, expand, batch, seq)
    x1_p, h2_p = dense_tail(xp, oa_p, ys_p, tm_p)
    y_p, tail_p = _ffn_prompt(h2_p, x1_p, w_upb, w_dnb, ffn_conv_w[l], _row(ffn_conv_b[l]),
                              _row(final_norm_w), seq)
    proj_p3 = proj_p.reshape(batch, seq, D_MAIN)
    cs_p = proj_p3[:, seq - (SSM_CONV - 1):, OFF_XS:OFF_XS + CONV_DIM]
    tails = tail_p.reshape(batch, seq // FFN_ROW_TILE, SUBLANES, D_FF)
    cf_p = tails[:, -1, SUBLANES - (FFN_CONV - 1):, :]

    xs_ = x_sample.reshape(dec_batch, D_MODEL)
    proj_s, lg_s, dt_s = _inproj(xs_, _row(norm1_w[l]), w_main, w_dt, lb_raw, dt_bias)
    oa_s, hgrn_s = _hgrn_step(proj_s, lg_s, _row(hgrn_norm_w[l]), state_hgrn[l])
    ys_s, ssm_s = _ssd_step(proj_s, dt_s, state_conv_ssm[l], ssm_conv_w[l], _row(ssm_conv_b[l]),
                            a_log, d_full, _row(ssm_norm_w[l]), expand, state_ssm[l])
    x1_s, h2_s = dense_tail(xs_, oa_s, ys_s, dec_batch)
    y_s, gate_s = _ffn_step(h2_s, x1_s, w_upb, w_dnb, ffn_conv_w[l], _row(ffn_conv_b[l]),
                            _row(final_norm_w), state_conv_ffn[l])
    cs_s = jnp.concatenate([state_conv_ssm[l][:, 1:], proj_s[:, None, OFF_XS:OFF_XS + CONV_DIM]],
                           axis=1)
    cf_s = jnp.concatenate([state_conv_ffn[l][:, 1:], gate_s[:, None, :]], axis=1)

    dt_ = x_prompt.dtype
    return (y_p.reshape(batch, seq, D_MODEL).astype(dt_),
            y_s.reshape(dec_batch, 1, D_MODEL).astype(dt_),
            hgrn_p[None].astype(dt_),
            hgrn_s[None].astype(dt_),
            ssm_p.reshape(1, batch, SSM_HEADS, SSM_HEAD_DIM, SSM_STATE).astype(dt_),
            ssm_s[None].astype(dt_),
            cs_p[None].astype(dt_),
            cs_s[None].astype(dt_),
            cf_p[None].astype(dt_),
            cf_s[None].astype(dt_))
```

```python
import functools

import numpy as np
import jax
import jax.numpy as jnp
from jax import lax
from jax.experimental import pallas as pl
from jax.experimental.pallas import tpu as pltpu

F32 = jnp.float32
BF16 = jnp.bfloat16
EPS = 1e-6

LANES = 128
SUBLANES = 8

D_MODEL = 1024
HGRN_HEADS = 8
HGRN_DK = 128
HGRN_DV = 128
D_HGRN = HGRN_HEADS * HGRN_DV
SSM_HEADS = 16
SSM_HEAD_DIM = 64
D_SSM = SSM_HEADS * SSM_HEAD_DIM
SSM_STATE = 128
SSM_GROUPS = 2
SSM_CONV = 4
CONV_DIM = D_SSM + 2 * SSM_GROUPS * SSM_STATE
D_FF = 2816
FFN_CONV = 3
D_MAIN = 4 * D_HGRN + D_SSM + CONV_DIM
OFF_Q, OFF_F, OFF_I, OFF_G = 0, 1024, 2048, 3072
OFF_Z, OFF_XS, OFF_BC = 4096, 5120, 6144

CHUNK = 128
GROUP_W = D_SSM // SSM_GROUPS
VMEM_LIMIT = 56 * 1024 * 1024


def _cp(sem):
    return pltpu.CompilerParams(dimension_semantics=sem, vmem_limit_bytes=VMEM_LIMIT)


def _dot(a, b):
    return jnp.dot(a, b, preferred_element_type=F32)


def _dot_nt(a, b):
    return lax.dot_general(a, b, (((1,), (1,)), ((), ())), preferred_element_type=F32)


def _split3(x):
    h = x.astype(BF16)
    r = x - h.astype(F32)
    m = r.astype(BF16)
    lo = (r - m.astype(F32)).astype(BF16)
    return h, m, lo


def _dot_exact_lhs01(m01, x):
    h, m, lo = _split3(x)
    return _dot(m01, h) + _dot(m01, m) + _dot(m01, lo)


def _dot_exact_rhs01(x, m01):
    h, m, lo = _split3(x)
    return _dot(h, m01) + _dot(m, m01) + _dot(lo, m01)


def _dot_split_lhs01(m01, x):
    h = x.astype(BF16)
    lo = (x - h.astype(F32)).astype(BF16)
    return _dot(m01, h) + _dot(m01, lo)


def _sigmoid(x):
    return 1.0 / (1.0 + jnp.exp(-x))


def _silu(x):
    return x * _sigmoid(x)


def _rms(x, w):
    ms = jnp.mean(x * x, axis=-1, keepdims=True)
    return x * lax.rsqrt(ms + EPS) * w


def _inproj_kernel(x_ref, nw_ref, w_ref, wdt_ref, lb_ref, dtb_ref, mix_ref, lg_ref, dt_ref):
    hb = _rms(x_ref[...], nw_ref[...]).astype(BF16)

    def sec(off, width=D_HGRN):
        return _dot(hb, w_ref[:, off:off + width])

    def put(off, val):
        mix_ref[:, off:off + val.shape[1]] = val.astype(BF16)

    lb = _hgrn_lb(lb_ref[...])
    f = lb + (1.0 - lb) * _sigmoid(sec(OFF_F))
    lg_ref[...] = jnp.log(f)
    put(OFF_F, 1.0 - f)
    put(OFF_Q, _silu(sec(OFF_Q)))
    put(OFF_I, sec(OFF_I))
    put(OFF_G, _silu(sec(OFF_G)))
    put(OFF_Z, _silu(sec(OFF_Z, D_SSM)))
    put(OFF_XS, sec(OFF_XS, CONV_DIM))
    dt_ref[...] = _softplus(_dot(hb, wdt_ref[...]) + dtb_ref[...])


INPROJ_ROW_TILE = 512


def _inproj(x2d, norm_w, w_main, w_dt, lb_raw, dt_bias):
    n = x2d.shape[0]
    tm = min(INPROJ_ROW_TILE, n)
    assert n % tm == 0
    row = lambda w: pl.BlockSpec((tm, w), lambda i: (i, 0))
    resident = lambda shape: pl.BlockSpec(shape, lambda i: (0, 0), pipeline_mode=pl.Buffered(1))
    return pl.pallas_call(
        _inproj_kernel,
        grid=(n // tm,),
        in_specs=[
            row(D_MODEL), resident((1, D_MODEL)),
            resident((D_MODEL, D_MAIN)), resident((D_MODEL, LANES)),
            resident(lb_raw.shape), resident((1, LANES)),
        ],
        out_specs=[row(D_MAIN), row(D_HGRN), row(LANES)],
        out_shape=[
            jax.ShapeDtypeStruct((n, D_MAIN), BF16),
            jax.ShapeDtypeStruct((n, D_HGRN), F32),
            jax.ShapeDtypeStruct((n, LANES), F32),
        ],
        compiler_params=_cp(("arbitrary",)),
        name="inproj",
    )(x2d, norm_w, w_main, w_dt, lb_raw, dt_bias)


LOG2E = 1.4426950408889634
N_LEVELS = 7
MXU_LEVEL_HALVES = (4, 2)


def _hgrn_const():
    c = CHUNK
    t = np.arange(c)[:, None]
    j = np.arange(c)[None, :]
    blocks = [(j <= t)]
    for h in MXU_LEVEL_HALVES:
        mid = (t // (2 * h)) * (2 * h) + h
        upper = (t >= mid) & (j >= mid) & (j <= t)
        lower = (t < mid) & (j > t) & (j < mid)
        blocks.append(upper | lower)
    return np.concatenate(blocks, axis=0).astype(np.float32)


def _midpoint_decay(b, h):
    pieces = []
    for start in range(0, CHUNK, 2 * h):
        mid = start + h
        m = b[mid - 1:mid, :]
        pieces.append(m - b[start:mid])
        pieces.append(b[mid:mid + h] - m)
    return jnp.concatenate(pieces, axis=0)


def _mix_rows(q, k, h):
    pieces = []
    for start in range(0, CHUNK, 2 * h):
        pieces.append(k[start:start + h])
        pieces.append(q[start + h:start + 2 * h])
    return jnp.concatenate(pieces, axis=0)


def _hgrn_lb(lb_raw):
    mx = jnp.max(lb_raw, axis=0, keepdims=True)
    e = jnp.exp(lb_raw - mx)
    return e[0:1, :] / jnp.sum(e, axis=0, keepdims=True)


def _level_map():
    t = lax.broadcasted_iota(jnp.int32, (CHUNK, CHUNK), 0)
    s = lax.broadcasted_iota(jnp.int32, (CHUNK, CHUNK), 1)
    bitlen = 32 - lax.clz(t ^ s)
    return jnp.where(t > s, bitlen, jnp.where(t == s, 0, -1))


def _hgrn_prompt_kernel(q_ref, k_ref, i_ref, g_ref, lg_ref, nw_ref, mc_ref,
                        o_ref, s_out_ref, st_scr):
    c = pl.program_id(1)

    @pl.when(c == 0)
    def _():
        st_scr[...] = jnp.zeros_like(st_scr)

    e_heads = _dot_split_lhs01(mc_ref[...], lg_ref[...] * LOG2E)
    lev = _level_map()
    row = lax.broadcasted_iota(jnp.int32, (CHUNK, HGRN_DK), 0)
    heads = range(HGRN_HEADS)
    sls = [slice(h * HGRN_DK, (h + 1) * HGRN_DK) for h in heads]

    qb = [q_ref[:, sl] for sl in sls]
    kb = [k_ref[:, sl] for sl in sls]
    vb = [i_ref[:, sl] for sl in sls]
    q = [x.astype(F32) for x in qb]
    k = [x.astype(F32) for x in kb]
    b = [e_heads[0:CHUNK, sl] for sl in sls]
    b_last = [x[CHUNK - 1:CHUNK, :] for x in b]
    st = [st_scr[h] for h in heads]
    o = [_dot_nt((q[h] * jnp.exp2(b[h])).astype(BF16), st[h].astype(BF16)) for h in heads]

    a = [jnp.where(lev == 0, _dot_nt(qb[h], kb[h]), 0.0) for h in heads]
    half = CHUNK // 2
    while half >= 1:
        for h in heads:
            if half >= SUBLANES:
                x = _mix_rows(q[h], k[h], half) * jnp.exp2(_midpoint_decay(b[h], half))
            else:
                upper = (row & half) != 0
                if half in MXU_LEVEL_HALVES:
                    blk = 1 + MXU_LEVEL_HALVES.index(half)
                    w = jnp.exp2(e_heads[blk * CHUNK:(blk + 1) * CHUNK, sls[h]])
                    x = jnp.where(upper, q[h], k[h]) * w
                else:
                    x = jnp.where(upper, q[h] * (1.0 - k[h]), k[h])
            xb = x.astype(BF16)
            a[h] = jnp.where(lev == half.bit_length(), _dot_nt(xb, xb), a[h])
        half //= 2

    for h in heads:
        o[h] = o[h] + _dot(a[h].astype(BF16), vb[h])
    for h in heads:
        ks = (k[h] * jnp.exp2(b_last[h] - b[h])).astype(BF16)
        st_scr[h] = (st[h] * jnp.exp2(b_last[h])
                     + _dot(vb[h].astype(F32).T.astype(BF16), ks))
    for h in heads:
        o_ref[:, sls[h]] = (_rms(o[h], nw_ref[...]) * g_ref[:, sls[h]].astype(F32)).astype(BF16)

    @pl.when(c == pl.num_programs(1) - 1)
    def _():
        for h in range(HGRN_HEADS):
            s_out_ref[0, h] = st_scr[h].T


def _hgrn_prompt(mix, lg, norm_w, mconst, batch, seq):
    nc = seq // CHUNK

    def col(off):
        return pl.BlockSpec((CHUNK, D_HGRN), lambda b, c: (b * nc + c, off // D_HGRN))

    return pl.pallas_call(
        _hgrn_prompt_kernel,
        grid=(batch, nc),
        in_specs=[
            col(OFF_Q), col(OFF_F), col(OFF_I), col(OFF_G), col(0),
            pl.BlockSpec((1, HGRN_DV), lambda b, c: (0, 0)),
            pl.BlockSpec(mconst.shape, lambda b, c: (0, 0)),
        ],
        out_specs=[
            pl.BlockSpec((CHUNK, D_HGRN), lambda b, c: (b * nc + c, 0)),
            pl.BlockSpec((1, HGRN_HEADS, HGRN_DK, HGRN_DV), lambda b, c: (b, 0, 0, 0)),
        ],
        out_shape=[
            jax.ShapeDtypeStruct((batch * seq, D_HGRN), BF16),
            jax.ShapeDtypeStruct((batch, HGRN_HEADS, HGRN_DK, HGRN_DV), F32),
        ],
        scratch_shapes=[pltpu.VMEM((HGRN_HEADS, HGRN_DV, HGRN_DK), F32)],
        compiler_params=_cp(("arbitrary", "arbitrary")),
        name="hgrn_prompt",
    )(mix, mix, mix, mix, lg, norm_w, mconst)


def _hgrn_step_kernel(q_ref, i_ref, g_ref, lg_ref, nw_ref, s_ref,
                      o_ref, s_out_ref, o_scr):
    nb = q_ref.shape[0]
    q = q_ref[...].astype(F32)
    f = jnp.exp(lg_ref[...])
    k = 1.0 - f
    v = i_ref[...].astype(F32)
    q_t, f_t, k_t = q.T, f.T, k.T
    for t in range(nb):
        s_new = f_t[:, t:t + 1] * s_ref[t, 0] + k_t[:, t:t + 1] * v[t:t + 1, :]
        s_out_ref[t, 0] = s_new
        o_scr[t:t + 1, :] = jnp.sum(q_t[:, t:t + 1] * s_new, axis=0, keepdims=True)
    o_ref[...] = (_rms(o_scr[...], nw_ref[...]) * g_ref[...].astype(F32)).astype(BF16)


def _hgrn_step(mix, lg, norm_w, state):
    nb = state.shape[0]
    hb = lambda off: off // HGRN_DK

    def col(off):
        return pl.BlockSpec((nb, HGRN_DK), lambda h: (0, hb(off) + h))

    st_spec = pl.BlockSpec((nb, 1, HGRN_DK, HGRN_DV), lambda h: (0, h, 0, 0))
    return pl.pallas_call(
        _hgrn_step_kernel,
        grid=(HGRN_HEADS,),
        in_specs=[
            col(OFF_Q), col(OFF_I), col(OFF_G), col(0),
            pl.BlockSpec((1, HGRN_DV), lambda h: (0, 0)),
            st_spec,
        ],
        out_specs=[pl.BlockSpec((nb, HGRN_DV), lambda h: (0, h)), st_spec],
        out_shape=[
            jax.ShapeDtypeStruct((nb, D_HGRN), BF16),
            jax.ShapeDtypeStruct(state.shape, F32),
        ],
        scratch_shapes=[pltpu.VMEM((nb, HGRN_DV), F32)],
        compiler_params=_cp(("arbitrary",)),
        name="hgrn_step",
    )(mix, mix, mix, lg, norm_w, state)


def _head_expand():
    e = np.zeros((LANES, D_SSM), np.float32)
    for h in range(SSM_HEADS):
        e[h, h * SSM_HEAD_DIM:(h + 1) * SSM_HEAD_DIM] = 1.0
    return e


def _softplus(x):
    return jnp.maximum(x, 0.0) + jnp.log(1.0 + jnp.exp(-jnp.abs(x)))


def _ssm_gate_norm(y, z_gate, nw):
    y = y * z_gate.astype(F32)
    parts = [_rms(y[:, g * GROUP_W:(g + 1) * GROUP_W], nw[:, g * GROUP_W:(g + 1) * GROUP_W])
             for g in range(SSM_GROUPS)]
    return jnp.concatenate(parts, axis=-1)


def _ssd_prompt_kernel(z_ref, xs_ref, bc_ref, dt_ref, cw_ref, cb_ref, alog_ref,
                       dvec_ref, nw_ref, tri_ref, exp_ref,
                       shift_ref, y_ref, st_out_ref, xprev_scr, st_scr):
    c = pl.program_id(1)
    t = CHUNK

    @pl.when(c == 0)
    def _():
        st_scr[...] = jnp.zeros_like(st_scr)
        xprev_scr[...] = jnp.zeros_like(xprev_scr)

    x_cur = jnp.concatenate([xs_ref[...], bc_ref[...]], axis=-1)
    taps = _dot(shift_ref[...], jnp.concatenate([xprev_scr[...], x_cur], axis=0))
    xprev_scr[...] = x_cur
    acc = cb_ref[...] + cw_ref[SSM_CONV - 1:SSM_CONV, :] * x_cur.astype(F32)
    for d in range(1, SSM_CONV):
        acc = acc + cw_ref[SSM_CONV - 1 - d:SSM_CONV - d, :] * taps[(d - 1) * t:d * t, :]
    xbc = _silu(acc)
    xs = xbc[:, 0:D_SSM]

    dt = dt_ref[...]
    da = dt * (-LOG2E * jnp.exp(alog_ref[...]))
    cs = _dot_exact_lhs01(tri_ref[...], da)
    ex = exp_ref[...]
    dt_full = _dot_exact_rhs01(dt, ex)
    cs_full = _dot_exact_rhs01(cs, ex)
    cs_last_full = cs_full[t - 1:t, :]
    x_dt = xs * dt_full
    x_end = (x_dt * jnp.exp2(cs_last_full - cs_full)).astype(BF16)
    cs_t = cs.T

    causal = (lax.broadcasted_iota(jnp.int32, (t, t), 0)
              >= lax.broadcasted_iota(jnp.int32, (t, t), 1))
    lane = lax.broadcasted_iota(jnp.int32, (1, D_SSM), 1)
    odd_head = (lane & SSM_HEAD_DIM) != 0
    x_b = x_dt.astype(BF16)
    zero = jnp.zeros_like(x_b)
    x_by_parity = (jnp.where(odd_head, zero, x_b), jnp.where(odd_head, x_b, zero))
    heads_per_group = SSM_HEADS // SSM_GROUPS
    pair_w = 2 * SSM_HEAD_DIM
    never = -1e30

    y_parts = []
    for g in range(SSM_GROUPS):
        b_g = xbc[:, D_SSM + g * SSM_STATE:D_SSM + (g + 1) * SSM_STATE]
        c_off = D_SSM + SSM_GROUPS * SSM_STATE + g * SSM_STATE
        c_g = xbc[:, c_off:c_off + SSM_STATE].astype(BF16)
        gmat = _dot_nt(c_g, b_g.astype(BF16))
        for pp in range(heads_per_group // 2):
            h0 = g * heads_per_group + 2 * pp
            psl = slice(h0 * SSM_HEAD_DIM, h0 * SSM_HEAD_DIM + pair_w)
            yp = None
            for sub in range(2):
                h = h0 + sub
                diff = cs[:, h:h + 1] - cs_t[h:h + 1, :]
                w = jnp.exp2(jnp.where(causal, diff, never)) * gmat
                part = _dot(w.astype(BF16), x_by_parity[sub][:, psl])
                yp = part if yp is None else yp + part
            y_parts.append(yp)

    y_diag = jnp.concatenate(y_parts, axis=-1)
    y_offs = []
    for g in range(SSM_GROUPS):
        sl = slice(g * GROUP_W, (g + 1) * GROUP_W)
        b_g = xbc[:, D_SSM + g * SSM_STATE:D_SSM + (g + 1) * SSM_STATE]
        c_off = D_SSM + SSM_GROUPS * SSM_STATE + g * SSM_STATE
        c_g = xbc[:, c_off:c_off + SSM_STATE].astype(BF16)
        st_g = st_scr[:, sl]
        y_offs.append(_dot(c_g, st_g.astype(BF16)))
        st_scr[:, sl] = (st_g * jnp.exp2(cs_last_full[:, sl])
                         + _dot(b_g.T.astype(BF16), x_end[:, sl]))
    y_off = jnp.concatenate(y_offs, axis=-1) * jnp.exp2(cs_full)

    y = y_diag + y_off + dvec_ref[...] * xs
    y_ref[...] = _ssm_gate_norm(y, z_ref[...], nw_ref[...]).astype(BF16)

    @pl.when(c == pl.num_programs(1) - 1)
    def _():
        for j in range(D_SSM // LANES):
            st_out_ref[0, j * LANES:(j + 1) * LANES, :] = st_scr[:, j * LANES:(j + 1) * LANES].T


def _ssd_prompt(mix, dt, conv_w, conv_b, a_log, d_full, norm_w, tri, expand, batch, seq):
    nc = seq // CHUNK
    const = lambda shape: pl.BlockSpec(shape, lambda b, c: (0, 0))
    return pl.pallas_call(
        _ssd_prompt_kernel,
        grid=(batch, nc),
        in_specs=[
            pl.BlockSpec((CHUNK, D_SSM), lambda b, c: (b * nc + c, OFF_Z // D_SSM)),
            pl.BlockSpec((CHUNK, D_SSM), lambda b, c: (b * nc + c, OFF_XS // D_SSM)),
            pl.BlockSpec((CHUNK, 512), lambda b, c: (b * nc + c, OFF_BC // 512)),
            pl.BlockSpec((CHUNK, LANES), lambda b, c: (b * nc + c, 0)),
            const((SSM_CONV, CONV_DIM)), const((1, CONV_DIM)),
            const((1, LANES)),
            const((1, D_SSM)), const((1, D_SSM)),
            const((CHUNK, CHUNK)), const((LANES, D_SSM)),
            const(((SSM_CONV - 1) * CHUNK, 2 * CHUNK)),
        ],
        out_specs=[
            pl.BlockSpec((CHUNK, D_SSM), lambda b, c: (b * nc + c, 0)),
            pl.BlockSpec((1, D_SSM, SSM_STATE), lambda b, c: (b, 0, 0)),
        ],
        out_shape=[
            jax.ShapeDtypeStruct((batch * seq, D_SSM), BF16),
            jax.ShapeDtypeStruct((batch, D_SSM, SSM_STATE), F32),
        ],
        scratch_shapes=[
            pltpu.VMEM((CHUNK, CONV_DIM), BF16),
            pltpu.VMEM((SSM_STATE, D_SSM), F32),
        ],
        compiler_params=_cp(("arbitrary", "arbitrary")),
        name="ssd_prompt",
    )(mix, mix, mix, dt, conv_w, conv_b, a_log, d_full, norm_w, tri, expand,
      jnp.asarray(_conv_shifts(), BF16))


def _conv_shifts():
    m = np.zeros(((SSM_CONV - 1) * CHUNK, 2 * CHUNK), np.float32)
    for d in range(1, SSM_CONV):
        for t in range(CHUNK):
            m[(d - 1) * CHUNK + t, CHUNK + t - d] = 1.0
    return m


def _ssd_step_kernel(z_ref, xs_ref, bc_ref, dt_ref, b0_ref, b1_ref, b2_ref, cw_ref, cb_ref,
                     alog_ref, dvec_ref, nw_ref, exp_ref, st_ref,
                     y_ref, st_out_ref, xt_scr, at_scr, xs_scr, bc_scr, y_scr):
    p = pl.program_id(0)
    nb = z_ref.shape[0]
    pair_w = 2 * SSM_HEAD_DIM
    pairs_per_group = SSM_HEADS // SSM_GROUPS // 2

    @pl.when(p == 0)
    def _():
        x_new = jnp.concatenate([xs_ref[...], bc_ref[...]], axis=-1).astype(F32)
        acc = (cb_ref[...] + cw_ref[0:1, :] * b0_ref[...] + cw_ref[1:2, :] * b1_ref[...]
               + cw_ref[2:3, :] * b2_ref[...] + cw_ref[3:4, :] * x_new)
        xbc = _silu(acc)
        xs = xbc[:, 0:D_SSM]
        dt = dt_ref[...]
        da = dt * (-jnp.exp(alog_ref[...]))
        ex = exp_ref[...]
        x_dt = xs * _dot_exact_rhs01(dt, ex)
        decay = jnp.exp(_dot_exact_rhs01(da, ex))
        xs_scr[...] = xs
        bc_scr[...] = xbc[:, D_SSM:]
        for j in range(D_SSM // LANES):
            sl = slice(j * LANES, (j + 1) * LANES)
            xt_scr[sl, :] = x_dt[:, sl].T
            at_scr[sl, :] = decay[:, sl].T

    g_is_1 = p >= pairs_per_group
    row0 = pl.multiple_of(p * pair_w, pair_w)
    x_t = xt_scr[pl.ds(row0, pair_w), :]
    a_t = at_scr[pl.ds(row0, pair_w), :]
    bc = bc_scr[...]
    b_all = jnp.where(g_is_1, bc[:, SSM_STATE:2 * SSM_STATE], bc[:, 0:SSM_STATE])
    c_all = jnp.where(g_is_1, bc[:, 3 * SSM_STATE:4 * SSM_STATE],
                      bc[:, 2 * SSM_STATE:3 * SSM_STATE]).astype(BF16)
    for t in range(nb):
        st = st_ref[t].reshape(pair_w, SSM_STATE)
        new = a_t[:, t:t + 1] * st + x_t[:, t:t + 1] * b_all[t:t + 1, :]
        st_out_ref[t] = new.reshape(2, SSM_HEAD_DIM, SSM_STATE)
        c_rows = jnp.broadcast_to(c_all[t:t + 1, :], (SUBLANES, SSM_STATE))
        y_scr[p, t:t + 1, :] = _dot_nt(c_rows, new.astype(BF16))[0:1, :]

    @pl.when(p == pl.num_programs(0) - 1)
    def _():
        y_mix = jnp.concatenate([y_scr[j] for j in range(SSM_HEADS // 2)], axis=-1)
        y = y_mix + dvec_ref[...] * xs_scr[...]
        y_ref[...] = _ssm_gate_norm(y, z_ref[...], nw_ref[...]).astype(BF16)


def _ssd_step(mix, dt, buf, conv_w, conv_b, a_log, d_full, norm_w, expand, state):
    nb = state.shape[0]
    n_pairs = SSM_HEADS // 2
    const = lambda shape: pl.BlockSpec(shape, lambda p: (0, 0))
    st_spec = pl.BlockSpec((nb, 2, SSM_HEAD_DIM, SSM_STATE), lambda p: (0, p, 0, 0))
    return pl.pallas_call(
        _ssd_step_kernel,
        grid=(n_pairs,),
        in_specs=[
            pl.BlockSpec((nb, D_SSM), lambda p: (0, OFF_Z // D_SSM)),
            pl.BlockSpec((nb, D_SSM), lambda p: (0, OFF_XS // D_SSM)),
            pl.BlockSpec((nb, 512), lambda p: (0, OFF_BC // 512)),
            const((nb, LANES)),
            const((nb, CONV_DIM)), const((nb, CONV_DIM)), const((nb, CONV_DIM)),
            const((SSM_CONV, CONV_DIM)), const((1, CONV_DIM)),
            const((1, LANES)),
            const((1, D_SSM)), const((1, D_SSM)),
            const((LANES, D_SSM)),
            st_spec,
        ],
        out_specs=[const((nb, D_SSM)), st_spec],
        out_shape=[
            jax.ShapeDtypeStruct((nb, D_SSM), BF16),
            jax.ShapeDtypeStruct(state.shape, F32),
        ],
        scratch_shapes=[
            pltpu.VMEM((D_SSM, nb), F32),
            pltpu.VMEM((D_SSM, nb), F32),
            pltpu.VMEM((nb, D_SSM), F32),
            pltpu.VMEM((nb, 2 * SSM_GROUPS * SSM_STATE), F32),
            pltpu.VMEM((n_pairs, nb, 2 * SSM_HEAD_DIM), F32),
        ],
        compiler_params=_cp(("arbitrary",)),
        name="ssd_step",
    )(mix, mix, mix, dt, buf[:, 0], buf[:, 1], buf[:, 2], conv_w, conv_b, a_log,
      d_full, norm_w, expand, state)


def _outproj_kernel(oa_ref, ys_ref, x_ref, wa_ref, ws_ref, nw_ref, x1_ref, h2_ref):
    x1 = (x_ref[...] + _dot(oa_ref[...].astype(BF16), wa_ref[...])
          + _dot(ys_ref[...].astype(BF16), ws_ref[...]))
    x1_ref[...] = x1
    h2_ref[...] = _rms(x1, nw_ref[...]).astype(BF16)


def _outproj(o_a, y_s, x2d, w_a, w_s, norm_w, tm):
    n = x2d.shape[0]
    row = lambda w: pl.BlockSpec((tm, w), lambda i: (i, 0))
    const = lambda shape: pl.BlockSpec(shape, lambda i: (0, 0))
    return pl.pallas_call(
        _outproj_kernel,
        grid=(n // tm,),
        in_specs=[row(D_HGRN), row(D_SSM), row(D_MODEL),
                  const((D_HGRN, D_MODEL)), const((D_SSM, D_MODEL)), const((1, D_MODEL))],
        out_specs=[row(D_MODEL), row(D_MODEL)],
        out_shape=[jax.ShapeDtypeStruct((n, D_MODEL), F32),
                   jax.ShapeDtypeStruct((n, D_MODEL), BF16)],
        compiler_params=_cp(("arbitrary",)),
        name="outproj",
    )(o_a, y_s, x2d, w_a, w_s, norm_w)


FF_BLOCK = 256


def _ffn_finish(j, contrib, x1_ref, fnw_ref, y_ref, acc_scr):
    @pl.when(j == 0)
    def _():
        acc_scr[...] = contrib

    @pl.when(j > 0)
    def _():
        acc_scr[...] = acc_scr[...] + contrib

    @pl.when(j == pl.num_programs(1) - 1)
    def _():
        y_ref[...] = _rms(x1_ref[...] + acc_scr[...], fnw_ref[...])


def _ffn_prompt_kernel(h2_ref, x1_ref, wup_ref, wd_ref, cw_ref, cb_ref, fnw_ref,
                       y_ref, tail_ref, ge_scr, *, tiles_per_seq):
    i = pl.program_id(0)
    tm = h2_ref.shape[0]
    pad = SUBLANES
    h2 = h2_ref[...]

    seq_start = lax.rem(i, tiles_per_seq) == 0

    @pl.when(seq_start)
    def _():
        ge_scr[0:pad, :] = jnp.zeros((pad, D_FF), F32)

    @pl.when(jnp.logical_not(seq_start))
    def _():
        ge_scr[0:pad, :] = ge_scr[tm:tm + pad, :]

    acc = None
    for j in range(D_FF // FF_BLOCK):
        c0, c1 = j * FF_BLOCK, (j + 1) * FF_BLOCK
        gate = _dot(h2, wup_ref[:, c0:c1])
        val = _dot(h2, wup_ref[:, D_FF + c0:D_FF + c1])
        ge_scr[pad:, c0:c1] = gate
        tail_ref[0, :, c0:c1] = gate[tm - pad:, :]
        conv = (cb_ref[:, c0:c1] + cw_ref[2:3, c0:c1] * gate
                + cw_ref[1:2, c0:c1] * ge_scr[pad - 1:pad - 1 + tm, c0:c1]
                + cw_ref[0:1, c0:c1] * ge_scr[pad - 2:pad - 2 + tm, c0:c1])
        act = (_silu(conv) * val).astype(BF16)
        part = _dot(act, wd_ref[c0:c1, :])
        acc = part if acc is None else acc + part
    y_ref[...] = _rms(x1_ref[...] + acc, fnw_ref[...])


FFN_ROW_TILE = 512


def _ffn_prompt(h2, x1, w_up, w_down, conv_w, conv_b, fnorm_w, seq):
    n = h2.shape[0]
    tm = FFN_ROW_TILE
    assert seq % tm == 0
    kern = functools.partial(_ffn_prompt_kernel, tiles_per_seq=seq // tm)
    row = lambda w: pl.BlockSpec((tm, w), lambda i: (i, 0))
    resident = lambda shape: pl.BlockSpec(shape, lambda i: (0, 0), pipeline_mode=pl.Buffered(1))
    return pl.pallas_call(
        kern,
        grid=(n // tm,),
        in_specs=[
            row(D_MODEL), row(D_MODEL),
            resident((D_MODEL, 2 * D_FF)), resident((D_FF, D_MODEL)),
            resident((FFN_CONV, D_FF)), resident((1, D_FF)), resident((1, D_MODEL)),
        ],
        out_specs=[
            row(D_MODEL),
            pl.BlockSpec((1, SUBLANES, D_FF), lambda i: (i, 0, 0)),
        ],
        out_shape=[
            jax.ShapeDtypeStruct((n, D_MODEL), F32),
            jax.ShapeDtypeStruct((n // tm, SUBLANES, D_FF), F32),
        ],
        scratch_shapes=[pltpu.VMEM((tm + SUBLANES, D_FF), F32)],
        compiler_params=_cp(("arbitrary",)),
        name="ffn_prompt",
    )(h2, x1, w_up, w_down, conv_w, conv_b, fnorm_w)


def _ffn_step_kernel(h2_ref, x1_ref, wg_ref, wv_ref, wd_ref, cw_ref, cb_ref, fnw_ref,
                     b0_ref, b1_ref, y_ref, gate_ref, acc_scr):
    j = pl.program_id(1)
    h2 = h2_ref[...]
    gate = _dot(h2, wg_ref[...])
    val = _dot(h2, wv_ref[...])
    gate_ref[...] = gate
    conv = (cb_ref[...] + cw_ref[2:3, :] * gate + cw_ref[1:2, :] * b1_ref[...]
            + cw_ref[0:1, :] * b0_ref[...])
    act = (_silu(conv) * val).astype(BF16)
    _ffn_finish(j, _dot(act, wd_ref[...]), x1_ref, fnw_ref, y_ref, acc_scr)


def _ffn_step(h2, x1, w_up, w_down, conv_w, conv_b, fnorm_w, buf):
    n = h2.shape[0]
    nj = D_FF // FF_BLOCK
    return pl.pallas_call(
        _ffn_step_kernel,
        grid=(1, nj),
        in_specs=[
            pl.BlockSpec((n, D_MODEL), lambda i, j: (0, 0)),
            pl.BlockSpec((n, D_MODEL), lambda i, j: (0, 0)),
            pl.BlockSpec((D_MODEL, FF_BLOCK), lambda i, j: (0, j)),
            pl.BlockSpec((D_MODEL, FF_BLOCK), lambda i, j: (0, nj + j)),
            pl.BlockSpec((FF_BLOCK, D_MODEL), lambda i, j: (j, 0)),
            pl.BlockSpec((FFN_CONV, FF_BLOCK), lambda i, j: (0, j)),
            pl.BlockSpec((1, FF_BLOCK), lambda i, j: (0, j)),
            pl.BlockSpec((1, D_MODEL), lambda i, j: (0, 0)),
            pl.BlockSpec((n, FF_BLOCK), lambda i, j: (0, j)),
            pl.BlockSpec((n, FF_BLOCK), lambda i, j: (0, j)),
        ],
        out_specs=[
            pl.BlockSpec((n, D_MODEL), lambda i, j: (0, 0)),
            pl.BlockSpec((n, FF_BLOCK), lambda i, j: (0, j)),
        ],
        out_shape=[
            jax.ShapeDtypeStruct((n, D_MODEL), F32),
            jax.ShapeDtypeStruct((n, D_FF), F32),
        ],
        scratch_shapes=[pltpu.VMEM((n, D_MODEL), F32)],
        compiler_params=_cp(("arbitrary", "arbitrary")),
        name="ffn_step",
    )(h2, x1, w_up, w_up, w_down, conv_w, conv_b, fnorm_w, buf[:, 0], buf[:, 1])


def _row(v):
    return v.reshape(1, -1).astype(F32)


def _pad_lanes(v):
    return jnp.pad(v.astype(F32), (0, LANES - v.shape[0])).reshape(1, LANES)


def _row_tile(n):
    for tm in (1024, 512, 256, 128):
        if n % tm == 0:
            return tm
    raise ValueError(f"token count {n} is not a multiple of 128")


def kernel(x_prompt, x_sample, state_hgrn, state_ssm, state_conv_ssm, state_conv_ffn, norm1_w, w_in, hgrn_lb, hgrn_norm_w, ssm_conv_w, ssm_conv_b, ssm_dt_bias, ssm_a_log, ssm_d, ssm_norm_w, w_out, norm2_w, w_up, ffn_conv_w, ffn_conv_b, w_down, final_norm_w):
    depth = w_in.shape[0]
    assert depth == 1, "single-layer trunk"
    l = 0
    batch, seq, _ = x_prompt.shape
    dec_batch, dec_seq, _ = x_sample.shape
    assert dec_seq == 1 and seq % CHUNK == 0 and seq >= SSM_CONV

    w_main = w_in[l].astype(BF16)
    w_dt = jnp.pad(w_in[l][:, D_MAIN:], ((0, 0), (0, LANES - SSM_HEADS))).astype(BF16)
    w_oa = w_out[l][:D_HGRN].astype(BF16)
    w_os = w_out[l][D_HGRN:].astype(BF16)
    w_upb = w_up[l].astype(BF16)
    w_dnb = w_down[l].astype(BF16)
    d_full = jnp.repeat(ssm_d[l].astype(F32), SSM_HEAD_DIM).reshape(1, D_SSM)
    dt_bias = _pad_lanes(ssm_dt_bias[l])
    a_log = _pad_lanes(ssm_a_log[l])
    mconst = jnp.asarray(_hgrn_const(), BF16)
    tri = jnp.asarray(np.tril(np.ones((CHUNK, CHUNK), np.float32)), BF16)
    expand = jnp.asarray(_head_expand(), BF16)
    lb_raw = hgrn_lb.astype(F32)

    def dense_tail(x2d, o_a, y_s, tm):
        return _outproj(o_a, y_s, x2d, w_oa, w_os, _row(norm2_w[l]), tm)

    xp = x_prompt.reshape(batch * seq, D_MODEL)
    tm_p = _row_tile(seq)
    proj_p, lg_p, dt_p = _inproj(xp, _row(norm1_w[l]), w_main, w_dt, lb_raw, dt_bias)
    oa_p, hgrn_p = _hgrn_prompt(proj_p, lg_p, _row(hgrn_norm_w[l]), mconst, batch, seq)
    ys_p, ssm_p = _ssd_prompt(proj_p, dt_p, ssm_conv_w[l], _row(ssm_conv_b[l]), a_log,
                              d_full, _row(ssm_norm_w[l]), tri, expand, batch, seq)
    x1_p, h2_p = dense_tail(xp, oa_p, ys_p, tm_p)
    y_p, tail_p = _ffn_prompt(h2_p, x1_p, w_upb, w_dnb, ffn_conv_w[l], _row(ffn_conv_b[l]),
                              _row(final_norm_w), seq)
    proj_p3 = proj_p.reshape(batch, seq, D_MAIN)
    cs_p = proj_p3[:, seq - (SSM_CONV - 1):, OFF_XS:OFF_XS + CONV_DIM]
    tails = tail_p.reshape(batch, seq // FFN_ROW_TILE, SUBLANES, D_FF)
    cf_p = tails[:, -1, SUBLANES - (FFN_CONV - 1):, :]

    xs_ = x_sample.reshape(dec_batch, D_MODEL)
    proj_s, lg_s, dt_s = _inproj(xs_, _row(norm1_w[l]), w_main, w_dt, lb_raw, dt_bias)
    oa_s, hgrn_s = _hgrn_step(proj_s, lg_s, _row(hgrn_norm_w[l]), state_hgrn[l])
    ys_s, ssm_s = _ssd_step(proj_s, dt_s, state_conv_ssm[l], ssm_conv_w[l], _row(ssm_conv_b[l]),
                            a_log, d_full, _row(ssm_norm_w[l]), expand, state_ssm[l])
    x1_s, h2_s = dense_tail(xs_, oa_s, ys_s, dec_batch)
    y_s, gate_s = _ffn_step(h2_s, x1_s, w_upb, w_dnb, ffn_conv_w[l], _row(ffn_conv_b[l]),
                            _row(final_norm_w), state_conv_ffn[l])
    cs_s = jnp.concatenate([state_conv_ssm[l][:, 1:], proj_s[:, None, OFF_XS:OFF_XS + CONV_DIM]],
                           axis=1)
    cf_s = jnp.concatenate([state_conv_ffn[l][:, 1:], gate_s[:, None, :]], axis=1)

    dt_ = x_prompt.dtype
    return (y_p.reshape(batch, seq, D_MODEL).astype(dt_),
            y_s.reshape(dec_batch, 1, D_MODEL).astype(dt_),
            hgrn_p[None].astype(dt_),
            hgrn_s[None].astype(dt_),
            ssm_p.reshape(1, batch, SSM_HEADS, SSM_HEAD_DIM, SSM_STATE).astype(dt_),
            ssm_s[None].astype(dt_),
            cs_p[None].astype(dt_),
            cs_s[None].astype(dt_),
            cf_p[None].astype(dt_),
            cf_s[None].astype(dt_))
```

```python
import functools

import numpy as np
import jax
import jax.numpy as jnp
from jax import lax
from jax.experimental import pallas as pl
from jax.experimental.pallas import tpu as pltpu

F32 = jnp.float32
BF16 = jnp.bfloat16
EPS = 1e-6

LANES = 128
SUBLANES = 8

D_MODEL = 1024
HGRN_HEADS = 8
HGRN_DK = 128
HGRN_DV = 128
D_HGRN = HGRN_HEADS * HGRN_DV
SSM_HEADS = 16
SSM_HEAD_DIM = 64
D_SSM = SSM_HEADS * SSM_HEAD_DIM
SSM_STATE = 128
SSM_GROUPS = 2
SSM_CONV = 4
CONV_DIM = D_SSM + 2 * SSM_GROUPS * SSM_STATE
D_FF = 2816
FFN_CONV = 3
D_MAIN = 4 * D_HGRN + D_SSM + CONV_DIM
OFF_Q, OFF_F, OFF_I, OFF_G = 0, 1024, 2048, 3072
OFF_Z, OFF_XS, OFF_BC = 4096, 5120, 6144

CHUNK = 128
GROUP_W = D_SSM // SSM_GROUPS
VMEM_LIMIT = 56 * 1024 * 1024


def _cp(sem):
    return pltpu.CompilerParams(dimension_semantics=sem, vmem_limit_bytes=VMEM_LIMIT)


def _dot(a, b):
    return jnp.dot(a, b, preferred_element_type=F32)


def _dot_nt(a, b):
    return lax.dot_general(a, b, (((1,), (1,)), ((), ())), preferred_element_type=F32)


def _split3(x):
    h = x.astype(BF16)
    r = x - h.astype(F32)
    m = r.astype(BF16)
    lo = (r - m.astype(F32)).astype(BF16)
    return h, m, lo


def _dot_exact_lhs01(m01, x):
    h, m, lo = _split3(x)
    return _dot(m01, h) + _dot(m01, m) + _dot(m01, lo)


def _dot_exact_rhs01(x, m01):
    h, m, lo = _split3(x)
    return _dot(h, m01) + _dot(m, m01) + _dot(lo, m01)


def _dot_split_lhs01(m01, x):
    h = x.astype(BF16)
    lo = (x - h.astype(F32)).astype(BF16)
    return _dot(m01, h) + _dot(m01, lo)


def _sigmoid(x):
    return 1.0 / (1.0 + jnp.exp(-x))


def _silu(x):
    return x * _sigmoid(x)


def _rms(x, w):
    ms = jnp.mean(x * x, axis=-1, keepdims=True)
    return x * lax.rsqrt(ms + EPS) * w


def _inproj_kernel(x_ref, nw_ref, w_ref, wdt_ref, lb_ref, dtb_ref, mix_ref, lg_ref, dt_ref):
    hb = _rms(x_ref[...], nw_ref[...]).astype(BF16)

    def sec(off, width=D_HGRN):
        return _dot(hb, w_ref[:, off:off + width])

    def put(off, val):
        mix_ref[:, off:off + val.shape[1]] = val.astype(BF16)

    lb = _hgrn_lb(lb_ref[...])
    f = lb + (1.0 - lb) * _sigmoid(sec(OFF_F))
    lg_ref[...] = jnp.log(f)
    put(OFF_F, 1.0 - f)
    put(OFF_Q, _silu(sec(OFF_Q)))
    put(OFF_I, sec(OFF_I))
    put(OFF_G, _silu(sec(OFF_G)))
    put(OFF_Z, _silu(sec(OFF_Z, D_SSM)))
    put(OFF_XS, sec(OFF_XS, CONV_DIM))
    dt_ref[...] = _softplus(_dot(hb, wdt_ref[...]) + dtb_ref[...])


INPROJ_ROW_TILE = 512


def _inproj(x2d, norm_w, w_main, w_dt, lb_raw, dt_bias):
    n = x2d.shape[0]
    tm = min(INPROJ_ROW_TILE, n)
    assert n % tm == 0
    row = lambda w: pl.BlockSpec((tm, w), lambda i: (i, 0))
    resident = lambda shape: pl.BlockSpec(shape, lambda i: (0, 0), pipeline_mode=pl.Buffered(1))
    return pl.pallas_call(
        _inproj_kernel,
        grid=(n // tm,),
        in_specs=[
            row(D_MODEL), resident((1, D_MODEL)),
            resident((D_MODEL, D_MAIN)), resident((D_MODEL, LANES)),
            resident(lb_raw.shape), resident((1, LANES)),
        ],
        out_specs=[row(D_MAIN), row(D_HGRN), row(LANES)],
        out_shape=[
            jax.ShapeDtypeStruct((n, D_MAIN), BF16),
            jax.ShapeDtypeStruct((n, D_HGRN), F32),
            jax.ShapeDtypeStruct((n, LANES), F32),
        ],
        compiler_params=_cp(("arbitrary",)),
        name="inproj",
    )(x2d, norm_w, w_main, w_dt, lb_raw, dt_bias)


LOG2E = 1.4426950408889634
N_LEVELS = 7
MXU_LEVEL_HALVES = (4, 2)


def _hgrn_const():
    c = CHUNK
    t = np.arange(c)[:, None]
    j = np.arange(c)[None, :]
    blocks = [(j <= t)]
    for h in MXU_LEVEL_HALVES:
        mid = (t // (2 * h)) * (2 * h) + h
        upper = (t >= mid) & (j >= mid) & (j <= t)
        lower = (t < mid) & (j > t) & (j < mid)
        blocks.append(upper | lower)
    return np.concatenate(blocks, axis=0).astype(np.float32)


def _midpoint_decay(b, h):
    pieces = []
    for start in range(0, CHUNK, 2 * h):
        mid = start + h
        m = b[mid - 1:mid, :]
        pieces.append(m - b[start:mid])
        pieces.append(b[mid:mid + h] - m)
    return jnp.concatenate(pieces, axis=0)


def _mix_rows(q, k, h):
    pieces = []
    for start in range(0, CHUNK, 2 * h):
        pieces.append(k[start:start + h])
        pieces.append(q[start + h:start + 2 * h])
    return jnp.concatenate(pieces, axis=0)


def _hgrn_lb(lb_raw):
    mx = jnp.max(lb_raw, axis=0, keepdims=True)
    e = jnp.exp(lb_raw - mx)
    return e[0:1, :] / jnp.sum(e, axis=0, keepdims=True)


def _level_map():
    t = lax.broadcasted_iota(jnp.int32, (CHUNK, CHUNK), 0)
    s = lax.broadcasted_iota(jnp.int32, (CHUNK, CHUNK), 1)
    bitlen = 32 - lax.clz(t ^ s)
    return jnp.where(t > s, bitlen, jnp.where(t == s, 0, -1))


def _hgrn_prompt_kernel(q_ref, k_ref, i_ref, g_ref, lg_ref, nw_ref, mc_ref,
                        o_ref, s_out_ref, st_scr):
    c = pl.program_id(1)

    @pl.when(c == 0)
    def _():
        st_scr[...] = jnp.zeros_like(st_scr)

    n_sub = q_ref.shape[0] // CHUNK
    lev = _level_map()
    row = lax.broadcasted_iota(jnp.int32, (CHUNK, HGRN_DK), 0)
    heads = range(HGRN_HEADS)
    pairs = [(s, h) for s in range(n_sub) for h in heads]
    rs = {s: slice(s * CHUNK, (s + 1) * CHUNK) for s in range(n_sub)}
    cs = {h: slice(h * HGRN_DK, (h + 1) * HGRN_DK) for h in heads}

    e_sub = {s: _dot_split_lhs01(mc_ref[...], lg_ref[rs[s], :] * LOG2E) for s in range(n_sub)}
    qb = {(s, h): q_ref[rs[s], cs[h]] for s, h in pairs}
    kb = {(s, h): k_ref[rs[s], cs[h]] for s, h in pairs}
    vb = {(s, h): i_ref[rs[s], cs[h]] for s, h in pairs}
    q = {p: qb[p].astype(F32) for p in pairs}
    k = {p: kb[p].astype(F32) for p in pairs}
    b = {(s, h): e_sub[s][0:CHUNK, cs[h]] for s, h in pairs}
    b_last = {p: b[p][CHUNK - 1:CHUNK, :] for p in pairs}

    st = {h: st_scr[h] for h in heads}
    o = {}
    for s, h in pairs:
        p = (s, h)
        o[p] = _dot_nt((q[p] * jnp.exp2(b[p])).astype(BF16), st[h].astype(BF16))
        ks = (k[p] * jnp.exp2(b_last[p] - b[p])).astype(BF16)
        st[h] = st[h] * jnp.exp2(b_last[p]) + _dot(vb[p].astype(F32).T.astype(BF16), ks)
    for h in heads:
        st_scr[h] = st[h]

    a = {p: jnp.where(lev == 0, _dot_nt(qb[p], kb[p]), 0.0) for p in pairs}
    half = CHUNK // 2
    while half >= 1:
        for p in pairs:
            if half >= SUBLANES:
                x = _mix_rows(q[p], k[p], half) * jnp.exp2(_midpoint_decay(b[p], half))
            else:
                upper = (row & half) != 0
                if half in MXU_LEVEL_HALVES:
                    blk = 1 + MXU_LEVEL_HALVES.index(half)
                    w = jnp.exp2(e_sub[p[0]][blk * CHUNK:(blk + 1) * CHUNK, cs[p[1]]])
                    x = jnp.where(upper, q[p], k[p]) * w
                else:
                    x = jnp.where(upper, q[p] * (1.0 - k[p]), k[p])
            xb = x.astype(BF16)
            a[p] = jnp.where(lev == half.bit_length(), _dot_nt(xb, xb), a[p])
        half //= 2

    for p in pairs:
        o[p] = o[p] + _dot(a[p].astype(BF16), vb[p])
    for s, h in pairs:
        gate = g_ref[rs[s], cs[h]].astype(F32)
        o_ref[rs[s], cs[h]] = (_rms(o[(s, h)], nw_ref[...]) * gate).astype(BF16)

    @pl.when(c == pl.num_programs(1) - 1)
    def _():
        for h in range(HGRN_HEADS):
            s_out_ref[0, h] = st_scr[h].T


HGRN_SUBCHUNKS = 2


def _hgrn_prompt(mix, lg, norm_w, mconst, batch, seq):
    rows = HGRN_SUBCHUNKS * CHUNK
    assert seq % rows == 0
    nc = seq // rows

    def col(off):
        return pl.BlockSpec((rows, D_HGRN), lambda b, c: (b * nc + c, off // D_HGRN))

    return pl.pallas_call(
        _hgrn_prompt_kernel,
        grid=(batch, nc),
        in_specs=[
            col(OFF_Q), col(OFF_F), col(OFF_I), col(OFF_G), col(0),
            pl.BlockSpec((1, HGRN_DV), lambda b, c: (0, 0)),
            pl.BlockSpec(mconst.shape, lambda b, c: (0, 0)),
        ],
        out_specs=[
            pl.BlockSpec((rows, D_HGRN), lambda b, c: (b * nc + c, 0)),
            pl.BlockSpec((1, HGRN_HEADS, HGRN_DK, HGRN_DV), lambda b, c: (b, 0, 0, 0)),
        ],
        out_shape=[
            jax.ShapeDtypeStruct((batch * seq, D_HGRN), BF16),
            jax.ShapeDtypeStruct((batch, HGRN_HEADS, HGRN_DK, HGRN_DV), F32),
        ],
        scratch_shapes=[pltpu.VMEM((HGRN_HEADS, HGRN_DV, HGRN_DK), F32)],
        compiler_params=_cp(("arbitrary", "arbitrary")),
        name="hgrn_prompt",
    )(mix, mix, mix, mix, lg, norm_w, mconst)


def _hgrn_step_kernel(q_ref, i_ref, g_ref, lg_ref, nw_ref, s_ref,
                      o_ref, s_out_ref, o_scr):
    nb = q_ref.shape[0]
    q = q_ref[...].astype(F32)
    f = jnp.exp(lg_ref[...])
    k = 1.0 - f
    v = i_ref[...].astype(F32)
    q_t, f_t, k_t = q.T, f.T, k.T
    for t in range(nb):
        s_new = f_t[:, t:t + 1] * s_ref[t, 0] + k_t[:, t:t + 1] * v[t:t + 1, :]
        s_out_ref[t, 0] = s_new
        o_scr[t:t + 1, :] = jnp.sum(q_t[:, t:t + 1] * s_new, axis=0, keepdims=True)
    o_ref[...] = (_rms(o_scr[...], nw_ref[...]) * g_ref[...].astype(F32)).astype(BF16)


def _hgrn_step(mix, lg, norm_w, state):
    nb = state.shape[0]
    hb = lambda off: off // HGRN_DK

    def col(off):
        return pl.BlockSpec((nb, HGRN_DK), lambda h: (0, hb(off) + h))

    st_spec = pl.BlockSpec((nb, 1, HGRN_DK, HGRN_DV), lambda h: (0, h, 0, 0))
    return pl.pallas_call(
        _hgrn_step_kernel,
        grid=(HGRN_HEADS,),
        in_specs=[
            col(OFF_Q), col(OFF_I), col(OFF_G), col(0),
            pl.BlockSpec((1, HGRN_DV), lambda h: (0, 0)),
            st_spec,
        ],
        out_specs=[pl.BlockSpec((nb, HGRN_DV), lambda h: (0, h)), st_spec],
        out_shape=[
            jax.ShapeDtypeStruct((nb, D_HGRN), BF16),
            jax.ShapeDtypeStruct(state.shape, F32),
        ],
        scratch_shapes=[pltpu.VMEM((nb, HGRN_DV), F32)],
        compiler_params=_cp(("arbitrary",)),
        name="hgrn_step",
    )(mix, mix, mix, lg, norm_w, state)


def _head_expand():
    e = np.zeros((LANES, D_SSM), np.float32)
    for h in range(SSM_HEADS):
        e[h, h * SSM_HEAD_DIM:(h + 1) * SSM_HEAD_DIM] = 1.0
    return e


def _softplus(x):
    return jnp.maximum(x, 0.0) + jnp.log(1.0 + jnp.exp(-jnp.abs(x)))


def _ssm_gate_norm(y, z_gate, nw):
    y = y * z_gate.astype(F32)
    parts = [_rms(y[:, g * GROUP_W:(g + 1) * GROUP_W], nw[:, g * GROUP_W:(g + 1) * GROUP_W])
             for g in range(SSM_GROUPS)]
    return jnp.concatenate(parts, axis=-1)


def _ssd_prompt_kernel(z_ref, xs_ref, bc_ref, dt_ref, cw_ref, cb_ref, alog_ref,
                       dvec_ref, nw_ref, tri_ref, exp_ref,
                       shift_ref, y_ref, st_out_ref, xprev_scr, st_scr):
    c = pl.program_id(1)
    t = CHUNK

    @pl.when(c == 0)
    def _():
        st_scr[...] = jnp.zeros_like(st_scr)
        xprev_scr[...] = jnp.zeros_like(xprev_scr)

    x_cur = jnp.concatenate([xs_ref[...], bc_ref[...]], axis=-1)
    taps = _dot(shift_ref[...], jnp.concatenate([xprev_scr[...], x_cur], axis=0))
    xprev_scr[...] = x_cur
    acc = cb_ref[...] + cw_ref[SSM_CONV - 1:SSM_CONV, :] * x_cur.astype(F32)
    for d in range(1, SSM_CONV):
        acc = acc + cw_ref[SSM_CONV - 1 - d:SSM_CONV - d, :] * taps[(d - 1) * t:d * t, :]
    xbc = _silu(acc)
    xs = xbc[:, 0:D_SSM]

    dt = dt_ref[...]
    da = dt * (-LOG2E * jnp.exp(alog_ref[...]))
    cs = _dot_exact_lhs01(tri_ref[...], da)
    ex = exp_ref[...]
    dt_full = _dot_exact_rhs01(dt, ex)
    cs_full = _dot_exact_rhs01(cs, ex)
    cs_last_full = cs_full[t - 1:t, :]
    x_dt = xs * dt_full
    x_end = (x_dt * jnp.exp2(cs_last_full - cs_full)).astype(BF16)
    cs_t = cs.T

    causal = (lax.broadcasted_iota(jnp.int32, (t, t), 0)
              >= lax.broadcasted_iota(jnp.int32, (t, t), 1))
    lane = lax.broadcasted_iota(jnp.int32, (1, D_SSM), 1)
    odd_head = (lane & SSM_HEAD_DIM) != 0
    x_b = x_dt.astype(BF16)
    zero = jnp.zeros_like(x_b)
    x_by_parity = (jnp.where(odd_head, zero, x_b), jnp.where(odd_head, x_b, zero))
    heads_per_group = SSM_HEADS // SSM_GROUPS
    pair_w = 2 * SSM_HEAD_DIM
    never = -1e30

    y_parts = []
    for g in range(SSM_GROUPS):
        b_g = xbc[:, D_SSM + g * SSM_STATE:D_SSM + (g + 1) * SSM_STATE]
        c_off = D_SSM + SSM_GROUPS * SSM_STATE + g * SSM_STATE
        c_g = xbc[:, c_off:c_off + SSM_STATE].astype(BF16)
        gmat = _dot_nt(c_g, b_g.astype(BF16))
        for pp in range(heads_per_group // 2):
            h0 = g * heads_per_group + 2 * pp
            psl = slice(h0 * SSM_HEAD_DIM, h0 * SSM_HEAD_DIM + pair_w)
            yp = None
            for sub in range(2):
                h = h0 + sub
                diff = cs[:, h:h + 1] - cs_t[h:h + 1, :]
                w = jnp.exp2(jnp.where(causal, diff, never)) * gmat
                part = _dot(w.astype(BF16), x_by_parity[sub][:, psl])
                yp = part if yp is None else yp + part
            y_parts.append(yp)

    y_diag = jnp.concatenate(y_parts, axis=-1)
    y_offs = []
    for g in range(SSM_GROUPS):
        sl = slice(g * GROUP_W, (g + 1) * GROUP_W)
        b_g = xbc[:, D_SSM + g * SSM_STATE:D_SSM + (g + 1) * SSM_STATE]
        c_off = D_SSM + SSM_GROUPS * SSM_STATE + g * SSM_STATE
        c_g = xbc[:, c_off:c_off + SSM_STATE].astype(BF16)
        st_g = st_scr[:, sl]
        y_offs.append(_dot(c_g, st_g.astype(BF16)))
        st_scr[:, sl] = (st_g * jnp.exp2(cs_last_full[:, sl])
                         + _dot(b_g.T.astype(BF16), x_end[:, sl]))
    y_off = jnp.concatenate(y_offs, axis=-1) * jnp.exp2(cs_full)

    y = y_diag + y_off + dvec_ref[...] * xs
    y_ref[...] = _ssm_gate_norm(y, z_ref[...], nw_ref[...]).astype(BF16)

    @pl.when(c == pl.num_programs(1) - 1)
    def _():
        for j in range(D_SSM // LANES):
            st_out_ref[0, j * LANES:(j + 1) * LANES, :] = st_scr[:, j * LANES:(j + 1) * LANES].T


def _ssd_prompt(mix, dt, conv_w, conv_b, a_log, d_full, norm_w, tri, expand, batch, seq):
    nc = seq // CHUNK
    const = lambda shape: pl.BlockSpec(shape, lambda b, c: (0, 0))
    return pl.pallas_call(
        _ssd_prompt_kernel,
        grid=(batch, nc),
        in_specs=[
            pl.BlockSpec((CHUNK, D_SSM), lambda b, c: (b * nc + c, OFF_Z // D_SSM)),
            pl.BlockSpec((CHUNK, D_SSM), lambda b, c: (b * nc + c, OFF_XS // D_SSM)),
            pl.BlockSpec((CHUNK, 512), lambda b, c: (b * nc + c, OFF_BC // 512)),
            pl.BlockSpec((CHUNK, LANES), lambda b, c: (b * nc + c, 0)),
            const((SSM_CONV, CONV_DIM)), const((1, CONV_DIM)),
            const((1, LANES)),
            const((1, D_SSM)), const((1, D_SSM)),
            const((CHUNK, CHUNK)), const((LANES, D_SSM)),
            const(((SSM_CONV - 1) * CHUNK, 2 * CHUNK)),
        ],
        out_specs=[
            pl.BlockSpec((CHUNK, D_SSM), lambda b, c: (b * nc + c, 0)),
            pl.BlockSpec((1, D_SSM, SSM_STATE), lambda b, c: (b, 0, 0)),
        ],
        out_shape=[
            jax.ShapeDtypeStruct((batch * seq, D_SSM), BF16),
            jax.ShapeDtypeStruct((batch, D_SSM, SSM_STATE), F32),
        ],
        scratch_shapes=[
            pltpu.VMEM((CHUNK, CONV_DIM), BF16),
            pltpu.VMEM((SSM_STATE, D_SSM), F32),
        ],
        compiler_params=_cp(("arbitrary", "arbitrary")),
        name="ssd_prompt",
    )(mix, mix, mix, dt, conv_w, conv_b, a_log, d_full, norm_w, tri, expand,
      jnp.asarray(_conv_shifts(), BF16))


def _conv_shifts():
    m = np.zeros(((SSM_CONV - 1) * CHUNK, 2 * CHUNK), np.float32)
    for d in range(1, SSM_CONV):
        for t in range(CHUNK):
            m[(d - 1) * CHUNK + t, CHUNK + t - d] = 1.0
    return m


def _ssd_step_kernel(z_ref, xs_ref, bc_ref, dt_ref, b0_ref, b1_ref, b2_ref, cw_ref, cb_ref,
                     alog_ref, dvec_ref, nw_ref, exp_ref, st_ref,
                     y_ref, st_out_ref, xt_scr, at_scr, xs_scr, bc_scr, y_scr):
    p = pl.program_id(0)
    nb = z_ref.shape[0]
    pair_w = 2 * SSM_HEAD_DIM
    pairs_per_group = SSM_HEADS // SSM_GROUPS // 2

    @pl.when(p == 0)
    def _():
        x_new = jnp.concatenate([xs_ref[...], bc_ref[...]], axis=-1).astype(F32)
        acc = (cb_ref[...] + cw_ref[0:1, :] * b0_ref[...] + cw_ref[1:2, :] * b1_ref[...]
               + cw_ref[2:3, :] * b2_ref[...] + cw_ref[3:4, :] * x_new)
        xbc = _silu(acc)
        xs = xbc[:, 0:D_SSM]
        dt = dt_ref[...]
        da = dt * (-jnp.exp(alog_ref[...]))
        ex = exp_ref[...]
        x_dt = xs * _dot_exact_rhs01(dt, ex)
        decay = jnp.exp(_dot_exact_rhs01(da, ex))
        xs_scr[...] = xs
        bc_scr[...] = xbc[:, D_SSM:]
        for j in range(D_SSM // LANES):
            sl = slice(j * LANES, (j + 1) * LANES)
            xt_scr[sl, :] = x_dt[:, sl].T
            at_scr[sl, :] = decay[:, sl].T

    g_is_1 = p >= pairs_per_group
    row0 = pl.multiple_of(p * pair_w, pair_w)
    x_t = xt_scr[pl.ds(row0, pair_w), :]
    a_t = at_scr[pl.ds(row0, pair_w), :]
    bc = bc_scr[...]
    b_all = jnp.where(g_is_1, bc[:, SSM_STATE:2 * SSM_STATE], bc[:, 0:SSM_STATE])
    c_all = jnp.where(g_is_1, bc[:, 3 * SSM_STATE:4 * SSM_STATE],
                      bc[:, 2 * SSM_STATE:3 * SSM_STATE]).astype(BF16)
    for t in range(nb):
        st = st_ref[t].reshape(pair_w, SSM_STATE)
        new = a_t[:, t:t + 1] * st + x_t[:, t:t + 1] * b_all[t:t + 1, :]
        st_out_ref[t] = new.reshape(2, SSM_HEAD_DIM, SSM_STATE)
        c_rows = jnp.broadcast_to(c_all[t:t + 1, :], (SUBLANES, SSM_STATE))
        y_scr[p, t:t + 1, :] = _dot_nt(c_rows, new.astype(BF16))[0:1, :]

    @pl.when(p == pl.num_programs(0) - 1)
    def _():
        y_mix = jnp.concatenate([y_scr[j] for j in range(SSM_HEADS // 2)], axis=-1)
        y = y_mix + dvec_ref[...] * xs_scr[...]
        y_ref[...] = _ssm_gate_norm(y, z_ref[...], nw_ref[...]).astype(BF16)


def _ssd_step(mix, dt, buf, conv_w, conv_b, a_log, d_full, norm_w, expand, state):
    nb = state.shape[0]
    n_pairs = SSM_HEADS // 2
    const = lambda shape: pl.BlockSpec(shape, lambda p: (0, 0))
    st_spec = pl.BlockSpec((nb, 2, SSM_HEAD_DIM, SSM_STATE), lambda p: (0, p, 0, 0))
    return pl.pallas_call(
        _ssd_step_kernel,
        grid=(n_pairs,),
        in_specs=[
            pl.BlockSpec((nb, D_SSM), lambda p: (0, OFF_Z // D_SSM)),
            pl.BlockSpec((nb, D_SSM), lambda p: (0, OFF_XS // D_SSM)),
            pl.BlockSpec((nb, 512), lambda p: (0, OFF_BC // 512)),
            const((nb, LANES)),
            const((nb, CONV_DIM)), const((nb, CONV_DIM)), const((nb, CONV_DIM)),
            const((SSM_CONV, CONV_DIM)), const((1, CONV_DIM)),
            const((1, LANES)),
            const((1, D_SSM)), const((1, D_SSM)),
            const((LANES, D_SSM)),
            st_spec,
        ],
        out_specs=[const((nb, D_SSM)), st_spec],
        out_shape=[
            jax.ShapeDtypeStruct((nb, D_SSM), BF16),
            jax.ShapeDtypeStruct(state.shape, F32),
        ],
        scratch_shapes=[
            pltpu.VMEM((D_SSM, nb), F32),
            pltpu.VMEM((D_SSM, nb), F32),
            pltpu.VMEM((nb, D_SSM), F32),
            pltpu.VMEM((nb, 2 * SSM_GROUPS * SSM_STATE), F32),
            pltpu.VMEM((n_pairs, nb, 2 * SSM_HEAD_DIM), F32),
        ],
        compiler_params=_cp(("arbitrary",)),
        name="ssd_step",
    )(mix, mix, mix, dt, buf[:, 0], buf[:, 1], buf[:, 2], conv_w, conv_b, a_log,
      d_full, norm_w, expand, state)


def _outproj_kernel(oa_ref, ys_ref, x_ref, wa_ref, ws_ref, nw_ref, x1_ref, h2_ref):
    x1 = (x_ref[...] + _dot(oa_ref[...].astype(BF16), wa_ref[...])
          + _dot(ys_ref[...].astype(BF16), ws_ref[...]))
    x1_ref[...] = x1
    h2_ref[...] = _rms(x1, nw_ref[...]).astype(BF16)


def _outproj(o_a, y_s, x2d, w_a, w_s, norm_w, tm):
    n = x2d.shape[0]
    row = lambda w: pl.BlockSpec((tm, w), lambda i: (i, 0))
    const = lambda shape: pl.BlockSpec(shape, lambda i: (0, 0))
    return pl.pallas_call(
        _outproj_kernel,
        grid=(n // tm,),
        in_specs=[row(D_HGRN), row(D_SSM), row(D_MODEL),
                  const((D_HGRN, D_MODEL)), const((D_SSM, D_MODEL)), const((1, D_MODEL))],
        out_specs=[row(D_MODEL), row(D_MODEL)],
        out_shape=[jax.ShapeDtypeStruct((n, D_MODEL), F32),
                   jax.ShapeDtypeStruct((n, D_MODEL), BF16)],
        compiler_params=_cp(("arbitrary",)),
        name="outproj",
    )(o_a, y_s, x2d, w_a, w_s, norm_w)


FF_BLOCK = 256


def _ffn_finish(j, contrib, x1_ref, fnw_ref, y_ref, acc_scr):
    @pl.when(j == 0)
    def _():
        acc_scr[...] = contrib

    @pl.when(j > 0)
    def _():
        acc_scr[...] = acc_scr[...] + contrib

    @pl.when(j == pl.num_programs(1) - 1)
    def _():
        y_ref[...] = _rms(x1_ref[...] + acc_scr[...], fnw_ref[...])


def _ffn_prompt_kernel(h2_ref, x1_ref, wup_ref, wd_ref, cw_ref, cb_ref, fnw_ref,
                       y_ref, tail_ref, ge_scr, *, tiles_per_seq):
    i = pl.program_id(0)
    tm = h2_ref.shape[0]
    pad = SUBLANES
    h2 = h2_ref[...]

    seq_start = lax.rem(i, tiles_per_seq) == 0

    @pl.when(seq_start)
    def _():
        ge_scr[0:pad, :] = jnp.zeros((pad, D_FF), F32)

    @pl.when(jnp.logical_not(seq_start))
    def _():
        ge_scr[0:pad, :] = ge_scr[tm:tm + pad, :]

    acc = None
    bounds = np.cumsum((0,) + FFN_COL_BLOCKS)
    for c0, c1 in zip(bounds[:-1].tolist(), bounds[1:].tolist()):
        gate = _dot(h2, wup_ref[:, c0:c1])
        val = _dot(h2, wup_ref[:, D_FF + c0:D_FF + c1])
        ge_scr[pad:, c0:c1] = gate
        tail_ref[0, :, c0:c1] = gate[tm - pad:, :]
        conv = (cb_ref[:, c0:c1] + cw_ref[2:3, c0:c1] * gate
                + cw_ref[1:2, c0:c1] * ge_scr[pad - 1:pad - 1 + tm, c0:c1]
                + cw_ref[0:1, c0:c1] * ge_scr[pad - 2:pad - 2 + tm, c0:c1])
        act = (_silu(conv) * val).astype(BF16)
        part = _dot(act, wd_ref[c0:c1, :])
        acc = part if acc is None else acc + part
    y_ref[...] = _rms(x1_ref[...] + acc, fnw_ref[...])


FFN_ROW_TILE = 512
FFN_COL_BLOCKS = (1024, 1024, 768)
assert sum(FFN_COL_BLOCKS) == D_FF and all(c % LANES == 0 for c in FFN_COL_BLOCKS)


def _ffn_prompt(h2, x1, w_up, w_down, conv_w, conv_b, fnorm_w, seq):
    n = h2.shape[0]
    tm = FFN_ROW_TILE
    assert seq % tm == 0
    kern = functools.partial(_ffn_prompt_kernel, tiles_per_seq=seq // tm)
    row = lambda w: pl.BlockSpec((tm, w), lambda i: (i, 0))
    resident = lambda shape: pl.BlockSpec(shape, lambda i: (0, 0), pipeline_mode=pl.Buffered(1))
    return pl.pallas_call(
        kern,
        grid=(n // tm,),
        in_specs=[
            row(D_MODEL), row(D_MODEL),
            resident((D_MODEL, 2 * D_FF)), resident((D_FF, D_MODEL)),
            resident((FFN_CONV, D_FF)), resident((1, D_FF)), resident((1, D_MODEL)),
        ],
        out_specs=[
            row(D_MODEL),
            pl.BlockSpec((1, SUBLANES, D_FF), lambda i: (i, 0, 0)),
        ],
        out_shape=[
            jax.ShapeDtypeStruct((n, D_MODEL), F32),
            jax.ShapeDtypeStruct((n // tm, SUBLANES, D_FF), F32),
        ],
        scratch_shapes=[pltpu.VMEM((tm + SUBLANES, D_FF), F32)],
        compiler_params=_cp(("arbitrary",)),
        name="ffn_prompt",
    )(h2, x1, w_up, w_down, conv_w, conv_b, fnorm_w)


def _ffn_step_kernel(h2_ref, x1_ref, wg_ref, wv_ref, wd_ref, cw_ref, cb_ref, fnw_ref,
                     b0_ref, b1_ref, y_ref, gate_ref, acc_scr):
    j = pl.program_id(1)
    h2 = h2_ref[...]
    gate = _dot(h2, wg_ref[...])
    val = _dot(h2, wv_ref[...])
    gate_ref[...] = gate
    conv = (cb_ref[...] + cw_ref[2:3, :] * gate + cw_ref[1:2, :] * b1_ref[...]
            + cw_ref[0:1, :] * b0_ref[...])
    act = (_silu(conv) * val).astype(BF16)
    _ffn_finish(j, _dot(act, wd_ref[...]), x1_ref, fnw_ref, y_ref, acc_scr)


def _ffn_step(h2, x1, w_up, w_down, conv_w, conv_b, fnorm_w, buf):
    n = h2.shape[0]
    nj = D_FF // FF_BLOCK
    return pl.pallas_call(
        _ffn_step_kernel,
        grid=(1, nj),
        in_specs=[
            pl.BlockSpec((n, D_MODEL), lambda i, j: (0, 0)),
            pl.BlockSpec((n, D_MODEL), lambda i, j: (0, 0)),
            pl.BlockSpec((D_MODEL, FF_BLOCK), lambda i, j: (0, j)),
            pl.BlockSpec((D_MODEL, FF_BLOCK), lambda i, j: (0, nj + j)),
            pl.BlockSpec((FF_BLOCK, D_MODEL), lambda i, j: (j, 0)),
            pl.BlockSpec((FFN_CONV, FF_BLOCK), lambda i, j: (0, j)),
            pl.BlockSpec((1, FF_BLOCK), lambda i, j: (0, j)),
            pl.BlockSpec((1, D_MODEL), lambda i, j: (0, 0)),
            pl.BlockSpec((n, FF_BLOCK), lambda i, j: (0, j)),
            pl.BlockSpec((n, FF_BLOCK), lambda i, j: (0, j)),
        ],
        out_specs=[
            pl.BlockSpec((n, D_MODEL), lambda i, j: (0, 0)),
            pl.BlockSpec((n, FF_BLOCK), lambda i, j: (0, j)),
        ],
        out_shape=[
            jax.ShapeDtypeStruct((n, D_MODEL), F32),
            jax.ShapeDtypeStruct((n, D_FF), F32),
        ],
        scratch_shapes=[pltpu.VMEM((n, D_MODEL), F32)],
        compiler_params=_cp(("arbitrary", "arbitrary")),
        name="ffn_step",
    )(h2, x1, w_up, w_up, w_down, conv_w, conv_b, fnorm_w, buf[:, 0], buf[:, 1])


def _row(v):
    return v.reshape(1, -1).astype(F32)


def _pad_lanes(v):
    return jnp.pad(v.astype(F32), (0, LANES - v.shape[0])).reshape(1, LANES)


def _row_tile(n):
    for tm in (1024, 512, 256, 128):
        if n % tm == 0:
            return tm
    raise ValueError(f"token count {n} is not a multiple of 128")


def kernel(x_prompt, x_sample, state_hgrn, state_ssm, state_conv_ssm, state_conv_ffn, norm1_w, w_in, hgrn_lb, hgrn_norm_w, ssm_conv_w, ssm_conv_b, ssm_dt_bias, ssm_a_log, ssm_d, ssm_norm_w, w_out, norm2_w, w_up, ffn_conv_w, ffn_conv_b, w_down, final_norm_w):
    depth = w_in.shape[0]
    assert depth == 1, "single-layer trunk"
    l = 0
    batch, seq, _ = x_prompt.shape
    dec_batch, dec_seq, _ = x_sample.shape
    assert dec_seq == 1 and seq % CHUNK == 0 and seq >= SSM_CONV

    w_main = w_in[l].astype(BF16)
    w_dt = jnp.pad(w_in[l][:, D_MAIN:], ((0, 0), (0, LANES - SSM_HEADS))).astype(BF16)
    w_oa = w_out[l][:D_HGRN].astype(BF16)
    w_os = w_out[l][D_HGRN:].astype(BF16)
    w_upb = w_up[l].astype(BF16)
    w_dnb = w_down[l].astype(BF16)
    d_full = jnp.repeat(ssm_d[l].astype(F32), SSM_HEAD_DIM).reshape(1, D_SSM)
    dt_bias = _pad_lanes(ssm_dt_bias[l])
    a_log = _pad_lanes(ssm_a_log[l])
    mconst = jnp.asarray(_hgrn_const(), BF16)
    tri = jnp.asarray(np.tril(np.ones((CHUNK, CHUNK), np.float32)), BF16)
    expand = jnp.asarray(_head_expand(), BF16)
    lb_raw = hgrn_lb.astype(F32)

    def dense_tail(x2d, o_a, y_s, tm):
        return _outproj(o_a, y_s, x2d, w_oa, w_os, _row(norm2_w[l]), tm)

    xp = x_prompt.reshape(batch * seq, D_MODEL)
    tm_p = _row_tile(seq)
    proj_p, lg_p, dt_p = _inproj(xp, _row(norm1_w[l]), w_main, w_dt, lb_raw, dt_bias)
    oa_p, hgrn_p = _hgrn_prompt(proj_p, lg_p, _row(hgrn_norm_w[l]), mconst, batch, seq)
    ys_p, ssm_p = _ssd_prompt(proj_p, dt_p, ssm_conv_w[l], _row(ssm_conv_b[l]), a_log,
                              d_full, _row(ssm_norm_w[l]), tri, expand, batch, seq)
    x1_p, h2_p = dense_tail(xp, oa_p, ys_p, tm_p)
    y_p, tail_p = _ffn_prompt(h2_p, x1_p, w_upb, w_dnb, ffn_conv_w[l], _row(ffn_conv_b[l]),
                              _row(final_norm_w), seq)
    proj_p3 = proj_p.reshape(batch, seq, D_MAIN)
    cs_p = proj_p3[:, seq - (SSM_CONV - 1):, OFF_XS:OFF_XS + CONV_DIM]
    tails = tail_p.reshape(batch, seq // FFN_ROW_TILE, SUBLANES, D_FF)
    cf_p = tails[:, -1, SUBLANES - (FFN_CONV - 1):, :]

    xs_ = x_sample.reshape(dec_batch, D_MODEL)
    proj_s, lg_s, dt_s = _inproj(xs_, _row(norm1_w[l]), w_main, w_dt, lb_raw, dt_bias)
    oa_s, hgrn_s = _hgrn_step(proj_s, lg_s, _row(hgrn_norm_w[l]), state_hgrn[l])
    ys_s, ssm_s = _ssd_step(proj_s, dt_s, state_conv_ssm[l], ssm_conv_w[l], _row(ssm_conv_b[l]),
                            a_log, d_full, _row(ssm_norm_w[l]), expand, state_ssm[l])
    x1_s, h2_s = dense_tail(xs_, oa_s, ys_s, dec_batch)
    y_s, gate_s = _ffn_step(h2_s, x1_s, w_upb, w_dnb, ffn_conv_w[l], _row(ffn_conv_b[l]),
                            _row(final_norm_w), state_conv_ffn[l])
    cs_s = jnp.concatenate([state_conv_ssm[l][:, 1:], proj_s[:, None, OFF_XS:OFF_XS + CONV_DIM]],
                           axis=1)
    cf_s = jnp.concatenate([state_conv_ffn[l][:, 1:], gate_s[:, None, :]], axis=1)

    dt_ = x_prompt.dtype
    return (y_p.reshape(batch, seq, D_MODEL).astype(dt_),
            y_s.reshape(dec_batch, 1, D_MODEL).astype(dt_),
            hgrn_p[None].astype(dt_),
            hgrn_s[None].astype(dt_),
            ssm_p.reshape(1, batch, SSM_HEADS, SSM_HEAD_DIM, SSM_STATE).astype(dt_),
            ssm_s[None].astype(dt_),
            cs_p[None].astype(dt_),
            cs_s[None].astype(dt_),
            cf_p[None].astype(dt_),
            cf_s[None].astype(dt_))
```

```python
import functools

import numpy as np
import jax
import jax.numpy as jnp
from jax import lax
from jax.experimental import pallas as pl
from jax.experimental.pallas import tpu as pltpu

F32 = jnp.float32
BF16 = jnp.bfloat16
EPS = 1e-6

LANES = 128
SUBLANES = 8

D_MODEL = 1024
HGRN_HEADS = 8
HGRN_DK = 128
HGRN_DV = 128
D_HGRN = HGRN_HEADS * HGRN_DV
SSM_HEADS = 16
SSM_HEAD_DIM = 64
D_SSM = SSM_HEADS * SSM_HEAD_DIM
SSM_STATE = 128
SSM_GROUPS = 2
SSM_CONV = 4
CONV_DIM = D_SSM + 2 * SSM_GROUPS * SSM_STATE
D_FF = 2816
FFN_CONV = 3
D_MAIN = 4 * D_HGRN + D_SSM + CONV_DIM
OFF_Q, OFF_F, OFF_I, OFF_G = 0, 1024, 2048, 3072
OFF_Z, OFF_XS, OFF_BC = 4096, 5120, 6144

CHUNK = 128
GROUP_W = D_SSM // SSM_GROUPS
VMEM_LIMIT = 56 * 1024 * 1024


def _cp(sem):
    return pltpu.CompilerParams(dimension_semantics=sem, vmem_limit_bytes=VMEM_LIMIT)


def _dot(a, b):
    return jnp.dot(a, b, preferred_element_type=F32)


def _dot_nt(a, b):
    return lax.dot_general(a, b, (((1,), (1,)), ((), ())), preferred_element_type=F32)


def _split3(x):
    h = x.astype(BF16)
    r = x - h.astype(F32)
    m = r.astype(BF16)
    lo = (r - m.astype(F32)).astype(BF16)
    return h, m, lo


def _dot_exact_lhs01(m01, x):
    h, m, lo = _split3(x)
    return _dot(m01, h) + _dot(m01, m) + _dot(m01, lo)


def _dot_exact_rhs01(x, m01):
    h, m, lo = _split3(x)
    return _dot(h, m01) + _dot(m, m01) + _dot(lo, m01)


def _dot_split_lhs01(m01, x):
    h = x.astype(BF16)
    lo = (x - h.astype(F32)).astype(BF16)
    return _dot(m01, h) + _dot(m01, lo)


def _sigmoid(x):
    return 1.0 / (1.0 + jnp.exp(-x))


def _silu(x):
    return x * _sigmoid(x)


def _rms(x, w):
    ms = jnp.mean(x * x, axis=-1, keepdims=True)
    return x * lax.rsqrt(ms + EPS) * w


def _inproj_kernel(x_ref, nw_ref, w_ref, wdt_ref, lb_ref, dtb_ref, mix_ref, lg_ref, dt_ref):
    hb = _rms(x_ref[...], nw_ref[...]).astype(BF16)

    def put(off, val):
        mix_ref[:, off:off + val.shape[1]] = val.astype(BF16)

    w = D_HGRN
    qf = _dot(hb, w_ref[:, OFF_Q:OFF_Q + 2 * w])
    lb = _hgrn_lb(lb_ref[...])
    f = lb + (1.0 - lb) * _sigmoid(qf[:, w:])
    lg_ref[...] = jnp.log(f)
    put(OFF_F, 1.0 - f)
    put(OFF_Q, _silu(qf[:, :w]))
    ig = _dot(hb, w_ref[:, OFF_I:OFF_I + 2 * w])
    put(OFF_I, ig[:, :w])
    put(OFF_G, _silu(ig[:, w:]))
    zx = _dot(hb, w_ref[:, OFF_Z:OFF_Z + D_SSM + CONV_DIM])
    put(OFF_Z, _silu(zx[:, :D_SSM]))
    put(OFF_XS, zx[:, D_SSM:])
    dt_ref[...] = _softplus(_dot(hb, wdt_ref[...]) + dtb_ref[...])


INPROJ_ROW_TILE = 512


def _inproj(x2d, norm_w, w_main, w_dt, lb_raw, dt_bias):
    n = x2d.shape[0]
    tm = min(INPROJ_ROW_TILE, n)
    assert n % tm == 0
    row = lambda w: pl.BlockSpec((tm, w), lambda i: (i, 0))
    resident = lambda shape: pl.BlockSpec(shape, lambda i: (0, 0), pipeline_mode=pl.Buffered(1))
    return pl.pallas_call(
        _inproj_kernel,
        grid=(n // tm,),
        in_specs=[
            row(D_MODEL), resident((1, D_MODEL)),
            resident((D_MODEL, D_MAIN)), resident((D_MODEL, LANES)),
            resident(lb_raw.shape), resident((1, LANES)),
        ],
        out_specs=[row(D_MAIN), row(D_HGRN), row(LANES)],
        out_shape=[
            jax.ShapeDtypeStruct((n, D_MAIN), BF16),
            jax.ShapeDtypeStruct((n, D_HGRN), F32),
            jax.ShapeDtypeStruct((n, LANES), F32),
        ],
        compiler_params=_cp(("arbitrary",)),
        name="inproj",
    )(x2d, norm_w, w_main, w_dt, lb_raw, dt_bias)


LOG2E = 1.4426950408889634
N_LEVELS = 7
MXU_LEVEL_HALVES = (4, 2)


def _hgrn_const():
    c = CHUNK
    t = np.arange(c)[:, None]
    j = np.arange(c)[None, :]
    blocks = [(j <= t)]
    for h in MXU_LEVEL_HALVES:
        mid = (t // (2 * h)) * (2 * h) + h
        upper = (t >= mid) & (j >= mid) & (j <= t)
        lower = (t < mid) & (j > t) & (j < mid)
        blocks.append(upper | lower)
    return np.concatenate(blocks, axis=0).astype(np.float32)


def _midpoint_decay(b, h):
    pieces = []
    for start in range(0, CHUNK, 2 * h):
        mid = start + h
        m = b[mid - 1:mid, :]
        pieces.append(m - b[start:mid])
        pieces.append(b[mid:mid + h] - m)
    return jnp.concatenate(pieces, axis=0)


def _mix_rows(q, k, h):
    pieces = []
    for start in range(0, CHUNK, 2 * h):
        pieces.append(k[start:start + h])
        pieces.append(q[start + h:start + 2 * h])
    return jnp.concatenate(pieces, axis=0)


def _hgrn_lb(lb_raw):
    mx = jnp.max(lb_raw, axis=0, keepdims=True)
    e = jnp.exp(lb_raw - mx)
    return e[0:1, :] / jnp.sum(e, axis=0, keepdims=True)


def _level_map():
    t = lax.broadcasted_iota(jnp.int32, (CHUNK, CHUNK), 0)
    s = lax.broadcasted_iota(jnp.int32, (CHUNK, CHUNK), 1)
    bitlen = 32 - lax.clz(t ^ s)
    return jnp.where(t > s, bitlen, jnp.where(t == s, 0, -1))


def _hgrn_prompt_kernel(q_ref, k_ref, i_ref, g_ref, lg_ref, nw_ref, mc_ref,
                        o_ref, s_out_ref, st_scr):
    c = pl.program_id(1)

    @pl.when(c == 0)
    def _():
        st_scr[...] = jnp.zeros_like(st_scr)

    n_sub = q_ref.shape[0] // CHUNK
    lev = _level_map()
    row = lax.broadcasted_iota(jnp.int32, (CHUNK, HGRN_DK), 0)
    heads = range(HGRN_HEADS)
    pairs = [(s, h) for s in range(n_sub) for h in heads]
    rs = {s: slice(s * CHUNK, (s + 1) * CHUNK) for s in range(n_sub)}
    cs = {h: slice(h * HGRN_DK, (h + 1) * HGRN_DK) for h in heads}

    e_sub = {s: _dot_split_lhs01(mc_ref[...], lg_ref[rs[s], :] * LOG2E) for s in range(n_sub)}
    qb = {(s, h): q_ref[rs[s], cs[h]] for s, h in pairs}
    kb = {(s, h): k_ref[rs[s], cs[h]] for s, h in pairs}
    vb = {(s, h): i_ref[rs[s], cs[h]] for s, h in pairs}
    q = {p: qb[p].astype(F32) for p in pairs}
    k = {p: kb[p].astype(F32) for p in pairs}
    b = {(s, h): e_sub[s][0:CHUNK, cs[h]] for s, h in pairs}
    b_last = {p: b[p][CHUNK - 1:CHUNK, :] for p in pairs}

    st = {h: st_scr[h] for h in heads}
    o = {}
    for s, h in pairs:
        p = (s, h)
        o[p] = _dot_nt((q[p] * jnp.exp2(b[p])).astype(BF16), st[h].astype(BF16))
        ks = (k[p] * jnp.exp2(b_last[p] - b[p])).astype(BF16)
        st[h] = st[h] * jnp.exp2(b_last[p]) + _dot(vb[p].astype(F32).T.astype(BF16), ks)
    for h in heads:
        st_scr[h] = st[h]

    a = {p: jnp.where(lev == 0, _dot_nt(qb[p], kb[p]), 0.0) for p in pairs}
    half = CHUNK // 2
    while half >= 1:
        for p in pairs:
            if half >= SUBLANES:
                x = _mix_rows(q[p], k[p], half) * jnp.exp2(_midpoint_decay(b[p], half))
            else:
                upper = (row & half) != 0
                if half in MXU_LEVEL_HALVES:
                    blk = 1 + MXU_LEVEL_HALVES.index(half)
                    w = jnp.exp2(e_sub[p[0]][blk * CHUNK:(blk + 1) * CHUNK, cs[p[1]]])
                    x = jnp.where(upper, q[p], k[p]) * w
                else:
                    x = jnp.where(upper, q[p] * (1.0 - k[p]), k[p])
            xb = x.astype(BF16)
            a[p] = jnp.where(lev == half.bit_length(), _dot_nt(xb, xb), a[p])
        half //= 2

    for p in pairs:
        o[p] = o[p] + _dot(a[p].astype(BF16), vb[p])
    for s, h in pairs:
        gate = g_ref[rs[s], cs[h]].astype(F32)
        o_ref[rs[s], cs[h]] = (_rms(o[(s, h)], nw_ref[...]) * gate).astype(BF16)

    @pl.when(c == pl.num_programs(1) - 1)
    def _():
        for h in range(HGRN_HEADS):
            s_out_ref[0, h] = st_scr[h].T


HGRN_SUBCHUNKS = 2


def _hgrn_prompt(mix, lg, norm_w, mconst, batch, seq):
    rows = HGRN_SUBCHUNKS * CHUNK
    assert seq % rows == 0
    nc = seq // rows

    def col(off):
        return pl.BlockSpec((rows, D_HGRN), lambda b, c: (b * nc + c, off // D_HGRN))

    return pl.pallas_call(
        _hgrn_prompt_kernel,
        grid=(batch, nc),
        in_specs=[
            col(OFF_Q), col(OFF_F), col(OFF_I), col(OFF_G), col(0),
            pl.BlockSpec((1, HGRN_DV), lambda b, c: (0, 0)),
            pl.BlockSpec(mconst.shape, lambda b, c: (0, 0)),
        ],
        out_specs=[
            pl.BlockSpec((rows, D_HGRN), lambda b, c: (b * nc + c, 0)),
            pl.BlockSpec((1, HGRN_HEADS, HGRN_DK, HGRN_DV), lambda b, c: (b, 0, 0, 0)),
        ],
        out_shape=[
            jax.ShapeDtypeStruct((batch * seq, D_HGRN), BF16),
            jax.ShapeDtypeStruct((batch, HGRN_HEADS, HGRN_DK, HGRN_DV), F32),
        ],
        scratch_shapes=[pltpu.VMEM((HGRN_HEADS, HGRN_DV, HGRN_DK), F32)],
        compiler_params=_cp(("arbitrary", "arbitrary")),
        name="hgrn_prompt",
    )(mix, mix, mix, mix, lg, norm_w, mconst)


def _hgrn_step_kernel(q_ref, i_ref, g_ref, lg_ref, nw_ref, s_ref,
                      o_ref, s_out_ref, o_scr):
    nb = q_ref.shape[0]
    qb = q_ref[...]
    f_t = jnp.exp(lg_ref[...]).T
    v = i_ref[...].astype(F32)
    lhs_rows = 2 * SUBLANES
    for t in range(nb):
        v_row = v[t:t + 1, :]
        s_new = v_row + f_t[:, t:t + 1] * (s_ref[t, 0] - v_row)
        s_out_ref[t, 0] = s_new
        q_rows = jnp.broadcast_to(qb[t:t + 1, :], (lhs_rows, HGRN_DK))
        o_scr[t:t + 1, :] = _dot(q_rows, s_new.astype(BF16))[0:1, :]
    o_ref[...] = (_rms(o_scr[...], nw_ref[...]) * g_ref[...].astype(F32)).astype(BF16)


def _hgrn_step(mix, lg, norm_w, state):
    nb = state.shape[0]
    hb = lambda off: off // HGRN_DK

    def col(off):
        return pl.BlockSpec((nb, HGRN_DK), lambda h: (0, hb(off) + h))

    st_spec = pl.BlockSpec((nb, 1, HGRN_DK, HGRN_DV), lambda h: (0, h, 0, 0))
    return pl.pallas_call(
        _hgrn_step_kernel,
        grid=(HGRN_HEADS,),
        in_specs=[
            col(OFF_Q), col(OFF_I), col(OFF_G), col(0),
            pl.BlockSpec((1, HGRN_DV), lambda h: (0, 0)),
            st_spec,
        ],
        out_specs=[pl.BlockSpec((nb, HGRN_DV), lambda h: (0, h)), st_spec],
        out_shape=[
            jax.ShapeDtypeStruct((nb, D_HGRN), BF16),
            jax.ShapeDtypeStruct(state.shape, F32),
        ],
        scratch_shapes=[pltpu.VMEM((nb, HGRN_DV), F32)],
        compiler_params=_cp(("arbitrary",)),
        name="hgrn_step",
    )(mix, mix, mix, lg, norm_w, state)


def _head_expand(width=SSM_HEAD_DIM):
    e = np.zeros((LANES, SSM_HEADS * width), np.float32)
    for h in range(SSM_HEADS):
        e[h, h * width:(h + 1) * width] = 1.0
    return e


def _softplus(x):
    return jnp.maximum(x, 0.0) + jnp.log(1.0 + jnp.exp(-jnp.abs(x)))


def _ssm_gate_norm(y, z_gate, nw):
    y = y * z_gate.astype(F32)
    parts = [_rms(y[:, g * GROUP_W:(g + 1) * GROUP_W], nw[:, g * GROUP_W:(g + 1) * GROUP_W])
             for g in range(SSM_GROUPS)]
    return jnp.concatenate(parts, axis=-1)


def _ssd_prompt_kernel(z_ref, xs_ref, bc_ref, dt_ref, cw_ref, cb_ref, alog_ref,
                       dvec_ref, nw_ref, tri_ref, exp_ref,
                       shift_ref, y_ref, st_out_ref, xprev_scr, st_scr):
    c = pl.program_id(1)
    t = CHUNK

    @pl.when(c == 0)
    def _():
        st_scr[...] = jnp.zeros_like(st_scr)
        xprev_scr[...] = jnp.zeros_like(xprev_scr)

    subs = range(xs_ref.shape[0] // t)
    rs = [slice(s * t, (s + 1) * t) for s in subs]

    x_cur = [jnp.concatenate([xs_ref[r, :], bc_ref[r, :]], axis=-1) for r in rs]
    x_prev = [xprev_scr[...]] + x_cur[:-1]
    xprev_scr[...] = x_cur[-1]
    taps = [_dot(shift_ref[...], jnp.concatenate([x_prev[s], x_cur[s]], axis=0)) for s in subs]
    xbc = []
    for s in subs:
        acc = cb_ref[...] + cw_ref[SSM_CONV - 1:SSM_CONV, :] * x_cur[s].astype(F32)
        for d in range(1, SSM_CONV):
            acc = acc + cw_ref[SSM_CONV - 1 - d:SSM_CONV - d, :] * taps[s][(d - 1) * t:d * t, :]
        xbc.append(_silu(acc))
    xs = [x[:, 0:D_SSM] for x in xbc]

    dt = [dt_ref[r, :] for r in rs]
    neg_a = -LOG2E * jnp.exp(alog_ref[...])
    cs = [_dot_exact_lhs01(tri_ref[...], dt[s] * neg_a) for s in subs]
    parts = []
    for s in subs:
        parts += list(_split3(dt[s])) + list(_split3(cs[s]))
    wide = _dot(jnp.concatenate(parts, axis=0), exp_ref[...])
    blk = lambda i: wide[i * t:(i + 1) * t, :]
    dt_full = [blk(6 * s) + blk(6 * s + 1) + blk(6 * s + 2) for s in subs]
    cs_full = [blk(6 * s + 3) + blk(6 * s + 4) + blk(6 * s + 5) for s in subs]
    cs_last_full = [x[t - 1:t, :] for x in cs_full]
    x_dt = [xs[s] * dt_full[s] for s in subs]
    x_end = [(x_dt[s] * jnp.exp2(cs_last_full[s] - cs_full[s])).astype(BF16) for s in subs]
    cs_t = [x.T for x in cs]

    causal = (lax.broadcasted_iota(jnp.int32, (t, t), 0)
              >= lax.broadcasted_iota(jnp.int32, (t, t), 1))
    lane = lax.broadcasted_iota(jnp.int32, (1, D_SSM), 1)
    odd_head = (lane & SSM_HEAD_DIM) != 0
    x_by_parity = []
    for s in subs:
        x_b = x_dt[s].astype(BF16)
        zero = jnp.zeros_like(x_b)
        x_by_parity.append((jnp.where(odd_head, zero, x_b), jnp.where(odd_head, x_b, zero)))
    heads_per_group = SSM_HEADS // SSM_GROUPS
    pair_w = 2 * SSM_HEAD_DIM
    never = -1e30

    def group_bc(s, g):
        b_g = xbc[s][:, D_SSM + g * SSM_STATE:D_SSM + (g + 1) * SSM_STATE]
        c_off = D_SSM + SSM_GROUPS * SSM_STATE + g * SSM_STATE
        return b_g, xbc[s][:, c_off:c_off + SSM_STATE].astype(BF16)

    y_off = []
    for s in subs:
        offs = []
        for g in range(SSM_GROUPS):
            sl = slice(g * GROUP_W, (g + 1) * GROUP_W)
            b_g, c_g = group_bc(s, g)
            st_g = st_scr[:, sl]
            offs.append(_dot(c_g, st_g.astype(BF16)))
            st_scr[:, sl] = (st_g * jnp.exp2(cs_last_full[s][:, sl])
                             + _dot(b_g.T.astype(BF16), x_end[s][:, sl]))
        y_off.append(jnp.concatenate(offs, axis=-1) * jnp.exp2(cs_full[s]))

    y_diag = []
    for s in subs:
        y_parts = []
        for g in range(SSM_GROUPS):
            b_g, c_g = group_bc(s, g)
            gmat = _dot_nt(c_g, b_g.astype(BF16))
            for pp in range(heads_per_group // 2):
                h0 = g * heads_per_group + 2 * pp
                psl = slice(h0 * SSM_HEAD_DIM, h0 * SSM_HEAD_DIM + pair_w)
                yp = None
                for sub in range(2):
                    h = h0 + sub
                    diff = cs[s][:, h:h + 1] - cs_t[s][h:h + 1, :]
                    w = jnp.exp2(jnp.where(causal, diff, never)) * gmat
                    part = _dot(w.astype(BF16), x_by_parity[s][sub][:, psl])
                    yp = part if yp is None else yp + part
                y_parts.append(yp)
        y_diag.append(jnp.concatenate(y_parts, axis=-1))

    for s in subs:
        y = y_diag[s] + y_off[s] + dvec_ref[...] * xs[s]
        y_ref[rs[s], :] = _ssm_gate_norm(y, z_ref[rs[s], :], nw_ref[...]).astype(BF16)

    @pl.when(c == pl.num_programs(1) - 1)
    def _():
        for j in range(D_SSM // LANES):
            st_out_ref[0, j * LANES:(j + 1) * LANES, :] = st_scr[:, j * LANES:(j + 1) * LANES].T


SSD_SUBCHUNKS = 2


def _ssd_prompt(mix, dt, conv_w, conv_b, a_log, d_full, norm_w, tri, expand, batch, seq):
    rows = SSD_SUBCHUNKS * CHUNK
    assert seq % rows == 0
    nc = seq // rows
    const = lambda shape: pl.BlockSpec(shape, lambda b, c: (0, 0))
    return pl.pallas_call(
        _ssd_prompt_kernel,
        grid=(batch, nc),
        in_specs=[
            pl.BlockSpec((rows, D_SSM), lambda b, c: (b * nc + c, OFF_Z // D_SSM)),
            pl.BlockSpec((rows, D_SSM), lambda b, c: (b * nc + c, OFF_XS // D_SSM)),
            pl.BlockSpec((rows, 512), lambda b, c: (b * nc + c, OFF_BC // 512)),
            pl.BlockSpec((rows, LANES), lambda b, c: (b * nc + c, 0)),
            const((SSM_CONV, CONV_DIM)), const((1, CONV_DIM)),
            const((1, LANES)),
            const((1, D_SSM)), const((1, D_SSM)),
            const((CHUNK, CHUNK)), const((LANES, D_SSM)),
            const(((SSM_CONV - 1) * CHUNK, 2 * CHUNK)),
        ],
        out_specs=[
            pl.BlockSpec((rows, D_SSM), lambda b, c: (b * nc + c, 0)),
            pl.BlockSpec((1, D_SSM, SSM_STATE), lambda b, c: (b, 0, 0)),
        ],
        out_shape=[
            jax.ShapeDtypeStruct((batch * seq, D_SSM), BF16),
            jax.ShapeDtypeStruct((batch, D_SSM, SSM_STATE), F32),
        ],
        scratch_shapes=[
            pltpu.VMEM((CHUNK, CONV_DIM), BF16),
            pltpu.VMEM((SSM_STATE, D_SSM), F32),
        ],
        compiler_params=_cp(("arbitrary", "arbitrary")),
        name="ssd_prompt",
    )(mix, mix, mix, dt, conv_w, conv_b, a_log, d_full, norm_w, tri, expand,
      jnp.asarray(_conv_shifts(), BF16))


def _conv_shifts():
    m = np.zeros(((SSM_CONV - 1) * CHUNK, 2 * CHUNK), np.float32)
    for d in range(1, SSM_CONV):
        for t in range(CHUNK):
            m[(d - 1) * CHUNK + t, CHUNK + t - d] = 1.0
    return m


def _ssd_step_kernel(z_ref, xs_ref, bc_ref, dt_ref, b0_ref, b1_ref, b2_ref, cw_ref, cb_ref,
                     alog_ref, dvec_ref, nw_ref, exp_ref, exl_ref, st_ref,
                     y_ref, st_out_ref, xt_scr, at_scr, xs_scr, bc_scr, y_scr):
    p = pl.program_id(0)
    nb = z_ref.shape[0]
    pair_w = 2 * SSM_HEAD_DIM
    pairs_per_group = SSM_HEADS // SSM_GROUPS // 2

    @pl.when(p == 0)
    def _():
        x_new = jnp.concatenate([xs_ref[...], bc_ref[...]], axis=-1).astype(F32)
        acc = (cb_ref[...] + cw_ref[0:1, :] * b0_ref[...] + cw_ref[1:2, :] * b1_ref[...]
               + cw_ref[2:3, :] * b2_ref[...] + cw_ref[3:4, :] * x_new)
        xbc = _silu(acc)
        xs = xbc[:, 0:D_SSM]
        dt = dt_ref[...]
        da = dt * (-jnp.exp(alog_ref[...]))
        ex = exp_ref[...]
        x_dt = xs * _dot_exact_rhs01(dt, ex)
        decay = jnp.exp(_dot_exact_rhs01(da, exl_ref[...]))
        xs_scr[...] = xs
        bc_scr[...] = xbc[:, D_SSM:]
        for j in range(D_SSM // LANES):
            sl = slice(j * LANES, (j + 1) * LANES)
            xt_scr[sl, :] = x_dt[:, sl].T
            at_scr[j] = decay[:, 2 * j * LANES:2 * (j + 1) * LANES]

    g_is_1 = p >= pairs_per_group
    row0 = pl.multiple_of(p * pair_w, pair_w)
    x_t = xt_scr[pl.ds(row0, pair_w), :]
    a_p = at_scr[p]
    bc = bc_scr[...]
    b_all = jnp.where(g_is_1, bc[:, SSM_STATE:2 * SSM_STATE], bc[:, 0:SSM_STATE])
    c_all = jnp.where(g_is_1, bc[:, 3 * SSM_STATE:4 * SSM_STATE],
                      bc[:, 2 * SSM_STATE:3 * SSM_STATE]).astype(BF16)
    for t in range(nb):
        inject = x_t[:, t:t + 1] * b_all[t:t + 1, :]
        halves = []
        for sub in range(2):
            rows = slice(sub * SSM_HEAD_DIM, (sub + 1) * SSM_HEAD_DIM)
            half = a_p[t:t + 1, sub * LANES:(sub + 1) * LANES] * st_ref[t, sub] + inject[rows]
            st_out_ref[t, sub] = half
            halves.append(half)
        new = jnp.concatenate(halves, axis=0)
        c_rows = jnp.broadcast_to(c_all[t:t + 1, :], (SUBLANES, SSM_STATE))
        y_scr[p, t:t + 1, :] = _dot_nt(c_rows, new.astype(BF16))[0:1, :]

    @pl.when(p == pl.num_programs(0) - 1)
    def _():
        y_mix = jnp.concatenate([y_scr[j] for j in range(SSM_HEADS // 2)], axis=-1)
        y = y_mix + dvec_ref[...] * xs_scr[...]
        y_ref[...] = _ssm_gate_norm(y, z_ref[...], nw_ref[...]).astype(BF16)


def _ssd_step(mix, dt, buf, conv_w, conv_b, a_log, d_full, norm_w, expand, state):
    nb = state.shape[0]
    n_pairs = SSM_HEADS // 2
    const = lambda shape: pl.BlockSpec(shape, lambda p: (0, 0))
    st_spec = pl.BlockSpec((nb, 2, SSM_HEAD_DIM, SSM_STATE), lambda p: (0, p, 0, 0))
    return pl.pallas_call(
        _ssd_step_kernel,
        grid=(n_pairs,),
        in_specs=[
            pl.BlockSpec((nb, D_SSM), lambda p: (0, OFF_Z // D_SSM)),
            pl.BlockSpec((nb, D_SSM), lambda p: (0, OFF_XS // D_SSM)),
            pl.BlockSpec((nb, 512), lambda p: (0, OFF_BC // 512)),
            const((nb, LANES)),
            const((nb, CONV_DIM)), const((nb, CONV_DIM)), const((nb, CONV_DIM)),
            const((SSM_CONV, CONV_DIM)), const((1, CONV_DIM)),
            const((1, LANES)),
            const((1, D_SSM)), const((1, D_SSM)),
            const((LANES, D_SSM)), const((LANES, SSM_HEADS * LANES)),
            st_spec,
        ],
        out_specs=[const((nb, D_SSM)), st_spec],
        out_shape=[
            jax.ShapeDtypeStruct((nb, D_SSM), BF16),
            jax.ShapeDtypeStruct(state.shape, F32),
        ],
        scratch_shapes=[
            pltpu.VMEM((D_SSM, nb), F32),
            pltpu.VMEM((n_pairs, nb, 2 * LANES), F32),
            pltpu.VMEM((nb, D_SSM), F32),
            pltpu.VMEM((nb, 2 * SSM_GROUPS * SSM_STATE), F32),
            pltpu.VMEM((n_pairs, nb, 2 * SSM_HEAD_DIM), F32),
        ],
        compiler_params=_cp(("arbitrary",)),
        name="ssd_step",
    )(mix, mix, mix, dt, buf[:, 0], buf[:, 1], buf[:, 2], conv_w, conv_b, a_log,
      d_full, norm_w, expand, jnp.asarray(_head_expand(LANES), BF16), state)


def _outproj_kernel(oa_ref, ys_ref, x_ref, wa_ref, ws_ref, nw_ref, x1_ref, h2_ref):
    x1 = (x_ref[...] + _dot(oa_ref[...].astype(BF16), wa_ref[...])
          + _dot(ys_ref[...].astype(BF16), ws_ref[...]))
    x1_ref[...] = x1
    h2_ref[...] = _rms(x1, nw_ref[...]).astype(BF16)


def _outproj(o_a, y_s, x2d, w_a, w_s, norm_w, tm):
    n = x2d.shape[0]
    row = lambda w: pl.BlockSpec((tm, w), lambda i: (i, 0))
    const = lambda shape: pl.BlockSpec(shape, lambda i: (0, 0))
    return pl.pallas_call(
        _outproj_kernel,
        grid=(n // tm,),
        in_specs=[row(D_HGRN), row(D_SSM), row(D_MODEL),
                  const((D_HGRN, D_MODEL)), const((D_SSM, D_MODEL)), const((1, D_MODEL))],
        out_specs=[row(D_MODEL), row(D_MODEL)],
        out_shape=[jax.ShapeDtypeStruct((n, D_MODEL), F32),
                   jax.ShapeDtypeStruct((n, D_MODEL), BF16)],
        compiler_params=_cp(("arbitrary",)),
        name="outproj",
    )(o_a, y_s, x2d, w_a, w_s, norm_w)


FF_BLOCK = 256


def _ffn_finish(j, contrib, x1_ref, fnw_ref, y_ref, acc_scr):
    @pl.when(j == 0)
    def _():
        acc_scr[...] = contrib

    @pl.when(j > 0)
    def _():
        acc_scr[...] = acc_scr[...] + contrib

    @pl.when(j == pl.num_programs(1) - 1)
    def _():
        y_ref[...] = _rms(x1_ref[...] + acc_scr[...], fnw_ref[...])


def _ffn_prompt_kernel(oa_ref, ys_ref, x_ref, wa_ref, ws_ref, n2_ref, wup_ref, wd_ref, cw_ref, cb_ref,
                       fnw_ref, y_ref, tail_ref, ge_scr, *, tiles_per_seq):
    i = pl.program_id(0)
    tm = x_ref.shape[0]
    pad = SUBLANES
    x1 = x_ref[...] + _dot(oa_ref[...], wa_ref[...]) + _dot(ys_ref[...], ws_ref[...])
    h2 = _rms(x1, n2_ref[...]).astype(BF16)

    seq_start = lax.rem(i, tiles_per_seq) == 0

    @pl.when(seq_start)
    def _():
        ge_scr[0:pad, :] = jnp.zeros((pad, D_FF), F32)

    @pl.when(jnp.logical_not(seq_start))
    def _():
        ge_scr[0:pad, :] = ge_scr[tm:tm + pad, :]

    acc = None
    bounds = np.cumsum((0,) + FFN_COL_BLOCKS)
    for c0, c1 in zip(bounds[:-1].tolist(), bounds[1:].tolist()):
        gate = _dot(h2, wup_ref[:, c0:c1])
        val = _dot(h2, wup_ref[:, D_FF + c0:D_FF + c1])
        ge_scr[pad:, c0:c1] = gate
        tail_ref[0, :, c0:c1] = gate[tm - pad:, :]
        conv = (cb_ref[:, c0:c1] + cw_ref[2:3, c0:c1] * gate
                + cw_ref[1:2, c0:c1] * ge_scr[pad - 1:pad - 1 + tm, c0:c1]
                + cw_ref[0:1, c0:c1] * ge_scr[pad - 2:pad - 2 + tm, c0:c1])
        act = (_silu(conv) * val).astype(BF16)
        part = _dot(act, wd_ref[c0:c1, :])
        acc = part if acc is None else acc + part
    y_ref[...] = _rms(x1 + acc, fnw_ref[...])


FFN_ROW_TILE = 512
FFN_COL_BLOCKS = (1024, 1024, 768)
assert sum(FFN_COL_BLOCKS) == D_FF and all(c % LANES == 0 for c in FFN_COL_BLOCKS)


def _ffn_prompt(o_a, y_s, x2d, w_a, w_s, norm2_w, w_up, w_down, conv_w, conv_b, fnorm_w, seq):
    n = x2d.shape[0]
    tm = FFN_ROW_TILE
    assert seq % tm == 0
    kern = functools.partial(_ffn_prompt_kernel, tiles_per_seq=seq // tm)
    row = lambda w: pl.BlockSpec((tm, w), lambda i: (i, 0))
    resident = lambda shape: pl.BlockSpec(shape, lambda i: (0, 0), pipeline_mode=pl.Buffered(1))
    return pl.pallas_call(
        kern,
        grid=(n // tm,),
        in_specs=[
            row(D_HGRN), row(D_SSM), row(D_MODEL),
            resident((D_HGRN, D_MODEL)), resident((D_SSM, D_MODEL)), resident((1, D_MODEL)),
            resident((D_MODEL, 2 * D_FF)), resident((D_FF, D_MODEL)),
            resident((FFN_CONV, D_FF)), resident((1, D_FF)), resident((1, D_MODEL)),
        ],
        out_specs=[
            row(D_MODEL),
            pl.BlockSpec((1, SUBLANES, D_FF), lambda i: (i, 0, 0)),
        ],
        out_shape=[
            jax.ShapeDtypeStruct((n, D_MODEL), F32),
            jax.ShapeDtypeStruct((n // tm, SUBLANES, D_FF), F32),
        ],
        scratch_shapes=[pltpu.VMEM((tm + SUBLANES, D_FF), F32)],
        compiler_params=_cp(("arbitrary",)),
        name="ffn_prompt",
    )(o_a, y_s, x2d, w_a, w_s, norm2_w, w_up, w_down, conv_w, conv_b, fnorm_w)


def _ffn_step_kernel(h2_ref, x1_ref, wg_ref, wv_ref, wd_ref, cw_ref, cb_ref, fnw_ref,
                     b0_ref, b1_ref, y_ref, gate_ref, acc_scr):
    j = pl.program_id(1)
    h2 = h2_ref[...]
    gate = _dot(h2, wg_ref[...])
    val = _dot(h2, wv_ref[...])
    gate_ref[...] = gate
    conv = (cb_ref[...] + cw_ref[2:3, :] * gate + cw_ref[1:2, :] * b1_ref[...]
            + cw_ref[0:1, :] * b0_ref[...])
    act = (_silu(conv) * val).astype(BF16)
    _ffn_finish(j, _dot(act, wd_ref[...]), x1_ref, fnw_ref, y_ref, acc_scr)


def _ffn_step(h2, x1, w_up, w_down, conv_w, conv_b, fnorm_w, buf):
    n = h2.shape[0]
    nj = D_FF // FF_BLOCK
    return pl.pallas_call(
        _ffn_step_kernel,
        grid=(1, nj),
        in_specs=[
            pl.BlockSpec((n, D_MODEL), lambda i, j: (0, 0)),
            pl.BlockSpec((n, D_MODEL), lambda i, j: (0, 0)),
            pl.BlockSpec((D_MODEL, FF_BLOCK), lambda i, j: (0, j)),
            pl.BlockSpec((D_MODEL, FF_BLOCK), lambda i, j: (0, nj + j)),
            pl.BlockSpec((FF_BLOCK, D_MODEL), lambda i, j: (j, 0)),
            pl.BlockSpec((FFN_CONV, FF_BLOCK), lambda i, j: (0, j)),
            pl.BlockSpec((1, FF_BLOCK), lambda i, j: (0, j)),
            pl.BlockSpec((1, D_MODEL), lambda i, j: (0, 0)),
            pl.BlockSpec((n, FF_BLOCK), lambda i, j: (0, j)),
            pl.BlockSpec((n, FF_BLOCK), lambda i, j: (0, j)),
        ],
        out_specs=[
            pl.BlockSpec((n, D_MODEL), lambda i, j: (0, 0)),
            pl.BlockSpec((n, FF_BLOCK), lambda i, j: (0, j)),
        ],
        out_shape=[
            jax.ShapeDtypeStruct((n, D_MODEL), F32),
            jax.ShapeDtypeStruct((n, D_FF), F32),
        ],
        scratch_shapes=[pltpu.VMEM((n, D_MODEL), F32)],
        compiler_params=_cp(("arbitrary", "arbitrary")),
        name="ffn_step",
    )(h2, x1, w_up, w_up, w_down, conv_w, conv_b, fnorm_w, buf[:, 0], buf[:, 1])


def _row(v):
    return v.reshape(1, -1).astype(F32)


def _pad_lanes(v):
    return jnp.pad(v.astype(F32), (0, LANES - v.shape[0])).reshape(1, LANES)


def kernel(x_prompt, x_sample, state_hgrn, state_ssm, state_conv_ssm, state_conv_ffn, norm1_w, w_in, hgrn_lb, hgrn_norm_w, ssm_conv_w, ssm_conv_b, ssm_dt_bias, ssm_a_log, ssm_d, ssm_norm_w, w_out, norm2_w, w_up, ffn_conv_w, ffn_conv_b, w_down, final_norm_w):
    depth = w_in.shape[0]
    assert depth == 1, "single-layer trunk"
    l = 0
    batch, seq, _ = x_prompt.shape
    dec_batch, dec_seq, _ = x_sample.shape
    assert dec_seq == 1 and seq % CHUNK == 0 and seq >= SSM_CONV

    w_main = w_in[l].astype(BF16)
    w_dt = jnp.pad(w_in[l][:, D_MAIN:], ((0, 0), (0, LANES - SSM_HEADS))).astype(BF16)
    w_oa = w_out[l][:D_HGRN].astype(BF16)
    w_os = w_out[l][D_HGRN:].astype(BF16)
    w_upb = w_up[l].astype(BF16)
    w_dnb = w_down[l].astype(BF16)
    d_full = jnp.repeat(ssm_d[l].astype(F32), SSM_HEAD_DIM).reshape(1, D_SSM)
    dt_bias = _pad_lanes(ssm_dt_bias[l])
    a_log = _pad_lanes(ssm_a_log[l])
    mconst = jnp.asarray(_hgrn_const(), BF16)
    tri = jnp.asarray(np.tril(np.ones((CHUNK, CHUNK), np.float32)), BF16)
    expand = jnp.asarray(_head_expand(), BF16)
    lb_raw = hgrn_lb.astype(F32)

    xp = x_prompt.reshape(batch * seq, D_MODEL)
    proj_p, lg_p, dt_p = _inproj(xp, _row(norm1_w[l]), w_main, w_dt, lb_raw, dt_bias)
    oa_p, hgrn_p = _hgrn_prompt(proj_p, lg_p, _row(hgrn_norm_w[l]), mconst, batch, seq)
    ys_p, ssm_p = _ssd_prompt(proj_p, dt_p, ssm_conv_w[l], _row(ssm_conv_b[l]), a_log,
                              d_full, _row(ssm_norm_w[l]), tri, expand, batch, seq)
    y_p, tail_p = _ffn_prompt(oa_p, ys_p, xp, w_oa, w_os, _row(norm2_w[l]), w_upb, w_dnb,
                              ffn_conv_w[l], _row(ffn_conv_b[l]), _row(final_norm_w), seq)
    proj_p3 = proj_p.reshape(batch, seq, D_MAIN)
    cs_p = proj_p3[:, seq - (SSM_CONV - 1):, OFF_XS:OFF_XS + CONV_DIM]
    tails = tail_p.reshape(batch, seq // FFN_ROW_TILE, SUBLANES, D_FF)
    cf_p = tails[:, -1, SUBLANES - (FFN_CONV - 1):, :]

    xs_ = x_sample.reshape(dec_batch, D_MODEL)
    proj_s, lg_s, dt_s = _inproj(xs_, _row(norm1_w[l]), w_main, w_dt, lb_raw, dt_bias)
    oa_s, hgrn_s = _hgrn_step(proj_s, lg_s, _row(hgrn_norm_w[l]), state_hgrn[l])
    ys_s, ssm_s = _ssd_step(proj_s, dt_s, state_conv_ssm[l], ssm_conv_w[l], _row(ssm_conv_b[l]),
                            a_log, d_full, _row(ssm_norm_w[l]), expand, state_ssm[l])
    x1_s, h2_s = _outproj(oa_s, ys_s, xs_, w_oa, w_os, _row(norm2_w[l]), dec_batch)
    y_s, gate_s = _ffn_step(h2_s, x1_s, w_upb, w_dnb, ffn_conv_w[l], _row(ffn_conv_b[l]),
                            _row(final_norm_w), state_conv_ffn[l])
    cs_s = jnp.concatenate([state_conv_ssm[l][:, 1:], proj_s[:, None, OFF_XS:OFF_XS + CONV_DIM]],
                           axis=1)
    cf_s = jnp.concatenate([state_conv_ffn[l][:, 1:], gate_s[:, None, :]], axis=1)

    dt_ = x_prompt.dtype
    return (y_p.reshape(batch, seq, D_MODEL).astype(dt_),
            y_s.reshape(dec_batch, 1, D_MODEL).astype(dt_),
            hgrn_p[None].astype(dt_),
            hgrn_s[None].astype(dt_),
            ssm_p.reshape(1, batch, SSM_HEADS, SSM_HEAD_DIM, SSM_STATE).astype(dt_),
            ssm_s[None].astype(dt_),
            cs_p[None].astype(dt_),
            cs_s[None].astype(dt_),
            cf_p[None].astype(dt_),
            cf_s[None].astype(dt_))
```

```python
import functools

import numpy as np
import jax
import jax.numpy as jnp
from jax import lax
from jax.experimental import pallas as pl
from jax.experimental.pallas import tpu as pltpu

F32 = jnp.float32
BF16 = jnp.bfloat16
EPS = 1e-6

LANES = 128
SUBLANES = 8

D_MODEL = 1024
HGRN_HEADS = 8
HGRN_DK = 128
HGRN_DV = 128
D_HGRN = HGRN_HEADS * HGRN_DV
SSM_HEADS = 16
SSM_HEAD_DIM = 64
D_SSM = SSM_HEADS * SSM_HEAD_DIM
SSM_STATE = 128
SSM_GROUPS = 2
SSM_CONV = 4
CONV_DIM = D_SSM + 2 * SSM_GROUPS * SSM_STATE
D_FF = 2816
FFN_CONV = 3
D_MAIN = 4 * D_HGRN + D_SSM + CONV_DIM
OFF_Q, OFF_F, OFF_I, OFF_G = 0, 1024, 2048, 3072
OFF_Z, OFF_XS, OFF_BC = 4096, 5120, 6144

CHUNK = 128
GROUP_W = D_SSM // SSM_GROUPS
VMEM_LIMIT = 56 * 1024 * 1024


def _cp(sem):
    return pltpu.CompilerParams(dimension_semantics=sem, vmem_limit_bytes=VMEM_LIMIT)


def _dot(a, b):
    return jnp.dot(a, b, preferred_element_type=F32)


def _dot_nt(a, b):
    return lax.dot_general(a, b, (((1,), (1,)), ((), ())), preferred_element_type=F32)


def _split3(x):
    h = x.astype(BF16)
    r = x - h.astype(F32)
    m = r.astype(BF16)
    lo = (r - m.astype(F32)).astype(BF16)
    return h, m, lo


def _dot_exact_lhs01(m01, x):
    h, m, lo = _split3(x)
    return _dot(m01, h) + _dot(m01, m) + _dot(m01, lo)


def _dot_exact_rhs01(x, m01):
    h, m, lo = _split3(x)
    return _dot(h, m01) + _dot(m, m01) + _dot(lo, m01)


def _dot_split_lhs01(m01, x):
    h = x.astype(BF16)
    lo = (x - h.astype(F32)).astype(BF16)
    return _dot(m01, h) + _dot(m01, lo)


def _sigmoid(x):
    return 1.0 / (1.0 + jnp.exp(-x))


def _silu(x):
    return x * _sigmoid(x)


def _rms(x, w):
    ms = jnp.mean(x * x, axis=-1, keepdims=True)
    return x * lax.rsqrt(ms + EPS) * w


def _inproj_kernel(x_ref, nw_ref, w_ref, wdt_ref, lb_ref, dtb_ref, mix_ref, lg_ref, dt_ref):
    hb = _rms(x_ref[...], nw_ref[...]).astype(BF16)

    def put(off, val):
        mix_ref[:, off:off + val.shape[1]] = val.astype(BF16)

    w = D_HGRN
    qf = _dot(hb, w_ref[:, OFF_Q:OFF_Q + 2 * w])
    lb = _hgrn_lb(lb_ref[...])
    f = lb + (1.0 - lb) * _sigmoid(qf[:, w:])
    lg_ref[...] = jnp.log(f)
    put(OFF_F, 1.0 - f)
    put(OFF_Q, _silu(qf[:, :w]))
    ig = _dot(hb, w_ref[:, OFF_I:OFF_I + 2 * w])
    put(OFF_I, ig[:, :w])
    put(OFF_G, _silu(ig[:, w:]))
    zx = _dot(hb, w_ref[:, OFF_Z:OFF_Z + D_SSM + CONV_DIM])
    put(OFF_Z, _silu(zx[:, :D_SSM]))
    put(OFF_XS, zx[:, D_SSM:])
    dt_ref[...] = _softplus(_dot(hb, wdt_ref[...]) + dtb_ref[...])


INPROJ_ROW_TILE = 512


def _inproj(x2d, norm_w, w_main, w_dt, lb_raw, dt_bias):
    n = x2d.shape[0]
    tm = min(INPROJ_ROW_TILE, n)
    assert n % tm == 0
    row = lambda w: pl.BlockSpec((tm, w), lambda i: (i, 0))
    resident = lambda shape: pl.BlockSpec(shape, lambda i: (0, 0), pipeline_mode=pl.Buffered(1))
    return pl.pallas_call(
        _inproj_kernel,
        grid=(n // tm,),
        in_specs=[
            row(D_MODEL), resident((1, D_MODEL)),
            resident((D_MODEL, D_MAIN)), resident((D_MODEL, LANES)),
            resident(lb_raw.shape), resident((1, LANES)),
        ],
        out_specs=[row(D_MAIN), row(D_HGRN), row(LANES)],
        out_shape=[
            jax.ShapeDtypeStruct((n, D_MAIN), BF16),
            jax.ShapeDtypeStruct((n, D_HGRN), F32),
            jax.ShapeDtypeStruct((n, LANES), F32),
        ],
        compiler_params=_cp(("arbitrary",)),
        name="inproj",
    )(x2d, norm_w, w_main, w_dt, lb_raw, dt_bias)


LOG2E = 1.4426950408889634
N_LEVELS = 7
MXU_LEVEL_HALVES = (4, 2)


def _hgrn_const():
    c = CHUNK
    t = np.arange(c)[:, None]
    j = np.arange(c)[None, :]
    blocks = [(j <= t)]
    for h in MXU_LEVEL_HALVES:
        mid = (t // (2 * h)) * (2 * h) + h
        upper = (t >= mid) & (j >= mid) & (j <= t)
        lower = (t < mid) & (j > t) & (j < mid)
        blocks.append(upper | lower)
    return np.concatenate(blocks, axis=0).astype(np.float32)


def _midpoint_decay(b, h):
    pieces = []
    for start in range(0, CHUNK, 2 * h):
        mid = start + h
        m = b[mid - 1:mid, :]
        pieces.append(m - b[start:mid])
        pieces.append(b[mid:mid + h] - m)
    return jnp.concatenate(pieces, axis=0)


def _mix_rows(q, k, h):
    pieces = []
    for start in range(0, CHUNK, 2 * h):
        pieces.append(k[start:start + h])
        pieces.append(q[start + h:start + 2 * h])
    return jnp.concatenate(pieces, axis=0)


def _hgrn_lb(lb_raw):
    mx = jnp.max(lb_raw, axis=0, keepdims=True)
    e = jnp.exp(lb_raw - mx)
    return e[0:1, :] / jnp.sum(e, axis=0, keepdims=True)


def _level_map():
    t = lax.broadcasted_iota(jnp.int32, (CHUNK, CHUNK), 0)
    s = lax.broadcasted_iota(jnp.int32, (CHUNK, CHUNK), 1)
    bitlen = 32 - lax.clz(t ^ s)
    return jnp.where(t > s, bitlen, jnp.where(t == s, 0, -1))


def _hgrn_prompt_kernel(q_ref, k_ref, i_ref, g_ref, lg_ref, nw_ref, mc_ref,
                        o_ref, s_out_ref, st_scr):
    c = pl.program_id(1)

    @pl.when(c == 0)
    def _():
        st_scr[...] = jnp.zeros_like(st_scr)

    n_sub = q_ref.shape[0] // CHUNK
    lev = _level_map().astype(jnp.int16)
    row = lax.broadcasted_iota(jnp.int32, (CHUNK, HGRN_DK), 0)
    heads = range(HGRN_HEADS)
    pairs = [(s, h) for s in range(n_sub) for h in heads]
    rs = {s: slice(s * CHUNK, (s + 1) * CHUNK) for s in range(n_sub)}
    cs = {h: slice(h * HGRN_DK, (h + 1) * HGRN_DK) for h in heads}

    e_sub = {s: _dot_split_lhs01(mc_ref[...], lg_ref[rs[s], :] * LOG2E) for s in range(n_sub)}
    qb = {(s, h): q_ref[rs[s], cs[h]] for s, h in pairs}
    kb = {(s, h): k_ref[rs[s], cs[h]] for s, h in pairs}
    vb = {(s, h): i_ref[rs[s], cs[h]] for s, h in pairs}
    q = {p: qb[p].astype(F32) for p in pairs}
    k = {p: kb[p].astype(F32) for p in pairs}
    b = {(s, h): e_sub[s][0:CHUNK, cs[h]] for s, h in pairs}
    b_last = {p: b[p][CHUNK - 1:CHUNK, :] for p in pairs}

    st = {h: st_scr[h] for h in heads}
    o = {}
    for s, h in pairs:
        p = (s, h)
        o[p] = _dot_nt((q[p] * jnp.exp2(b[p])).astype(BF16), st[h].astype(BF16))
        ks = (k[p] * jnp.exp2(b_last[p] - b[p])).astype(BF16)
        st[h] = st[h] * jnp.exp2(b_last[p]) + _dot(vb[p].astype(F32).T.astype(BF16), ks)
    for h in heads:
        st_scr[h] = st[h]

    a = {p: jnp.where(lev == 0, _dot_nt(qb[p], kb[p]).astype(BF16), jnp.zeros((), BF16))
         for p in pairs}
    half = CHUNK // 2
    while half >= 1:
        for p in pairs:
            if half >= SUBLANES:
                x = _mix_rows(q[p], k[p], half) * jnp.exp2(_midpoint_decay(b[p], half))
            else:
                upper = (row & half) != 0
                if half in MXU_LEVEL_HALVES:
                    blk = 1 + MXU_LEVEL_HALVES.index(half)
                    w = jnp.exp2(e_sub[p[0]][blk * CHUNK:(blk + 1) * CHUNK, cs[p[1]]])
                    x = jnp.where(upper, q[p], k[p]) * w
                else:
                    x = jnp.where(upper, q[p] * (1.0 - k[p]), k[p])
            gram = _dot(x.astype(BF16), x.T.astype(BF16))
            a[p] = jnp.where(lev == half.bit_length(), gram.astype(BF16), a[p])
        half //= 2

    for p in pairs:
        o[p] = o[p] + _dot(a[p], vb[p])
    for s, h in pairs:
        gate = g_ref[rs[s], cs[h]].astype(F32)
        o_ref[rs[s], cs[h]] = (_rms(o[(s, h)], nw_ref[...]) * gate).astype(BF16)

    @pl.when(c == pl.num_programs(1) - 1)
    def _():
        for h in range(HGRN_HEADS):
            s_out_ref[0, h] = st_scr[h].T


HGRN_SUBCHUNKS = 2


def _hgrn_prompt(mix, lg, norm_w, mconst, batch, seq):
    rows = HGRN_SUBCHUNKS * CHUNK
    assert seq % rows == 0
    nc = seq // rows

    def col(off):
        return pl.BlockSpec((rows, D_HGRN), lambda b, c: (b * nc + c, off // D_HGRN))

    return pl.pallas_call(
        _hgrn_prompt_kernel,
        grid=(batch, nc),
        in_specs=[
            col(OFF_Q), col(OFF_F), col(OFF_I), col(OFF_G), col(0),
            pl.BlockSpec((1, HGRN_DV), lambda b, c: (0, 0)),
            pl.BlockSpec(mconst.shape, lambda b, c: (0, 0)),
        ],
        out_specs=[
            pl.BlockSpec((rows, D_HGRN), lambda b, c: (b * nc + c, 0)),
            pl.BlockSpec((1, HGRN_HEADS, HGRN_DK, HGRN_DV), lambda b, c: (b, 0, 0, 0)),
        ],
        out_shape=[
            jax.ShapeDtypeStruct((batch * seq, D_HGRN), BF16),
            jax.ShapeDtypeStruct((batch, HGRN_HEADS, HGRN_DK, HGRN_DV), F32),
        ],
        scratch_shapes=[pltpu.VMEM((HGRN_HEADS, HGRN_DV, HGRN_DK), F32)],
        compiler_params=_cp(("arbitrary", "arbitrary")),
        name="hgrn_prompt",
    )(mix, mix, mix, mix, lg, norm_w, mconst)


def _hgrn_step_kernel(q_ref, i_ref, g_ref, lg_ref, nw_ref, s_ref,
                      o_ref, s_out_ref, o_scr):
    nb = q_ref.shape[0]
    qb = q_ref[...]
    f_t = jnp.exp(lg_ref[...]).T
    v = i_ref[...].astype(F32)
    lhs_rows = 2 * SUBLANES
    for t in range(nb):
        v_row = v[t:t + 1, :]
        s_new = v_row + f_t[:, t:t + 1] * (s_ref[t, 0] - v_row)
        s_out_ref[t, 0] = s_new
        q_rows = jnp.broadcast_to(qb[t:t + 1, :], (lhs_rows, HGRN_DK))
        o_scr[t:t + 1, :] = _dot(q_rows, s_new.astype(BF16))[0:1, :]
    o_ref[...] = (_rms(o_scr[...], nw_ref[...]) * g_ref[...].astype(F32)).astype(BF16)


def _hgrn_step(mix, lg, norm_w, state):
    nb = state.shape[0]
    hb = lambda off: off // HGRN_DK

    def col(off):
        return pl.BlockSpec((nb, HGRN_DK), lambda h: (0, hb(off) + h))

    st_spec = pl.BlockSpec((nb, 1, HGRN_DK, HGRN_DV), lambda h: (0, h, 0, 0))
    return pl.pallas_call(
        _hgrn_step_kernel,
        grid=(HGRN_HEADS,),
        in_specs=[
            col(OFF_Q), col(OFF_I), col(OFF_G), col(0),
            pl.BlockSpec((1, HGRN_DV), lambda h: (0, 0)),
            st_spec,
        ],
        out_specs=[pl.BlockSpec((nb, HGRN_DV), lambda h: (0, h)), st_spec],
        out_shape=[
            jax.ShapeDtypeStruct((nb, D_HGRN), BF16),
            jax.ShapeDtypeStruct(state.shape, F32),
        ],
        scratch_shapes=[pltpu.VMEM((nb, HGRN_DV), F32)],
        compiler_params=_cp(("arbitrary",)),
        name="hgrn_step",
    )(mix, mix, mix, lg, norm_w, state)


def _head_expand(width=SSM_HEAD_DIM):
    e = np.zeros((LANES, SSM_HEADS * width), np.float32)
    for h in range(SSM_HEADS):
        e[h, h * width:(h + 1) * width] = 1.0
    return e


def _softplus(x):
    return jnp.maximum(x, 0.0) + jnp.log(1.0 + jnp.exp(-jnp.abs(x)))


def _ssm_gate_norm(y, z_gate, nw):
    y = y * z_gate.astype(F32)
    parts = [_rms(y[:, g * GROUP_W:(g + 1) * GROUP_W], nw[:, g * GROUP_W:(g + 1) * GROUP_W])
             for g in range(SSM_GROUPS)]
    return jnp.concatenate(parts, axis=-1)


def _ssd_prompt_kernel(z_ref, xs_ref, bc_ref, dt_ref, cw_ref, cb_ref, alog_ref,
                       dvec_ref, nw_ref, tri_ref, exp_ref,
                       shift_ref, y_ref, st_out_ref, xprev_scr, st_scr):
    c = pl.program_id(1)
    t = CHUNK

    @pl.when(c == 0)
    def _():
        st_scr[...] = jnp.zeros_like(st_scr)
        xprev_scr[...] = jnp.zeros_like(xprev_scr)

    subs = range(xs_ref.shape[0] // t)
    rs = [slice(s * t, (s + 1) * t) for s in subs]

    x_cur = [jnp.concatenate([xs_ref[r, :], bc_ref[r, :]], axis=-1) for r in rs]
    x_prev = [xprev_scr[...]] + x_cur[:-1]
    xprev_scr[...] = x_cur[-1]
    taps = [_dot(shift_ref[...], jnp.concatenate([x_prev[s], x_cur[s]], axis=0)) for s in subs]
    xbc = []
    for s in subs:
        acc = cb_ref[...] + cw_ref[SSM_CONV - 1:SSM_CONV, :] * x_cur[s].astype(F32)
        for d in range(1, SSM_CONV):
            acc = acc + cw_ref[SSM_CONV - 1 - d:SSM_CONV - d, :] * taps[s][(d - 1) * t:d * t, :]
        xbc.append(_silu(acc))
    xs = [x[:, 0:D_SSM] for x in xbc]

    dt = [dt_ref[r, :] for r in rs]
    neg_a = -LOG2E * jnp.exp(alog_ref[...])
    cs = [_dot_exact_lhs01(tri_ref[...], dt[s] * neg_a) for s in subs]
    parts = []
    for s in subs:
        parts += list(_split3(dt[s])) + list(_split3(cs[s]))
    wide = _dot(jnp.concatenate(parts, axis=0), exp_ref[...])
    blk = lambda i: wide[i * t:(i + 1) * t, :]
    dt_full = [blk(6 * s) + blk(6 * s + 1) + blk(6 * s + 2) for s in subs]
    cs_full = [blk(6 * s + 3) + blk(6 * s + 4) + blk(6 * s + 5) for s in subs]
    cs_last_full = [x[t - 1:t, :] for x in cs_full]
    x_dt = [xs[s] * dt_full[s] for s in subs]
    x_end = [(x_dt[s] * jnp.exp2(cs_last_full[s] - cs_full[s])).astype(BF16) for s in subs]
    cs_t = [x.T for x in cs]

    causal = (lax.broadcasted_iota(jnp.int32, (t, t), 0)
              >= lax.broadcasted_iota(jnp.int32, (t, t), 1))
    lane = lax.broadcasted_iota(jnp.int32, (1, D_SSM), 1)
    odd_head = (lane & SSM_HEAD_DIM) != 0
    x_by_parity = []
    for s in subs:
        x_b = x_dt[s].astype(BF16)
        zero = jnp.zeros_like(x_b)
        x_by_parity.append((jnp.where(odd_head, zero, x_b), jnp.where(odd_head, x_b, zero)))
    heads_per_group = SSM_HEADS // SSM_GROUPS
    pair_w = 2 * SSM_HEAD_DIM
    never = -1e30

    def group_bc(s, g):
        b_g = xbc[s][:, D_SSM + g * SSM_STATE:D_SSM + (g + 1) * SSM_STATE]
        c_off = D_SSM + SSM_GROUPS * SSM_STATE + g * SSM_STATE
        return b_g, xbc[s][:, c_off:c_off + SSM_STATE].astype(BF16)

    y_off = []
    for s in subs:
        offs = []
        for g in range(SSM_GROUPS):
            sl = slice(g * GROUP_W, (g + 1) * GROUP_W)
            b_g, c_g = group_bc(s, g)
            st_g = st_scr[:, sl]
            offs.append(_dot(c_g, st_g.astype(BF16)))
            st_scr[:, sl] = (st_g * jnp.exp2(cs_last_full[s][:, sl])
                             + _dot(b_g.T.astype(BF16), x_end[s][:, sl]))
        y_off.append(jnp.concatenate(offs, axis=-1) * jnp.exp2(cs_full[s]))

    y_diag = []
    for s in subs:
        y_parts = []
        for g in range(SSM_GROUPS):
            b_g, c_g = group_bc(s, g)
            gmat = _dot_nt(c_g, b_g.astype(BF16))
            for pp in range(heads_per_group // 2):
                h0 = g * heads_per_group + 2 * pp
                psl = slice(h0 * SSM_HEAD_DIM, h0 * SSM_HEAD_DIM + pair_w)
                yp = None
                for sub in range(2):
                    h = h0 + sub
                    diff = cs[s][:, h:h + 1] - cs_t[s][h:h + 1, :]
                    w = jnp.exp2(jnp.where(causal, diff, never)) * gmat
                    part = _dot(w.astype(BF16), x_by_parity[s][sub][:, psl])
                    yp = part if yp is None else yp + part
                y_parts.append(yp)
        y_diag.append(jnp.concatenate(y_parts, axis=-1))

    for s in subs:
        y = y_diag[s] + y_off[s] + dvec_ref[...] * xs[s]
        y_ref[rs[s], :] = _ssm_gate_norm(y, z_ref[rs[s], :], nw_ref[...]).astype(BF16)

    @pl.when(c == pl.num_programs(1) - 1)
    def _():
        for j in range(D_SSM // LANES):
            st_out_ref[0, j * LANES:(j + 1) * LANES, :] = st_scr[:, j * LANES:(j + 1) * LANES].T


SSD_SUBCHUNKS = 1


def _ssd_prompt(mix, dt, conv_w, conv_b, a_log, d_full, norm_w, tri, expand, batch, seq):
    rows = SSD_SUBCHUNKS * CHUNK
    assert seq % rows == 0
    nc = seq // rows
    const = lambda shape: pl.BlockSpec(shape, lambda b, c: (0, 0))
    return pl.pallas_call(
        _ssd_prompt_kernel,
        grid=(batch, nc),
        in_specs=[
            pl.BlockSpec((rows, D_SSM), lambda b, c: (b * nc + c, OFF_Z // D_SSM)),
            pl.BlockSpec((rows, D_SSM), lambda b, c: (b * nc + c, OFF_XS // D_SSM)),
            pl.BlockSpec((rows, 512), lambda b, c: (b * nc + c, OFF_BC // 512)),
            pl.BlockSpec((rows, LANES), lambda b, c: (b * nc + c, 0)),
            const((SSM_CONV, CONV_DIM)), const((1, CONV_DIM)),
            const((1, LANES)),
            const((1, D_SSM)), const((1, D_SSM)),
            const((CHUNK, CHUNK)), const((LANES, D_SSM)),
            const(((SSM_CONV - 1) * CHUNK, 2 * CHUNK)),
        ],
        out_specs=[
            pl.BlockSpec((rows, D_SSM), lambda b, c: (b * nc + c, 0)),
            pl.BlockSpec((1, D_SSM, SSM_STATE), lambda b, c: (b, 0, 0)),
        ],
        out_shape=[
            jax.ShapeDtypeStruct((batch * seq, D_SSM), BF16),
            jax.ShapeDtypeStruct((batch, D_SSM, SSM_STATE), F32),
        ],
        scratch_shapes=[
            pltpu.VMEM((CHUNK, CONV_DIM), BF16),
            pltpu.VMEM((SSM_STATE, D_SSM), F32),
        ],
        compiler_params=_cp(("arbitrary", "arbitrary")),
        name="ssd_prompt",
    )(mix, mix, mix, dt, conv_w, conv_b, a_log, d_full, norm_w, tri, expand,
      jnp.asarray(_conv_shifts(), BF16))


def _conv_shifts():
    m = np.zeros(((SSM_CONV - 1) * CHUNK, 2 * CHUNK), np.float32)
    for d in range(1, SSM_CONV):
        for t in range(CHUNK):
            m[(d - 1) * CHUNK + t, CHUNK + t - d] = 1.0
    return m


def _ssd_step_kernel(z_ref, xs_ref, bc_ref, dt_ref, b0_ref, b1_ref, b2_ref, cw_ref, cb_ref,
                     alog_ref, dvec_ref, nw_ref, exp_ref, exl_ref, st_ref,
                     y_ref, st_out_ref, xt_scr, at_scr, xs_scr, bc_scr, y_scr):
    p = pl.program_id(0)
    nb = z_ref.shape[0]
    pair_w = 2 * SSM_HEAD_DIM
    pairs_per_group = SSM_HEADS // SSM_GROUPS // 2

    @pl.when(p == 0)
    def _():
        x_new = jnp.concatenate([xs_ref[...], bc_ref[...]], axis=-1).astype(F32)
        acc = (cb_ref[...] + cw_ref[0:1, :] * b0_ref[...] + cw_ref[1:2, :] * b1_ref[...]
               + cw_ref[2:3, :] * b2_ref[...] + cw_ref[3:4, :] * x_new)
        xbc = _silu(acc)
        xs = xbc[:, 0:D_SSM]
        dt = dt_ref[...]
        da = dt * (-jnp.exp(alog_ref[...]))
        ex = exp_ref[...]
        x_dt = xs * _dot_exact_rhs01(dt, ex)
        decay = jnp.exp(_dot_exact_rhs01(da, exl_ref[...]))
        xs_scr[...] = xs
        bc_scr[...] = xbc[:, D_SSM:]
        for j in range(D_SSM // LANES):
            sl = slice(j * LANES, (j + 1) * LANES)
            xt_scr[sl, :] = x_dt[:, sl].T
            at_scr[j] = decay[:, 2 * j * LANES:2 * (j + 1) * LANES]

    g_is_1 = p >= pairs_per_group
    row0 = pl.multiple_of(p * pair_w, pair_w)
    x_t = xt_scr[pl.ds(row0, pair_w), :]
    a_p = at_scr[p]
    bc = bc_scr[...]
    b_all = jnp.where(g_is_1, bc[:, SSM_STATE:2 * SSM_STATE], bc[:, 0:SSM_STATE])
    c_all = jnp.where(g_is_1, bc[:, 3 * SSM_STATE:4 * SSM_STATE],
                      bc[:, 2 * SSM_STATE:3 * SSM_STATE]).astype(BF16)
    for t in range(nb):
        inject = x_t[:, t:t + 1] * b_all[t:t + 1, :]
        halves = []
        for sub in range(2):
            rows = slice(sub * SSM_HEAD_DIM, (sub + 1) * SSM_HEAD_DIM)
            half = a_p[t:t + 1, sub * LANES:(sub + 1) * LANES] * st_ref[t, sub] + inject[rows]
            st_out_ref[t, sub] = half
            halves.append(half)
        new = jnp.concatenate(halves, axis=0)
        c_rows = jnp.broadcast_to(c_all[t:t + 1, :], (SUBLANES, SSM_STATE))
        y_scr[p, t:t + 1, :] = _dot_nt(c_rows, new.astype(BF16))[0:1, :]

    @pl.when(p == pl.num_programs(0) - 1)
    def _():
        y_mix = jnp.concatenate([y_scr[j] for j in range(SSM_HEADS // 2)], axis=-1)
        y = y_mix + dvec_ref[...] * xs_scr[...]
        y_ref[...] = _ssm_gate_norm(y, z_ref[...], nw_ref[...]).astype(BF16)


def _ssd_step(mix, dt, buf, conv_w, conv_b, a_log, d_full, norm_w, expand, state):
    nb = state.shape[0]
    n_pairs = SSM_HEADS // 2
    const = lambda shape: pl.BlockSpec(shape, lambda p: (0, 0))
    st_spec = pl.BlockSpec((nb, 2, SSM_HEAD_DIM, SSM_STATE), lambda p: (0, p, 0, 0))
    return pl.pallas_call(
        _ssd_step_kernel,
        grid=(n_pairs,),
        in_specs=[
            pl.BlockSpec((nb, D_SSM), lambda p: (0, OFF_Z // D_SSM)),
            pl.BlockSpec((nb, D_SSM), lambda p: (0, OFF_XS // D_SSM)),
            pl.BlockSpec((nb, 512), lambda p: (0, OFF_BC // 512)),
            const((nb, LANES)),
            const((nb, CONV_DIM)), const((nb, CONV_DIM)), const((nb, CONV_DIM)),
            const((SSM_CONV, CONV_DIM)), const((1, CONV_DIM)),
            const((1, LANES)),
            const((1, D_SSM)), const((1, D_SSM)),
            const((LANES, D_SSM)), const((LANES, SSM_HEADS * LANES)),
            st_spec,
        ],
        out_specs=[const((nb, D_SSM)), st_spec],
        out_shape=[
            jax.ShapeDtypeStruct((nb, D_SSM), BF16),
            jax.ShapeDtypeStruct(state.shape, F32),
        ],
        scratch_shapes=[
            pltpu.VMEM((D_SSM, nb), F32),
            pltpu.VMEM((n_pairs, nb, 2 * LANES), F32),
            pltpu.VMEM((nb, D_SSM), F32),
            pltpu.VMEM((nb, 2 * SSM_GROUPS * SSM_STATE), F32),
            pltpu.VMEM((n_pairs, nb, 2 * SSM_HEAD_DIM), F32),
        ],
        compiler_params=_cp(("arbitrary",)),
        name="ssd_step",
    )(mix, mix, mix, dt, buf[:, 0], buf[:, 1], buf[:, 2], conv_w, conv_b, a_log,
      d_full, norm_w, expand, jnp.asarray(_head_expand(LANES), BF16), state)


def _outproj_kernel(oa_ref, ys_ref, x_ref, wa_ref, ws_ref, nw_ref, x1_ref, h2_ref):
    x1 = (x_ref[...] + _dot(oa_ref[...].astype(BF16), wa_ref[...])
          + _dot(ys_ref[...].astype(BF16), ws_ref[...]))
    x1_ref[...] = x1
    h2_ref[...] = _rms(x1, nw_ref[...]).astype(BF16)


def _outproj(o_a, y_s, x2d, w_a, w_s, norm_w, tm):
    n = x2d.shape[0]
    row = lambda w: pl.BlockSpec((tm, w), lambda i: (i, 0))
    const = lambda shape: pl.BlockSpec(shape, lambda i: (0, 0))
    return pl.pallas_call(
        _outproj_kernel,
        grid=(n // tm,),
        in_specs=[row(D_HGRN), row(D_SSM), row(D_MODEL),
                  const((D_HGRN, D_MODEL)), const((D_SSM, D_MODEL)), const((1, D_MODEL))],
        out_specs=[row(D_MODEL), row(D_MODEL)],
        out_shape=[jax.ShapeDtypeStruct((n, D_MODEL), F32),
                   jax.ShapeDtypeStruct((n, D_MODEL), BF16)],
        compiler_params=_cp(("arbitrary",)),
        name="outproj",
    )(o_a, y_s, x2d, w_a, w_s, norm_w)


FF_BLOCK = 256


def _ffn_finish(j, contrib, x1_ref, fnw_ref, y_ref, acc_scr):
    @pl.when(j == 0)
    def _():
        acc_scr[...] = contrib

    @pl.when(j > 0)
    def _():
        acc_scr[...] = acc_scr[...] + contrib

    @pl.when(j == pl.num_programs(1) - 1)
    def _():
        y_ref[...] = _rms(x1_ref[...] + acc_scr[...], fnw_ref[...])


def _ffn_prompt_kernel(oa_ref, ys_ref, x_ref, wa_ref, ws_ref, n2_ref, wup_ref, wd_ref, cw_ref, cb_ref,
                       fnw_ref, y_ref, tail_ref, ge_scr, *, tiles_per_seq):
    i = pl.program_id(0)
    tm = x_ref.shape[0]
    pad = SUBLANES
    x1 = x_ref[...] + _dot(oa_ref[...], wa_ref[...]) + _dot(ys_ref[...], ws_ref[...])
    h2 = _rms(x1, n2_ref[...]).astype(BF16)

    seq_start = lax.rem(i, tiles_per_seq) == 0

    @pl.when(seq_start)
    def _():
        ge_scr[0:pad, :] = jnp.zeros((pad, D_FF), F32)

    @pl.when(jnp.logical_not(seq_start))
    def _():
        ge_scr[0:pad, :] = ge_scr[tm:tm + pad, :]

    acc = None
    bounds = np.cumsum((0,) + FFN_COL_BLOCKS)
    for c0, c1 in zip(bounds[:-1].tolist(), bounds[1:].tolist()):
        gate = _dot(h2, wup_ref[:, c0:c1])
        val = _dot(h2, wup_ref[:, D_FF + c0:D_FF + c1])
        ge_scr[pad:, c0:c1] = gate
        tail_ref[0, :, c0:c1] = gate[tm - pad:, :]
        conv = (cb_ref[:, c0:c1] + cw_ref[2:3, c0:c1] * gate
                + cw_ref[1:2, c0:c1] * ge_scr[pad - 1:pad - 1 + tm, c0:c1]
                + cw_ref[0:1, c0:c1] * ge_scr[pad - 2:pad - 2 + tm, c0:c1])
        act = (_silu(conv) * val).astype(BF16)
        part = _dot(act, wd_ref[c0:c1, :])
        acc = part if acc is None else acc + part
    y_ref[...] = _rms(x1 + acc, fnw_ref[...])


FFN_ROW_TILE = 512
FFN_COL_BLOCKS = (1024, 1024, 768)
assert sum(FFN_COL_BLOCKS) == D_FF and all(c % LANES == 0 for c in FFN_COL_BLOCKS)


def _ffn_prompt(o_a, y_s, x2d, w_a, w_s, norm2_w, w_up, w_down, conv_w, conv_b, fnorm_w, seq):
    n = x2d.shape[0]
    tm = FFN_ROW_TILE
    assert seq % tm == 0
    kern = functools.partial(_ffn_prompt_kernel, tiles_per_seq=seq // tm)
    row = lambda w: pl.BlockSpec((tm, w), lambda i: (i, 0))
    resident = lambda shape: pl.BlockSpec(shape, lambda i: (0, 0), pipeline_mode=pl.Buffered(1))
    return pl.pallas_call(
        kern,
        grid=(n // tm,),
        in_specs=[
            row(D_HGRN), row(D_SSM), row(D_MODEL),
            resident((D_HGRN, D_MODEL)), resident((D_SSM, D_MODEL)), resident((1, D_MODEL)),
            resident((D_MODEL, 2 * D_FF)), resident((D_FF, D_MODEL)),
            resident((FFN_CONV, D_FF)), resident((1, D_FF)), resident((1, D_MODEL)),
        ],
        out_specs=[
            row(D_MODEL),
            pl.BlockSpec((1, SUBLANES, D_FF), lambda i: (i, 0, 0)),
        ],
        out_shape=[
            jax.ShapeDtypeStruct((n, D_MODEL), F32),
            jax.ShapeDtypeStruct((n // tm, SUBLANES, D_FF), F32),
        ],
        scratch_shapes=[pltpu.VMEM((tm + SUBLANES, D_FF), F32)],
        compiler_params=_cp(("arbitrary",)),
        name="ffn_prompt",
    )(o_a, y_s, x2d, w_a, w_s, norm2_w, w_up, w_down, conv_w, conv_b, fnorm_w)


def _ffn_step_kernel(h2_ref, x1_ref, wg_ref, wv_ref, wd_ref, cw_ref, cb_ref, fnw_ref,
                     b0_ref, b1_ref, y_ref, gate_ref, acc_scr):
    j = pl.program_id(1)
    h2 = h2_ref[...]
    gate = _dot(h2, wg_ref[...])
    val = _dot(h2, wv_ref[...])
    gate_ref[...] = gate
    conv = (cb_ref[...] + cw_ref[2:3, :] * gate + cw_ref[1:2, :] * b1_ref[...]
            + cw_ref[0:1, :] * b0_ref[...])
    act = (_silu(conv) * val).astype(BF16)
    _ffn_finish(j, _dot(act, wd_ref[...]), x1_ref, fnw_ref, y_ref, acc_scr)


def _ffn_step(h2, x1, w_up, w_down, conv_w, conv_b, fnorm_w, buf):
    n = h2.shape[0]
    nj = D_FF // FF_BLOCK
    return pl.pallas_call(
        _ffn_step_kernel,
        grid=(1, nj),
        in_specs=[
            pl.BlockSpec((n, D_MODEL), lambda i, j: (0, 0)),
            pl.BlockSpec((n, D_MODEL), lambda i, j: (0, 0)),
            pl.BlockSpec((D_MODEL, FF_BLOCK), lambda i, j: (0, j)),
            pl.BlockSpec((D_MODEL, FF_BLOCK), lambda i, j: (0, nj + j)),
            pl.BlockSpec((FF_BLOCK, D_MODEL), lambda i, j: (j, 0)),
            pl.BlockSpec((FFN_CONV, FF_BLOCK), lambda i, j: (0, j)),
            pl.BlockSpec((1, FF_BLOCK), lambda i, j: (0, j)),
            pl.BlockSpec((1, D_MODEL), lambda i, j: (0, 0)),
            pl.BlockSpec((n, FF_BLOCK), lambda i, j: (0, j)),
            pl.BlockSpec((n, FF_BLOCK), lambda i, j: (0, j)),
        ],
        out_specs=[
            pl.BlockSpec((n, D_MODEL), lambda i, j: (0, 0)),
            pl.BlockSpec((n, FF_BLOCK), lambda i, j: (0, j)),
        ],
        out_shape=[
            jax.ShapeDtypeStruct((n, D_MODEL), F32),
            jax.ShapeDtypeStruct((n, D_FF), F32),
        ],
        scratch_shapes=[pltpu.VMEM((n, D_MODEL), F32)],
        compiler_params=_cp(("arbitrary", "arbitrary")),
        name="ffn_step",
    )(h2, x1, w_up, w_up, w_down, conv_w, conv_b, fnorm_w, buf[:, 0], buf[:, 1])


def _row(v):
    return v.reshape(1, -1).astype(F32)


def _pad_lanes(v):
    return jnp.pad(v.astype(F32), (0, LANES - v.shape[0])).reshape(1, LANES)


def kernel(x_prompt, x_sample, state_hgrn, state_ssm, state_conv_ssm, state_conv_ffn, norm1_w, w_in, hgrn_lb, hgrn_norm_w, ssm_conv_w, ssm_conv_b, ssm_dt_bias, ssm_a_log, ssm_d, ssm_norm_w, w_out, norm2_w, w_up, ffn_conv_w, ffn_conv_b, w_down, final_norm_w):
    depth = w_in.shape[0]
    assert depth == 1, "single-layer trunk"
    l = 0
    batch, seq, _ = x_prompt.shape
    dec_batch, dec_seq, _ = x_sample.shape
    assert dec_seq == 1 and seq % CHUNK == 0 and seq >= SSM_CONV

    w_main = w_in[l].astype(BF16)
    w_dt = jnp.pad(w_in[l][:, D_MAIN:], ((0, 0), (0, LANES - SSM_HEADS))).astype(BF16)
    w_oa = w_out[l][:D_HGRN].astype(BF16)
    w_os = w_out[l][D_HGRN:].astype(BF16)
    w_upb = w_up[l].astype(BF16)
    w_dnb = w_down[l].astype(BF16)
    d_full = jnp.repeat(ssm_d[l].astype(F32), SSM_HEAD_DIM).reshape(1, D_SSM)
    dt_bias = _pad_lanes(ssm_dt_bias[l])
    a_log = _pad_lanes(ssm_a_log[l])
    mconst = jnp.asarray(_hgrn_const(), BF16)
    tri = jnp.asarray(np.tril(np.ones((CHUNK, CHUNK), np.float32)), BF16)
    expand = jnp.asarray(_head_expand(), BF16)
    lb_raw = hgrn_lb.astype(F32)

    xp = x_prompt.reshape(batch * seq, D_MODEL)
    proj_p, lg_p, dt_p = _inproj(xp, _row(norm1_w[l]), w_main, w_dt, lb_raw, dt_bias)
    oa_p, hgrn_p = _hgrn_prompt(proj_p, lg_p, _row(hgrn_norm_w[l]), mconst, batch, seq)
    ys_p, ssm_p = _ssd_prompt(proj_p, dt_p, ssm_conv_w[l], _row(ssm_conv_b[l]), a_log,
                              d_full, _row(ssm_norm_w[l]), tri, expand, batch, seq)
    y_p, tail_p = _ffn_prompt(oa_p, ys_p, xp, w_oa, w_os, _row(norm2_w[l]), w_upb, w_dnb,
                              ffn_conv_w[l], _row(ffn_conv_b[l]), _row(final_norm_w), seq)
    proj_p3 = proj_p.reshape(batch, seq, D_MAIN)
    cs_p = proj_p3[:, seq - (SSM_CONV - 1):, OFF_XS:OFF_XS + CONV_DIM]
    tails = tail_p.reshape(batch, seq // FFN_ROW_TILE, SUBLANES, D_FF)
    cf_p = tails[:, -1, SUBLANES - (FFN_CONV - 1):, :]

    xs_ = x_sample.reshape(dec_batch, D_MODEL)
    proj_s, lg_s, dt_s = _inproj(xs_, _row(norm1_w[l]), w_main, w_dt, lb_raw, dt_bias)
    oa_s, hgrn_s = _hgrn_step(proj_s, lg_s, _row(hgrn_norm_w[l]), state_hgrn[l])
    ys_s, ssm_s = _ssd_step(proj_s, dt_s, state_conv_ssm[l], ssm_conv_w[l], _row(ssm_conv_b[l]),
                            a_log, d_full, _row(ssm_norm_w[l]), expand, state_ssm[l])
    x1_s, h2_s = _outproj(oa_s, ys_s, xs_, w_oa, w_os, _row(norm2_w[l]), dec_batch)
    y_s, gate_s = _ffn_step(h2_s, x1_s, w_upb, w_dnb, ffn_conv_w[l], _row(ffn_conv_b[l]),
                            _row(final_norm_w), state_conv_ffn[l])
    cs_s = jnp.concatenate([state_conv_ssm[l][:, 1:], proj_s[:, None, OFF_XS:OFF_XS + CONV_DIM]],
                           axis=1)
    cf_s = jnp.concatenate([state_conv_ffn[l][:, 1:], gate_s[:, None, :]], axis=1)

    dt_ = x_prompt.dtype
    return (y_p.reshape(batch, seq, D_MODEL).astype(dt_),
            y_s.reshape(dec_batch, 1, D_MODEL).astype(dt_),
            hgrn_p[None].astype(dt_),
            hgrn_s[None].astype(dt_),
            ssm_p.reshape(1, batch, SSM_HEADS, SSM_HEAD_DIM, SSM_STATE).astype(dt_),
            ssm_s[None].astype(dt_),
            cs_p[None].astype(dt_),
            cs_s[None].astype(dt_),
            cf_p[None].astype(dt_),
            cf_s[None].astype(dt_))
```

```python
import functools

import numpy as np
import jax
import jax.numpy as jnp
from jax import lax
from jax.experimental import pallas as pl
from jax.experimental.pallas import tpu as pltpu

F32 = jnp.float32
BF16 = jnp.bfloat16
EPS = 1e-6

LANES = 128
SUBLANES = 8

D_MODEL = 1024
HGRN_HEADS = 8
HGRN_DK = 128
HGRN_DV = 128
D_HGRN = HGRN_HEADS * HGRN_DV
SSM_HEADS = 16
SSM_HEAD_DIM = 64
D_SSM = SSM_HEADS * SSM_HEAD_DIM
SSM_STATE = 128
SSM_GROUPS = 2
SSM_CONV = 4
CONV_DIM = D_SSM + 2 * SSM_GROUPS * SSM_STATE
D_FF = 2816
FFN_CONV = 3
D_MAIN = 4 * D_HGRN + D_SSM + CONV_DIM
OFF_Q, OFF_F, OFF_I, OFF_G = 0, 1024, 2048, 3072
OFF_Z, OFF_XS, OFF_BC = 4096, 5120, 6144

CHUNK = 128
GROUP_W = D_SSM // SSM_GROUPS
VMEM_LIMIT = 56 * 1024 * 1024


def _cp(sem):
    return pltpu.CompilerParams(dimension_semantics=sem, vmem_limit_bytes=VMEM_LIMIT)


def _dot(a, b):
    return jnp.dot(a, b, preferred_element_type=F32)


def _dot_nt(a, b):
    return lax.dot_general(a, b, (((1,), (1,)), ((), ())), preferred_element_type=F32)


def _split3(x):
    h = x.astype(BF16)
    r = x - h.astype(F32)
    m = r.astype(BF16)
    lo = (r - m.astype(F32)).astype(BF16)
    return h, m, lo


def _dot_exact_lhs01(m01, x):
    h, m, lo = _split3(x)
    return _dot(m01, h) + _dot(m01, m) + _dot(m01, lo)


def _dot_exact_rhs01(x, m01):
    h, m, lo = _split3(x)
    return _dot(h, m01) + _dot(m, m01) + _dot(lo, m01)


def _dot_split_lhs01(m01, x):
    h = x.astype(BF16)
    lo = (x - h.astype(F32)).astype(BF16)
    return _dot(m01, h) + _dot(m01, lo)


def _sigmoid(x):
    return 1.0 / (1.0 + jnp.exp(-x))


def _silu(x):
    return x * _sigmoid(x)


def _rms(x, w):
    ms = jnp.mean(x * x, axis=-1, keepdims=True)
    return x * lax.rsqrt(ms + EPS) * w


def _inproj_kernel(x_ref, nw_ref, w_ref, wdt_ref, lb_ref, dtb_ref, mix_ref, lg_ref, dt_ref):
    hb = _rms(x_ref[...], nw_ref[...]).astype(BF16)

    def put(off, val):
        mix_ref[:, off:off + val.shape[1]] = val.astype(BF16)

    w = D_HGRN
    qf = _dot(hb, w_ref[:, OFF_Q:OFF_Q + 2 * w])
    lb = _hgrn_lb(lb_ref[...])
    f = lb + (1.0 - lb) * _sigmoid(qf[:, w:])
    lg_ref[...] = jnp.log(f)
    put(OFF_F, 1.0 - f)
    put(OFF_Q, _silu(qf[:, :w]))
    ig = _dot(hb, w_ref[:, OFF_I:OFF_I + 2 * w])
    put(OFF_I, ig[:, :w])
    put(OFF_G, _silu(ig[:, w:]))
    zx = _dot(hb, w_ref[:, OFF_Z:OFF_Z + D_SSM + CONV_DIM])
    put(OFF_Z, _silu(zx[:, :D_SSM]))
    put(OFF_XS, zx[:, D_SSM:])
    dt_ref[...] = _softplus(_dot(hb, wdt_ref[...]) + dtb_ref[...])


INPROJ_ROW_TILE = 512


def _inproj(x2d, norm_w, w_main, w_dt, lb_raw, dt_bias):
    n = x2d.shape[0]
    tm = min(INPROJ_ROW_TILE, n)
    assert n % tm == 0
    row = lambda w: pl.BlockSpec((tm, w), lambda i: (i, 0))
    resident = lambda shape: pl.BlockSpec(shape, lambda i: (0, 0), pipeline_mode=pl.Buffered(1))
    return pl.pallas_call(
        _inproj_kernel,
        grid=(n // tm,),
        in_specs=[
            row(D_MODEL), resident((1, D_MODEL)),
            resident((D_MODEL, D_MAIN)), resident((D_MODEL, LANES)),
            resident(lb_raw.shape), resident((1, LANES)),
        ],
        out_specs=[row(D_MAIN), row(D_HGRN), row(LANES)],
        out_shape=[
            jax.ShapeDtypeStruct((n, D_MAIN), BF16),
            jax.ShapeDtypeStruct((n, D_HGRN), F32),
            jax.ShapeDtypeStruct((n, LANES), F32),
        ],
        compiler_params=_cp(("arbitrary",)),
        name="inproj",
    )(x2d, norm_w, w_main, w_dt, lb_raw, dt_bias)


LOG2E = 1.4426950408889634
N_LEVELS = 7
MXU_LEVEL_HALVES = (4, 2)


def _hgrn_const():
    c = CHUNK
    t = np.arange(c)[:, None]
    j = np.arange(c)[None, :]
    blocks = [(j <= t)]
    for h in MXU_LEVEL_HALVES:
        mid = (t // (2 * h)) * (2 * h) + h
        upper = (t >= mid) & (j >= mid) & (j <= t)
        lower = (t < mid) & (j > t) & (j < mid)
        blocks.append(upper | lower)
    return np.concatenate(blocks, axis=0).astype(np.float32)


def _midpoint_decay(b, h):
    pieces = []
    for start in range(0, CHUNK, 2 * h):
        mid = start + h
        m = b[mid - 1:mid, :]
        pieces.append(m - b[start:mid])
        pieces.append(b[mid:mid + h] - m)
    return jnp.concatenate(pieces, axis=0)


def _mix_rows(q, k, h):
    pieces = []
    for start in range(0, CHUNK, 2 * h):
        pieces.append(k[start:start + h])
        pieces.append(q[start + h:start + 2 * h])
    return jnp.concatenate(pieces, axis=0)


def _hgrn_lb(lb_raw):
    mx = jnp.max(lb_raw, axis=0, keepdims=True)
    e = jnp.exp(lb_raw - mx)
    return e[0:1, :] / jnp.sum(e, axis=0, keepdims=True)


def _level_map():
    t = lax.broadcasted_iota(jnp.int32, (CHUNK, CHUNK), 0)
    s = lax.broadcasted_iota(jnp.int32, (CHUNK, CHUNK), 1)
    bitlen = 32 - lax.clz(t ^ s)
    return jnp.where(t > s, bitlen, jnp.where(t == s, 0, -1))


def _hgrn_prompt_kernel(q_ref, k_ref, i_ref, g_ref, lg_ref, nw_ref, mc_ref,
                        o_ref, s_out_ref, st_scr):
    c = pl.program_id(1)

    @pl.when(c == 0)
    def _():
        st_scr[...] = jnp.zeros_like(st_scr)

    n_sub = q_ref.shape[0] // CHUNK
    lev = _level_map().astype(jnp.int16)
    row16 = lax.broadcasted_iota(jnp.int32, (CHUNK, HGRN_DK), 0).astype(jnp.int16)
    heads = range(HGRN_HEADS)
    pairs = [(s, h) for s in range(n_sub) for h in heads]
    rs = {s: slice(s * CHUNK, (s + 1) * CHUNK) for s in range(n_sub)}
    cs = {h: slice(h * HGRN_DK, (h + 1) * HGRN_DK) for h in heads}

    e_sub = {s: _dot_split_lhs01(mc_ref[...], lg_ref[rs[s], :] * LOG2E) for s in range(n_sub)}
    qb = {(s, h): q_ref[rs[s], cs[h]] for s, h in pairs}
    kb = {(s, h): k_ref[rs[s], cs[h]] for s, h in pairs}
    vb = {(s, h): i_ref[rs[s], cs[h]] for s, h in pairs}
    q = {p: qb[p].astype(F32) for p in pairs}
    k = {p: kb[p].astype(F32) for p in pairs}
    b = {(s, h): e_sub[s][0:CHUNK, cs[h]] for s, h in pairs}
    b_last = {p: b[p][CHUNK - 1:CHUNK, :] for p in pairs}

    st = {h: st_scr[h] for h in heads}
    o = {}
    for s, h in pairs:
        p = (s, h)
        o[p] = _dot((q[p] * jnp.exp2(b[p])).astype(BF16), st[h].T.astype(BF16))
        ks = (k[p] * jnp.exp2(b_last[p] - b[p])).astype(BF16)
        st[h] = st[h] * jnp.exp2(b_last[p]) + _dot(vb[p].astype(F32).T.astype(BF16), ks)
    for h in heads:
        st_scr[h] = st[h]

    a = {p: jnp.where(lev == 0, _dot(qb[p], k[p].T.astype(BF16)).astype(BF16), jnp.zeros((), BF16))
         for p in pairs}
    half = CHUNK // 2
    while half >= 1:
        for p in pairs:
            if half >= 2 * SUBLANES:
                w = jnp.exp2(_midpoint_decay(b[p], half)).astype(BF16)
                xb = _mix_rows(qb[p], kb[p], half) * w
            else:
                upper = (row16 & half) != 0
                if half == SUBLANES:
                    w = jnp.exp2(_midpoint_decay(b[p], half)).astype(BF16)
                    xb = jnp.where(upper, qb[p], kb[p]) * w
                elif half in MXU_LEVEL_HALVES:
                    blk = 1 + MXU_LEVEL_HALVES.index(half)
                    w = jnp.exp2(e_sub[p[0]][blk * CHUNK:(blk + 1) * CHUNK, cs[p[1]]]).astype(BF16)
                    xb = jnp.where(upper, qb[p], kb[p]) * w
                else:
                    xb = jnp.where(upper, qb[p] * (1.0 - kb[p]), kb[p])
            gram = _dot(xb, xb.T)
            a[p] = jnp.where(lev == half.bit_length(), gram.astype(BF16), a[p])
        half //= 2

    for p in pairs:
        o[p] = o[p] + _dot(a[p], vb[p])
    for s, h in pairs:
        gate = g_ref[rs[s], cs[h]].astype(F32)
        o_ref[rs[s], cs[h]] = (_rms(o[(s, h)], nw_ref[...]) * gate).astype(BF16)

    @pl.when(c == pl.num_programs(1) - 1)
    def _():
        for h in range(HGRN_HEADS):
            s_out_ref[0, h] = st_scr[h].T


HGRN_SUBCHUNKS = 2


def _hgrn_prompt(mix, lg, norm_w, mconst, batch, seq):
    rows = HGRN_SUBCHUNKS * CHUNK
    assert seq % rows == 0
    nc = seq // rows

    def col(off):
        return pl.BlockSpec((rows, D_HGRN), lambda b, c: (b * nc + c, off // D_HGRN))

    return pl.pallas_call(
        _hgrn_prompt_kernel,
        grid=(batch, nc),
        in_specs=[
            col(OFF_Q), col(OFF_F), col(OFF_I), col(OFF_G), col(0),
            pl.BlockSpec((1, HGRN_DV), lambda b, c: (0, 0)),
            pl.BlockSpec(mconst.shape, lambda b, c: (0, 0)),
        ],
        out_specs=[
            pl.BlockSpec((rows, D_HGRN), lambda b, c: (b * nc + c, 0)),
            pl.BlockSpec((1, HGRN_HEADS, HGRN_DK, HGRN_DV), lambda b, c: (b, 0, 0, 0)),
        ],
        out_shape=[
            jax.ShapeDtypeStruct((batch * seq, D_HGRN), BF16),
            jax.ShapeDtypeStruct((batch, HGRN_HEADS, HGRN_DK, HGRN_DV), F32),
        ],
        scratch_shapes=[pltpu.VMEM((HGRN_HEADS, HGRN_DV, HGRN_DK), F32)],
        compiler_params=_cp(("arbitrary", "arbitrary")),
        name="hgrn_prompt",
    )(mix, mix, mix, mix, lg, norm_w, mconst)


def _hgrn_step_kernel(q_ref, i_ref, g_ref, lg_ref, nw_ref, s_ref,
                      o_ref, s_out_ref, o_scr):
    nb = q_ref.shape[0]
    qb = q_ref[...]
    f_t = jnp.exp(lg_ref[...]).T
    v = i_ref[...].astype(F32)
    lhs_rows = 2 * SUBLANES
    for t in range(nb):
        v_row = v[t:t + 1, :]
        s_new = v_row + f_t[:, t:t + 1] * (s_ref[t, 0] - v_row)
        s_out_ref[t, 0] = s_new
        q_rows = jnp.broadcast_to(qb[t:t + 1, :], (lhs_rows, HGRN_DK))
        o_scr[t:t + 1, :] = _dot(q_rows, s_new.astype(BF16))[0:1, :]
    o_ref[...] = (_rms(o_scr[...], nw_ref[...]) * g_ref[...].astype(F32)).astype(BF16)


def _hgrn_step(mix, lg, norm_w, state):
    nb = state.shape[0]
    hb = lambda off: off // HGRN_DK

    def col(off):
        return pl.BlockSpec((nb, HGRN_DK), lambda h: (0, hb(off) + h))

    st_spec = pl.BlockSpec((nb, 1, HGRN_DK, HGRN_DV), lambda h: (0, h, 0, 0))
    return pl.pallas_call(
        _hgrn_step_kernel,
        grid=(HGRN_HEADS,),
        in_specs=[
            col(OFF_Q), col(OFF_I), col(OFF_G), col(0),
            pl.BlockSpec((1, HGRN_DV), lambda h: (0, 0)),
            st_spec,
        ],
        out_specs=[pl.BlockSpec((nb, HGRN_DV), lambda h: (0, h)), st_spec],
        out_shape=[
            jax.ShapeDtypeStruct((nb, D_HGRN), BF16),
            jax.ShapeDtypeStruct(state.shape, F32),
        ],
        scratch_shapes=[pltpu.VMEM((nb, HGRN_DV), F32)],
        compiler_params=_cp(("arbitrary",)),
        name="hgrn_step",
    )(mix, mix, mix, lg, norm_w, state)


def _head_expand(width=SSM_HEAD_DIM):
    e = np.zeros((LANES, SSM_HEADS * width), np.float32)
    for h in range(SSM_HEADS):
        e[h, h * width:(h + 1) * width] = 1.0
    return e


def _softplus(x):
    return jnp.maximum(x, 0.0) + jnp.log(1.0 + jnp.exp(-jnp.abs(x)))


def _ssm_gate_norm(y, z_gate, nw):
    y = y * z_gate.astype(F32)
    parts = [_rms(y[:, g * GROUP_W:(g + 1) * GROUP_W], nw[:, g * GROUP_W:(g + 1) * GROUP_W])
             for g in range(SSM_GROUPS)]
    return jnp.concatenate(parts, axis=-1)


def _ssd_prompt_kernel(z_ref, xs_ref, bc_ref, dt_ref, cw_ref, cb_ref, alog_ref,
                       dvec_ref, nw_ref, tri_ref, exp_ref,
                       shift_ref, y_ref, st_out_ref, xprev_scr, st_scr):
    c = pl.program_id(1)
    t = CHUNK

    @pl.when(c == 0)
    def _():
        st_scr[...] = jnp.zeros_like(st_scr)
        xprev_scr[...] = jnp.zeros_like(xprev_scr)

    subs = range(xs_ref.shape[0] // t)
    rs = [slice(s * t, (s + 1) * t) for s in subs]

    x_cur = [jnp.concatenate([xs_ref[r, :], bc_ref[r, :]], axis=-1) for r in rs]
    x_prev = [xprev_scr[...]] + x_cur[:-1]
    xprev_scr[...] = x_cur[-1]
    taps = [_dot(shift_ref[...], jnp.concatenate([x_prev[s], x_cur[s]], axis=0)) for s in subs]
    xbc = []
    for s in subs:
        acc = cb_ref[...] + cw_ref[SSM_CONV - 1:SSM_CONV, :] * x_cur[s].astype(F32)
        for d in range(1, SSM_CONV):
            acc = acc + cw_ref[SSM_CONV - 1 - d:SSM_CONV - d, :] * taps[s][(d - 1) * t:d * t, :]
        xbc.append(_silu(acc))
    xs = [x[:, 0:D_SSM] for x in xbc]

    dt = [dt_ref[r, :] for r in rs]
    neg_a = -LOG2E * jnp.exp(alog_ref[...])
    cs = [_dot_exact_lhs01(tri_ref[...], dt[s] * neg_a) for s in subs]
    ex = exp_ref[...]
    dt_full = [_dot_exact_rhs01(dt[s], ex) for s in subs]
    cs_full = [_dot_exact_rhs01(cs[s], ex) for s in subs]
    cs_last_full = [x[t - 1:t, :] for x in cs_full]
    x_dt = [xs[s] * dt_full[s] for s in subs]
    x_end = [(x_dt[s] * jnp.exp2(cs_last_full[s] - cs_full[s])).astype(BF16) for s in subs]
    cs_t = [x.T for x in cs]

    causal = (lax.broadcasted_iota(jnp.int32, (t, t), 0)
              >= lax.broadcasted_iota(jnp.int32, (t, t), 1))
    lane = lax.broadcasted_iota(jnp.int32, (1, D_SSM), 1)
    odd_head = (lane & SSM_HEAD_DIM) != 0
    x_by_parity = []
    for s in subs:
        x_b = x_dt[s].astype(BF16)
        zero = jnp.zeros_like(x_b)
        x_by_parity.append((jnp.where(odd_head, zero, x_b), jnp.where(odd_head, x_b, zero)))
    heads_per_group = SSM_HEADS // SSM_GROUPS
    pair_w = 2 * SSM_HEAD_DIM
    never = -1e30

    def group_bc(s, g):
        b_g = xbc[s][:, D_SSM + g * SSM_STATE:D_SSM + (g + 1) * SSM_STATE]
        c_off = D_SSM + SSM_GROUPS * SSM_STATE + g * SSM_STATE
        return b_g.T.astype(BF16), xbc[s][:, c_off:c_off + SSM_STATE].astype(BF16)

    bc_t = {(s, g): group_bc(s, g) for s in subs for g in range(SSM_GROUPS)}

    y_diag = []
    for s in subs:
        y_parts = []
        for g in range(SSM_GROUPS):
            b_t, c_g = bc_t[(s, g)]
            gmat = _dot(c_g, b_t)
            for pp in range(heads_per_group // 2):
                h0 = g * heads_per_group + 2 * pp
                psl = slice(h0 * SSM_HEAD_DIM, h0 * SSM_HEAD_DIM + pair_w)
                yp = None
                for sub in range(2):
                    h = h0 + sub
                    diff = cs[s][:, h:h + 1] - cs_t[s][h:h + 1, :]
                    w = jnp.exp2(jnp.where(causal, diff, never)) * gmat
                    part = _dot(w.astype(BF16), x_by_parity[s][sub][:, psl])
                    yp = part if yp is None else yp + part
                y_parts.append(yp)
        y_diag.append(jnp.concatenate(y_parts, axis=-1))

    y_off = []
    for s in subs:
        offs = []
        for g in range(SSM_GROUPS):
            sl = slice(g * GROUP_W, (g + 1) * GROUP_W)
            b_t, c_g = bc_t[(s, g)]
            st_g = st_scr[:, sl]
            offs.append(_dot(c_g, st_g.astype(BF16)))
            st_scr[:, sl] = st_g * jnp.exp2(cs_last_full[s][:, sl]) + _dot(b_t, x_end[s][:, sl])
        y_off.append(jnp.concatenate(offs, axis=-1) * jnp.exp2(cs_full[s]))

    for s in subs:
        y = y_diag[s] + y_off[s] + dvec_ref[...] * xs[s]
        y_ref[rs[s], :] = _ssm_gate_norm(y, z_ref[rs[s], :], nw_ref[...]).astype(BF16)

    @pl.when(c == pl.num_programs(1) - 1)
    def _():
        for j in range(D_SSM // LANES):
            st_out_ref[0, j * LANES:(j + 1) * LANES, :] = st_scr[:, j * LANES:(j + 1) * LANES].T


SSD_SUBCHUNKS = 1


def _ssd_prompt(mix, dt, conv_w, conv_b, a_log, d_full, norm_w, tri, expand, batch, seq):
    rows = SSD_SUBCHUNKS * CHUNK
    assert seq % rows == 0
    nc = seq // rows
    const = lambda shape: pl.BlockSpec(shape, lambda b, c: (0, 0))
    return pl.pallas_call(
        _ssd_prompt_kernel,
        grid=(batch, nc),
        in_specs=[
            pl.BlockSpec((rows, D_SSM), lambda b, c: (b * nc + c, OFF_Z // D_SSM)),
            pl.BlockSpec((rows, D_SSM), lambda b, c: (b * nc + c, OFF_XS // D_SSM)),
            pl.BlockSpec((rows, 512), lambda b, c: (b * nc + c, OFF_BC // 512)),
            pl.BlockSpec((rows, LANES), lambda b, c: (b * nc + c, 0)),
            const((SSM_CONV, CONV_DIM)), const((1, CONV_DIM)),
            const((1, LANES)),
            const((1, D_SSM)), const((1, D_SSM)),
            const((CHUNK, CHUNK)), const((LANES, D_SSM)),
            const(((SSM_CONV - 1) * CHUNK, 2 * CHUNK)),
        ],
        out_specs=[
            pl.BlockSpec((rows, D_SSM), lambda b, c: (b * nc + c, 0)),
            pl.BlockSpec((1, D_SSM, SSM_STATE), lambda b, c: (b, 0, 0)),
        ],
        out_shape=[
            jax.ShapeDtypeStruct((batch * seq, D_SSM), BF16),
            jax.ShapeDtypeStruct((batch, D_SSM, SSM_STATE), F32),
        ],
        scratch_shapes=[
            pltpu.VMEM((CHUNK, CONV_DIM), BF16),
            pltpu.VMEM((SSM_STATE, D_SSM), F32),
        ],
        compiler_params=_cp(("arbitrary", "arbitrary")),
        name="ssd_prompt",
    )(mix, mix, mix, dt, conv_w, conv_b, a_log, d_full, norm_w, tri, expand,
      jnp.asarray(_conv_shifts(), BF16))


def _conv_shifts():
    m = np.zeros(((SSM_CONV - 1) * CHUNK, 2 * CHUNK), np.float32)
    for d in range(1, SSM_CONV):
        for t in range(CHUNK):
            m[(d - 1) * CHUNK + t, CHUNK + t - d] = 1.0
    return m


def _ssd_step_kernel(z_ref, xs_ref, bc_ref, dt_ref, b0_ref, b1_ref, b2_ref, cw_ref, cb_ref,
                     alog_ref, dvec_ref, nw_ref, exp_ref, exl_ref, st_ref,
                     y_ref, st_out_ref, xt_scr, at_scr, xs_scr, bc_scr, y_scr):
    p = pl.program_id(0)
    nb = z_ref.shape[0]
    pair_w = 2 * SSM_HEAD_DIM
    pairs_per_group = SSM_HEADS // SSM_GROUPS // 2

    @pl.when(p == 0)
    def _():
        x_new = jnp.concatenate([xs_ref[...], bc_ref[...]], axis=-1).astype(F32)
        acc = (cb_ref[...] + cw_ref[0:1, :] * b0_ref[...] + cw_ref[1:2, :] * b1_ref[...]
               + cw_ref[2:3, :] * b2_ref[...] + cw_ref[3:4, :] * x_new)
        xbc = _silu(acc)
        xs = xbc[:, 0:D_SSM]
        dt = dt_ref[...]
        da = dt * (-jnp.exp(alog_ref[...]))
        ex = exp_ref[...]
        x_dt = xs * _dot_exact_rhs01(dt, ex)
        decay = jnp.exp(_dot_exact_rhs01(da, exl_ref[...]))
        xs_scr[...] = xs
        bc_scr[...] = xbc[:, D_SSM:]
        for j in range(D_SSM // LANES):
            sl = slice(j * LANES, (j + 1) * LANES)
            xt_scr[sl, :] = x_dt[:, sl].T
            at_scr[j] = decay[:, 2 * j * LANES:2 * (j + 1) * LANES]

    g_is_1 = p >= pairs_per_group
    row0 = pl.multiple_of(p * pair_w, pair_w)
    x_t = xt_scr[pl.ds(row0, pair_w), :]
    a_p = at_scr[p]
    bc = bc_scr[...]
    b_all = jnp.where(g_is_1, bc[:, SSM_STATE:2 * SSM_STATE], bc[:, 0:SSM_STATE])
    c_all = jnp.where(g_is_1, bc[:, 3 * SSM_STATE:4 * SSM_STATE],
                      bc[:, 2 * SSM_STATE:3 * SSM_STATE]).astype(BF16)
    for t in range(nb):
        inject = x_t[:, t:t + 1] * b_all[t:t + 1, :]
        halves = []
        for sub in range(2):
            rows = slice(sub * SSM_HEAD_DIM, (sub + 1) * SSM_HEAD_DIM)
            half = a_p[t:t + 1, sub * LANES:(sub + 1) * LANES] * st_ref[t, sub] + inject[rows]
            st_out_ref[t, sub] = half
            halves.append(half)
        new = jnp.concatenate(halves, axis=0)
        c_rows = jnp.broadcast_to(c_all[t:t + 1, :], (SUBLANES, SSM_STATE))
        y_scr[p, t:t + 1, :] = _dot_nt(c_rows, new.astype(BF16))[0:1, :]

    @pl.when(p == pl.num_programs(0) - 1)
    def _():
        y_mix = jnp.concatenate([y_scr[j] for j in range(SSM_HEADS // 2)], axis=-1)
        y = y_mix + dvec_ref[...] * xs_scr[...]
        y_ref[...] = _ssm_gate_norm(y, z_ref[...], nw_ref[...]).astype(BF16)


def _ssd_step(mix, dt, buf, conv_w, conv_b, a_log, d_full, norm_w, expand, state):
    nb = state.shape[0]
    n_pairs = SSM_HEADS // 2
    const = lambda shape: pl.BlockSpec(shape, lambda p: (0, 0))
    st_spec = pl.BlockSpec((nb, 2, SSM_HEAD_DIM, SSM_STATE), lambda p: (0, p, 0, 0))
    return pl.pallas_call(
        _ssd_step_kernel,
        grid=(n_pairs,),
        in_specs=[
            pl.BlockSpec((nb, D_SSM), lambda p: (0, OFF_Z // D_SSM)),
            pl.BlockSpec((nb, D_SSM), lambda p: (0, OFF_XS // D_SSM)),
            pl.BlockSpec((nb, 512), lambda p: (0, OFF_BC // 512)),
            const((nb, LANES)),
            const((nb, CONV_DIM)), const((nb, CONV_DIM)), const((nb, CONV_DIM)),
            const((SSM_CONV, CONV_DIM)), const((1, CONV_DIM)),
            const((1, LANES)),
            const((1, D_SSM)), const((1, D_SSM)),
            const((LANES, D_SSM)), const((LANES, SSM_HEADS * LANES)),
            st_spec,
        ],
        out_specs=[const((nb, D_SSM)), st_spec],
        out_shape=[
            jax.ShapeDtypeStruct((nb, D_SSM), BF16),
            jax.ShapeDtypeStruct(state.shape, F32),
        ],
        scratch_shapes=[
            pltpu.VMEM((D_SSM, nb), F32),
            pltpu.VMEM((n_pairs, nb, 2 * LANES), F32),
            pltpu.VMEM((nb, D_SSM), F32),
            pltpu.VMEM((nb, 2 * SSM_GROUPS * SSM_STATE), F32),
            pltpu.VMEM((n_pairs, nb, 2 * SSM_HEAD_DIM), F32),
        ],
        compiler_params=_cp(("arbitrary",)),
        name="ssd_step",
    )(mix, mix, mix, dt, buf[:, 0], buf[:, 1], buf[:, 2], conv_w, conv_b, a_log,
      d_full, norm_w, expand, jnp.asarray(_head_expand(LANES), BF16), state)


def _outproj_kernel(oa_ref, ys_ref, x_ref, wa_ref, ws_ref, nw_ref, x1_ref, h2_ref):
    x1 = (x_ref[...] + _dot(oa_ref[...].astype(BF16), wa_ref[...])
          + _dot(ys_ref[...].astype(BF16), ws_ref[...]))
    x1_ref[...] = x1
    h2_ref[...] = _rms(x1, nw_ref[...]).astype(BF16)


def _outproj(o_a, y_s, x2d, w_a, w_s, norm_w, tm):
    n = x2d.shape[0]
    row = lambda w: pl.BlockSpec((tm, w), lambda i: (i, 0))
    const = lambda shape: pl.BlockSpec(shape, lambda i: (0, 0))
    return pl.pallas_call(
        _outproj_kernel,
        grid=(n // tm,),
        in_specs=[row(D_HGRN), row(D_SSM), row(D_MODEL),
                  const((D_HGRN, D_MODEL)), const((D_SSM, D_MODEL)), const((1, D_MODEL))],
        out_specs=[row(D_MODEL), row(D_MODEL)],
        out_shape=[jax.ShapeDtypeStruct((n, D_MODEL), F32),
                   jax.ShapeDtypeStruct((n, D_MODEL), BF16)],
        compiler_params=_cp(("arbitrary",)),
        name="outproj",
    )(o_a, y_s, x2d, w_a, w_s, norm_w)


FF_BLOCK = 256


def _ffn_finish(j, contrib, x1_ref, fnw_ref, y_ref, acc_scr):
    @pl.when(j == 0)
    def _():
        acc_scr[...] = contrib

    @pl.when(j > 0)
    def _():
        acc_scr[...] = acc_scr[...] + contrib

    @pl.when(j == pl.num_programs(1) - 1)
    def _():
        y_ref[...] = _rms(x1_ref[...] + acc_scr[...], fnw_ref[...])


def _ffn_prompt_kernel(oa_ref, ys_ref, x_ref, wa_ref, ws_ref, n2_ref, wup_ref, wd_ref, cw_ref, cb_ref,
                       fnw_ref, y_ref, tail_ref, ge_scr, *, tiles_per_seq):
    i = pl.program_id(0)
    tm = x_ref.shape[0]
    pad = SUBLANES
    x1 = x_ref[...] + _dot(oa_ref[...], wa_ref[...]) + _dot(ys_ref[...], ws_ref[...])
    h2 = _rms(x1, n2_ref[...]).astype(BF16)

    seq_start = lax.rem(i, tiles_per_seq) == 0

    @pl.when(seq_start)
    def _():
        ge_scr[0:pad, :] = jnp.zeros((pad, D_FF), F32)

    @pl.when(jnp.logical_not(seq_start))
    def _():
        ge_scr[0:pad, :] = ge_scr[tm:tm + pad, :]

    acc = None
    bounds = np.cumsum((0,) + FFN_COL_BLOCKS)
    for c0, c1 in zip(bounds[:-1].tolist(), bounds[1:].tolist()):
        gate = _dot(h2, wup_ref[:, c0:c1])
        val = _dot(h2, wup_ref[:, D_FF + c0:D_FF + c1])
        ge_scr[pad:, c0:c1] = gate
        tail_ref[0, :, c0:c1] = gate[tm - pad:, :]
        conv = (cb_ref[:, c0:c1] + cw_ref[2:3, c0:c1] * gate
                + cw_ref[1:2, c0:c1] * ge_scr[pad - 1:pad - 1 + tm, c0:c1]
                + cw_ref[0:1, c0:c1] * ge_scr[pad - 2:pad - 2 + tm, c0:c1])
        act = (_silu(conv) * val).astype(BF16)
        part = _dot(act, wd_ref[c0:c1, :])
        acc = part if acc is None else acc + part
    y_ref[...] = _rms(x1 + acc, fnw_ref[...])


FFN_ROW_TILE = 512
FFN_COL_BLOCKS = (1024, 1024, 768)
assert sum(FFN_COL_BLOCKS) == D_FF and all(c % LANES == 0 for c in FFN_COL_BLOCKS)


def _ffn_prompt(o_a, y_s, x2d, w_a, w_s, norm2_w, w_up, w_down, conv_w, conv_b, fnorm_w, seq):
    n = x2d.shape[0]
    tm = FFN_ROW_TILE
    assert seq % tm == 0
    kern = functools.partial(_ffn_prompt_kernel, tiles_per_seq=seq // tm)
    row = lambda w: pl.BlockSpec((tm, w), lambda i: (i, 0))
    resident = lambda shape: pl.BlockSpec(shape, lambda i: (0, 0), pipeline_mode=pl.Buffered(1))
    return pl.pallas_call(
        kern,
        grid=(n // tm,),
        in_specs=[
            row(D_HGRN), row(D_SSM), row(D_MODEL),
            resident((D_HGRN, D_MODEL)), resident((D_SSM, D_MODEL)), resident((1, D_MODEL)),
            resident((D_MODEL, 2 * D_FF)), resident((D_FF, D_MODEL)),
            resident((FFN_CONV, D_FF)), resident((1, D_FF)), resident((1, D_MODEL)),
        ],
        out_specs=[
            row(D_MODEL),
            pl.BlockSpec((1, SUBLANES, D_FF), lambda i: (i, 0, 0)),
        ],
        out_shape=[
            jax.ShapeDtypeStruct((n, D_MODEL), F32),
            jax.ShapeDtypeStruct((n // tm, SUBLANES, D_FF), F32),
        ],
        scratch_shapes=[pltpu.VMEM((tm + SUBLANES, D_FF), F32)],
        compiler_params=_cp(("arbitrary",)),
        name="ffn_prompt",
    )(o_a, y_s, x2d, w_a, w_s, norm2_w, w_up, w_down, conv_w, conv_b, fnorm_w)


def _ffn_step_kernel(h2_ref, x1_ref, wg_ref, wv_ref, wd_ref, cw_ref, cb_ref, fnw_ref,
                     b0_ref, b1_ref, y_ref, gate_ref, acc_scr):
    j = pl.program_id(1)
    h2 = h2_ref[...]
    gate = _dot(h2, wg_ref[...])
    val = _dot(h2, wv_ref[...])
    gate_ref[...] = gate
    conv = (cb_ref[...] + cw_ref[2:3, :] * gate + cw_ref[1:2, :] * b1_ref[...]
            + cw_ref[0:1, :] * b0_ref[...])
    act = (_silu(conv) * val).astype(BF16)
    _ffn_finish(j, _dot(act, wd_ref[...]), x1_ref, fnw_ref, y_ref, acc_scr)


def _ffn_step(h2, x1, w_up, w_down, conv_w, conv_b, fnorm_w, buf):
    n = h2.shape[0]
    nj = D_FF // FF_BLOCK
    return pl.pallas_call(
        _ffn_step_kernel,
        grid=(1, nj),
        in_specs=[
            pl.BlockSpec((n, D_MODEL), lambda i, j: (0, 0)),
            pl.BlockSpec((n, D_MODEL), lambda i, j: (0, 0)),
            pl.BlockSpec((D_MODEL, FF_BLOCK), lambda i, j: (0, j)),
            pl.BlockSpec((D_MODEL, FF_BLOCK), lambda i, j: (0, nj + j)),
            pl.BlockSpec((FF_BLOCK, D_MODEL), lambda i, j: (j, 0)),
            pl.BlockSpec((FFN_CONV, FF_BLOCK), lambda i, j: (0, j)),
            pl.BlockSpec((1, FF_BLOCK), lambda i, j: (0, j)),
            pl.BlockSpec((1, D_MODEL), lambda i, j: (0, 0)),
            pl.BlockSpec((n, FF_BLOCK), lambda i, j: (0, j)),
            pl.BlockSpec((n, FF_BLOCK), lambda i, j: (0, j)),
        ],
        out_specs=[
            pl.BlockSpec((n, D_MODEL), lambda i, j: (0, 0)),
            pl.BlockSpec((n, FF_BLOCK), lambda i, j: (0, j)),
        ],
        out_shape=[
            jax.ShapeDtypeStruct((n, D_MODEL), F32),
            jax.ShapeDtypeStruct((n, D_FF), F32),
        ],
        scratch_shapes=[pltpu.VMEM((n, D_MODEL), F32)],
        compiler_params=_cp(("arbitrary", "arbitrary")),
        name="ffn_step",
    )(h2, x1, w_up, w_up, w_down, conv_w, conv_b, fnorm_w, buf[:, 0], buf[:, 1])


def _row(v):
    return v.reshape(1, -1).astype(F32)


def _pad_lanes(v):
    return jnp.pad(v.astype(F32), (0, LANES - v.shape[0])).reshape(1, LANES)


def kernel(x_prompt, x_sample, state_hgrn, state_ssm, state_conv_ssm, state_conv_ffn, norm1_w, w_in, hgrn_lb, hgrn_norm_w, ssm_conv_w, ssm_conv_b, ssm_dt_bias, ssm_a_log, ssm_d, ssm_norm_w, w_out, norm2_w, w_up, ffn_conv_w, ffn_conv_b, w_down, final_norm_w):
    depth = w_in.shape[0]
    assert depth == 1, "single-layer trunk"
    l = 0
    batch, seq, _ = x_prompt.shape
    dec_batch, dec_seq, _ = x_sample.shape
    assert dec_seq == 1 and seq % CHUNK == 0 and seq >= SSM_CONV

    w_main = w_in[l].astype(BF16)
    w_dt = jnp.pad(w_in[l][:, D_MAIN:], ((0, 0), (0, LANES - SSM_HEADS))).astype(BF16)
    w_oa = w_out[l][:D_HGRN].astype(BF16)
    w_os = w_out[l][D_HGRN:].astype(BF16)
    w_upb = w_up[l].astype(BF16)
    w_dnb = w_down[l].astype(BF16)
    d_full = jnp.repeat(ssm_d[l].astype(F32), SSM_HEAD_DIM).reshape(1, D_SSM)
    dt_bias = _pad_lanes(ssm_dt_bias[l])
    a_log = _pad_lanes(ssm_a_log[l])
    mconst = jnp.asarray(_hgrn_const(), BF16)
    tri = jnp.asarray(np.tril(np.ones((CHUNK, CHUNK), np.float32)), BF16)
    expand = jnp.asarray(_head_expand(), BF16)
    lb_raw = hgrn_lb.astype(F32)

    xp = x_prompt.reshape(batch * seq, D_MODEL)
    proj_p, lg_p, dt_p = _inproj(xp, _row(norm1_w[l]), w_main, w_dt, lb_raw, dt_bias)
    oa_p, hgrn_p = _hgrn_prompt(proj_p, lg_p, _row(hgrn_norm_w[l]), mconst, batch, seq)
    ys_p, ssm_p = _ssd_prompt(proj_p, dt_p, ssm_conv_w[l], _row(ssm_conv_b[l]), a_log,
                              d_full, _row(ssm_norm_w[l]), tri, expand, batch, seq)
    y_p, tail_p = _ffn_prompt(oa_p, ys_p, xp, w_oa, w_os, _row(norm2_w[l]), w_upb, w_dnb,
                              ffn_conv_w[l], _row(ffn_conv_b[l]), _row(final_norm_w), seq)
    proj_p3 = proj_p.reshape(batch, seq, D_MAIN)
    cs_p = proj_p3[:, seq - (SSM_CONV - 1):, OFF_XS:OFF_XS + CONV_DIM]
    tails = tail_p.reshape(batch, seq // FFN_ROW_TILE, SUBLANES, D_FF)
    cf_p = tails[:, -1, SUBLANES - (FFN_CONV - 1):, :]

    xs_ = x_sample.reshape(dec_batch, D_MODEL)
    proj_s, lg_s, dt_s = _inproj(xs_, _row(norm1_w[l]), w_main, w_dt, lb_raw, dt_bias)
    oa_s, hgrn_s = _hgrn_step(proj_s, lg_s, _row(hgrn_norm_w[l]), state_hgrn[l])
    ys_s, ssm_s = _ssd_step(proj_s, dt_s, state_conv_ssm[l], ssm_conv_w[l], _row(ssm_conv_b[l]),
                            a_log, d_full, _row(ssm_norm_w[l]), expand, state_ssm[l])
    x1_s, h2_s = _outproj(oa_s, ys_s, xs_, w_oa, w_os, _row(norm2_w[l]), dec_batch)
    y_s, gate_s = _ffn_step(h2_s, x1_s, w_upb, w_dnb, ffn_conv_w[l], _row(ffn_conv_b[l]),
                            _row(final_norm_w), state_conv_ffn[l])
    cs_s = jnp.concatenate([state_conv_ssm[l][:, 1:], proj_s[:, None, OFF_XS:OFF_XS + CONV_DIM]],
                           axis=1)
    cf_s = jnp.concatenate([state_conv_ffn[l][:, 1:], gate_s[:, None, :]], axis=1)

    dt_ = x_prompt.dtype
    return (y_p.reshape(batch, seq, D_MODEL).astype(dt_),
            y_s.reshape(dec_batch, 1, D_MODEL).astype(dt_),
            hgrn_p[None].astype(dt_),
            hgrn_s[None].astype(dt_),
            ssm_p.reshape(1, batch, SSM_HEADS, SSM_HEAD_DIM, SSM_STATE).astype(dt_),
            ssm_s[None].astype(dt_),
            cs_p[None].astype(dt_),
            cs_s[None].astype(dt_),
            cf_p[None].astype(dt_),
            cf_s[None].astype(dt_))
```

```python
import functools

import numpy as np
import jax
import jax.numpy as jnp
from jax import lax
from jax.experimental import pallas as pl
from jax.experimental.pallas import tpu as pltpu

F32 = jnp.float32
BF16 = jnp.bfloat16
EPS = 1e-6

LANES = 128
SUBLANES = 8

D_MODEL = 1024
HGRN_HEADS = 8
HGRN_DK = 128
HGRN_DV = 128
D_HGRN = HGRN_HEADS * HGRN_DV
SSM_HEADS = 16
SSM_HEAD_DIM = 64
D_SSM = SSM_HEADS * SSM_HEAD_DIM
SSM_STATE = 128
SSM_GROUPS = 2
SSM_CONV = 4
CONV_DIM = D_SSM + 2 * SSM_GROUPS * SSM_STATE
D_FF = 2816
FFN_CONV = 3
D_MAIN = 4 * D_HGRN + D_SSM + CONV_DIM
OFF_Q, OFF_F, OFF_I, OFF_G = 0, 1024, 2048, 3072
OFF_Z, OFF_XS, OFF_BC = 4096, 5120, 6144

CHUNK = 128
GROUP_W = D_SSM // SSM_GROUPS
VMEM_LIMIT = 56 * 1024 * 1024


def _cp(sem):
    return pltpu.CompilerParams(dimension_semantics=sem, vmem_limit_bytes=VMEM_LIMIT)


def _dot(a, b):
    return jnp.dot(a, b, preferred_element_type=F32)


def _dot_nt(a, b):
    return lax.dot_general(a, b, (((1,), (1,)), ((), ())), preferred_element_type=F32)


def _split3(x):
    h = x.astype(BF16)
    r = x - h.astype(F32)
    m = r.astype(BF16)
    lo = (r - m.astype(F32)).astype(BF16)
    return h, m, lo


def _dot_exact_lhs01(m01, x):
    h, m, lo = _split3(x)
    return _dot(m01, h) + _dot(m01, m) + _dot(m01, lo)


def _dot_exact_rhs01(x, m01):
    h, m, lo = _split3(x)
    return _dot(h, m01) + _dot(m, m01) + _dot(lo, m01)


def _dot_split_lhs01(m01, x):
    h = x.astype(BF16)
    lo = (x - h.astype(F32)).astype(BF16)
    return _dot(m01, h) + _dot(m01, lo)


def _sigmoid(x):
    return 1.0 / (1.0 + jnp.exp(-x))


def _silu(x):
    return x * _sigmoid(x)


def _rms(x, w):
    ms = jnp.mean(x * x, axis=-1, keepdims=True)
    return x * lax.rsqrt(ms + EPS) * w


def _inproj_kernel(x_ref, nw_ref, w_ref, wdt_ref, lb_ref, dtb_ref, mix_ref, lg_ref, dt_ref):
    hb = _rms(x_ref[...], nw_ref[...]).astype(BF16)

    def put(off, val):
        mix_ref[:, off:off + val.shape[1]] = val.astype(BF16)

    w = D_HGRN
    qf = _dot(hb, w_ref[:, OFF_Q:OFF_Q + 2 * w])
    lb = _hgrn_lb(lb_ref[...])
    f = lb + (1.0 - lb) * _sigmoid(qf[:, w:])
    lg_ref[...] = jnp.log(f)
    put(OFF_F, 1.0 - f)
    put(OFF_Q, _silu(qf[:, :w]))
    ig = _dot(hb, w_ref[:, OFF_I:OFF_I + 2 * w])
    put(OFF_I, ig[:, :w])
    put(OFF_G, _silu(ig[:, w:]))
    zx = _dot(hb, w_ref[:, OFF_Z:OFF_Z + D_SSM + CONV_DIM])
    put(OFF_Z, _silu(zx[:, :D_SSM]))
    put(OFF_XS, zx[:, D_SSM:])
    dt_ref[...] = _softplus(_dot(hb, wdt_ref[...]) + dtb_ref[...])


INPROJ_ROW_TILE = 512


def _inproj(x2d, norm_w, w_main, w_dt, lb_raw, dt_bias):
    n = x2d.shape[0]
    tm = min(INPROJ_ROW_TILE, n)
    assert n % tm == 0
    row = lambda w: pl.BlockSpec((tm, w), lambda i: (i, 0))
    resident = lambda shape: pl.BlockSpec(shape, lambda i: (0, 0), pipeline_mode=pl.Buffered(1))
    return pl.pallas_call(
        _inproj_kernel,
        grid=(n // tm,),
        in_specs=[
            row(D_MODEL), resident((1, D_MODEL)),
            resident((D_MODEL, D_MAIN)), resident((D_MODEL, LANES)),
            resident(lb_raw.shape), resident((1, LANES)),
        ],
        out_specs=[row(D_MAIN), row(D_HGRN), row(LANES)],
        out_shape=[
            jax.ShapeDtypeStruct((n, D_MAIN), BF16),
            jax.ShapeDtypeStruct((n, D_HGRN), F32),
            jax.ShapeDtypeStruct((n, LANES), F32),
        ],
        compiler_params=_cp(("arbitrary",)),
        name="inproj",
    )(x2d, norm_w, w_main, w_dt, lb_raw, dt_bias)


LOG2E = 1.4426950408889634
N_LEVELS = 7
MXU_LEVEL_HALVES = (4, 2)


def _hgrn_const():
    c = CHUNK
    t = np.arange(c)[:, None]
    j = np.arange(c)[None, :]
    blocks = [(j <= t)]
    for h in MXU_LEVEL_HALVES:
        mid = (t // (2 * h)) * (2 * h) + h
        upper = (t >= mid) & (j >= mid) & (j <= t)
        lower = (t < mid) & (j > t) & (j < mid)
        blocks.append(upper | lower)
    return np.concatenate(blocks, axis=0).astype(np.float32)


def _midpoint_decay(b, h):
    pieces = []
    for start in range(0, CHUNK, 2 * h):
        mid = start + h
        m = b[mid - 1:mid, :]
        pieces.append(m - b[start:mid])
        pieces.append(b[mid:mid + h] - m)
    return jnp.concatenate(pieces, axis=0)


def _mix_rows(q, k, h):
    pieces = []
    for start in range(0, CHUNK, 2 * h):
        pieces.append(k[start:start + h])
        pieces.append(q[start + h:start + 2 * h])
    return jnp.concatenate(pieces, axis=0)


def _hgrn_lb(lb_raw):
    mx = jnp.max(lb_raw, axis=0, keepdims=True)
    e = jnp.exp(lb_raw - mx)
    return e[0:1, :] / jnp.sum(e, axis=0, keepdims=True)


def _level_map():
    t = lax.broadcasted_iota(jnp.int32, (CHUNK, CHUNK), 0)
    s = lax.broadcasted_iota(jnp.int32, (CHUNK, CHUNK), 1)
    bitlen = 32 - lax.clz(t ^ s)
    return jnp.where(t > s, bitlen, jnp.where(t == s, 0, -1))


def _hgrn_prompt_kernel(q_ref, k_ref, i_ref, g_ref, lg_ref, nw_ref, mc_ref,
                        o_ref, s_out_ref, st_scr):
    c = pl.program_id(1)

    @pl.when(c == 0)
    def _():
        st_scr[...] = jnp.zeros_like(st_scr)

    n_sub = q_ref.shape[0] // CHUNK
    lev = _level_map().astype(jnp.int16)
    row16 = lax.broadcasted_iota(jnp.int32, (CHUNK, HGRN_DK), 0).astype(jnp.int16)
    heads = range(HGRN_HEADS)
    pairs = [(s, h) for s in range(n_sub) for h in heads]
    rs = {s: slice(s * CHUNK, (s + 1) * CHUNK) for s in range(n_sub)}
    cs = {h: slice(h * HGRN_DK, (h + 1) * HGRN_DK) for h in heads}

    e_sub = {s: _dot_split_lhs01(mc_ref[...], lg_ref[rs[s], :] * LOG2E) for s in range(n_sub)}
    qb = {(s, h): q_ref[rs[s], cs[h]] for s, h in pairs}
    kb = {(s, h): k_ref[rs[s], cs[h]] for s, h in pairs}
    vb = {(s, h): i_ref[rs[s], cs[h]] for s, h in pairs}
    q = {p: qb[p].astype(F32) for p in pairs}
    k = {p: kb[p].astype(F32) for p in pairs}
    b = {(s, h): e_sub[s][0:CHUNK, cs[h]] for s, h in pairs}
    b_last = {p: b[p][CHUNK - 1:CHUNK, :] for p in pairs}

    st = {h: st_scr[h] for h in heads}
    o = {}
    for s, h in pairs:
        p = (s, h)
        o[p] = _dot_nt((q[p] * jnp.exp2(b[p])).astype(BF16), st[h].astype(BF16))
        ks = (k[p] * jnp.exp2(b_last[p] - b[p])).astype(BF16)
        st[h] = st[h] * jnp.exp2(b_last[p]) + _dot(vb[p].astype(F32).T.astype(BF16), ks)
    for h in heads:
        st_scr[h] = st[h]

    a = {p: jnp.where(lev == 0, _dot_nt(qb[p], kb[p]).astype(BF16), jnp.zeros((), BF16))
         for p in pairs}
    half = CHUNK // 2
    while half >= 1:
        for p in pairs:
            if half >= 2 * SUBLANES:
                w = jnp.exp2(_midpoint_decay(b[p], half)).astype(BF16)
                xb = _mix_rows(qb[p], kb[p], half) * w
            else:
                upper = (row16 & half) != 0
                if half == SUBLANES:
                    w = jnp.exp2(_midpoint_decay(b[p], half)).astype(BF16)
                    xb = jnp.where(upper, qb[p], kb[p]) * w
                elif half in MXU_LEVEL_HALVES:
                    blk = 1 + MXU_LEVEL_HALVES.index(half)
                    w = jnp.exp2(e_sub[p[0]][blk * CHUNK:(blk + 1) * CHUNK, cs[p[1]]]).astype(BF16)
                    xb = jnp.where(upper, qb[p], kb[p]) * w
                else:
                    xb = jnp.where(upper, qb[p] * (1.0 - kb[p]), kb[p])
            gram = _dot(xb, xb.T)
            a[p] = jnp.where(lev == half.bit_length(), gram.astype(BF16), a[p])
        half //= 2

    for p in pairs:
        o[p] = o[p] + _dot(a[p], vb[p])
    for s, h in pairs:
        gate = g_ref[rs[s], cs[h]].astype(F32)
        o_ref[rs[s], cs[h]] = (_rms(o[(s, h)], nw_ref[...]) * gate).astype(BF16)

    @pl.when(c == pl.num_programs(1) - 1)
    def _():
        for h in range(HGRN_HEADS):
            s_out_ref[0, h] = st_scr[h].T


HGRN_SUBCHUNKS = 2


def _hgrn_prompt(mix, lg, norm_w, mconst, batch, seq):
    rows = HGRN_SUBCHUNKS * CHUNK
    assert seq % rows == 0
    nc = seq // rows

    def col(off):
        return pl.BlockSpec((rows, D_HGRN), lambda b, c: (b * nc + c, off // D_HGRN))

    return pl.pallas_call(
        _hgrn_prompt_kernel,
        grid=(batch, nc),
        in_specs=[
            col(OFF_Q), col(OFF_F), col(OFF_I), col(OFF_G), col(0),
            pl.BlockSpec((1, HGRN_DV), lambda b, c: (0, 0)),
            pl.BlockSpec(mconst.shape, lambda b, c: (0, 0)),
        ],
        out_specs=[
            pl.BlockSpec((rows, D_HGRN), lambda b, c: (b * nc + c, 0)),
            pl.BlockSpec((1, HGRN_HEADS, HGRN_DK, HGRN_DV), lambda b, c: (b, 0, 0, 0)),
        ],
        out_shape=[
            jax.ShapeDtypeStruct((batch * seq, D_HGRN), BF16),
            jax.ShapeDtypeStruct((batch, HGRN_HEADS, HGRN_DK, HGRN_DV), F32),
        ],
        scratch_shapes=[pltpu.VMEM((HGRN_HEADS, HGRN_DV, HGRN_DK), F32)],
        compiler_params=_cp(("arbitrary", "arbitrary")),
        name="hgrn_prompt",
    )(mix, mix, mix, mix, lg, norm_w, mconst)


def _hgrn_step_kernel(q_ref, i_ref, g_ref, lg_ref, nw_ref, s_ref,
                      o_ref, s_out_ref, o_scr):
    nb = q_ref.shape[0]
    qb = q_ref[...]
    f_t = jnp.exp(lg_ref[...]).T
    v = i_ref[...].astype(F32)
    lhs_rows = 2 * SUBLANES
    for t in range(nb):
        v_row = v[t:t + 1, :]
        s_new = v_row + f_t[:, t:t + 1] * (s_ref[t, 0] - v_row)
        s_out_ref[t, 0] = s_new
        q_rows = jnp.broadcast_to(qb[t:t + 1, :], (lhs_rows, HGRN_DK))
        o_scr[t:t + 1, :] = _dot(q_rows, s_new.astype(BF16))[0:1, :]
    o_ref[...] = (_rms(o_scr[...], nw_ref[...]) * g_ref[...].astype(F32)).astype(BF16)


def _hgrn_step(mix, lg, norm_w, state):
    nb = state.shape[0]
    hb = lambda off: off // HGRN_DK

    def col(off):
        return pl.BlockSpec((nb, HGRN_DK), lambda h: (0, hb(off) + h))

    st_spec = pl.BlockSpec((nb, 1, HGRN_DK, HGRN_DV), lambda h: (0, h, 0, 0))
    return pl.pallas_call(
        _hgrn_step_kernel,
        grid=(HGRN_HEADS,),
        in_specs=[
            col(OFF_Q), col(OFF_I), col(OFF_G), col(0),
            pl.BlockSpec((1, HGRN_DV), lambda h: (0, 0)),
            st_spec,
        ],
        out_specs=[pl.BlockSpec((nb, HGRN_DV), lambda h: (0, h)), st_spec],
        out_shape=[
            jax.ShapeDtypeStruct((nb, D_HGRN), BF16),
            jax.ShapeDtypeStruct(state.shape, F32),
        ],
        scratch_shapes=[pltpu.VMEM((nb, HGRN_DV), F32)],
        compiler_params=_cp(("arbitrary",)),
        name="hgrn_step",
    )(mix, mix, mix, lg, norm_w, state)


def _head_expand(width=SSM_HEAD_DIM):
    e = np.zeros((LANES, SSM_HEADS * width), np.float32)
    for h in range(SSM_HEADS):
        e[h, h * width:(h + 1) * width] = 1.0
    return e


def _softplus(x):
    return jnp.maximum(x, 0.0) + jnp.log(1.0 + jnp.exp(-jnp.abs(x)))


def _ssm_gate_norm(y, z_gate, nw):
    y = y * z_gate.astype(F32)
    parts = [_rms(y[:, g * GROUP_W:(g + 1) * GROUP_W], nw[:, g * GROUP_W:(g + 1) * GROUP_W])
             for g in range(SSM_GROUPS)]
    return jnp.concatenate(parts, axis=-1)


def _ssd_prompt_kernel(z_ref, xs_ref, bc_ref, dt_ref, cw_ref, cb_ref, alog_ref,
                       dvec_ref, nw_ref, tri_ref, exp_ref,
                       shift_ref, y_ref, st_out_ref, xprev_scr, st_scr):
    c = pl.program_id(1)
    t = CHUNK

    @pl.when(c == 0)
    def _():
        st_scr[...] = jnp.zeros_like(st_scr)
        xprev_scr[...] = jnp.zeros_like(xprev_scr)

    subs = range(xs_ref.shape[0] // t)
    rs = [slice(s * t, (s + 1) * t) for s in subs]

    x_cur = [jnp.concatenate([xs_ref[r, :], bc_ref[r, :]], axis=-1) for r in rs]
    x_prev = [xprev_scr[...]] + x_cur[:-1]
    xprev_scr[...] = x_cur[-1]
    taps = [_dot(shift_ref[...], jnp.concatenate([x_prev[s], x_cur[s]], axis=0)) for s in subs]
    xbc = []
    for s in subs:
        acc = cb_ref[...] + cw_ref[SSM_CONV - 1:SSM_CONV, :] * x_cur[s].astype(F32)
        for d in range(1, SSM_CONV):
            acc = acc + cw_ref[SSM_CONV - 1 - d:SSM_CONV - d, :] * taps[s][(d - 1) * t:d * t, :]
        xbc.append(_silu(acc))
    xs = [x[:, 0:D_SSM] for x in xbc]

    dt = [dt_ref[r, :] for r in rs]
    neg_a = -LOG2E * jnp.exp(alog_ref[...])
    cs = [_dot_exact_lhs01(tri_ref[...], dt[s] * neg_a) for s in subs]
    ex = exp_ref[...]
    dt_full = [_dot_exact_rhs01(dt[s], ex) for s in subs]
    cs_full = [_dot_exact_rhs01(cs[s], ex) for s in subs]
    cs_last_full = [x[t - 1:t, :] for x in cs_full]
    x_dt = [xs[s] * dt_full[s] for s in subs]
    x_end = [(x_dt[s] * jnp.exp2(cs_last_full[s] - cs_full[s])).astype(BF16) for s in subs]
    cs_t = [x.T for x in cs]

    causal = (lax.broadcasted_iota(jnp.int32, (t, t), 0)
              >= lax.broadcasted_iota(jnp.int32, (t, t), 1))
    lane = lax.broadcasted_iota(jnp.int32, (1, D_SSM), 1)
    odd_head = (lane & SSM_HEAD_DIM) != 0
    x_by_parity = []
    for s in subs:
        x_b = x_dt[s].astype(BF16)
        zero = jnp.zeros_like(x_b)
        x_by_parity.append((jnp.where(odd_head, zero, x_b), jnp.where(odd_head, x_b, zero)))
    heads_per_group = SSM_HEADS // SSM_GROUPS
    pair_w = 2 * SSM_HEAD_DIM
    never = -1e30

    def group_bc(s, g):
        b_g = xbc[s][:, D_SSM + g * SSM_STATE:D_SSM + (g + 1) * SSM_STATE]
        c_off = D_SSM + SSM_GROUPS * SSM_STATE + g * SSM_STATE
        return b_g.T.astype(BF16), xbc[s][:, c_off:c_off + SSM_STATE].astype(BF16)

    bc_t = {(s, g): group_bc(s, g) for s in subs for g in range(SSM_GROUPS)}

    y_diag = []
    for s in subs:
        y_parts = []
        for g in range(SSM_GROUPS):
            b_t, c_g = bc_t[(s, g)]
            gmat = _dot(c_g, b_t)
            for pp in range(heads_per_group // 2):
                h0 = g * heads_per_group + 2 * pp
                psl = slice(h0 * SSM_HEAD_DIM, h0 * SSM_HEAD_DIM + pair_w)
                yp = None
                for sub in range(2):
                    h = h0 + sub
                    diff = cs[s][:, h:h + 1] - cs_t[s][h:h + 1, :]
                    w = jnp.exp2(jnp.where(causal, diff, never)) * gmat
                    part = _dot(w.astype(BF16), x_by_parity[s][sub][:, psl])
                    yp = part if yp is None else yp + part
                y_parts.append(yp)
        y_diag.append(jnp.concatenate(y_parts, axis=-1))

    y_off = []
    for s in subs:
        offs = []
        for g in range(SSM_GROUPS):
            sl = slice(g * GROUP_W, (g + 1) * GROUP_W)
            b_t, c_g = bc_t[(s, g)]
            st_g = st_scr[:, sl]
            offs.append(_dot(c_g, st_g.astype(BF16)))
            st_scr[:, sl] = st_g * jnp.exp2(cs_last_full[s][:, sl]) + _dot(b_t, x_end[s][:, sl])
        y_off.append(jnp.concatenate(offs, axis=-1) * jnp.exp2(cs_full[s]))

    for s in subs:
        y = y_diag[s] + y_off[s] + dvec_ref[...] * xs[s]
        y_ref[rs[s], :] = _ssm_gate_norm(y, z_ref[rs[s], :], nw_ref[...]).astype(BF16)

    @pl.when(c == pl.num_programs(1) - 1)
    def _():
        for j in range(D_SSM // LANES):
            st_out_ref[0, j * LANES:(j + 1) * LANES, :] = st_scr[:, j * LANES:(j + 1) * LANES].T


SSD_SUBCHUNKS = 1


def _ssd_prompt(mix, dt, conv_w, conv_b, a_log, d_full, norm_w, tri, expand, batch, seq):
    rows = SSD_SUBCHUNKS * CHUNK
    assert seq % rows == 0
    nc = seq // rows
    const = lambda shape: pl.BlockSpec(shape, lambda b, c: (0, 0))
    return pl.pallas_call(
        _ssd_prompt_kernel,
        grid=(batch, nc),
        in_specs=[
            pl.BlockSpec((rows, D_SSM), lambda b, c: (b * nc + c, OFF_Z // D_SSM)),
            pl.BlockSpec((rows, D_SSM), lambda b, c: (b * nc + c, OFF_XS // D_SSM)),
            pl.BlockSpec((rows, 512), lambda b, c: (b * nc + c, OFF_BC // 512)),
            pl.BlockSpec((rows, LANES), lambda b, c: (b * nc + c, 0)),
            const((SSM_CONV, CONV_DIM)), const((1, CONV_DIM)),
            const((1, LANES)),
            const((1, D_SSM)), const((1, D_SSM)),
            const((CHUNK, CHUNK)), const((LANES, D_SSM)),
            const(((SSM_CONV - 1) * CHUNK, 2 * CHUNK)),
        ],
        out_specs=[
            pl.BlockSpec((rows, D_SSM), lambda b, c: (b * nc + c, 0)),
            pl.BlockSpec((1, D_SSM, SSM_STATE), lambda b, c: (b, 0, 0)),
        ],
        out_shape=[
            jax.ShapeDtypeStruct((batch * seq, D_SSM), BF16),
            jax.ShapeDtypeStruct((batch, D_SSM, SSM_STATE), F32),
        ],
        scratch_shapes=[
            pltpu.VMEM((CHUNK, CONV_DIM), BF16),
            pltpu.VMEM((SSM_STATE, D_SSM), F32),
        ],
        compiler_params=_cp(("arbitrary", "arbitrary")),
        name="ssd_prompt",
    )(mix, mix, mix, dt, conv_w, conv_b, a_log, d_full, norm_w, tri, expand,
      jnp.asarray(_conv_shifts(), BF16))


def _conv_shifts():
    m = np.zeros(((SSM_CONV - 1) * CHUNK, 2 * CHUNK), np.float32)
    for d in range(1, SSM_CONV):
        for t in range(CHUNK):
            m[(d - 1) * CHUNK + t, CHUNK + t - d] = 1.0
    return m


def _ssd_step_kernel(z_ref, xs_ref, bc_ref, dt_ref, b0_ref, b1_ref, b2_ref, cw_ref, cb_ref,
                     alog_ref, dvec_ref, nw_ref, exp_ref, exl_ref, st_ref,
                     y_ref, st_out_ref, xt_scr, at_scr, xs_scr, bc_scr, y_scr):
    p = pl.program_id(0)
    nb = z_ref.shape[0]
    pair_w = 2 * SSM_HEAD_DIM
    pairs_per_group = SSM_HEADS // SSM_GROUPS // 2

    @pl.when(p == 0)
    def _():
        x_new = jnp.concatenate([xs_ref[...], bc_ref[...]], axis=-1).astype(F32)
        acc = (cb_ref[...] + cw_ref[0:1, :] * b0_ref[...] + cw_ref[1:2, :] * b1_ref[...]
               + cw_ref[2:3, :] * b2_ref[...] + cw_ref[3:4, :] * x_new)
        xbc = _silu(acc)
        xs = xbc[:, 0:D_SSM]
        dt = dt_ref[...]
        da = dt * (-jnp.exp(alog_ref[...]))
        ex = exp_ref[...]
        x_dt = xs * _dot_exact_rhs01(dt, ex)
        decay = jnp.exp(_dot_exact_rhs01(da, exl_ref[...]))
        xs_scr[...] = xs
        bc_scr[...] = xbc[:, D_SSM:]
        for j in range(D_SSM // LANES):
            sl = slice(j * LANES, (j + 1) * LANES)
            xt_scr[sl, :] = x_dt[:, sl].T
            at_scr[j] = decay[:, 2 * j * LANES:2 * (j + 1) * LANES]

    g_is_1 = p >= pairs_per_group
    row0 = pl.multiple_of(p * pair_w, pair_w)
    x_t = xt_scr[pl.ds(row0, pair_w), :]
    a_p = at_scr[p]
    bc = bc_scr[...]
    b_all = jnp.where(g_is_1, bc[:, SSM_STATE:2 * SSM_STATE], bc[:, 0:SSM_STATE])
    c_all = jnp.where(g_is_1, bc[:, 3 * SSM_STATE:4 * SSM_STATE],
                      bc[:, 2 * SSM_STATE:3 * SSM_STATE]).astype(BF16)
    for t in range(nb):
        inject = x_t[:, t:t + 1] * b_all[t:t + 1, :]
        halves = []
        for sub in range(2):
            rows = slice(sub * SSM_HEAD_DIM, (sub + 1) * SSM_HEAD_DIM)
            half = a_p[t:t + 1, sub * LANES:(sub + 1) * LANES] * st_ref[t, sub] + inject[rows]
            st_out_ref[t, sub] = half
            halves.append(half)
        new = jnp.concatenate(halves, axis=0)
        c_rows = jnp.broadcast_to(c_all[t:t + 1, :], (SUBLANES, SSM_STATE))
        y_scr[p, t:t + 1, :] = _dot_nt(c_rows, new.astype(BF16))[0:1, :]

    @pl.when(p == pl.num_programs(0) - 1)
    def _():
        y_mix = jnp.concatenate([y_scr[j] for j in range(SSM_HEADS // 2)], axis=-1)
        y = y_mix + dvec_ref[...] * xs_scr[...]
        y_ref[...] = _ssm_gate_norm(y, z_ref[...], nw_ref[...]).astype(BF16)


def _ssd_step(mix, dt, buf, conv_w, conv_b, a_log, d_full, norm_w, expand, state):
    nb = state.shape[0]
    n_pairs = SSM_HEADS // 2
    const = lambda shape: pl.BlockSpec(shape, lambda p: (0, 0))
    st_spec = pl.BlockSpec((nb, 2, SSM_HEAD_DIM, SSM_STATE), lambda p: (0, p, 0, 0))
    return pl.pallas_call(
        _ssd_step_kernel,
        grid=(n_pairs,),
        in_specs=[
            pl.BlockSpec((nb, D_SSM), lambda p: (0, OFF_Z // D_SSM)),
            pl.BlockSpec((nb, D_SSM), lambda p: (0, OFF_XS // D_SSM)),
            pl.BlockSpec((nb, 512), lambda p: (0, OFF_BC // 512)),
            const((nb, LANES)),
            const((nb, CONV_DIM)), const((nb, CONV_DIM)), const((nb, CONV_DIM)),
            const((SSM_CONV, CONV_DIM)), const((1, CONV_DIM)),
            const((1, LANES)),
            const((1, D_SSM)), const((1, D_SSM)),
            const((LANES, D_SSM)), const((LANES, SSM_HEADS * LANES)),
            st_spec,
        ],
        out_specs=[const((nb, D_SSM)), st_spec],
        out_shape=[
            jax.ShapeDtypeStruct((nb, D_SSM), BF16),
            jax.ShapeDtypeStruct(state.shape, F32),
        ],
        scratch_shapes=[
            pltpu.VMEM((D_SSM, nb), F32),
            pltpu.VMEM((n_pairs, nb, 2 * LANES), F32),
            pltpu.VMEM((nb, D_SSM), F32),
            pltpu.VMEM((nb, 2 * SSM_GROUPS * SSM_STATE), F32),
            pltpu.VMEM((n_pairs, nb, 2 * SSM_HEAD_DIM), F32),
        ],
        compiler_params=_cp(("arbitrary",)),
        name="ssd_step",
    )(mix, mix, mix, dt, buf[:, 0], buf[:, 1], buf[:, 2], conv_w, conv_b, a_log,
      d_full, norm_w, expand, jnp.asarray(_head_expand(LANES), BF16), state)


def _outproj_kernel(oa_ref, ys_ref, x_ref, wa_ref, ws_ref, nw_ref, x1_ref, h2_ref):
    x1 = (x_ref[...] + _dot(oa_ref[...].astype(BF16), wa_ref[...])
          + _dot(ys_ref[...].astype(BF16), ws_ref[...]))
    x1_ref[...] = x1
    h2_ref[...] = _rms(x1, nw_ref[...]).astype(BF16)


def _outproj(o_a, y_s, x2d, w_a, w_s, norm_w, tm):
    n = x2d.shape[0]
    row = lambda w: pl.BlockSpec((tm, w), lambda i: (i, 0))
    const = lambda shape: pl.BlockSpec(shape, lambda i: (0, 0))
    return pl.pallas_call(
        _outproj_kernel,
        grid=(n // tm,),
        in_specs=[row(D_HGRN), row(D_SSM), row(D_MODEL),
                  const((D_HGRN, D_MODEL)), const((D_SSM, D_MODEL)), const((1, D_MODEL))],
        out_specs=[row(D_MODEL), row(D_MODEL)],
        out_shape=[jax.ShapeDtypeStruct((n, D_MODEL), F32),
                   jax.ShapeDtypeStruct((n, D_MODEL), BF16)],
        compiler_params=_cp(("arbitrary",)),
        name="outproj",
    )(o_a, y_s, x2d, w_a, w_s, norm_w)


FF_BLOCK = 256


def _ffn_finish(j, contrib, x1_ref, fnw_ref, y_ref, acc_scr):
    @pl.when(j == 0)
    def _():
        acc_scr[...] = contrib

    @pl.when(j > 0)
    def _():
        acc_scr[...] = acc_scr[...] + contrib

    @pl.when(j == pl.num_programs(1) - 1)
    def _():
        y_ref[...] = _rms(x1_ref[...] + acc_scr[...], fnw_ref[...])


def _ffn_prompt_kernel(oa_ref, ys_ref, x_ref, wa_ref, ws_ref, n2_ref, wup_ref, wd_ref, cw_ref, cb_ref,
                       fnw_ref, y_ref, tail_ref, ge_scr, *, tiles_per_seq):
    i = pl.program_id(0)
    tm = x_ref.shape[0]
    pad = SUBLANES
    x1 = x_ref[...] + _dot(oa_ref[...], wa_ref[...]) + _dot(ys_ref[...], ws_ref[...])
    h2 = _rms(x1, n2_ref[...]).astype(BF16)

    seq_start = lax.rem(i, tiles_per_seq) == 0

    @pl.when(seq_start)
    def _():
        ge_scr[0:pad, :] = jnp.zeros((pad, D_FF), F32)

    @pl.when(jnp.logical_not(seq_start))
    def _():
        ge_scr[0:pad, :] = ge_scr[tm:tm + pad, :]

    acc = None
    bounds = np.cumsum((0,) + FFN_COL_BLOCKS)
    for c0, c1 in zip(bounds[:-1].tolist(), bounds[1:].tolist()):
        gate = _dot(h2, wup_ref[:, c0:c1])
        val = _dot(h2, wup_ref[:, D_FF + c0:D_FF + c1])
        ge_scr[pad:, c0:c1] = gate
        tail_ref[0, :, c0:c1] = gate[tm - pad:, :]
        conv = (cb_ref[:, c0:c1] + cw_ref[2:3, c0:c1] * gate
                + cw_ref[1:2, c0:c1] * ge_scr[pad - 1:pad - 1 + tm, c0:c1]
                + cw_ref[0:1, c0:c1] * ge_scr[pad - 2:pad - 2 + tm, c0:c1])
        act = (_silu(conv) * val).astype(BF16)
        part = _dot(act, wd_ref[c0:c1, :])
        acc = part if acc is None else acc + part
    y_ref[...] = _rms(x1 + acc, fnw_ref[...])


FFN_ROW_TILE = 512
FFN_COL_BLOCKS = (1024, 1024, 768)
assert sum(FFN_COL_BLOCKS) == D_FF and all(c % LANES == 0 for c in FFN_COL_BLOCKS)


def _ffn_prompt(o_a, y_s, x2d, w_a, w_s, norm2_w, w_up, w_down, conv_w, conv_b, fnorm_w, seq):
    n = x2d.shape[0]
    tm = FFN_ROW_TILE
    assert seq % tm == 0
    kern = functools.partial(_ffn_prompt_kernel, tiles_per_seq=seq // tm)
    row = lambda w: pl.BlockSpec((tm, w), lambda i: (i, 0))
    resident = lambda shape: pl.BlockSpec(shape, lambda i: (0, 0), pipeline_mode=pl.Buffered(1))
    return pl.pallas_call(
        kern,
        grid=(n // tm,),
        in_specs=[
            row(D_HGRN), row(D_SSM), row(D_MODEL),
            resident((D_HGRN, D_MODEL)), resident((D_SSM, D_MODEL)), resident((1, D_MODEL)),
            resident((D_MODEL, 2 * D_FF)), resident((D_FF, D_MODEL)),
            resident((FFN_CONV, D_FF)), resident((1, D_FF)), resident((1, D_MODEL)),
        ],
        out_specs=[
            row(D_MODEL),
            pl.BlockSpec((1, SUBLANES, D_FF), lambda i: (i, 0, 0)),
        ],
        out_shape=[
            jax.ShapeDtypeStruct((n, D_MODEL), F32),
            jax.ShapeDtypeStruct((n // tm, SUBLANES, D_FF), F32),
        ],
        scratch_shapes=[pltpu.VMEM((tm + SUBLANES, D_FF), F32)],
        compiler_params=_cp(("arbitrary",)),
        name="ffn_prompt",
    )(o_a, y_s, x2d, w_a, w_s, norm2_w, w_up, w_down, conv_w, conv_b, fnorm_w)


def _ffn_step_kernel(h2_ref, x1_ref, wg_ref, wv_ref, wd_ref, cw_ref, cb_ref, fnw_ref,
                     b0_ref, b1_ref, y_ref, gate_ref, acc_scr):
    j = pl.program_id(1)
    h2 = h2_ref[...]
    gate = _dot(h2, wg_ref[...])
    val = _dot(h2, wv_ref[...])
    gate_ref[...] = gate
    conv = (cb_ref[...] + cw_ref[2:3, :] * gate + cw_ref[1:2, :] * b1_ref[...]
            + cw_ref[0:1, :] * b0_ref[...])
    act = (_silu(conv) * val).astype(BF16)
    _ffn_finish(j, _dot(act, wd_ref[...]), x1_ref, fnw_ref, y_ref, acc_scr)


def _ffn_step(h2, x1, w_up, w_down, conv_w, conv_b, fnorm_w, buf):
    n = h2.shape[0]
    nj = D_FF // FF_BLOCK
    return pl.pallas_call(
        _ffn_step_kernel,
        grid=(1, nj),
        in_specs=[
            pl.BlockSpec((n, D_MODEL), lambda i, j: (0, 0)),
            pl.BlockSpec((n, D_MODEL), lambda i, j: (0, 0)),
            pl.BlockSpec((D_MODEL, FF_BLOCK), lambda i, j: (0, j)),
            pl.BlockSpec((D_MODEL, FF_BLOCK), lambda i, j: (0, nj + j)),
            pl.BlockSpec((FF_BLOCK, D_MODEL), lambda i, j: (j, 0)),
            pl.BlockSpec((FFN_CONV, FF_BLOCK), lambda i, j: (0, j)),
            pl.BlockSpec((1, FF_BLOCK), lambda i, j: (0, j)),
            pl.BlockSpec((1, D_MODEL), lambda i, j: (0, 0)),
            pl.BlockSpec((n, FF_BLOCK), lambda i, j: (0, j)),
            pl.BlockSpec((n, FF_BLOCK), lambda i, j: (0, j)),
        ],
        out_specs=[
            pl.BlockSpec((n, D_MODEL), lambda i, j: (0, 0)),
            pl.BlockSpec((n, FF_BLOCK), lambda i, j: (0, j)),
        ],
        out_shape=[
            jax.ShapeDtypeStruct((n, D_MODEL), F32),
            jax.ShapeDtypeStruct((n, D_FF), F32),
        ],
        scratch_shapes=[pltpu.VMEM((n, D_MODEL), F32)],
        compiler_params=_cp(("arbitrary", "arbitrary")),
        name="ffn_step",
    )(h2, x1, w_up, w_up, w_down, conv_w, conv_b, fnorm_w, buf[:, 0], buf[:, 1])


def _row(v):
    return v.reshape(1, -1).astype(F32)


def _pad_lanes(v):
    return jnp.pad(v.astype(F32), (0, LANES - v.shape[0])).reshape(1, LANES)


def kernel(x_prompt, x_sample, state_hgrn, state_ssm, state_conv_ssm, state_conv_ffn, norm1_w, w_in, hgrn_lb, hgrn_norm_w, ssm_conv_w, ssm_conv_b, ssm_dt_bias, ssm_a_log, ssm_d, ssm_norm_w, w_out, norm2_w, w_up, ffn_conv_w, ffn_conv_b, w_down, final_norm_w):
    depth = w_in.shape[0]
    assert depth == 1, "single-layer trunk"
    l = 0
    batch, seq, _ = x_prompt.shape
    dec_batch, dec_seq, _ = x_sample.shape
    assert dec_seq == 1 and seq % CHUNK == 0 and seq >= SSM_CONV

    w_main = w_in[l].astype(BF16)
    w_dt = jnp.pad(w_in[l][:, D_MAIN:], ((0, 0), (0, LANES - SSM_HEADS))).astype(BF16)
    w_oa = w_out[l][:D_HGRN].astype(BF16)
    w_os = w_out[l][D_HGRN:].astype(BF16)
    w_upb = w_up[l].astype(BF16)
    w_dnb = w_down[l].astype(BF16)
    d_full = jnp.repeat(ssm_d[l].astype(F32), SSM_HEAD_DIM).reshape(1, D_SSM)
    dt_bias = _pad_lanes(ssm_dt_bias[l])
    a_log = _pad_lanes(ssm_a_log[l])
    mconst = jnp.asarray(_hgrn_const(), BF16)
    tri = jnp.asarray(np.tril(np.ones((CHUNK, CHUNK), np.float32)), BF16)
    expand = jnp.asarray(_head_expand(), BF16)
    lb_raw = hgrn_lb.astype(F32)

    xp = x_prompt.reshape(batch * seq, D_MODEL)
    proj_p, lg_p, dt_p = _inproj(xp, _row(norm1_w[l]), w_main, w_dt, lb_raw, dt_bias)
    oa_p, hgrn_p = _hgrn_prompt(proj_p, lg_p, _row(hgrn_norm_w[l]), mconst, batch, seq)
    ys_p, ssm_p = _ssd_prompt(proj_p, dt_p, ssm_conv_w[l], _row(ssm_conv_b[l]), a_log,
                              d_full, _row(ssm_norm_w[l]), tri, expand, batch, seq)
    y_p, tail_p = _ffn_prompt(oa_p, ys_p, xp, w_oa, w_os, _row(norm2_w[l]), w_upb, w_dnb,
                              ffn_conv_w[l], _row(ffn_conv_b[l]), _row(final_norm_w), seq)
    proj_p3 = proj_p.reshape(batch, seq, D_MAIN)
    cs_p = proj_p3[:, seq - (SSM_CONV - 1):, OFF_XS:OFF_XS + CONV_DIM]
    tails = tail_p.reshape(batch, seq // FFN_ROW_TILE, SUBLANES, D_FF)
    cf_p = tails[:, -1, SUBLANES - (FFN_CONV - 1):, :]

    xs_ = x_sample.reshape(dec_batch, D_MODEL)
    proj_s, lg_s, dt_s = _inproj(xs_, _row(norm1_w[l]), w_main, w_dt, lb_raw, dt_bias)
    oa_s, hgrn_s = _hgrn_step(proj_s, lg_s, _row(hgrn_norm_w[l]), state_hgrn[l])
    ys_s, ssm_s = _ssd_step(proj_s, dt_s, state_conv_ssm[l], ssm_conv_w[l], _row(ssm_conv_b[l]),
                            a_log, d_full, _row(ssm_norm_w[l]), expand, state_ssm[l])
    x1_s, h2_s = _outproj(oa_s, ys_s, xs_, w_oa, w_os, _row(norm2_w[l]), dec_batch)
    y_s, gate_s = _ffn_step(h2_s, x1_s, w_upb, w_dnb, ffn_conv_w[l], _row(ffn_conv_b[l]),
                            _row(final_norm_w), state_conv_ffn[l])
    cs_s = jnp.concatenate([state_conv_ssm[l][:, 1:], proj_s[:, None, OFF_XS:OFF_XS + CONV_DIM]],
                           axis=1)
    cf_s = jnp.concatenate([state_conv_ffn[l][:, 1:], gate_s[:, None, :]], axis=1)

    dt_ = x_prompt.dtype
    return (y_p.reshape(batch, seq, D_MODEL).astype(dt_),
            y_s.reshape(dec_batch, 1, D_MODEL).astype(dt_),
            hgrn_p[None].astype(dt_),
            hgrn_s[None].astype(dt_),
            ssm_p.reshape(1, batch, SSM_HEADS, SSM_HEAD_DIM, SSM_STATE).astype(dt_),
            ssm_s[None].astype(dt_),
            cs_p[None].astype(dt_),
            cs_s[None].astype(dt_),
            cf_p[None].astype(dt_),
            cf_s[None].astype(dt_))
```

```python
import functools

import numpy as np
import jax
import jax.numpy as jnp
from jax import lax
from jax.experimental import pallas as pl
from jax.experimental.pallas import tpu as pltpu

F32 = jnp.float32
BF16 = jnp.bfloat16
EPS = 1e-6

LANES = 128
SUBLANES = 8

D_MODEL = 1024
HGRN_HEADS = 8
HGRN_DK = 128
HGRN_DV = 128
D_HGRN = HGRN_HEADS * HGRN_DV
SSM_HEADS = 16
SSM_HEAD_DIM = 64
D_SSM = SSM_HEADS * SSM_HEAD_DIM
SSM_STATE = 128
SSM_GROUPS = 2
SSM_CONV = 4
CONV_DIM = D_SSM + 2 * SSM_GROUPS * SSM_STATE
D_FF = 2816
FFN_CONV = 3
D_MAIN = 4 * D_HGRN + D_SSM + CONV_DIM
OFF_Q, OFF_F, OFF_I, OFF_G = 0, 1024, 2048, 3072
OFF_Z, OFF_XS, OFF_BC = 4096, 5120, 6144

CHUNK = 128
GROUP_W = D_SSM // SSM_GROUPS
VMEM_LIMIT = 56 * 1024 * 1024


def _cp(sem):
    return pltpu.CompilerParams(dimension_semantics=sem, vmem_limit_bytes=VMEM_LIMIT)


def _dot(a, b):
    return jnp.dot(a, b, preferred_element_type=F32)


def _dot_nt(a, b):
    return lax.dot_general(a, b, (((1,), (1,)), ((), ())), preferred_element_type=F32)


def _split3(x):
    h = x.astype(BF16)
    r = x - h.astype(F32)
    m = r.astype(BF16)
    lo = (r - m.astype(F32)).astype(BF16)
    return h, m, lo


def _dot_exact_lhs01(m01, x):
    h, m, lo = _split3(x)
    return _dot(m01, h) + _dot(m01, m) + _dot(m01, lo)


def _dot_exact_rhs01(x, m01):
    h, m, lo = _split3(x)
    return _dot(h, m01) + _dot(m, m01) + _dot(lo, m01)


def _dot_split_lhs01(m01, x):
    h = x.astype(BF16)
    lo = (x - h.astype(F32)).astype(BF16)
    return _dot(m01, h) + _dot(m01, lo)


def _sigmoid(x):
    return 1.0 / (1.0 + jnp.exp(-x))


def _silu(x):
    return x * _sigmoid(x)


def _rms(x, w):
    ms = jnp.mean(x * x, axis=-1, keepdims=True)
    return x * lax.rsqrt(ms + EPS) * w


def _inproj_kernel(x_ref, nw_ref, w_ref, wdt_ref, lb_ref, dtb_ref, mix_ref, lg_ref, dt_ref):
    hb = _rms(x_ref[...], nw_ref[...]).astype(BF16)

    def put(off, val):
        mix_ref[:, off:off + val.shape[1]] = val.astype(BF16)

    w = D_HGRN
    qf = _dot(hb, w_ref[:, OFF_Q:OFF_Q + 2 * w])
    lb = _hgrn_lb(lb_ref[...])
    f = lb + (1.0 - lb) * _sigmoid(qf[:, w:])
    lg_ref[...] = jnp.log(f)
    put(OFF_F, 1.0 - f)
    put(OFF_Q, _silu(qf[:, :w]))
    ig = _dot(hb, w_ref[:, OFF_I:OFF_I + 2 * w])
    put(OFF_I, ig[:, :w])
    put(OFF_G, _silu(ig[:, w:]))
    zx = _dot(hb, w_ref[:, OFF_Z:OFF_Z + D_SSM + CONV_DIM])
    put(OFF_Z, _silu(zx[:, :D_SSM]))
    put(OFF_XS, zx[:, D_SSM:])
    dt_ref[...] = _softplus(_dot(hb, wdt_ref[...]) + dtb_ref[...])


INPROJ_ROW_TILE = 512


def _inproj(x2d, norm_w, w_main, w_dt, lb_raw, dt_bias):
    n = x2d.shape[0]
    tm = min(INPROJ_ROW_TILE, n)
    assert n % tm == 0
    row = lambda w: pl.BlockSpec((tm, w), lambda i: (i, 0))
    resident = lambda shape: pl.BlockSpec(shape, lambda i: (0, 0), pipeline_mode=pl.Buffered(1))
    return pl.pallas_call(
        _inproj_kernel,
        grid=(n // tm,),
        in_specs=[
            row(D_MODEL), resident((1, D_MODEL)),
            resident((D_MODEL, D_MAIN)), resident((D_MODEL, LANES)),
            resident(lb_raw.shape), resident((1, LANES)),
        ],
        out_specs=[row(D_MAIN), row(D_HGRN), row(LANES)],
        out_shape=[
            jax.ShapeDtypeStruct((n, D_MAIN), BF16),
            jax.ShapeDtypeStruct((n, D_HGRN), F32),
            jax.ShapeDtypeStruct((n, LANES), F32),
        ],
        compiler_params=_cp(("arbitrary",)),
        name="inproj",
    )(x2d, norm_w, w_main, w_dt, lb_raw, dt_bias)


LOG2E = 1.4426950408889634
N_LEVELS = 7
MXU_LEVEL_HALVES = (4, 2)


def _hgrn_const():
    c = CHUNK
    t = np.arange(c)[:, None]
    j = np.arange(c)[None, :]
    blocks = [(j <= t)]
    for h in MXU_LEVEL_HALVES:
        mid = (t // (2 * h)) * (2 * h) + h
        upper = (t >= mid) & (j >= mid) & (j <= t)
        lower = (t < mid) & (j > t) & (j < mid)
        blocks.append(upper | lower)
    return np.concatenate(blocks, axis=0).astype(np.float32)


def _midpoint_decay(b, h):
    pieces = []
    for start in range(0, CHUNK, 2 * h):
        mid = start + h
        m = b[mid - 1:mid, :]
        pieces.append(m - b[start:mid])
        pieces.append(b[mid:mid + h] - m)
    return jnp.concatenate(pieces, axis=0)


def _mix_rows(q, k, h):
    pieces = []
    for start in range(0, CHUNK, 2 * h):
        pieces.append(k[start:start + h])
        pieces.append(q[start + h:start + 2 * h])
    return jnp.concatenate(pieces, axis=0)


def _hgrn_lb(lb_raw):
    mx = jnp.max(lb_raw, axis=0, keepdims=True)
    e = jnp.exp(lb_raw - mx)
    return e[0:1, :] / jnp.sum(e, axis=0, keepdims=True)


def _level_map():
    t = lax.broadcasted_iota(jnp.int32, (CHUNK, CHUNK), 0)
    s = lax.broadcasted_iota(jnp.int32, (CHUNK, CHUNK), 1)
    bitlen = 32 - lax.clz(t ^ s)
    return jnp.where(t > s, bitlen, jnp.where(t == s, 0, -1))


def _hgrn_prompt_kernel(q_ref, k_ref, i_ref, g_ref, lg_ref, nw_ref, mc_ref,
                        o_ref, s_out_ref, st_scr):
    c = pl.program_id(1)

    @pl.when(c == 0)
    def _():
        st_scr[...] = jnp.zeros_like(st_scr)

    n_sub = q_ref.shape[0] // CHUNK
    lev = _level_map().astype(jnp.int16)
    row = lax.broadcasted_iota(jnp.int32, (CHUNK, HGRN_DK), 0)
    heads = range(HGRN_HEADS)
    pairs = [(s, h) for s in range(n_sub) for h in heads]
    rs = {s: slice(s * CHUNK, (s + 1) * CHUNK) for s in range(n_sub)}
    cs = {h: slice(h * HGRN_DK, (h + 1) * HGRN_DK) for h in heads}

    e_sub = {s: _dot_split_lhs01(mc_ref[...], lg_ref[rs[s], :] * LOG2E) for s in range(n_sub)}
    qb = {(s, h): q_ref[rs[s], cs[h]] for s, h in pairs}
    kb = {(s, h): k_ref[rs[s], cs[h]] for s, h in pairs}
    vb = {(s, h): i_ref[rs[s], cs[h]] for s, h in pairs}
    q = {p: qb[p].astype(F32) for p in pairs}
    k = {p: kb[p].astype(F32) for p in pairs}
    b = {(s, h): e_sub[s][0:CHUNK, cs[h]] for s, h in pairs}
    b_last = {p: b[p][CHUNK - 1:CHUNK, :] for p in pairs}

    st = {h: st_scr[h] for h in heads}
    o = {}
    for s, h in pairs:
        p = (s, h)
        o[p] = _dot((q[p] * jnp.exp2(b[p])).astype(BF16), st[h].T.astype(BF16))
        ks = (k[p] * jnp.exp2(b_last[p] - b[p])).astype(BF16)
        st[h] = st[h] * jnp.exp2(b_last[p]) + _dot(vb[p].astype(F32).T.astype(BF16), ks)
    for h in heads:
        st_scr[h] = st[h]

    a = {p: jnp.where(lev == 0, _dot(qb[p], k[p].T.astype(BF16)).astype(BF16), jnp.zeros((), BF16))
         for p in pairs}
    half = CHUNK // 2
    while half >= 1:
        for p in pairs:
            if half >= SUBLANES:
                x = _mix_rows(q[p], k[p], half) * jnp.exp2(_midpoint_decay(b[p], half))
            else:
                upper = (row & half) != 0
                if half in MXU_LEVEL_HALVES:
                    blk = 1 + MXU_LEVEL_HALVES.index(half)
                    w = jnp.exp2(e_sub[p[0]][blk * CHUNK:(blk + 1) * CHUNK, cs[p[1]]])
                    x = jnp.where(upper, q[p], k[p]) * w
                else:
                    x = jnp.where(upper, q[p] * (1.0 - k[p]), k[p])
            gram = _dot(x.astype(BF16), x.T.astype(BF16))
            a[p] = jnp.where(lev == half.bit_length(), gram.astype(BF16), a[p])
        half //= 2

    for p in pairs:
        o[p] = o[p] + _dot(a[p], vb[p])
    for s, h in pairs:
        gate = g_ref[rs[s], cs[h]].astype(F32)
        o_ref[rs[s], cs[h]] = (_rms(o[(s, h)], nw_ref[...]) * gate).astype(BF16)

    @pl.when(c == pl.num_programs(1) - 1)
    def _():
        for h in range(HGRN_HEADS):
            s_out_ref[0, h] = st_scr[h].T


HGRN_SUBCHUNKS = 2


def _hgrn_prompt(mix, lg, norm_w, mconst, batch, seq):
    rows = HGRN_SUBCHUNKS * CHUNK
    assert seq % rows == 0
    nc = seq // rows

    def col(off):
        return pl.BlockSpec((rows, D_HGRN), lambda b, c: (b * nc + c, off // D_HGRN))

    return pl.pallas_call(
        _hgrn_prompt_kernel,
        grid=(batch, nc),
        in_specs=[
            col(OFF_Q), col(OFF_F), col(OFF_I), col(OFF_G), col(0),
            pl.BlockSpec((1, HGRN_DV), lambda b, c: (0, 0)),
            pl.BlockSpec(mconst.shape, lambda b, c: (0, 0)),
        ],
        out_specs=[
            pl.BlockSpec((rows, D_HGRN), lambda b, c: (b * nc + c, 0)),
            pl.BlockSpec((1, HGRN_HEADS, HGRN_DK, HGRN_DV), lambda b, c: (b, 0, 0, 0)),
        ],
        out_shape=[
            jax.ShapeDtypeStruct((batch * seq, D_HGRN), BF16),
            jax.ShapeDtypeStruct((batch, HGRN_HEADS, HGRN_DK, HGRN_DV), F32),
        ],
        scratch_shapes=[pltpu.VMEM((HGRN_HEADS, HGRN_DV, HGRN_DK), F32)],
        compiler_params=_cp(("arbitrary", "arbitrary")),
        name="hgrn_prompt",
    )(mix, mix, mix, mix, lg, norm_w, mconst)


def _hgrn_step_kernel(q_ref, i_ref, g_ref, lg_ref, nw_ref, s_ref,
                      o_ref, s_out_ref, o_scr):
    nb = q_ref.shape[0]
    qb = q_ref[...]
    f_t = jnp.exp(lg_ref[...]).T
    v = i_ref[...].astype(F32)
    lhs_rows = 2 * SUBLANES
    for t in range(nb):
        v_row = v[t:t + 1, :]
        s_new = v_row + f_t[:, t:t + 1] * (s_ref[t, 0] - v_row)
        s_out_ref[t, 0] = s_new
        q_rows = jnp.broadcast_to(qb[t:t + 1, :], (lhs_rows, HGRN_DK))
        o_scr[t:t + 1, :] = _dot(q_rows, s_new.astype(BF16))[0:1, :]
    o_ref[...] = (_rms(o_scr[...], nw_ref[...]) * g_ref[...].astype(F32)).astype(BF16)


def _hgrn_step(mix, lg, norm_w, state):
    nb = state.shape[0]
    hb = lambda off: off // HGRN_DK

    def col(off):
        return pl.BlockSpec((nb, HGRN_DK), lambda h: (0, hb(off) + h))

    st_spec = pl.BlockSpec((nb, 1, HGRN_DK, HGRN_DV), lambda h: (0, h, 0, 0))
    return pl.pallas_call(
        _hgrn_step_kernel,
        grid=(HGRN_HEADS,),
        in_specs=[
            col(OFF_Q), col(OFF_I), col(OFF_G), col(0),
            pl.BlockSpec((1, HGRN_DV), lambda h: (0, 0)),
            st_spec,
        ],
        out_specs=[pl.BlockSpec((nb, HGRN_DV), lambda h: (0, h)), st_spec],
        out_shape=[
            jax.ShapeDtypeStruct((nb, D_HGRN), BF16),
            jax.ShapeDtypeStruct(state.shape, F32),
        ],
        scratch_shapes=[pltpu.VMEM((nb, HGRN_DV), F32)],
        compiler_params=_cp(("arbitrary",)),
        name="hgrn_step",
    )(mix, mix, mix, lg, norm_w, state)


def _head_expand(width=SSM_HEAD_DIM):
    e = np.zeros((LANES, SSM_HEADS * width), np.float32)
    for h in range(SSM_HEADS):
        e[h, h * width:(h + 1) * width] = 1.0
    return e


def _softplus(x):
    return jnp.maximum(x, 0.0) + jnp.log(1.0 + jnp.exp(-jnp.abs(x)))


def _ssm_gate_norm(y, z_gate, nw):
    y = y * z_gate.astype(F32)
    parts = [_rms(y[:, g * GROUP_W:(g + 1) * GROUP_W], nw[:, g * GROUP_W:(g + 1) * GROUP_W])
             for g in range(SSM_GROUPS)]
    return jnp.concatenate(parts, axis=-1)


def _ssd_prompt_kernel(z_ref, xs_ref, bc_ref, dt_ref, cw_ref, cb_ref, alog_ref,
                       dvec_ref, nw_ref, tri_ref, exp_ref,
                       shift_ref, y_ref, st_out_ref, xprev_scr, st_scr):
    c = pl.program_id(1)
    t = CHUNK

    @pl.when(c == 0)
    def _():
        st_scr[...] = jnp.zeros_like(st_scr)
        xprev_scr[...] = jnp.zeros_like(xprev_scr)

    subs = range(xs_ref.shape[0] // t)
    rs = [slice(s * t, (s + 1) * t) for s in subs]

    x_cur = [jnp.concatenate([xs_ref[r, :], bc_ref[r, :]], axis=-1) for r in rs]
    x_prev = [xprev_scr[...]] + x_cur[:-1]
    xprev_scr[...] = x_cur[-1]
    taps = [_dot(shift_ref[...], jnp.concatenate([x_prev[s], x_cur[s]], axis=0)) for s in subs]
    xbc = []
    for s in subs:
        acc = cb_ref[...] + cw_ref[SSM_CONV - 1:SSM_CONV, :] * x_cur[s].astype(F32)
        for d in range(1, SSM_CONV):
            acc = acc + cw_ref[SSM_CONV - 1 - d:SSM_CONV - d, :] * taps[s][(d - 1) * t:d * t, :]
        xbc.append(_silu(acc))
    xs = [x[:, 0:D_SSM] for x in xbc]

    dt = [dt_ref[r, :] for r in rs]
    neg_a = -LOG2E * jnp.exp(alog_ref[...])
    cs = [_dot_exact_lhs01(tri_ref[...], dt[s] * neg_a) for s in subs]
    ex = exp_ref[...]
    dt_full = [_dot_exact_rhs01(dt[s], ex) for s in subs]
    cs_full = [_dot_exact_rhs01(cs[s], ex) for s in subs]
    cs_last_full = [x[t - 1:t, :] for x in cs_full]
    x_dt = [xs[s] * dt_full[s] for s in subs]
    x_end = [(x_dt[s] * jnp.exp2(cs_last_full[s] - cs_full[s])).astype(BF16) for s in subs]
    cs_t = [x.T for x in cs]

    causal = (lax.broadcasted_iota(jnp.int32, (t, t), 0)
              >= lax.broadcasted_iota(jnp.int32, (t, t), 1))
    lane = lax.broadcasted_iota(jnp.int32, (1, D_SSM), 1)
    odd_head = (lane & SSM_HEAD_DIM) != 0
    x_by_parity = []
    for s in subs:
        x_b = x_dt[s].astype(BF16)
        zero = jnp.zeros_like(x_b)
        x_by_parity.append((jnp.where(odd_head, zero, x_b), jnp.where(odd_head, x_b, zero)))
    heads_per_group = SSM_HEADS // SSM_GROUPS
    pair_w = 2 * SSM_HEAD_DIM
    never = -1e30

    def group_bc(s, g):
        b_g = xbc[s][:, D_SSM + g * SSM_STATE:D_SSM + (g + 1) * SSM_STATE]
        c_off = D_SSM + SSM_GROUPS * SSM_STATE + g * SSM_STATE
        return b_g.T.astype(BF16), xbc[s][:, c_off:c_off + SSM_STATE].astype(BF16)

    bc_t = {(s, g): group_bc(s, g) for s in subs for g in range(SSM_GROUPS)}

    y_diag = []
    for s in subs:
        y_parts = []
        for g in range(SSM_GROUPS):
            b_t, c_g = bc_t[(s, g)]
            gmat = _dot(c_g, b_t)
            for pp in range(heads_per_group // 2):
                h0 = g * heads_per_group + 2 * pp
                psl = slice(h0 * SSM_HEAD_DIM, h0 * SSM_HEAD_DIM + pair_w)
                yp = None
                for sub in range(2):
                    h = h0 + sub
                    diff = cs[s][:, h:h + 1] - cs_t[s][h:h + 1, :]
                    w = jnp.exp2(jnp.where(causal, diff, never)) * gmat
                    part = _dot(w.astype(BF16), x_by_parity[s][sub][:, psl])
                    yp = part if yp is None else yp + part
                y_parts.append(yp)
        y_diag.append(jnp.concatenate(y_parts, axis=-1))

    y_off = []
    for s in subs:
        offs = []
        for g in range(SSM_GROUPS):
            sl = slice(g * GROUP_W, (g + 1) * GROUP_W)
            b_t, c_g = bc_t[(s, g)]
            st_g = st_scr[:, sl]
            offs.append(_dot(c_g, st_g.astype(BF16)))
            st_scr[:, sl] = st_g * jnp.exp2(cs_last_full[s][:, sl]) + _dot(b_t, x_end[s][:, sl])
        y_off.append(jnp.concatenate(offs, axis=-1) * jnp.exp2(cs_full[s]))

    for s in subs:
        y = y_diag[s] + y_off[s] + dvec_ref[...] * xs[s]
        y_ref[rs[s], :] = _ssm_gate_norm(y, z_ref[rs[s], :], nw_ref[...]).astype(BF16)

    @pl.when(c == pl.num_programs(1) - 1)
    def _():
        for j in range(D_SSM // LANES):
            st_out_ref[0, j * LANES:(j + 1) * LANES, :] = st_scr[:, j * LANES:(j + 1) * LANES].T


SSD_SUBCHUNKS = 1


def _ssd_prompt(mix, dt, conv_w, conv_b, a_log, d_full, norm_w, tri, expand, batch, seq):
    rows = SSD_SUBCHUNKS * CHUNK
    assert seq % rows == 0
    nc = seq // rows
    const = lambda shape: pl.BlockSpec(shape, lambda b, c: (0, 0))
    return pl.pallas_call(
        _ssd_prompt_kernel,
        grid=(batch, nc),
        in_specs=[
            pl.BlockSpec((rows, D_SSM), lambda b, c: (b * nc + c, OFF_Z // D_SSM)),
            pl.BlockSpec((rows, D_SSM), lambda b, c: (b * nc + c, OFF_XS // D_SSM)),
            pl.BlockSpec((rows, 512), lambda b, c: (b * nc + c, OFF_BC // 512)),
            pl.BlockSpec((rows, LANES), lambda b, c: (b * nc + c, 0)),
            const((SSM_CONV, CONV_DIM)), const((1, CONV_DIM)),
            const((1, LANES)),
            const((1, D_SSM)), const((1, D_SSM)),
            const((CHUNK, CHUNK)), const((LANES, D_SSM)),
            const(((SSM_CONV - 1) * CHUNK, 2 * CHUNK)),
        ],
        out_specs=[
            pl.BlockSpec((rows, D_SSM), lambda b, c: (b * nc + c, 0)),
            pl.BlockSpec((1, D_SSM, SSM_STATE), lambda b, c: (b, 0, 0)),
        ],
        out_shape=[
            jax.ShapeDtypeStruct((batch * seq, D_SSM), BF16),
            jax.ShapeDtypeStruct((batch, D_SSM, SSM_STATE), F32),
        ],
        scratch_shapes=[
            pltpu.VMEM((CHUNK, CONV_DIM), BF16),
            pltpu.VMEM((SSM_STATE, D_SSM), F32),
        ],
        compiler_params=_cp(("arbitrary", "arbitrary")),
        name="ssd_prompt",
    )(mix, mix, mix, dt, conv_w, conv_b, a_log, d_full, norm_w, tri, expand,
      jnp.asarray(_conv_shifts(), BF16))


def _conv_shifts():
    m = np.zeros(((SSM_CONV - 1) * CHUNK, 2 * CHUNK), np.float32)
    for d in range(1, SSM_CONV):
        for t in range(CHUNK):
            m[(d - 1) * CHUNK + t, CHUNK + t - d] = 1.0
    return m


def _ssd_step_kernel(z_ref, xs_ref, bc_ref, dt_ref, b0_ref, b1_ref, b2_ref, cw_ref, cb_ref,
                     alog_ref, dvec_ref, nw_ref, exp_ref, exl_ref, st_ref,
                     y_ref, st_out_ref, xt_scr, at_scr, xs_scr, bc_scr, y_scr):
    p = pl.program_id(0)
    nb = z_ref.shape[0]
    pair_w = 2 * SSM_HEAD_DIM
    pairs_per_group = SSM_HEADS // SSM_GROUPS // 2

    @pl.when(p == 0)
    def _():
        x_new = jnp.concatenate([xs_ref[...], bc_ref[...]], axis=-1).astype(F32)
        acc = (cb_ref[...] + cw_ref[0:1, :] * b0_ref[...] + cw_ref[1:2, :] * b1_ref[...]
               + cw_ref[2:3, :] * b2_ref[...] + cw_ref[3:4, :] * x_new)
        xbc = _silu(acc)
        xs = xbc[:, 0:D_SSM]
        dt = dt_ref[...]
        da = dt * (-jnp.exp(alog_ref[...]))
        ex = exp_ref[...]
        x_dt = xs * _dot_exact_rhs01(dt, ex)
        decay = jnp.exp(_dot_exact_rhs01(da, exl_ref[...]))
        xs_scr[...] = xs
        bc_scr[...] = xbc[:, D_SSM:]
        for j in range(D_SSM // LANES):
            sl = slice(j * LANES, (j + 1) * LANES)
            xt_scr[sl, :] = x_dt[:, sl].T
            at_scr[j] = decay[:, 2 * j * LANES:2 * (j + 1) * LANES]

    g_is_1 = p >= pairs_per_group
    row0 = pl.multiple_of(p * pair_w, pair_w)
    x_t = xt_scr[pl.ds(row0, pair_w), :]
    a_p = at_scr[p]
    bc = bc_scr[...]
    b_all = jnp.where(g_is_1, bc[:, SSM_STATE:2 * SSM_STATE], bc[:, 0:SSM_STATE])
    c_all = jnp.where(g_is_1, bc[:, 3 * SSM_STATE:4 * SSM_STATE],
                      bc[:, 2 * SSM_STATE:3 * SSM_STATE]).astype(BF16)
    for t in range(nb):
        inject = x_t[:, t:t + 1] * b_all[t:t + 1, :]
        halves = []
        for sub in range(2):
            rows = slice(sub * SSM_HEAD_DIM, (sub + 1) * SSM_HEAD_DIM)
            half = a_p[t:t + 1, sub * LANES:(sub + 1) * LANES] * st_ref[t, sub] + inject[rows]
            st_out_ref[t, sub] = half
            halves.append(half)
        new = jnp.concatenate(halves, axis=0)
        c_rows = jnp.broadcast_to(c_all[t:t + 1, :], (SUBLANES, SSM_STATE))
        y_scr[p, t:t + 1, :] = _dot_nt(c_rows, new.astype(BF16))[0:1, :]

    @pl.when(p == pl.num_programs(0) - 1)
    def _():
        y_mix = jnp.concatenate([y_scr[j] for j in range(SSM_HEADS // 2)], axis=-1)
        y = y_mix + dvec_ref[...] * xs_scr[...]
        y_ref[...] = _ssm_gate_norm(y, z_ref[...], nw_ref[...]).astype(BF16)


def _ssd_step(mix, dt, buf, conv_w, conv_b, a_log, d_full, norm_w, expand, state):
    nb = state.shape[0]
    n_pairs = SSM_HEADS // 2
    const = lambda shape: pl.BlockSpec(shape, lambda p: (0, 0))
    st_spec = pl.BlockSpec((nb, 2, SSM_HEAD_DIM, SSM_STATE), lambda p: (0, p, 0, 0))
    return pl.pallas_call(
        _ssd_step_kernel,
        grid=(n_pairs,),
        in_specs=[
            pl.BlockSpec((nb, D_SSM), lambda p: (0, OFF_Z // D_SSM)),
            pl.BlockSpec((nb, D_SSM), lambda p: (0, OFF_XS // D_SSM)),
            pl.BlockSpec((nb, 512), lambda p: (0, OFF_BC // 512)),
            const((nb, LANES)),
            const((nb, CONV_DIM)), const((nb, CONV_DIM)), const((nb, CONV_DIM)),
            const((SSM_CONV, CONV_DIM)), const((1, CONV_DIM)),
            const((1, LANES)),
            const((1, D_SSM)), const((1, D_SSM)),
            const((LANES, D_SSM)), const((LANES, SSM_HEADS * LANES)),
            st_spec,
        ],
        out_specs=[const((nb, D_SSM)), st_spec],
        out_shape=[
            jax.ShapeDtypeStruct((nb, D_SSM), BF16),
            jax.ShapeDtypeStruct(state.shape, F32),
        ],
        scratch_shapes=[
            pltpu.VMEM((D_SSM, nb), F32),
            pltpu.VMEM((n_pairs, nb, 2 * LANES), F32),
            pltpu.VMEM((nb, D_SSM), F32),
            pltpu.VMEM((nb, 2 * SSM_GROUPS * SSM_STATE), F32),
            pltpu.VMEM((n_pairs, nb, 2 * SSM_HEAD_DIM), F32),
        ],
        compiler_params=_cp(("arbitrary",)),
        name="ssd_step",
    )(mix, mix, mix, dt, buf[:, 0], buf[:, 1], buf[:, 2], conv_w, conv_b, a_log,
      d_full, norm_w, expand, jnp.asarray(_head_expand(LANES), BF16), state)


def _ffn_prompt_kernel(oa_ref, ys_ref, x_ref, wa_ref, ws_ref, n2_ref, wup_ref, wd_ref, cw_ref, cb_ref,
                       fnw_ref, y_ref, tail_ref, ge_scr, *, tiles_per_seq):
    i = pl.program_id(0)
    tm = x_ref.shape[0]
    pad = SUBLANES
    x1 = x_ref[...] + _dot(oa_ref[...], wa_ref[...]) + _dot(ys_ref[...], ws_ref[...])
    h2 = _rms(x1, n2_ref[...]).astype(BF16)

    seq_start = lax.rem(i, tiles_per_seq) == 0

    @pl.when(seq_start)
    def _():
        ge_scr[0:pad, :] = jnp.zeros((pad, D_FF), F32)

    @pl.when(jnp.logical_not(seq_start))
    def _():
        ge_scr[0:pad, :] = ge_scr[tm:tm + pad, :]

    acc = None
    bounds = np.cumsum((0,) + FFN_COL_BLOCKS)
    for c0, c1 in zip(bounds[:-1].tolist(), bounds[1:].tolist()):
        gate = _dot(h2, wup_ref[:, c0:c1])
        val = _dot(h2, wup_ref[:, D_FF + c0:D_FF + c1])
        ge_scr[pad:, c0:c1] = gate
        tail_ref[0, :, c0:c1] = gate[tm - pad:, :]
        conv = (cb_ref[:, c0:c1] + cw_ref[2:3, c0:c1] * gate
                + cw_ref[1:2, c0:c1] * ge_scr[pad - 1:pad - 1 + tm, c0:c1]
                + cw_ref[0:1, c0:c1] * ge_scr[pad - 2:pad - 2 + tm, c0:c1])
        act = (_silu(conv) * val).astype(BF16)
        part = _dot(act, wd_ref[c0:c1, :])
        acc = part if acc is None else acc + part
    y_ref[...] = _rms(x1 + acc, fnw_ref[...])


FFN_ROW_TILE = 512
FFN_COL_BLOCKS = (1024, 1024, 768)
assert sum(FFN_COL_BLOCKS) == D_FF and all(c % LANES == 0 for c in FFN_COL_BLOCKS)


def _ffn_prompt(o_a, y_s, x2d, w_a, w_s, norm2_w, w_up, w_down, conv_w, conv_b, fnorm_w, seq):
    n = x2d.shape[0]
    tm = FFN_ROW_TILE
    assert seq % tm == 0
    kern = functools.partial(_ffn_prompt_kernel, tiles_per_seq=seq // tm)
    row = lambda w: pl.BlockSpec((tm, w), lambda i: (i, 0))
    resident = lambda shape: pl.BlockSpec(shape, lambda i: (0, 0), pipeline_mode=pl.Buffered(1))
    return pl.pallas_call(
        kern,
        grid=(n // tm,),
        in_specs=[
            row(D_HGRN), row(D_SSM), row(D_MODEL),
            resident((D_HGRN, D_MODEL)), resident((D_SSM, D_MODEL)), resident((1, D_MODEL)),
            resident((D_MODEL, 2 * D_FF)), resident((D_FF, D_MODEL)),
            resident((FFN_CONV, D_FF)), resident((1, D_FF)), resident((1, D_MODEL)),
        ],
        out_specs=[
            row(D_MODEL),
            pl.BlockSpec((1, SUBLANES, D_FF), lambda i: (i, 0, 0)),
        ],
        out_shape=[
            jax.ShapeDtypeStruct((n, D_MODEL), F32),
            jax.ShapeDtypeStruct((n // tm, SUBLANES, D_FF), F32),
        ],
        scratch_shapes=[pltpu.VMEM((tm + SUBLANES, D_FF), F32)],
        compiler_params=_cp(("arbitrary",)),
        name="ffn_prompt",
    )(o_a, y_s, x2d, w_a, w_s, norm2_w, w_up, w_down, conv_w, conv_b, fnorm_w)


def _ffn_step_kernel(oa_ref, ys_ref, x_ref, wa_ref, ws_ref, n2_ref, wup_ref, wd_ref, cw_ref, cb_ref,
                     fnw_ref, b0_ref, b1_ref, y_ref, gate_ref):
    x1 = x_ref[...] + _dot(oa_ref[...], wa_ref[...]) + _dot(ys_ref[...], ws_ref[...])
    h2 = _rms(x1, n2_ref[...]).astype(BF16)
    gate = _dot(h2, wup_ref[:, 0:D_FF])
    val = _dot(h2, wup_ref[:, D_FF:2 * D_FF])
    gate_ref[...] = gate
    conv = (cb_ref[...] + cw_ref[2:3, :] * gate + cw_ref[1:2, :] * b1_ref[...]
            + cw_ref[0:1, :] * b0_ref[...])
    act = (_silu(conv) * val).astype(BF16)
    y_ref[...] = _rms(x1 + _dot(act, wd_ref[...]), fnw_ref[...])


def _ffn_step(o_a, y_s, x2d, w_a, w_s, norm2_w, w_up, w_down, conv_w, conv_b, fnorm_w, buf):
    n = x2d.shape[0]
    args = (o_a, y_s, x2d, w_a, w_s, norm2_w, w_up, w_down, conv_w, conv_b, fnorm_w,
            buf[:, 0], buf[:, 1])
    whole = lambda a: pl.BlockSpec(a.shape, lambda i: (0,) * a.ndim, pipeline_mode=pl.Buffered(1))
    return pl.pallas_call(
        _ffn_step_kernel,
        grid=(1,),
        in_specs=[whole(a) for a in args],
        out_specs=[pl.BlockSpec((n, D_MODEL), lambda i: (0, 0)),
                   pl.BlockSpec((n, D_FF), lambda i: (0, 0))],
        out_shape=[
            jax.ShapeDtypeStruct((n, D_MODEL), F32),
            jax.ShapeDtypeStruct((n, D_FF), F32),
        ],
        compiler_params=_cp(("arbitrary",)),
        name="ffn_step",
    )(*args)


def _row(v):
    return v.reshape(1, -1).astype(F32)


def _pad_lanes(v):
    return jnp.pad(v.astype(F32), (0, LANES - v.shape[0])).reshape(1, LANES)


def kernel(x_prompt, x_sample, state_hgrn, state_ssm, state_conv_ssm, state_conv_ffn, norm1_w, w_in, hgrn_lb, hgrn_norm_w, ssm_conv_w, ssm_conv_b, ssm_dt_bias, ssm_a_log, ssm_d, ssm_norm_w, w_out, norm2_w, w_up, ffn_conv_w, ffn_conv_b, w_down, final_norm_w):
    depth = w_in.shape[0]
    assert depth == 1, "single-layer trunk"
    l = 0
    batch, seq, _ = x_prompt.shape
    dec_batch, dec_seq, _ = x_sample.shape
    assert dec_seq == 1 and seq % CHUNK == 0 and seq >= SSM_CONV

    w_main = w_in[l].astype(BF16)
    w_dt = jnp.pad(w_in[l][:, D_MAIN:], ((0, 0), (0, LANES - SSM_HEADS))).astype(BF16)
    w_oa = w_out[l][:D_HGRN].astype(BF16)
    w_os = w_out[l][D_HGRN:].astype(BF16)
    w_upb = w_up[l].astype(BF16)
    w_dnb = w_down[l].astype(BF16)
    d_full = jnp.repeat(ssm_d[l].astype(F32), SSM_HEAD_DIM).reshape(1, D_SSM)
    dt_bias = _pad_lanes(ssm_dt_bias[l])
    a_log = _pad_lanes(ssm_a_log[l])
    mconst = jnp.asarray(_hgrn_const(), BF16)
    tri = jnp.asarray(np.tril(np.ones((CHUNK, CHUNK), np.float32)), BF16)
    expand = jnp.asarray(_head_expand(), BF16)
    lb_raw = hgrn_lb.astype(F32)

    xp = x_prompt.reshape(batch * seq, D_MODEL)
    proj_p, lg_p, dt_p = _inproj(xp, _row(norm1_w[l]), w_main, w_dt, lb_raw, dt_bias)
    oa_p, hgrn_p = _hgrn_prompt(proj_p, lg_p, _row(hgrn_norm_w[l]), mconst, batch, seq)
    ys_p, ssm_p = _ssd_prompt(proj_p, dt_p, ssm_conv_w[l], _row(ssm_conv_b[l]), a_log,
                              d_full, _row(ssm_norm_w[l]), tri, expand, batch, seq)
    y_p, tail_p = _ffn_prompt(oa_p, ys_p, xp, w_oa, w_os, _row(norm2_w[l]), w_upb, w_dnb,
                              ffn_conv_w[l], _row(ffn_conv_b[l]), _row(final_norm_w), seq)
    proj_p3 = proj_p.reshape(batch, seq, D_MAIN)
    cs_p = proj_p3[:, seq - (SSM_CONV - 1):, OFF_XS:OFF_XS + CONV_DIM]
    tails = tail_p.reshape(batch, seq // FFN_ROW_TILE, SUBLANES, D_FF)
    cf_p = tails[:, -1, SUBLANES - (FFN_CONV - 1):, :]

    xs_ = x_sample.reshape(dec_batch, D_MODEL)
    proj_s, lg_s, dt_s = _inproj(xs_, _row(norm1_w[l]), w_main, w_dt, lb_raw, dt_bias)
    oa_s, hgrn_s = _hgrn_step(proj_s, lg_s, _row(hgrn_norm_w[l]), state_hgrn[l])
    ys_s, ssm_s = _ssd_step(proj_s, dt_s, state_conv_ssm[l], ssm_conv_w[l], _row(ssm_conv_b[l]),
                            a_log, d_full, _row(ssm_norm_w[l]), expand, state_ssm[l])
    y_s, gate_s = _ffn_step(oa_s, ys_s, xs_, w_oa, w_os, _row(norm2_w[l]), w_upb, w_dnb,
                            ffn_conv_w[l], _row(ffn_conv_b[l]), _row(final_norm_w),
                            state_conv_ffn[l])
    cs_s = jnp.concatenate([state_conv_ssm[l][:, 1:], proj_s[:, None, OFF_XS:OFF_XS + CONV_DIM]],
                           axis=1)
    cf_s = jnp.concatenate([state_conv_ffn[l][:, 1:], gate_s[:, None, :]], axis=1)

    dt_ = x_prompt.dtype
    return (y_p.reshape(batch, seq, D_MODEL).astype(dt_),
            y_s.reshape(dec_batch, 1, D_MODEL).astype(dt_),
            hgrn_p[None].astype(dt_),
            hgrn_s[None].astype(dt_),
            ssm_p.reshape(1, batch, SSM_HEADS, SSM_HEAD_DIM, SSM_STATE).astype(dt_),
            ssm_s[None].astype(dt_),
            cs_p[None].astype(dt_),
            cs_s[None].astype(dt_),
            cf_p[None].astype(dt_),
            cf_s[None].astype(dt_))
```

```python
import functools

import numpy as np
import jax
import jax.numpy as jnp
from jax import lax
from jax.experimental import pallas as pl
from jax.experimental.pallas import tpu as pltpu

F32 = jnp.float32
BF16 = jnp.bfloat16
EPS = 1e-6

LANES = 128
SUBLANES = 8

D_MODEL = 1024
HGRN_HEADS = 8
HGRN_DK = 128
HGRN_DV = 128
D_HGRN = HGRN_HEADS * HGRN_DV
SSM_HEADS = 16
SSM_HEAD_DIM = 64
D_SSM = SSM_HEADS * SSM_HEAD_DIM
SSM_STATE = 128
SSM_GROUPS = 2
SSM_CONV = 4
CONV_DIM = D_SSM + 2 * SSM_GROUPS * SSM_STATE
D_FF = 2816
FFN_CONV = 3
D_MAIN = 4 * D_HGRN + D_SSM + CONV_DIM
OFF_Q, OFF_F, OFF_I, OFF_G = 0, 1024, 2048, 3072
OFF_Z, OFF_XS, OFF_BC = 4096, 5120, 6144

CHUNK = 128
GROUP_W = D_SSM // SSM_GROUPS
VMEM_LIMIT = 56 * 1024 * 1024


def _cp(sem):
    return pltpu.CompilerParams(dimension_semantics=sem, vmem_limit_bytes=VMEM_LIMIT)


def _dot(a, b):
    return jnp.dot(a, b, preferred_element_type=F32)


def _dot_nt(a, b):
    return lax.dot_general(a, b, (((1,), (1,)), ((), ())), preferred_element_type=F32)


def _split3(x):
    h = x.astype(BF16)
    r = x - h.astype(F32)
    m = r.astype(BF16)
    lo = (r - m.astype(F32)).astype(BF16)
    return h, m, lo


def _dot_exact_lhs01(m01, x):
    h, m, lo = _split3(x)
    return _dot(m01, h) + _dot(m01, m) + _dot(m01, lo)


def _dot_exact_rhs01(x, m01):
    h, m, lo = _split3(x)
    return _dot(h, m01) + _dot(m, m01) + _dot(lo, m01)


def _dot_split_lhs01(m01, x):
    h = x.astype(BF16)
    lo = (x - h.astype(F32)).astype(BF16)
    return _dot(m01, h) + _dot(m01, lo)


def _sigmoid(x):
    return 1.0 / (1.0 + jnp.exp(-x))


def _silu(x):
    return x * _sigmoid(x)


def _rms(x, w):
    ms = jnp.mean(x * x, axis=-1, keepdims=True)
    return x * lax.rsqrt(ms + EPS) * w


def _inproj_kernel(x_ref, nw_ref, w_ref, wdt_ref, lb_ref, dtb_ref, mix_ref, lg_ref, dt_ref):
    hb = _rms(x_ref[...], nw_ref[...]).astype(BF16)

    def put(off, val):
        mix_ref[:, off:off + val.shape[1]] = val.astype(BF16)

    w = D_HGRN
    qf = _dot(hb, w_ref[:, OFF_Q:OFF_Q + 2 * w])
    lb = _hgrn_lb(lb_ref[...])
    f = lb + (1.0 - lb) * _sigmoid(qf[:, w:])
    lg_ref[...] = jnp.log(f)
    put(OFF_F, 1.0 - f)
    put(OFF_Q, _silu(qf[:, :w]))
    ig = _dot(hb, w_ref[:, OFF_I:OFF_I + 2 * w])
    put(OFF_I, ig[:, :w])
    put(OFF_G, _silu(ig[:, w:]))
    zx = _dot(hb, w_ref[:, OFF_Z:OFF_Z + D_SSM + CONV_DIM])
    put(OFF_Z, _silu(zx[:, :D_SSM]))
    put(OFF_XS, zx[:, D_SSM:])
    dt_ref[...] = _softplus(_dot(hb, wdt_ref[...]) + dtb_ref[...])


INPROJ_ROW_TILE = 512


def _inproj(x2d, norm_w, w_main, w_dt, lb_raw, dt_bias):
    n = x2d.shape[0]
    tm = min(INPROJ_ROW_TILE, n)
    assert n % tm == 0
    row = lambda w: pl.BlockSpec((tm, w), lambda i: (i, 0))
    resident = lambda shape: pl.BlockSpec(shape, lambda i: (0, 0), pipeline_mode=pl.Buffered(1))
    return pl.pallas_call(
        _inproj_kernel,
        grid=(n // tm,),
        in_specs=[
            row(D_MODEL), resident((1, D_MODEL)),
            resident((D_MODEL, D_MAIN)), resident((D_MODEL, LANES)),
            resident(lb_raw.shape), resident((1, LANES)),
        ],
        out_specs=[row(D_MAIN), row(D_HGRN), row(LANES)],
        out_shape=[
            jax.ShapeDtypeStruct((n, D_MAIN), BF16),
            jax.ShapeDtypeStruct((n, D_HGRN), F32),
            jax.ShapeDtypeStruct((n, LANES), F32),
        ],
        compiler_params=_cp(("arbitrary",)),
        name="inproj",
    )(x2d, norm_w, w_main, w_dt, lb_raw, dt_bias)


CAST_BLOCK = 512


def _inproj_cast_kernel(x_ref, nw_ref, w_ref, wdt_ref, lb_ref, dtb_ref,
                        wb_ref, mix_ref, lg_ref, dt_ref, h_scr):
    j = pl.program_id(0)
    blk = CAST_BLOCK
    q0, f0, i0, g0, z0, x0 = (off // blk for off in (OFF_Q, OFF_F, OFF_I, OFF_G, OFF_Z, OFF_XS))

    @pl.when(j == 0)
    def _():
        hb = _rms(x_ref[...], nw_ref[...]).astype(BF16)
        h_scr[...] = hb
        dt_ref[...] = _softplus(_dot(hb, wdt_ref[...]) + dtb_ref[...])

    wb = w_ref[...].astype(BF16)
    wb_ref[...] = wb
    p = _dot(h_scr[...], wb)

    @pl.when(((j >= q0) & (j < f0)) | ((j >= g0) & (j < x0)))
    def _():
        mix_ref[...] = _silu(p).astype(BF16)

    @pl.when(((j >= i0) & (j < g0)) | (j >= x0))
    def _():
        mix_ref[...] = p.astype(BF16)

    for fj in range(f0, i0):
        @pl.when(j == fj)
        def _(fj=fj):
            cols = slice((fj - f0) * blk, (fj - f0 + 1) * blk)
            lb = _hgrn_lb(lb_ref[:, cols])
            f = lb + (1.0 - lb) * _sigmoid(p)
            lg_ref[:, cols] = jnp.log(f)
            mix_ref[...] = (1.0 - f).astype(BF16)


def _inproj_cast(x2d, norm_w, w_f32, w_dt, lb_raw, dt_bias):
    n = x2d.shape[0]
    blk = CAST_BLOCK
    const = lambda shape: pl.BlockSpec(shape, lambda j: (0, 0))
    return pl.pallas_call(
        _inproj_cast_kernel,
        grid=(D_MAIN // blk,),
        in_specs=[
            const((n, D_MODEL)), const((1, D_MODEL)),
            pl.BlockSpec((D_MODEL, blk), lambda j: (0, j)), const((D_MODEL, LANES)),
            const(lb_raw.shape), const((1, LANES)),
        ],
        out_specs=[
            pl.BlockSpec((D_MODEL, blk), lambda j: (0, j)),
            pl.BlockSpec((n, blk), lambda j: (0, j)),
            const((n, D_HGRN)), const((n, LANES)),
        ],
        out_shape=[
            jax.ShapeDtypeStruct((D_MODEL, D_MAIN), BF16),
            jax.ShapeDtypeStruct((n, D_MAIN), BF16),
            jax.ShapeDtypeStruct((n, D_HGRN), F32),
            jax.ShapeDtypeStruct((n, LANES), F32),
        ],
        scratch_shapes=[pltpu.VMEM((n, D_MODEL), BF16)],
        compiler_params=_cp(("arbitrary",)),
        name="inproj_cast",
    )(x2d, norm_w, w_f32, w_dt, lb_raw, dt_bias)


LOG2E = 1.4426950408889634
N_LEVELS = 7
MXU_LEVEL_HALVES = (4, 2)


def _hgrn_const():
    c = CHUNK
    t = np.arange(c)[:, None]
    j = np.arange(c)[None, :]
    blocks = [(j <= t)]
    for h in MXU_LEVEL_HALVES:
        mid = (t // (2 * h)) * (2 * h) + h
        upper = (t >= mid) & (j >= mid) & (j <= t)
        lower = (t < mid) & (j > t) & (j < mid)
        blocks.append(upper | lower)
    return np.concatenate(blocks, axis=0).astype(np.float32)


def _midpoint_decay(b, h):
    pieces = []
    for start in range(0, CHUNK, 2 * h):
        mid = start + h
        m = b[mid - 1:mid, :]
        pieces.append(m - b[start:mid])
        pieces.append(b[mid:mid + h] - m)
    return jnp.concatenate(pieces, axis=0)


def _mix_rows(q, k, h):
    pieces = []
    for start in range(0, CHUNK, 2 * h):
        pieces.append(k[start:start + h])
        pieces.append(q[start + h:start + 2 * h])
    return jnp.concatenate(pieces, axis=0)


def _hgrn_lb(lb_raw):
    mx = jnp.max(lb_raw, axis=0, keepdims=True)
    e = jnp.exp(lb_raw - mx)
    return e[0:1, :] / jnp.sum(e, axis=0, keepdims=True)


def _level_map():
    t = lax.broadcasted_iota(jnp.int32, (CHUNK, CHUNK), 0)
    s = lax.broadcasted_iota(jnp.int32, (CHUNK, CHUNK), 1)
    bitlen = 32 - lax.clz(t ^ s)
    return jnp.where(t > s, bitlen, jnp.where(t == s, 0, -1))


def _hgrn_prompt_kernel(q_ref, k_ref, i_ref, g_ref, lg_ref, nw_ref, mc_ref,
                        o_ref, s_out_ref, st_scr):
    c = pl.program_id(1)

    @pl.when(c == 0)
    def _():
        st_scr[...] = jnp.zeros_like(st_scr)

    n_sub = q_ref.shape[0] // CHUNK
    lev = _level_map().astype(jnp.int16)
    row = lax.broadcasted_iota(jnp.int32, (CHUNK, HGRN_DK), 0)
    heads = range(HGRN_HEADS)
    pairs = [(s, h) for s in range(n_sub) for h in heads]
    rs = {s: slice(s * CHUNK, (s + 1) * CHUNK) for s in range(n_sub)}
    cs = {h: slice(h * HGRN_DK, (h + 1) * HGRN_DK) for h in heads}

    e_sub = {s: _dot_split_lhs01(mc_ref[...], lg_ref[rs[s], :] * LOG2E) for s in range(n_sub)}
    qb = {(s, h): q_ref[rs[s], cs[h]] for s, h in pairs}
    kb = {(s, h): k_ref[rs[s], cs[h]] for s, h in pairs}
    vb = {(s, h): i_ref[rs[s], cs[h]] for s, h in pairs}
    q = {p: qb[p].astype(F32) for p in pairs}
    k = {p: kb[p].astype(F32) for p in pairs}
    b = {(s, h): e_sub[s][0:CHUNK, cs[h]] for s, h in pairs}
    b_last = {p: b[p][CHUNK - 1:CHUNK, :] for p in pairs}

    st = {h: st_scr[h] for h in heads}
    o = {}
    for s, h in pairs:
        p = (s, h)
        o[p] = _dot((q[p] * jnp.exp2(b[p])).astype(BF16), st[h].T.astype(BF16))
        ks = (k[p] * jnp.exp2(b_last[p] - b[p])).astype(BF16)
        st[h] = st[h] * jnp.exp2(b_last[p]) + _dot(vb[p].astype(F32).T.astype(BF16), ks)
    for h in heads:
        st_scr[h] = st[h]

    a = {p: jnp.where(lev == 0, _dot(qb[p], k[p].T.astype(BF16)).astype(BF16), jnp.zeros((), BF16))
         for p in pairs}
    half = CHUNK // 2
    while half >= 1:
        for p in pairs:
            if half >= SUBLANES:
                x = _mix_rows(q[p], k[p], half) * jnp.exp2(_midpoint_decay(b[p], half))
            else:
                upper = (row & half) != 0
                if half in MXU_LEVEL_HALVES:
                    blk = 1 + MXU_LEVEL_HALVES.index(half)
                    w = jnp.exp2(e_sub[p[0]][blk * CHUNK:(blk + 1) * CHUNK, cs[p[1]]])
                    x = jnp.where(upper, q[p], k[p]) * w
                else:
                    x = jnp.where(upper, q[p] * (1.0 - k[p]), k[p])
            gram = _dot(x.astype(BF16), x.T.astype(BF16))
            a[p] = jnp.where(lev == half.bit_length(), gram.astype(BF16), a[p])
        half //= 2

    for p in pairs:
        o[p] = o[p] + _dot(a[p], vb[p])
    for s, h in pairs:
        gate = g_ref[rs[s], cs[h]].astype(F32)
        o_ref[rs[s], cs[h]] = (_rms(o[(s, h)], nw_ref[...]) * gate).astype(BF16)

    @pl.when(c == pl.num_programs(1) - 1)
    def _():
        for h in range(HGRN_HEADS):
            s_out_ref[0, h] = st_scr[h].T


HGRN_SUBCHUNKS = 2


def _hgrn_prompt(mix, lg, norm_w, mconst, batch, seq):
    rows = HGRN_SUBCHUNKS * CHUNK
    assert seq % rows == 0
    nc = seq // rows

    def col(off):
        return pl.BlockSpec((rows, D_HGRN), lambda b, c: (b * nc + c, off // D_HGRN))

    return pl.pallas_call(
        _hgrn_prompt_kernel,
        grid=(batch, nc),
        in_specs=[
            col(OFF_Q), col(OFF_F), col(OFF_I), col(OFF_G), col(0),
            pl.BlockSpec((1, HGRN_DV), lambda b, c: (0, 0)),
            pl.BlockSpec(mconst.shape, lambda b, c: (0, 0)),
        ],
        out_specs=[
            pl.BlockSpec((rows, D_HGRN), lambda b, c: (b * nc + c, 0)),
            pl.BlockSpec((1, HGRN_HEADS, HGRN_DK, HGRN_DV), lambda b, c: (b, 0, 0, 0)),
        ],
        out_shape=[
            jax.ShapeDtypeStruct((batch * seq, D_HGRN), BF16),
            jax.ShapeDtypeStruct((batch, HGRN_HEADS, HGRN_DK, HGRN_DV), F32),
        ],
        scratch_shapes=[pltpu.VMEM((HGRN_HEADS, HGRN_DV, HGRN_DK), F32)],
        compiler_params=_cp(("arbitrary", "arbitrary")),
        name="hgrn_prompt",
    )(mix, mix, mix, mix, lg, norm_w, mconst)


def _hgrn_step_kernel(q_ref, i_ref, g_ref, lg_ref, nw_ref, s_ref,
                      o_ref, s_out_ref, o_scr):
    nb = q_ref.shape[0]
    qb = q_ref[...]
    f_t = jnp.exp(lg_ref[...]).T
    v = i_ref[...].astype(F32)
    lhs_rows = 2 * SUBLANES
    for t in range(nb):
        v_row = v[t:t + 1, :]
        s_new = v_row + f_t[:, t:t + 1] * (s_ref[t, 0] - v_row)
        s_out_ref[t, 0] = s_new
        q_rows = jnp.broadcast_to(qb[t:t + 1, :], (lhs_rows, HGRN_DK))
        o_scr[t:t + 1, :] = _dot(q_rows, s_new.astype(BF16))[0:1, :]
    o_ref[...] = (_rms(o_scr[...], nw_ref[...]) * g_ref[...].astype(F32)).astype(BF16)


def _hgrn_step(mix, lg, norm_w, state):
    nb = state.shape[0]
    hb = lambda off: off // HGRN_DK

    def col(off):
        return pl.BlockSpec((nb, HGRN_DK), lambda h: (0, hb(off) + h))

    st_spec = pl.BlockSpec((nb, 1, HGRN_DK, HGRN_DV), lambda h: (0, h, 0, 0))
    return pl.pallas_call(
        _hgrn_step_kernel,
        grid=(HGRN_HEADS,),
        in_specs=[
            col(OFF_Q), col(OFF_I), col(OFF_G), col(0),
            pl.BlockSpec((1, HGRN_DV), lambda h: (0, 0)),
            st_spec,
        ],
        out_specs=[pl.BlockSpec((nb, HGRN_DV), lambda h: (0, h)), st_spec],
        out_shape=[
            jax.ShapeDtypeStruct((nb, D_HGRN), BF16),
            jax.ShapeDtypeStruct(state.shape, F32),
        ],
        scratch_shapes=[pltpu.VMEM((nb, HGRN_DV), F32)],
        compiler_params=_cp(("arbitrary",)),
        name="hgrn_step",
    )(mix, mix, mix, lg, norm_w, state)


def _head_expand(width=SSM_HEAD_DIM):
    e = np.zeros((LANES, SSM_HEADS * width), np.float32)
    for h in range(SSM_HEADS):
        e[h, h * width:(h + 1) * width] = 1.0
    return e


def _softplus(x):
    return jnp.maximum(x, 0.0) + jnp.log(1.0 + jnp.exp(-jnp.abs(x)))


def _ssm_gate_norm(y, z_gate, nw):
    y = y * z_gate.astype(F32)
    parts = [_rms(y[:, g * GROUP_W:(g + 1) * GROUP_W], nw[:, g * GROUP_W:(g + 1) * GROUP_W])
             for g in range(SSM_GROUPS)]
    return jnp.concatenate(parts, axis=-1)


def _ssd_prompt_kernel(z_ref, xs_ref, bc_ref, dt_ref, cw_ref, cb_ref, alog_ref,
                       dvec_ref, nw_ref, tri_ref, exp_ref,
                       shift_ref, y_ref, st_out_ref, xprev_scr, st_scr):
    c = pl.program_id(1)
    t = CHUNK

    @pl.when(c == 0)
    def _():
        st_scr[...] = jnp.zeros_like(st_scr)
        xprev_scr[...] = jnp.zeros_like(xprev_scr)

    subs = range(xs_ref.shape[0] // t)
    rs = [slice(s * t, (s + 1) * t) for s in subs]

    x_cur = [jnp.concatenate([xs_ref[r, :], bc_ref[r, :]], axis=-1) for r in rs]
    x_prev = [xprev_scr[...]] + x_cur[:-1]
    xprev_scr[...] = x_cur[-1]
    taps = [_dot(shift_ref[...], jnp.concatenate([x_prev[s], x_cur[s]], axis=0)) for s in subs]
    xbc = []
    for s in subs:
        acc = cb_ref[...] + cw_ref[SSM_CONV - 1:SSM_CONV, :] * x_cur[s].astype(F32)
        for d in range(1, SSM_CONV):
            acc = acc + cw_ref[SSM_CONV - 1 - d:SSM_CONV - d, :] * taps[s][(d - 1) * t:d * t, :]
        xbc.append(_silu(acc))
    xs = [x[:, 0:D_SSM] for x in xbc]

    dt = [dt_ref[r, :] for r in rs]
    neg_a = -LOG2E * jnp.exp(alog_ref[...])
    cs = [_dot_exact_lhs01(tri_ref[...], dt[s] * neg_a) for s in subs]
    ex = exp_ref[...]
    dt_full = [_dot_exact_rhs01(dt[s], ex) for s in subs]
    cs_full = [_dot_exact_rhs01(cs[s], ex) for s in subs]
    cs_last_full = [x[t - 1:t, :] for x in cs_full]
    x_dt = [xs[s] * dt_full[s] for s in subs]
    x_end = [(x_dt[s] * jnp.exp2(cs_last_full[s] - cs_full[s])).astype(BF16) for s in subs]
    cs_t = [x.T for x in cs]

    causal = (lax.broadcasted_iota(jnp.int32, (t, t), 0)
              >= lax.broadcasted_iota(jnp.int32, (t, t), 1))
    lane = lax.broadcasted_iota(jnp.int32, (1, D_SSM), 1)
    odd_head = (lane & SSM_HEAD_DIM) != 0
    x_by_parity = []
    for s in subs:
        x_b = x_dt[s].astype(BF16)
        zero = jnp.zeros_like(x_b)
        x_by_parity.append((jnp.where(odd_head, zero, x_b), jnp.where(odd_head, x_b, zero)))
    heads_per_group = SSM_HEADS // SSM_GROUPS
    pair_w = 2 * SSM_HEAD_DIM
    never = -1e30

    def group_bc(s, g):
        b_g = xbc[s][:, D_SSM + g * SSM_STATE:D_SSM + (g + 1) * SSM_STATE]
        c_off = D_SSM + SSM_GROUPS * SSM_STATE + g * SSM_STATE
        return b_g.T.astype(BF16), xbc[s][:, c_off:c_off + SSM_STATE].astype(BF16)

    bc_t = {(s, g): group_bc(s, g) for s in subs for g in range(SSM_GROUPS)}

    y_diag = []
    for s in subs:
        y_parts = []
        for g in range(SSM_GROUPS):
            b_t, c_g = bc_t[(s, g)]
            gmat = _dot(c_g, b_t)
            for pp in range(heads_per_group // 2):
                h0 = g * heads_per_group + 2 * pp
                psl = slice(h0 * SSM_HEAD_DIM, h0 * SSM_HEAD_DIM + pair_w)
                yp = None
                for sub in range(2):
                    h = h0 + sub
                    diff = cs[s][:, h:h + 1] - cs_t[s][h:h + 1, :]
                    w = jnp.exp2(jnp.where(causal, diff, never)) * gmat
                    part = _dot(w.astype(BF16), x_by_parity[s][sub][:, psl])
                    yp = part if yp is None else yp + part
                y_parts.append(yp)
        y_diag.append(jnp.concatenate(y_parts, axis=-1))

    y_off = []
    for s in subs:
        offs = []
        for g in range(SSM_GROUPS):
            sl = slice(g * GROUP_W, (g + 1) * GROUP_W)
            b_t, c_g = bc_t[(s, g)]
            st_g = st_scr[:, sl]
            offs.append(_dot(c_g, st_g.astype(BF16)))
            st_scr[:, sl] = st_g * jnp.exp2(cs_last_full[s][:, sl]) + _dot(b_t, x_end[s][:, sl])
        y_off.append(jnp.concatenate(offs, axis=-1) * jnp.exp2(cs_full[s]))

    for s in subs:
        y = y_diag[s] + y_off[s] + dvec_ref[...] * xs[s]
        y_ref[rs[s], :] = _ssm_gate_norm(y, z_ref[rs[s], :], nw_ref[...]).astype(BF16)

    @pl.when(c == pl.num_programs(1) - 1)
    def _():
        for j in range(D_SSM // LANES):
            st_out_ref[0, j * LANES:(j + 1) * LANES, :] = st_scr[:, j * LANES:(j + 1) * LANES].T


SSD_SUBCHUNKS = 1


def _ssd_prompt(mix, dt, conv_w, conv_b, a_log, d_full, norm_w, tri, expand, batch, seq):
    rows = SSD_SUBCHUNKS * CHUNK
    assert seq % rows == 0
    nc = seq // rows
    const = lambda shape: pl.BlockSpec(shape, lambda b, c: (0, 0))
    return pl.pallas_call(
        _ssd_prompt_kernel,
        grid=(batch, nc),
        in_specs=[
            pl.BlockSpec((rows, D_SSM), lambda b, c: (b * nc + c, OFF_Z // D_SSM)),
            pl.BlockSpec((rows, D_SSM), lambda b, c: (b * nc + c, OFF_XS // D_SSM)),
            pl.BlockSpec((rows, 512), lambda b, c: (b * nc + c, OFF_BC // 512)),
            pl.BlockSpec((rows, LANES), lambda b, c: (b * nc + c, 0)),
            const((SSM_CONV, CONV_DIM)), const((1, CONV_DIM)),
            const((1, LANES)),
            const((1, D_SSM)), const((1, D_SSM)),
            const((CHUNK, CHUNK)), const((LANES, D_SSM)),
            const(((SSM_CONV - 1) * CHUNK, 2 * CHUNK)),
        ],
        out_specs=[
            pl.BlockSpec((rows, D_SSM), lambda b, c: (b * nc + c, 0)),
            pl.BlockSpec((1, D_SSM, SSM_STATE), lambda b, c: (b, 0, 0)),
        ],
        out_shape=[
            jax.ShapeDtypeStruct((batch * seq, D_SSM), BF16),
            jax.ShapeDtypeStruct((batch, D_SSM, SSM_STATE), F32),
        ],
        scratch_shapes=[
            pltpu.VMEM((CHUNK, CONV_DIM), BF16),
            pltpu.VMEM((SSM_STATE, D_SSM), F32),
        ],
        compiler_params=_cp(("arbitrary", "arbitrary")),
        name="ssd_prompt",
    )(mix, mix, mix, dt, conv_w, conv_b, a_log, d_full, norm_w, tri, expand,
      jnp.asarray(_conv_shifts(), BF16))


def _conv_shifts():
    m = np.zeros(((SSM_CONV - 1) * CHUNK, 2 * CHUNK), np.float32)
    for d in range(1, SSM_CONV):
        for t in range(CHUNK):
            m[(d - 1) * CHUNK + t, CHUNK + t - d] = 1.0
    return m


def _ssd_step_kernel(z_ref, xs_ref, bc_ref, dt_ref, b0_ref, b1_ref, b2_ref, cw_ref, cb_ref,
                     alog_ref, dvec_ref, nw_ref, exp_ref, exl_ref, st_ref,
                     y_ref, st_out_ref, xt_scr, at_scr, xs_scr, bc_scr, y_scr):
    p = pl.program_id(0)
    nb = z_ref.shape[0]
    pair_w = 2 * SSM_HEAD_DIM
    pairs_per_group = SSM_HEADS // SSM_GROUPS // 2

    @pl.when(p == 0)
    def _():
        x_new = jnp.concatenate([xs_ref[...], bc_ref[...]], axis=-1).astype(F32)
        acc = (cb_ref[...] + cw_ref[0:1, :] * b0_ref[...] + cw_ref[1:2, :] * b1_ref[...]
               + cw_ref[2:3, :] * b2_ref[...] + cw_ref[3:4, :] * x_new)
        xbc = _silu(acc)
        xs = xbc[:, 0:D_SSM]
        dt = dt_ref[...]
        da = dt * (-jnp.exp(alog_ref[...]))
        ex = exp_ref[...]
        x_dt = xs * _dot_exact_rhs01(dt, ex)
        decay = jnp.exp(_dot_exact_rhs01(da, exl_ref[...]))
        xs_scr[...] = xs
        bc_scr[...] = xbc[:, D_SSM:]
        for j in range(D_SSM // LANES):
            sl = slice(j * LANES, (j + 1) * LANES)
            xt_scr[sl, :] = x_dt[:, sl].T
            at_scr[j] = decay[:, 2 * j * LANES:2 * (j + 1) * LANES]

    g_is_1 = p >= pairs_per_group
    row0 = pl.multiple_of(p * pair_w, pair_w)
    x_t = xt_scr[pl.ds(row0, pair_w), :]
    a_p = at_scr[p]
    bc = bc_scr[...]
    b_all = jnp.where(g_is_1, bc[:, SSM_STATE:2 * SSM_STATE], bc[:, 0:SSM_STATE])
    c_all = jnp.where(g_is_1, bc[:, 3 * SSM_STATE:4 * SSM_STATE],
                      bc[:, 2 * SSM_STATE:3 * SSM_STATE]).astype(BF16)
    for t in range(nb):
        inject = x_t[:, t:t + 1] * b_all[t:t + 1, :]
        halves = []
        for sub in range(2):
            rows = slice(sub * SSM_HEAD_DIM, (sub + 1) * SSM_HEAD_DIM)
            half = a_p[t:t + 1, sub * LANES:(sub + 1) * LANES] * st_ref[t, sub] + inject[rows]
            st_out_ref[t, sub] = half
            halves.append(half)
        new = jnp.concatenate(halves, axis=0)
        c_rows = jnp.broadcast_to(c_all[t:t + 1, :], (SUBLANES, SSM_STATE))
        y_scr[p, t:t + 1, :] = _dot_nt(c_rows, new.astype(BF16))[0:1, :]

    @pl.when(p == pl.num_programs(0) - 1)
    def _():
        y_mix = jnp.concatenate([y_scr[j] for j in range(SSM_HEADS // 2)], axis=-1)
        y = y_mix + dvec_ref[...] * xs_scr[...]
        y_ref[...] = _ssm_gate_norm(y, z_ref[...], nw_ref[...]).astype(BF16)


def _ssd_step(mix, dt, buf, conv_w, conv_b, a_log, d_full, norm_w, expand, state):
    nb = state.shape[0]
    n_pairs = SSM_HEADS // 2
    const = lambda shape: pl.BlockSpec(shape, lambda p: (0, 0))
    st_spec = pl.BlockSpec((nb, 2, SSM_HEAD_DIM, SSM_STATE), lambda p: (0, p, 0, 0))
    return pl.pallas_call(
        _ssd_step_kernel,
        grid=(n_pairs,),
        in_specs=[
            pl.BlockSpec((nb, D_SSM), lambda p: (0, OFF_Z // D_SSM)),
            pl.BlockSpec((nb, D_SSM), lambda p: (0, OFF_XS // D_SSM)),
            pl.BlockSpec((nb, 512), lambda p: (0, OFF_BC // 512)),
            const((nb, LANES)),
            const((nb, CONV_DIM)), const((nb, CONV_DIM)), const((nb, CONV_DIM)),
            const((SSM_CONV, CONV_DIM)), const((1, CONV_DIM)),
            const((1, LANES)),
            const((1, D_SSM)), const((1, D_SSM)),
            const((LANES, D_SSM)), const((LANES, SSM_HEADS * LANES)),
            st_spec,
        ],
        out_specs=[const((nb, D_SSM)), st_spec],
        out_shape=[
            jax.ShapeDtypeStruct((nb, D_SSM), BF16),
            jax.ShapeDtypeStruct(state.shape, F32),
        ],
        scratch_shapes=[
            pltpu.VMEM((D_SSM, nb), F32),
            pltpu.VMEM((n_pairs, nb, 2 * LANES), F32),
            pltpu.VMEM((nb, D_SSM), F32),
            pltpu.VMEM((nb, 2 * SSM_GROUPS * SSM_STATE), F32),
            pltpu.VMEM((n_pairs, nb, 2 * SSM_HEAD_DIM), F32),
        ],
        compiler_params=_cp(("arbitrary",)),
        name="ssd_step",
    )(mix, mix, mix, dt, buf[:, 0], buf[:, 1], buf[:, 2], conv_w, conv_b, a_log,
      d_full, norm_w, expand, jnp.asarray(_head_expand(LANES), BF16), state)


def _ffn_prompt_kernel(oa_ref, ys_ref, x_ref, wo_ref, n2_ref, wg_ref, wv_ref, wd_ref, cw_ref, cb_ref,
                       fnw_ref, y_ref, tail_ref, ge_scr, *, tiles_per_seq):
    i = pl.program_id(0)
    tm = x_ref.shape[0]
    pad = SUBLANES
    x1 = (x_ref[...] + _dot(oa_ref[...], wo_ref[0:D_HGRN, :])
          + _dot(ys_ref[...], wo_ref[D_HGRN:D_HGRN + D_SSM, :]))
    h2 = _rms(x1, n2_ref[...]).astype(BF16)

    seq_start = lax.rem(i, tiles_per_seq) == 0

    @pl.when(seq_start)
    def _():
        ge_scr[0:pad, :] = jnp.zeros((pad, D_FF), F32)

    @pl.when(jnp.logical_not(seq_start))
    def _():
        ge_scr[0:pad, :] = ge_scr[tm:tm + pad, :]

    acc = None
    bounds = np.cumsum((0,) + FFN_COL_BLOCKS)
    for c0, c1 in zip(bounds[:-1].tolist(), bounds[1:].tolist()):
        gate = _dot(h2, wg_ref[:, c0:c1])
        val = _dot(h2, wv_ref[:, c0:c1])
        ge_scr[pad:, c0:c1] = gate
        tail_ref[0, :, c0:c1] = gate[tm - pad:, :]
        conv = (cb_ref[:, c0:c1] + cw_ref[2:3, c0:c1] * gate
                + cw_ref[1:2, c0:c1] * ge_scr[pad - 1:pad - 1 + tm, c0:c1]
                + cw_ref[0:1, c0:c1] * ge_scr[pad - 2:pad - 2 + tm, c0:c1])
        act = (_silu(conv) * val).astype(BF16)
        part = _dot(act, wd_ref[c0:c1, :])
        acc = part if acc is None else acc + part
    y_ref[...] = _rms(x1 + acc, fnw_ref[...])


FFN_ROW_TILE = 512
FFN_COL_BLOCKS = (1024, 1024, 768)
assert sum(FFN_COL_BLOCKS) == D_FF and all(c % LANES == 0 for c in FFN_COL_BLOCKS)


def _ffn_prompt(o_a, y_s, x2d, w_o, norm2_w, w_gate, w_val, w_down, conv_w, conv_b, fnorm_w, seq):
    n = x2d.shape[0]
    tm = FFN_ROW_TILE
    assert seq % tm == 0
    kern = functools.partial(_ffn_prompt_kernel, tiles_per_seq=seq // tm)
    row = lambda w: pl.BlockSpec((tm, w), lambda i: (i, 0))
    resident = lambda shape: pl.BlockSpec(shape, lambda i: (0, 0), pipeline_mode=pl.Buffered(1))
    return pl.pallas_call(
        kern,
        grid=(n // tm,),
        in_specs=[
            row(D_HGRN), row(D_SSM), row(D_MODEL),
            resident((D_HGRN + D_SSM, D_MODEL)), resident((1, D_MODEL)),
            resident((D_MODEL, D_FF)), resident((D_MODEL, D_FF)), resident((D_FF, D_MODEL)),
            resident((FFN_CONV, D_FF)), resident((1, D_FF)), resident((1, D_MODEL)),
        ],
        out_specs=[
            row(D_MODEL),
            pl.BlockSpec((1, SUBLANES, D_FF), lambda i: (i, 0, 0)),
        ],
        out_shape=[
            jax.ShapeDtypeStruct((n, D_MODEL), F32),
            jax.ShapeDtypeStruct((n // tm, SUBLANES, D_FF), F32),
        ],
        scratch_shapes=[pltpu.VMEM((tm + SUBLANES, D_FF), F32)],
        compiler_params=_cp(("arbitrary",)),
        name="ffn_prompt",
    )(o_a, y_s, x2d, w_o, norm2_w, w_gate, w_val, w_down, conv_w, conv_b, fnorm_w)


FF_CAST_BLOCK = 256


def _ffn_step_kernel(oa_ref, ys_ref, x_ref, wo_ref, n2_ref, wg_ref, wv_ref, wd_ref, cw_ref, cb_ref,
                     fnw_ref, b0_ref, b1_ref,
                     y_ref, gate_ref, wob_ref, wgb_ref, wvb_ref, wdb_ref, x1_scr, h2_scr, acc_scr):
    j = pl.program_id(0)

    @pl.when(j == 0)
    def _():
        wo = wo_ref[...].astype(BF16)
        wob_ref[...] = wo
        x1 = (x_ref[...] + _dot(oa_ref[...], wo[0:D_HGRN, :]) + _dot(ys_ref[...], wo[D_HGRN:, :]))
        x1_scr[...] = x1
        h2_scr[...] = _rms(x1, n2_ref[...]).astype(BF16)
        acc_scr[...] = jnp.zeros_like(acc_scr)

    wg = wg_ref[...].astype(BF16)
    wv = wv_ref[...].astype(BF16)
    wd = wd_ref[...].astype(BF16)
    wgb_ref[...] = wg
    wvb_ref[...] = wv
    wdb_ref[...] = wd
    h2 = h2_scr[...]
    gate = _dot(h2, wg)
    val = _dot(h2, wv)
    gate_ref[...] = gate
    conv = (cb_ref[...] + cw_ref[2:3, :] * gate + cw_ref[1:2, :] * b1_ref[...]
            + cw_ref[0:1, :] * b0_ref[...])
    act = (_silu(conv) * val).astype(BF16)
    acc_scr[...] = acc_scr[...] + _dot(act, wd)

    @pl.when(j == pl.num_programs(0) - 1)
    def _():
        y_ref[...] = _rms(x1_scr[...] + acc_scr[...], fnw_ref[...])


def _ffn_step(o_a, y_s, x2d, w_out, norm2_w, w_up, w_down, conv_w, conv_b, fnorm_w, buf):
    n = x2d.shape[0]
    blk = FF_CAST_BLOCK
    nj = D_FF // blk
    const = lambda shape: pl.BlockSpec(shape, lambda j: (0, 0))
    col = lambda rows: pl.BlockSpec((rows, blk), lambda j: (0, j))
    return pl.pallas_call(
        _ffn_step_kernel,
        grid=(nj,),
        in_specs=[
            const((n, D_HGRN)), const((n, D_SSM)), const((n, D_MODEL)),
            const((D_HGRN + D_SSM, D_MODEL)), const((1, D_MODEL)),
            col(D_MODEL), pl.BlockSpec((D_MODEL, blk), lambda j: (0, nj + j)),
            pl.BlockSpec((blk, D_MODEL), lambda j: (j, 0)),
            col(FFN_CONV), col(1), const((1, D_MODEL)), col(n), col(n),
        ],
        out_specs=[
            const((n, D_MODEL)), col(n),
            const((D_HGRN + D_SSM, D_MODEL)), col(D_MODEL), col(D_MODEL),
            pl.BlockSpec((blk, D_MODEL), lambda j: (j, 0)),
        ],
        out_shape=[
            jax.ShapeDtypeStruct((n, D_MODEL), F32),
            jax.ShapeDtypeStruct((n, D_FF), F32),
            jax.ShapeDtypeStruct((D_HGRN + D_SSM, D_MODEL), BF16),
            jax.ShapeDtypeStruct((D_MODEL, D_FF), BF16),
            jax.ShapeDtypeStruct((D_MODEL, D_FF), BF16),
            jax.ShapeDtypeStruct((D_FF, D_MODEL), BF16),
        ],
        scratch_shapes=[
            pltpu.VMEM((n, D_MODEL), F32),
            pltpu.VMEM((n, D_MODEL), BF16),
            pltpu.VMEM((n, D_MODEL), F32),
        ],
        compiler_params=_cp(("arbitrary",)),
        name="ffn_step",
    )(o_a, y_s, x2d, w_out, norm2_w, w_up, w_up, w_down, conv_w, conv_b, fnorm_w,
      buf[:, 0], buf[:, 1])


def _row(v):
    return v.reshape(1, -1).astype(F32)


def _pad_lanes(v):
    return jnp.pad(v.astype(F32), (0, LANES - v.shape[0])).reshape(1, LANES)


def kernel(x_prompt, x_sample, state_hgrn, state_ssm, state_conv_ssm, state_conv_ffn, norm1_w, w_in, hgrn_lb, hgrn_norm_w, ssm_conv_w, ssm_conv_b, ssm_dt_bias, ssm_a_log, ssm_d, ssm_norm_w, w_out, norm2_w, w_up, ffn_conv_w, ffn_conv_b, w_down, final_norm_w):
    depth = w_in.shape[0]
    assert depth == 1, "single-layer trunk"
    l = 0
    batch, seq, _ = x_prompt.shape
    dec_batch, dec_seq, _ = x_sample.shape
    assert dec_seq == 1 and seq % CHUNK == 0 and seq >= SSM_CONV

    w_dt = jnp.pad(w_in[l][:, D_MAIN:], ((0, 0), (0, LANES - SSM_HEADS))).astype(BF16)
    d_full = jnp.repeat(ssm_d[l].astype(F32), SSM_HEAD_DIM).reshape(1, D_SSM)
    dt_bias = _pad_lanes(ssm_dt_bias[l])
    a_log = _pad_lanes(ssm_a_log[l])
    mconst = jnp.asarray(_hgrn_const(), BF16)
    tri = jnp.asarray(np.tril(np.ones((CHUNK, CHUNK), np.float32)), BF16)
    expand = jnp.asarray(_head_expand(), BF16)
    lb_raw = hgrn_lb.astype(F32)

    xs_ = x_sample.reshape(dec_batch, D_MODEL)
    w_main, proj_s, lg_s, dt_s = _inproj_cast(xs_, _row(norm1_w[l]), w_in[l], w_dt, lb_raw, dt_bias)
    oa_s, hgrn_s = _hgrn_step(proj_s, lg_s, _row(hgrn_norm_w[l]), state_hgrn[l])
    ys_s, ssm_s = _ssd_step(proj_s, dt_s, state_conv_ssm[l], ssm_conv_w[l], _row(ssm_conv_b[l]),
                            a_log, d_full, _row(ssm_norm_w[l]), expand, state_ssm[l])
    y_s, gate_s, w_ob, w_gb, w_vb, w_db = _ffn_step(
        oa_s, ys_s, xs_, w_out[l], _row(norm2_w[l]), w_up[l], w_down[l],
        ffn_conv_w[l], _row(ffn_conv_b[l]), _row(final_norm_w), state_conv_ffn[l])
    cs_s = jnp.concatenate([state_conv_ssm[l][:, 1:], proj_s[:, None, OFF_XS:OFF_XS + CONV_DIM]],
                           axis=1)
    cf_s = jnp.concatenate([state_conv_ffn[l][:, 1:], gate_s[:, None, :]], axis=1)

    xp = x_prompt.reshape(batch * seq, D_MODEL)
    proj_p, lg_p, dt_p = _inproj(xp, _row(norm1_w[l]), w_main, w_dt, lb_raw, dt_bias)
    oa_p, hgrn_p = _hgrn_prompt(proj_p, lg_p, _row(hgrn_norm_w[l]), mconst, batch, seq)
    ys_p, ssm_p = _ssd_prompt(proj_p, dt_p, ssm_conv_w[l], _row(ssm_conv_b[l]), a_log,
                              d_full, _row(ssm_norm_w[l]), tri, expand, batch, seq)
    y_p, tail_p = _ffn_prompt(oa_p, ys_p, xp, w_ob, _row(norm2_w[l]), w_gb, w_vb, w_db,
                              ffn_conv_w[l], _row(ffn_conv_b[l]), _row(final_norm_w), seq)
    proj_p3 = proj_p.reshape(batch, seq, D_MAIN)
    cs_p = proj_p3[:, seq - (SSM_CONV - 1):, OFF_XS:OFF_XS + CONV_DIM]
    tails = tail_p.reshape(batch, seq // FFN_ROW_TILE, SUBLANES, D_FF)
    cf_p = tails[:, -1, SUBLANES - (FFN_CONV - 1):, :]

    dt_ = x_prompt.dtype
    return (y_p.reshape(batch, seq, D_MODEL).astype(dt_),
            y_s.reshape(dec_batch, 1, D_MODEL).astype(dt_),
            hgrn_p[None].astype(dt_),
            hgrn_s[None].astype(dt_),
            ssm_p.reshape(1, batch, SSM_HEADS, SSM_HEAD_DIM, SSM_STATE).astype(dt_),
            ssm_s[None].astype(dt_),
            cs_p[None].astype(dt_),
            cs_s[None].astype(dt_),
            cf_p[None].astype(dt_),
            cf_s[None].astype(dt_))
```

```python
import functools

import numpy as np
import jax
import jax.numpy as jnp
from jax import lax
from jax.experimental import pallas as pl
from jax.experimental.pallas import tpu as pltpu

F32 = jnp.float32
BF16 = jnp.bfloat16
EPS = 1e-6

LANES = 128
SUBLANES = 8

D_MODEL = 1024
HGRN_HEADS = 8
HGRN_DK = 128
HGRN_DV = 128
D_HGRN = HGRN_HEADS * HGRN_DV
SSM_HEADS = 16
SSM_HEAD_DIM = 64
D_SSM = SSM_HEADS * SSM_HEAD_DIM
SSM_STATE = 128
SSM_GROUPS = 2
SSM_CONV = 4
CONV_DIM = D_SSM + 2 * SSM_GROUPS * SSM_STATE
D_FF = 2816
FFN_CONV = 3
D_MAIN = 4 * D_HGRN + D_SSM + CONV_DIM
OFF_Q, OFF_F, OFF_I, OFF_G = 0, 1024, 2048, 3072
OFF_Z, OFF_XS, OFF_BC = 4096, 5120, 6144

CHUNK = 128
GROUP_W = D_SSM // SSM_GROUPS
VMEM_LIMIT = 56 * 1024 * 1024


def _cp(sem):
    return pltpu.CompilerParams(dimension_semantics=sem, vmem_limit_bytes=VMEM_LIMIT)


def _dot(a, b):
    return jnp.dot(a, b, preferred_element_type=F32)


def _dot_nt(a, b):
    return lax.dot_general(a, b, (((1,), (1,)), ((), ())), preferred_element_type=F32)


def _split3(x):
    h = x.astype(BF16)
    r = x - h.astype(F32)
    m = r.astype(BF16)
    lo = (r - m.astype(F32)).astype(BF16)
    return h, m, lo


def _dot_exact_lhs01(m01, x):
    h, m, lo = _split3(x)
    return _dot(m01, h) + _dot(m01, m) + _dot(m01, lo)


def _dot_exact_rhs01(x, m01):
    h, m, lo = _split3(x)
    return _dot(h, m01) + _dot(m, m01) + _dot(lo, m01)


def _dot_split_lhs01(m01, x):
    h = x.astype(BF16)
    lo = (x - h.astype(F32)).astype(BF16)
    return _dot(m01, h) + _dot(m01, lo)


def _sigmoid(x):
    return 1.0 / (1.0 + jnp.exp(-x))


def _silu(x):
    return x * _sigmoid(x)


def _rms(x, w):
    ms = jnp.mean(x * x, axis=-1, keepdims=True)
    return x * lax.rsqrt(ms + EPS) * w


def _dt_proj(hb, wdt_ref):
    rows = lax.broadcasted_iota(jnp.int32, wdt_ref.shape, 0)
    wdt = jnp.where(rows < SSM_HEADS, wdt_ref[...], 0.0).astype(BF16)
    return _dot_nt(hb, wdt)

def _inproj_kernel(x_ref, nw_ref, w_ref, wdt_ref, lb_ref, dtb_ref, mix_ref, lg_ref, dt_ref):
    hb = _rms(x_ref[...], nw_ref[...]).astype(BF16)

    def put(off, val):
        mix_ref[:, off:off + val.shape[1]] = val.astype(BF16)

    w = D_HGRN
    qf = _dot(hb, w_ref[:, OFF_Q:OFF_Q + 2 * w])
    lb = _hgrn_lb(lb_ref[...])
    f = lb + (1.0 - lb) * _sigmoid(qf[:, w:])
    lg_ref[...] = jnp.log(f)
    put(OFF_F, 1.0 - f)
    put(OFF_Q, _silu(qf[:, :w]))
    ig = _dot(hb, w_ref[:, OFF_I:OFF_I + 2 * w])
    put(OFF_I, ig[:, :w])
    put(OFF_G, _silu(ig[:, w:]))
    zx = _dot(hb, w_ref[:, OFF_Z:OFF_Z + D_SSM + CONV_DIM])
    put(OFF_Z, _silu(zx[:, :D_SSM]))
    put(OFF_XS, zx[:, D_SSM:])
    dt_ref[...] = _softplus(_dt_proj(hb, wdt_ref) + dtb_ref[...])


INPROJ_ROW_TILE = 512


def _inproj(x2d, norm_w, w_main, w_t, lb_raw, dt_bias):
    n = x2d.shape[0]
    tm = min(INPROJ_ROW_TILE, n)
    assert n % tm == 0
    row = lambda w: pl.BlockSpec((tm, w), lambda i: (i, 0))
    resident = lambda shape: pl.BlockSpec(shape, lambda i: (0, 0), pipeline_mode=pl.Buffered(1))
    return pl.pallas_call(
        _inproj_kernel,
        grid=(n // tm,),
        in_specs=[
            row(D_MODEL), resident((1, D_MODEL)),
            resident((D_MODEL, D_MAIN)),
            pl.BlockSpec((LANES, D_MODEL), lambda i: (D_MAIN // LANES, 0), pipeline_mode=pl.Buffered(1)),
            resident(lb_raw.shape), resident((1, LANES)),
        ],
        out_specs=[row(D_MAIN), row(D_HGRN), row(LANES)],
        out_shape=[
            jax.ShapeDtypeStruct((n, D_MAIN), BF16),
            jax.ShapeDtypeStruct((n, D_HGRN), F32),
            jax.ShapeDtypeStruct((n, LANES), F32),
        ],
        compiler_params=_cp(("arbitrary",)),
        name="inproj",
    )(x2d, norm_w, w_main, w_t, lb_raw, dt_bias)


CAST_BLOCK = 512


def _inproj_cast_kernel(x_ref, nw_ref, w_ref, wdt_ref, lb_ref, dtb_ref,
                        wb_ref, mix_ref, lg_ref, dt_ref, h_scr):
    j = pl.program_id(0)
    blk = CAST_BLOCK
    q0, f0, i0, g0, z0, x0 = (off // blk for off in (OFF_Q, OFF_F, OFF_I, OFF_G, OFF_Z, OFF_XS))

    @pl.when(j == 0)
    def _():
        hb = _rms(x_ref[...], nw_ref[...]).astype(BF16)
        h_scr[...] = hb
        dt_ref[...] = _softplus(_dt_proj(hb, wdt_ref) + dtb_ref[...])

    wb = w_ref[...].T.astype(BF16)
    wb_ref[...] = wb
    p = _dot(h_scr[...], wb)

    @pl.when(((j >= q0) & (j < f0)) | ((j >= g0) & (j < x0)))
    def _():
        mix_ref[...] = _silu(p).astype(BF16)

    @pl.when(((j >= i0) & (j < g0)) | (j >= x0))
    def _():
        mix_ref[...] = p.astype(BF16)

    for fj in range(f0, i0):
        @pl.when(j == fj)
        def _(fj=fj):
            cols = slice((fj - f0) * blk, (fj - f0 + 1) * blk)
            lb = _hgrn_lb(lb_ref[:, cols])
            f = lb + (1.0 - lb) * _sigmoid(p)
            lg_ref[:, cols] = jnp.log(f)
            mix_ref[...] = (1.0 - f).astype(BF16)


def _inproj_cast(x2d, norm_w, w_t, lb_raw, dt_bias):
    n = x2d.shape[0]
    blk = CAST_BLOCK
    const = lambda shape: pl.BlockSpec(shape, lambda j: (0, 0))
    return pl.pallas_call(
        _inproj_cast_kernel,
        grid=(D_MAIN // blk,),
        in_specs=[
            const((n, D_MODEL)), const((1, D_MODEL)),
            pl.BlockSpec((blk, D_MODEL), lambda j: (j, 0)),
            pl.BlockSpec((LANES, D_MODEL), lambda j: (D_MAIN // LANES, 0)),
            const(lb_raw.shape), const((1, LANES)),
        ],
        out_specs=[
            pl.BlockSpec((D_MODEL, blk), lambda j: (0, j)),
            pl.BlockSpec((n, blk), lambda j: (0, j)),
            const((n, D_HGRN)), const((n, LANES)),
        ],
        out_shape=[
            jax.ShapeDtypeStruct((D_MODEL, D_MAIN), BF16),
            jax.ShapeDtypeStruct((n, D_MAIN), BF16),
            jax.ShapeDtypeStruct((n, D_HGRN), F32),
            jax.ShapeDtypeStruct((n, LANES), F32),
        ],
        scratch_shapes=[pltpu.VMEM((n, D_MODEL), BF16)],
        compiler_params=_cp(("arbitrary",)),
        name="inproj_cast",
    )(x2d, norm_w, w_t, w_t, lb_raw, dt_bias)


LOG2E = 1.4426950408889634
N_LEVELS = 7
MXU_LEVEL_HALVES = (4, 2)


def _hgrn_const():
    c = CHUNK
    t = np.arange(c)[:, None]
    j = np.arange(c)[None, :]
    blocks = [(j <= t)]
    for h in MXU_LEVEL_HALVES:
        mid = (t // (2 * h)) * (2 * h) + h
        upper = (t >= mid) & (j >= mid) & (j <= t)
        lower = (t < mid) & (j > t) & (j < mid)
        blocks.append(upper | lower)
    return np.concatenate(blocks, axis=0).astype(np.float32)


def _midpoint_decay(b, h):
    pieces = []
    for start in range(0, CHUNK, 2 * h):
        mid = start + h
        m = b[mid - 1:mid, :]
        pieces.append(m - b[start:mid])
        pieces.append(b[mid:mid + h] - m)
    return jnp.concatenate(pieces, axis=0)


def _mix_rows(q, k, h):
    pieces = []
    for start in range(0, CHUNK, 2 * h):
        pieces.append(k[start:start + h])
        pieces.append(q[start + h:start + 2 * h])
    return jnp.concatenate(pieces, axis=0)


def _hgrn_lb(lb_raw):
    mx = jnp.max(lb_raw, axis=0, keepdims=True)
    e = jnp.exp(lb_raw - mx)
    return e[0:1, :] / jnp.sum(e, axis=0, keepdims=True)


def _level_map():
    t = lax.broadcasted_iota(jnp.int32, (CHUNK, CHUNK), 0)
    s = lax.broadcasted_iota(jnp.int32, (CHUNK, CHUNK), 1)
    bitlen = 32 - lax.clz(t ^ s)
    return jnp.where(t > s, bitlen, jnp.where(t == s, 0, -1))


def _hgrn_prompt_kernel(q_ref, k_ref, i_ref, g_ref, lg_ref, nw_ref, mc_ref,
                        o_ref, s_out_ref, st_scr):
    c = pl.program_id(1)

    @pl.when(c == 0)
    def _():
        st_scr[...] = jnp.zeros_like(st_scr)

    n_sub = q_ref.shape[0] // CHUNK
    lev = _level_map().astype(jnp.int16)
    row = lax.broadcasted_iota(jnp.int32, (CHUNK, HGRN_DK), 0)
    heads = range(HGRN_HEADS)
    pairs = [(s, h) for s in range(n_sub) for h in heads]
    rs = {s: slice(s * CHUNK, (s + 1) * CHUNK) for s in range(n_sub)}
    cs = {h: slice(h * HGRN_DK, (h + 1) * HGRN_DK) for h in heads}

    e_sub = {s: _dot_split_lhs01(mc_ref[...], lg_ref[rs[s], :] * LOG2E) for s in range(n_sub)}
    qb = {(s, h): q_ref[rs[s], cs[h]] for s, h in pairs}
    kb = {(s, h): k_ref[rs[s], cs[h]] for s, h in pairs}
    vb = {(s, h): i_ref[rs[s], cs[h]] for s, h in pairs}
    q = {p: qb[p].astype(F32) for p in pairs}
    k = {p: kb[p].astype(F32) for p in pairs}
    b = {(s, h): e_sub[s][0:CHUNK, cs[h]] for s, h in pairs}
    b_last = {p: b[p][CHUNK - 1:CHUNK, :] for p in pairs}

    st = {h: st_scr[h] for h in heads}
    o = {}
    for s, h in pairs:
        p = (s, h)
        o[p] = _dot((q[p] * jnp.exp2(b[p])).astype(BF16), st[h].T.astype(BF16))
        ks = (k[p] * jnp.exp2(b_last[p] - b[p])).astype(BF16)
        st[h] = st[h] * jnp.exp2(b_last[p]) + _dot(vb[p].astype(F32).T.astype(BF16), ks)
    for h in heads:
        st_scr[h] = st[h]

    a = {p: jnp.where(lev == 0, _dot(qb[p], k[p].T.astype(BF16)).astype(BF16), jnp.zeros((), BF16))
         for p in pairs}
    half = CHUNK // 2
    while half >= 1:
        for p in pairs:
            if half >= SUBLANES:
                x = _mix_rows(q[p], k[p], half) * jnp.exp2(_midpoint_decay(b[p], half))
            else:
                upper = (row & half) != 0
                if half in MXU_LEVEL_HALVES:
                    blk = 1 + MXU_LEVEL_HALVES.index(half)
                    w = jnp.exp2(e_sub[p[0]][blk * CHUNK:(blk + 1) * CHUNK, cs[p[1]]])
                    x = jnp.where(upper, q[p], k[p]) * w
                else:
                    x = jnp.where(upper, q[p] * (1.0 - k[p]), k[p])
            gram = _dot(x.astype(BF16), x.T.astype(BF16))
            a[p] = jnp.where(lev == half.bit_length(), gram.astype(BF16), a[p])
        half //= 2

    for p in pairs:
        o[p] = o[p] + _dot(a[p], vb[p])
    for s, h in pairs:
        gate = g_ref[rs[s], cs[h]].astype(F32)
        o_ref[rs[s], cs[h]] = (_rms(o[(s, h)], nw_ref[...]) * gate).astype(BF16)

    @pl.when(c == pl.num_programs(1) - 1)
    def _():
        for h in range(HGRN_HEADS):
            s_out_ref[0, h] = st_scr[h].T


HGRN_SUBCHUNKS = 2


def _hgrn_prompt(mix, lg, norm_w, mconst, batch, seq):
    rows = HGRN_SUBCHUNKS * CHUNK
    assert seq % rows == 0
    nc = seq // rows

    def col(off):
        return pl.BlockSpec((rows, D_HGRN), lambda b, c: (b * nc + c, off // D_HGRN))

    return pl.pallas_call(
        _hgrn_prompt_kernel,
        grid=(batch, nc),
        in_specs=[
            col(OFF_Q), col(OFF_F), col(OFF_I), col(OFF_G), col(0),
            pl.BlockSpec((1, HGRN_DV), lambda b, c: (0, 0)),
            pl.BlockSpec(mconst.shape, lambda b, c: (0, 0)),
        ],
        out_specs=[
            pl.BlockSpec((rows, D_HGRN), lambda b, c: (b * nc + c, 0)),
            pl.BlockSpec((1, HGRN_HEADS, HGRN_DK, HGRN_DV), lambda b, c: (b, 0, 0, 0)),
        ],
        out_shape=[
            jax.ShapeDtypeStruct((batch * seq, D_HGRN), BF16),
            jax.ShapeDtypeStruct((batch, HGRN_HEADS, HGRN_DK, HGRN_DV), F32),
        ],
        scratch_shapes=[pltpu.VMEM((HGRN_HEADS, HGRN_DV, HGRN_DK), F32)],
        compiler_params=_cp(("arbitrary", "arbitrary")),
        name="hgrn_prompt",
    )(mix, mix, mix, mix, lg, norm_w, mconst)


def _hgrn_step_kernel(q_ref, i_ref, g_ref, lg_ref, nw_ref, s_ref,
                      o_ref, s_out_ref, o_scr):
    nb = q_ref.shape[0]
    qb = q_ref[...]
    f_t = jnp.exp(lg_ref[...]).T
    v = i_ref[...].astype(F32)
    lhs_rows = 2 * SUBLANES
    for t in range(nb):
        v_row = v[t:t + 1, :]
        s_new = v_row + f_t[:, t:t + 1] * (s_ref[t, 0] - v_row)
        s_out_ref[t, 0] = s_new
        q_rows = jnp.broadcast_to(qb[t:t + 1, :], (lhs_rows, HGRN_DK))
        o_scr[t:t + 1, :] = _dot(q_rows, s_new.astype(BF16))[0:1, :]
    o_ref[...] = (_rms(o_scr[...], nw_ref[...]) * g_ref[...].astype(F32)).astype(BF16)


def _hgrn_step(mix, lg, norm_w, state):
    nb = state.shape[0]
    hb = lambda off: off // HGRN_DK

    def col(off):
        return pl.BlockSpec((nb, HGRN_DK), lambda h: (0, hb(off) + h))

    st_spec = pl.BlockSpec((nb, 1, HGRN_DK, HGRN_DV), lambda h: (0, h, 0, 0))
    return pl.pallas_call(
        _hgrn_step_kernel,
        grid=(HGRN_HEADS,),
        in_specs=[
            col(OFF_Q), col(OFF_I), col(OFF_G), col(0),
            pl.BlockSpec((1, HGRN_DV), lambda h: (0, 0)),
            st_spec,
        ],
        out_specs=[pl.BlockSpec((nb, HGRN_DV), lambda h: (0, h)), st_spec],
        out_shape=[
            jax.ShapeDtypeStruct((nb, D_HGRN), BF16),
            jax.ShapeDtypeStruct(state.shape, F32),
        ],
        scratch_shapes=[pltpu.VMEM((nb, HGRN_DV), F32)],
        compiler_params=_cp(("arbitrary",)),
        name="hgrn_step",
    )(mix, mix, mix, lg, norm_w, state)


def _head_expand(width=SSM_HEAD_DIM):
    e = np.zeros((LANES, SSM_HEADS * width), np.float32)
    for h in range(SSM_HEADS):
        e[h, h * width:(h + 1) * width] = 1.0
    return e


def _softplus(x):
    return jnp.maximum(x, 0.0) + jnp.log(1.0 + jnp.exp(-jnp.abs(x)))


def _ssm_gate_norm(y, z_gate, nw):
    y = y * z_gate.astype(F32)
    parts = [_rms(y[:, g * GROUP_W:(g + 1) * GROUP_W], nw[:, g * GROUP_W:(g + 1) * GROUP_W])
             for g in range(SSM_GROUPS)]
    return jnp.concatenate(parts, axis=-1)


def _ssd_prompt_kernel(z_ref, xs_ref, bc_ref, dt_ref, cw_ref, cb_ref, alog_ref,
                       dvec_ref, nw_ref, tri_ref, exp_ref,
                       shift_ref, y_ref, st_out_ref, xprev_scr, st_scr):
    c = pl.program_id(1)
    t = CHUNK

    @pl.when(c == 0)
    def _():
        st_scr[...] = jnp.zeros_like(st_scr)
        xprev_scr[...] = jnp.zeros_like(xprev_scr)

    subs = range(xs_ref.shape[0] // t)
    rs = [slice(s * t, (s + 1) * t) for s in subs]

    x_cur = [jnp.concatenate([xs_ref[r, :], bc_ref[r, :]], axis=-1) for r in rs]
    x_prev = [xprev_scr[...]] + x_cur[:-1]
    xprev_scr[...] = x_cur[-1]
    taps = [_dot(shift_ref[...], jnp.concatenate([x_prev[s], x_cur[s]], axis=0)) for s in subs]
    xbc = []
    for s in subs:
        acc = cb_ref[...] + cw_ref[SSM_CONV - 1:SSM_CONV, :] * x_cur[s].astype(F32)
        for d in range(1, SSM_CONV):
            acc = acc + cw_ref[SSM_CONV - 1 - d:SSM_CONV - d, :] * taps[s][(d - 1) * t:d * t, :]
        xbc.append(_silu(acc))
    xs = [x[:, 0:D_SSM] for x in xbc]

    dt = [dt_ref[r, :] for r in rs]
    neg_a = -LOG2E * jnp.exp(alog_ref[...])
    cs = [_dot_exact_lhs01(tri_ref[...], dt[s] * neg_a) for s in subs]
    ex = exp_ref[...]
    dt_full = [_dot_exact_rhs01(dt[s], ex) for s in subs]
    cs_full = [_dot_exact_rhs01(cs[s], ex) for s in subs]
    cs_last_full = [x[t - 1:t, :] for x in cs_full]
    x_dt = [xs[s] * dt_full[s] for s in subs]
    x_end = [(x_dt[s] * jnp.exp2(cs_last_full[s] - cs_full[s])).astype(BF16) for s in subs]
    cs_t = [x.T for x in cs]

    causal = (lax.broadcasted_iota(jnp.int32, (t, t), 0)
              >= lax.broadcasted_iota(jnp.int32, (t, t), 1))
    lane = lax.broadcasted_iota(jnp.int32, (1, D_SSM), 1)
    odd_head = (lane & SSM_HEAD_DIM) != 0
    x_by_parity = []
    for s in subs:
        x_b = x_dt[s].astype(BF16)
        zero = jnp.zeros_like(x_b)
        x_by_parity.append((jnp.where(odd_head, zero, x_b), jnp.where(odd_head, x_b, zero)))
    heads_per_group = SSM_HEADS // SSM_GROUPS
    pair_w = 2 * SSM_HEAD_DIM
    never = -1e30

    def group_bc(s, g):
        b_g = xbc[s][:, D_SSM + g * SSM_STATE:D_SSM + (g + 1) * SSM_STATE]
        c_off = D_SSM + SSM_GROUPS * SSM_STATE + g * SSM_STATE
        return b_g.T.astype(BF16), xbc[s][:, c_off:c_off + SSM_STATE].astype(BF16)

    bc_t = {(s, g): group_bc(s, g) for s in subs for g in range(SSM_GROUPS)}

    y_diag = []
    for s in subs:
        y_parts = []
        for g in range(SSM_GROUPS):
            b_t, c_g = bc_t[(s, g)]
            gmat = _dot(c_g, b_t)
            for pp in range(heads_per_group // 2):
                h0 = g * heads_per_group + 2 * pp
                psl = slice(h0 * SSM_HEAD_DIM, h0 * SSM_HEAD_DIM + pair_w)
                yp = None
                for sub in range(2):
                    h = h0 + sub
                    diff = cs[s][:, h:h + 1] - cs_t[s][h:h + 1, :]
                    w = jnp.exp2(jnp.where(causal, diff, never)) * gmat
                    part = _dot(w.astype(BF16), x_by_parity[s][sub][:, psl])
                    yp = part if yp is None else yp + part
                y_parts.append(yp)
        y_diag.append(jnp.concatenate(y_parts, axis=-1))

    y_off = []
    for s in subs:
        offs = []
        for g in range(SSM_GROUPS):
            sl = slice(g * GROUP_W, (g + 1) * GROUP_W)
            b_t, c_g = bc_t[(s, g)]
            st_g = st_scr[:, sl]
            offs.append(_dot(c_g, st_g.astype(BF16)))
            st_scr[:, sl] = st_g * jnp.exp2(cs_last_full[s][:, sl]) + _dot(b_t, x_end[s][:, sl])
        y_off.append(jnp.concatenate(offs, axis=-1) * jnp.exp2(cs_full[s]))

    for s in subs:
        y = y_diag[s] + y_off[s] + dvec_ref[...] * xs[s]
        y_ref[rs[s], :] = _ssm_gate_norm(y, z_ref[rs[s], :], nw_ref[...]).astype(BF16)

    @pl.when(c == pl.num_programs(1) - 1)
    def _():
        for j in range(D_SSM // LANES):
            st_out_ref[0, j * LANES:(j + 1) * LANES, :] = st_scr[:, j * LANES:(j + 1) * LANES].T


SSD_SUBCHUNKS = 1


def _ssd_prompt(mix, dt, conv_w, conv_b, a_log, d_full, norm_w, tri, expand, batch, seq):
    rows = SSD_SUBCHUNKS * CHUNK
    assert seq % rows == 0
    nc = seq // rows
    const = lambda shape: pl.BlockSpec(shape, lambda b, c: (0, 0))
    return pl.pallas_call(
        _ssd_prompt_kernel,
        grid=(batch, nc),
        in_specs=[
            pl.BlockSpec((rows, D_SSM), lambda b, c: (b * nc + c, OFF_Z // D_SSM)),
            pl.BlockSpec((rows, D_SSM), lambda b, c: (b * nc + c, OFF_XS // D_SSM)),
            pl.BlockSpec((rows, 512), lambda b, c: (b * nc + c, OFF_BC // 512)),
            pl.BlockSpec((rows, LANES), lambda b, c: (b * nc + c, 0)),
            const((SSM_CONV, CONV_DIM)), const((1, CONV_DIM)),
            const((1, LANES)),
            const((1, D_SSM)), const((1, D_SSM)),
            const((CHUNK, CHUNK)), const((LANES, D_SSM)),
            const(((SSM_CONV - 1) * CHUNK, 2 * CHUNK)),
        ],
        out_specs=[
            pl.BlockSpec((rows, D_SSM), lambda b, c: (b * nc + c, 0)),
            pl.BlockSpec((1, D_SSM, SSM_STATE), lambda b, c: (b, 0, 0)),
        ],
        out_shape=[
            jax.ShapeDtypeStruct((batch * seq, D_SSM), BF16),
            jax.ShapeDtypeStruct((batch, D_SSM, SSM_STATE), F32),
        ],
        scratch_shapes=[
            pltpu.VMEM((CHUNK, CONV_DIM), BF16),
            pltpu.VMEM((SSM_STATE, D_SSM), F32),
        ],
        compiler_params=_cp(("arbitrary", "arbitrary")),
        name="ssd_prompt",
    )(mix, mix, mix, dt, conv_w, conv_b, a_log, d_full, norm_w, tri, expand,
      jnp.asarray(_conv_shifts(), BF16))


def _conv_shifts():
    m = np.zeros(((SSM_CONV - 1) * CHUNK, 2 * CHUNK), np.float32)
    for d in range(1, SSM_CONV):
        for t in range(CHUNK):
            m[(d - 1) * CHUNK + t, CHUNK + t - d] = 1.0
    return m


def _ssd_step_kernel(z_ref, xs_ref, bc_ref, dt_ref, b0_ref, b1_ref, b2_ref, cw_ref, cb_ref,
                     alog_ref, dvec_ref, nw_ref, exp_ref, exl_ref, st_ref,
                     y_ref, st_out_ref, xt_scr, at_scr, xs_scr, bc_scr, y_scr):
    p = pl.program_id(0)
    nb = z_ref.shape[0]
    pair_w = 2 * SSM_HEAD_DIM
    pairs_per_group = SSM_HEADS // SSM_GROUPS // 2

    @pl.when(p == 0)
    def _():
        x_new = jnp.concatenate([xs_ref[...], bc_ref[...]], axis=-1).astype(F32)
        acc = (cb_ref[...] + cw_ref[0:1, :] * b0_ref[...] + cw_ref[1:2, :] * b1_ref[...]
               + cw_ref[2:3, :] * b2_ref[...] + cw_ref[3:4, :] * x_new)
        xbc = _silu(acc)
        xs = xbc[:, 0:D_SSM]
        dt = dt_ref[...]
        da = dt * (-jnp.exp(alog_ref[...]))
        ex = exp_ref[...]
        x_dt = xs * _dot_exact_rhs01(dt, ex)
        decay = jnp.exp(_dot_exact_rhs01(da, exl_ref[...]))
        xs_scr[...] = xs
        bc_scr[...] = xbc[:, D_SSM:]
        for j in range(D_SSM // LANES):
            sl = slice(j * LANES, (j + 1) * LANES)
            xt_scr[sl, :] = x_dt[:, sl].T
            at_scr[j] = decay[:, 2 * j * LANES:2 * (j + 1) * LANES]

    g_is_1 = p >= pairs_per_group
    row0 = pl.multiple_of(p * pair_w, pair_w)
    x_t = xt_scr[pl.ds(row0, pair_w), :]
    a_p = at_scr[p]
    bc = bc_scr[...]
    b_all = jnp.where(g_is_1, bc[:, SSM_STATE:2 * SSM_STATE], bc[:, 0:SSM_STATE])
    c_all = jnp.where(g_is_1, bc[:, 3 * SSM_STATE:4 * SSM_STATE],
                      bc[:, 2 * SSM_STATE:3 * SSM_STATE]).astype(BF16)
    for t in range(nb):
        inject = x_t[:, t:t + 1] * b_all[t:t + 1, :]
        halves = []
        for sub in range(2):
            rows = slice(sub * SSM_HEAD_DIM, (sub + 1) * SSM_HEAD_DIM)
            half = a_p[t:t + 1, sub * LANES:(sub + 1) * LANES] * st_ref[t, sub] + inject[rows]
            st_out_ref[t, sub] = half
            halves.append(half)
        new = jnp.concatenate(halves, axis=0)
        c_rows = jnp.broadcast_to(c_all[t:t + 1, :], (SUBLANES, SSM_STATE))
        y_scr[p, t:t + 1, :] = _dot_nt(c_rows, new.astype(BF16))[0:1, :]

    @pl.when(p == pl.num_programs(0) - 1)
    def _():
        y_mix = jnp.concatenate([y_scr[j] for j in range(SSM_HEADS // 2)], axis=-1)
        y = y_mix + dvec_ref[...] * xs_scr[...]
        y_ref[...] = _ssm_gate_norm(y, z_ref[...], nw_ref[...]).astype(BF16)


def _ssd_step(mix, dt, buf, conv_w, conv_b, a_log, d_full, norm_w, expand, state):
    nb = state.shape[0]
    n_pairs = SSM_HEADS // 2
    const = lambda shape: pl.BlockSpec(shape, lambda p: (0, 0))
    st_spec = pl.BlockSpec((nb, 2, SSM_HEAD_DIM, SSM_STATE), lambda p: (0, p, 0, 0))
    return pl.pallas_call(
        _ssd_step_kernel,
        grid=(n_pairs,),
        in_specs=[
            pl.BlockSpec((nb, D_SSM), lambda p: (0, OFF_Z // D_SSM)),
            pl.BlockSpec((nb, D_SSM), lambda p: (0, OFF_XS // D_SSM)),
            pl.BlockSpec((nb, 512), lambda p: (0, OFF_BC // 512)),
            const((nb, LANES)),
            const((nb, CONV_DIM)), const((nb, CONV_DIM)), const((nb, CONV_DIM)),
            const((SSM_CONV, CONV_DIM)), const((1, CONV_DIM)),
            const((1, LANES)),
            const((1, D_SSM)), const((1, D_SSM)),
            const((LANES, D_SSM)), const((LANES, SSM_HEADS * LANES)),
            st_spec,
        ],
        out_specs=[const((nb, D_SSM)), st_spec],
        out_shape=[
            jax.ShapeDtypeStruct((nb, D_SSM), BF16),
            jax.ShapeDtypeStruct(state.shape, F32),
        ],
        scratch_shapes=[
            pltpu.VMEM((D_SSM, nb), F32),
            pltpu.VMEM((n_pairs, nb, 2 * LANES), F32),
            pltpu.VMEM((nb, D_SSM), F32),
            pltpu.VMEM((nb, 2 * SSM_GROUPS * SSM_STATE), F32),
            pltpu.VMEM((n_pairs, nb, 2 * SSM_HEAD_DIM), F32),
        ],
        compiler_params=_cp(("arbitrary",)),
        name="ssd_step",
    )(mix, mix, mix, dt, buf[:, 0], buf[:, 1], buf[:, 2], conv_w, conv_b, a_log,
      d_full, norm_w, expand, jnp.asarray(_head_expand(LANES), BF16), state)


def _ffn_prompt_kernel(oa_ref, ys_ref, x_ref, wo_ref, n2_ref, wg_ref, wv_ref, wd_ref, cw_ref, cb_ref,
                       fnw_ref, y_ref, tail_ref, ge_scr, *, tiles_per_seq):
    i = pl.program_id(0)
    tm = x_ref.shape[0]
    pad = SUBLANES
    x1 = (x_ref[...] + _dot(oa_ref[...], wo_ref[0:D_HGRN, :])
          + _dot(ys_ref[...], wo_ref[D_HGRN:D_HGRN + D_SSM, :]))
    h2 = _rms(x1, n2_ref[...]).astype(BF16)

    seq_start = lax.rem(i, tiles_per_seq) == 0

    @pl.when(seq_start)
    def _():
        ge_scr[0:pad, :] = jnp.zeros((pad, D_FF), F32)

    @pl.when(jnp.logical_not(seq_start))
    def _():
        ge_scr[0:pad, :] = ge_scr[tm:tm + pad, :]

    acc = None
    bounds = np.cumsum((0,) + FFN_COL_BLOCKS)
    for c0, c1 in zip(bounds[:-1].tolist(), bounds[1:].tolist()):
        gate = _dot(h2, wg_ref[:, c0:c1])
        val = _dot(h2, wv_ref[:, c0:c1])
        ge_scr[pad:, c0:c1] = gate
        tail_ref[0, :, c0:c1] = gate[tm - pad:, :]
        conv = (cb_ref[:, c0:c1] + cw_ref[2:3, c0:c1] * gate
                + cw_ref[1:2, c0:c1] * ge_scr[pad - 1:pad - 1 + tm, c0:c1]
                + cw_ref[0:1, c0:c1] * ge_scr[pad - 2:pad - 2 + tm, c0:c1])
        act = (_silu(conv) * val).astype(BF16)
        part = _dot(act, wd_ref[c0:c1, :])
        acc = part if acc is None else acc + part
    y_ref[...] = _rms(x1 + acc, fnw_ref[...])


FFN_ROW_TILE = 512
FFN_COL_BLOCKS = (1024, 1024, 768)
assert sum(FFN_COL_BLOCKS) == D_FF and all(c % LANES == 0 for c in FFN_COL_BLOCKS)


def _ffn_prompt(o_a, y_s, x2d, w_o, norm2_w, w_gate, w_val, w_down, conv_w, conv_b, fnorm_w, seq):
    n = x2d.shape[0]
    tm = FFN_ROW_TILE
    assert seq % tm == 0
    kern = functools.partial(_ffn_prompt_kernel, tiles_per_seq=seq // tm)
    row = lambda w: pl.BlockSpec((tm, w), lambda i: (i, 0))
    resident = lambda shape: pl.BlockSpec(shape, lambda i: (0, 0), pipeline_mode=pl.Buffered(1))
    return pl.pallas_call(
        kern,
        grid=(n // tm,),
        in_specs=[
            row(D_HGRN), row(D_SSM), row(D_MODEL),
            resident((D_HGRN + D_SSM, D_MODEL)), resident((1, D_MODEL)),
            resident((D_MODEL, D_FF)), resident((D_MODEL, D_FF)), resident((D_FF, D_MODEL)),
            resident((FFN_CONV, D_FF)), resident((1, D_FF)), resident((1, D_MODEL)),
        ],
        out_specs=[
            row(D_MODEL),
            pl.BlockSpec((1, SUBLANES, D_FF), lambda i: (i, 0, 0)),
        ],
        out_shape=[
            jax.ShapeDtypeStruct((n, D_MODEL), F32),
            jax.ShapeDtypeStruct((n // tm, SUBLANES, D_FF), F32),
        ],
        scratch_shapes=[pltpu.VMEM((tm + SUBLANES, D_FF), F32)],
        compiler_params=_cp(("arbitrary",)),
        name="ffn_prompt",
    )(o_a, y_s, x2d, w_o, norm2_w, w_gate, w_val, w_down, conv_w, conv_b, fnorm_w)


FF_CAST_BLOCK = 256


def _ffn_step_kernel(oa_ref, ys_ref, x_ref, wo_ref, n2_ref, wg_ref, wv_ref, wd_ref, cw_ref, cb_ref,
                     fnw_ref, b0_ref, b1_ref,
                     y_ref, gate_ref, wob_ref, wgb_ref, wvb_ref, wdb_ref, x1_scr, h2_scr, acc_scr):
    j = pl.program_id(0)

    @pl.when(j == 0)
    def _():
        wo = wo_ref[...].astype(BF16)
        wob_ref[...] = wo
        x1 = (x_ref[...] + _dot(oa_ref[...], wo[0:D_HGRN, :]) + _dot(ys_ref[...], wo[D_HGRN:, :]))
        x1_scr[...] = x1
        h2_scr[...] = _rms(x1, n2_ref[...]).astype(BF16)
        acc_scr[...] = jnp.zeros_like(acc_scr)

    wg = wg_ref[...].astype(BF16)
    wv = wv_ref[...].astype(BF16)
    wd = wd_ref[...].astype(BF16)
    wgb_ref[...] = wg
    wvb_ref[...] = wv
    wdb_ref[...] = wd
    h2 = h2_scr[...]
    gate = _dot(h2, wg)
    val = _dot(h2, wv)
    gate_ref[...] = gate
    conv = (cb_ref[...] + cw_ref[2:3, :] * gate + cw_ref[1:2, :] * b1_ref[...]
            + cw_ref[0:1, :] * b0_ref[...])
    act = (_silu(conv) * val).astype(BF16)
    acc_scr[...] = acc_scr[...] + _dot(act, wd)

    @pl.when(j == pl.num_programs(0) - 1)
    def _():
        y_ref[...] = _rms(x1_scr[...] + acc_scr[...], fnw_ref[...])


def _ffn_step(o_a, y_s, x2d, w_out, norm2_w, w_up, w_down, conv_w, conv_b, fnorm_w, buf):
    n = x2d.shape[0]
    blk = FF_CAST_BLOCK
    nj = D_FF // blk
    const = lambda shape: pl.BlockSpec(shape, lambda j: (0, 0))
    col = lambda rows: pl.BlockSpec((rows, blk), lambda j: (0, j))
    return pl.pallas_call(
        _ffn_step_kernel,
        grid=(nj,),
        in_specs=[
            const((n, D_HGRN)), const((n, D_SSM)), const((n, D_MODEL)),
            const((D_HGRN + D_SSM, D_MODEL)), const((1, D_MODEL)),
            col(D_MODEL), pl.BlockSpec((D_MODEL, blk), lambda j: (0, nj + j)),
            pl.BlockSpec((blk, D_MODEL), lambda j: (j, 0)),
            col(FFN_CONV), col(1), const((1, D_MODEL)), col(n), col(n),
        ],
        out_specs=[
            const((n, D_MODEL)), col(n),
            const((D_HGRN + D_SSM, D_MODEL)), col(D_MODEL), col(D_MODEL),
            pl.BlockSpec((blk, D_MODEL), lambda j: (j, 0)),
        ],
        out_shape=[
            jax.ShapeDtypeStruct((n, D_MODEL), F32),
            jax.ShapeDtypeStruct((n, D_FF), F32),
            jax.ShapeDtypeStruct((D_HGRN + D_SSM, D_MODEL), BF16),
            jax.ShapeDtypeStruct((D_MODEL, D_FF), BF16),
            jax.ShapeDtypeStruct((D_MODEL, D_FF), BF16),
            jax.ShapeDtypeStruct((D_FF, D_MODEL), BF16),
        ],
        scratch_shapes=[
            pltpu.VMEM((n, D_MODEL), F32),
            pltpu.VMEM((n, D_MODEL), BF16),
            pltpu.VMEM((n, D_MODEL), F32),
        ],
        compiler_params=_cp(("arbitrary",)),
        name="ffn_step",
    )(o_a, y_s, x2d, w_out, norm2_w, w_up, w_up, w_down, conv_w, conv_b, fnorm_w,
      buf[:, 0], buf[:, 1])


def _row(v):
    return v.reshape(1, -1).astype(F32)


def _pad_lanes(v):
    return jnp.pad(v.astype(F32), (0, LANES - v.shape[0])).reshape(1, LANES)


def kernel(x_prompt, x_sample, state_hgrn, state_ssm, state_conv_ssm, state_conv_ffn, norm1_w, w_in, hgrn_lb, hgrn_norm_w, ssm_conv_w, ssm_conv_b, ssm_dt_bias, ssm_a_log, ssm_d, ssm_norm_w, w_out, norm2_w, w_up, ffn_conv_w, ffn_conv_b, w_down, final_norm_w):
    depth = w_in.shape[0]
    assert depth == 1, "single-layer trunk"
    l = 0
    batch, seq, _ = x_prompt.shape
    dec_batch, dec_seq, _ = x_sample.shape
    assert dec_seq == 1 and seq % CHUNK == 0 and seq >= SSM_CONV

    w_in_t = w_in[l].T
    d_full = jnp.repeat(ssm_d[l].astype(F32), SSM_HEAD_DIM).reshape(1, D_SSM)
    dt_bias = _pad_lanes(ssm_dt_bias[l])
    a_log = _pad_lanes(ssm_a_log[l])
    mconst = jnp.asarray(_hgrn_const(), BF16)
    tri = jnp.asarray(np.tril(np.ones((CHUNK, CHUNK), np.float32)), BF16)
    expand = jnp.asarray(_head_expand(), BF16)
    lb_raw = hgrn_lb.astype(F32)

    xs_ = x_sample.reshape(dec_batch, D_MODEL)
    w_main, proj_s, lg_s, dt_s = _inproj_cast(xs_, _row(norm1_w[l]), w_in_t, lb_raw, dt_bias)
    oa_s, hgrn_s = _hgrn_step(proj_s, lg_s, _row(hgrn_norm_w[l]), state_hgrn[l])
    ys_s, ssm_s = _ssd_step(proj_s, dt_s, state_conv_ssm[l], ssm_conv_w[l], _row(ssm_conv_b[l]),
                            a_log, d_full, _row(ssm_norm_w[l]), expand, state_ssm[l])
    y_s, gate_s, w_ob, w_gb, w_vb, w_db = _ffn_step(
        oa_s, ys_s, xs_, w_out[l], _row(norm2_w[l]), w_up[l], w_down[l],
        ffn_conv_w[l], _row(ffn_conv_b[l]), _row(final_norm_w), state_conv_ffn[l])
    cs_s = jnp.concatenate([state_conv_ssm[l][:, 1:], proj_s[:, None, OFF_XS:OFF_XS + CONV_DIM]],
                           axis=1)
    cf_s = jnp.concatenate([state_conv_ffn[l][:, 1:], gate_s[:, None, :]], axis=1)

    xp = x_prompt.reshape(batch * seq, D_MODEL)
    proj_p, lg_p, dt_p = _inproj(xp, _row(norm1_w[l]), w_main, w_in_t, lb_raw, dt_bias)
    oa_p, hgrn_p = _hgrn_prompt(proj_p, lg_p, _row(hgrn_norm_w[l]), mconst, batch, seq)
    ys_p, ssm_p = _ssd_prompt(proj_p, dt_p, ssm_conv_w[l], _row(ssm_conv_b[l]), a_log,
                              d_full, _row(ssm_norm_w[l]), tri, expand, batch, seq)
    y_p, tail_p = _ffn_prompt(oa_p, ys_p, xp, w_ob, _row(norm2_w[l]), w_gb, w_vb, w_db,
                              ffn_conv_w[l], _row(ffn_conv_b[l]), _row(final_norm_w), seq)
    proj_p3 = proj_p.reshape(batch, seq, D_MAIN)
    cs_p = proj_p3[:, seq - (SSM_CONV - 1):, OFF_XS:OFF_XS + CONV_DIM]
    tails = tail_p.reshape(batch, seq // FFN_ROW_TILE, SUBLANES, D_FF)
    cf_p = tails[:, -1, SUBLANES - (FFN_CONV - 1):, :]

    dt_ = x_prompt.dtype
    return (y_p.reshape(batch, seq, D_MODEL).astype(dt_),
            y_s.reshape(dec_batch, 1, D_MODEL).astype(dt_),
            hgrn_p[None].astype(dt_),
            hgrn_s[None].astype(dt_),
            ssm_p.reshape(1, batch, SSM_HEADS, SSM_HEAD_DIM, SSM_STATE).astype(dt_),
            ssm_s[None].astype(dt_),
            cs_p[None].astype(dt_),
            cs_s[None].astype(dt_),
            cf_p[None].astype(dt_),
            cf_s[None].astype(dt_))
```

```python
import functools

import numpy as np
import jax
import jax.numpy as jnp
from jax import lax
from jax.experimental import pallas as pl
from jax.experimental.pallas import tpu as pltpu

F32 = jnp.float32
BF16 = jnp.bfloat16
EPS = 1e-6

LANES = 128
SUBLANES = 8

D_MODEL = 1024
HGRN_HEADS = 8
HGRN_DK = 128
HGRN_DV = 128
D_HGRN = HGRN_HEADS * HGRN_DV
SSM_HEADS = 16
SSM_HEAD_DIM = 64
D_SSM = SSM_HEADS * SSM_HEAD_DIM
SSM_STATE = 128
SSM_GROUPS = 2
SSM_CONV = 4
CONV_DIM = D_SSM + 2 * SSM_GROUPS * SSM_STATE
D_FF = 2816
FFN_CONV = 3
D_MAIN = 4 * D_HGRN + D_SSM + CONV_DIM
OFF_Q, OFF_F, OFF_I, OFF_G = 0, 1024, 2048, 3072
OFF_Z, OFF_XS, OFF_BC = 4096, 5120, 6144

CHUNK = 128
GROUP_W = D_SSM // SSM_GROUPS
VMEM_LIMIT = 56 * 1024 * 1024


def _cp(sem):
    return pltpu.CompilerParams(dimension_semantics=sem, vmem_limit_bytes=VMEM_LIMIT)


def _dot(a, b):
    return jnp.dot(a, b, preferred_element_type=F32)


def _dot_nt(a, b):
    return lax.dot_general(a, b, (((1,), (1,)), ((), ())), preferred_element_type=F32)


def _split3(x):
    h = x.astype(BF16)
    r = x - h.astype(F32)
    m = r.astype(BF16)
    lo = (r - m.astype(F32)).astype(BF16)
    return h, m, lo


def _dot_exact_lhs01(m01, x):
    h, m, lo = _split3(x)
    return _dot(m01, h) + _dot(m01, m) + _dot(m01, lo)


def _dot_exact_rhs01(x, m01):
    h, m, lo = _split3(x)
    return _dot(h, m01) + _dot(m, m01) + _dot(lo, m01)


def _dot_split_lhs01(m01, x):
    h = x.astype(BF16)
    lo = (x - h.astype(F32)).astype(BF16)
    return _dot(m01, h) + _dot(m01, lo)


def _sigmoid(x):
    return 1.0 / (1.0 + jnp.exp(-x))


def _silu(x):
    return x * _sigmoid(x)


def _rms(x, w):
    ms = jnp.mean(x * x, axis=-1, keepdims=True)
    return x * lax.rsqrt(ms + EPS) * w


def _dt_proj(hb, wdt_ref):
    rows = lax.broadcasted_iota(jnp.int32, wdt_ref.shape, 0)
    wdt = jnp.where(rows < SSM_HEADS, wdt_ref[...], 0.0).astype(BF16)
    return _dot_nt(hb, wdt)

def _inproj_kernel(x_ref, nw_ref, w_ref, wdt_ref, lb_ref, dtb_ref, mix_ref, lg_ref, dt_ref):
    n_sub = max(1, x_ref.shape[0] // INPROJ_SUB_ROWS)
    sub_rows = x_ref.shape[0] // n_sub
    rs = [slice(s * sub_rows, (s + 1) * sub_rows) for s in range(n_sub)]
    hb = [_rms(x_ref[r, :], nw_ref[...]).astype(BF16) for r in rs]

    def put(r, off, val):
        mix_ref[r, off:off + val.shape[1]] = val.astype(BF16)

    w = D_HGRN
    lb = _hgrn_lb(lb_ref[...])
    for s, r in enumerate(rs):
        qf = _dot(hb[s], w_ref[:, OFF_Q:OFF_Q + 2 * w])
        f = lb + (1.0 - lb) * _sigmoid(qf[:, w:])
        lg_ref[r, :] = jnp.log(f)
        put(r, OFF_F, 1.0 - f)
        put(r, OFF_Q, _silu(qf[:, :w]))
    for s, r in enumerate(rs):
        ig = _dot(hb[s], w_ref[:, OFF_I:OFF_I + 2 * w])
        put(r, OFF_I, ig[:, :w])
        put(r, OFF_G, _silu(ig[:, w:]))
    for s, r in enumerate(rs):
        zx = _dot(hb[s], w_ref[:, OFF_Z:OFF_Z + D_SSM + CONV_DIM])
        put(r, OFF_Z, _silu(zx[:, :D_SSM]))
        put(r, OFF_XS, zx[:, D_SSM:])
    for s, r in enumerate(rs):
        dt_ref[r, :] = _softplus(_dt_proj(hb[s], wdt_ref) + dtb_ref[...])


INPROJ_ROW_TILE = 512
INPROJ_SUB_ROWS = 256


def _inproj(x2d, norm_w, w_main, w_t, lb_raw, dt_bias):
    n = x2d.shape[0]
    tm = min(INPROJ_ROW_TILE, n)
    assert n % tm == 0
    row = lambda w: pl.BlockSpec((tm, w), lambda i: (i, 0))
    resident = lambda shape: pl.BlockSpec(shape, lambda i: (0, 0), pipeline_mode=pl.Buffered(1))
    return pl.pallas_call(
        _inproj_kernel,
        grid=(n // tm,),
        in_specs=[
            row(D_MODEL), resident((1, D_MODEL)),
            resident((D_MODEL, D_MAIN)),
            pl.BlockSpec((LANES, D_MODEL), lambda i: (D_MAIN // LANES, 0), pipeline_mode=pl.Buffered(1)),
            resident(lb_raw.shape), resident((1, LANES)),
        ],
        out_specs=[row(D_MAIN), row(D_HGRN), row(LANES)],
        out_shape=[
            jax.ShapeDtypeStruct((n, D_MAIN), BF16),
            jax.ShapeDtypeStruct((n, D_HGRN), F32),
            jax.ShapeDtypeStruct((n, LANES), F32),
        ],
        compiler_params=_cp(("arbitrary",)),
        name="inproj",
    )(x2d, norm_w, w_main, w_t, lb_raw, dt_bias)


CAST_BLOCK = 512


def _inproj_cast_kernel(x_ref, nw_ref, w_ref, wdt_ref, lb_ref, dtb_ref,
                        wb_ref, mix_ref, lg_ref, dt_ref, h_scr):
    j = pl.program_id(0)
    blk = CAST_BLOCK
    q0, f0, i0, g0, z0, x0 = (off // blk for off in (OFF_Q, OFF_F, OFF_I, OFF_G, OFF_Z, OFF_XS))

    @pl.when(j == 0)
    def _():
        hb = _rms(x_ref[...], nw_ref[...]).astype(BF16)
        h_scr[...] = hb
        dt_ref[...] = _softplus(_dt_proj(hb, wdt_ref) + dtb_ref[...])

    wb = w_ref[...].T.astype(BF16)
    wb_ref[...] = wb
    p = _dot(h_scr[...], wb)

    @pl.when(((j >= q0) & (j < f0)) | ((j >= g0) & (j < x0)))
    def _():
        mix_ref[...] = _silu(p).astype(BF16)

    @pl.when(((j >= i0) & (j < g0)) | (j >= x0))
    def _():
        mix_ref[...] = p.astype(BF16)

    for fj in range(f0, i0):
        @pl.when(j == fj)
        def _(fj=fj):
            cols = slice((fj - f0) * blk, (fj - f0 + 1) * blk)
            lb = _hgrn_lb(lb_ref[:, cols])
            f = lb + (1.0 - lb) * _sigmoid(p)
            lg_ref[:, cols] = jnp.log(f)
            mix_ref[...] = (1.0 - f).astype(BF16)


def _inproj_cast(x2d, norm_w, w_t, lb_raw, dt_bias):
    n = x2d.shape[0]
    blk = CAST_BLOCK
    const = lambda shape: pl.BlockSpec(shape, lambda j: (0, 0))
    return pl.pallas_call(
        _inproj_cast_kernel,
        grid=(D_MAIN // blk,),
        in_specs=[
            const((n, D_MODEL)), const((1, D_MODEL)),
            pl.BlockSpec((blk, D_MODEL), lambda j: (j, 0)),
            pl.BlockSpec((LANES, D_MODEL), lambda j: (D_MAIN // LANES, 0)),
            const(lb_raw.shape), const((1, LANES)),
        ],
        out_specs=[
            pl.BlockSpec((D_MODEL, blk), lambda j: (0, j)),
            pl.BlockSpec((n, blk), lambda j: (0, j)),
            const((n, D_HGRN)), const((n, LANES)),
        ],
        out_shape=[
            jax.ShapeDtypeStruct((D_MODEL, D_MAIN), BF16),
            jax.ShapeDtypeStruct((n, D_MAIN), BF16),
            jax.ShapeDtypeStruct((n, D_HGRN), F32),
            jax.ShapeDtypeStruct((n, LANES), F32),
        ],
        scratch_shapes=[pltpu.VMEM((n, D_MODEL), BF16)],
        compiler_params=_cp(("arbitrary",)),
        name="inproj_cast",
    )(x2d, norm_w, w_t, w_t, lb_raw, dt_bias)


LOG2E = 1.4426950408889634
N_LEVELS = 7
MXU_LEVEL_HALVES = (4, 2)


def _hgrn_const():
    c = CHUNK
    t = np.arange(c)[:, None]
    j = np.arange(c)[None, :]
    blocks = [(j <= t)]
    for h in MXU_LEVEL_HALVES:
        mid = (t // (2 * h)) * (2 * h) + h
        upper = (t >= mid) & (j >= mid) & (j <= t)
        lower = (t < mid) & (j > t) & (j < mid)
        blocks.append(upper | lower)
    return np.concatenate(blocks, axis=0).astype(np.float32)


def _midpoint_decay(b, h):
    pieces = []
    for start in range(0, CHUNK, 2 * h):
        mid = start + h
        m = b[mid - 1:mid, :]
        pieces.append(m - b[start:mid])
        pieces.append(b[mid:mid + h] - m)
    return jnp.concatenate(pieces, axis=0)


def _mix_rows(q, k, h):
    pieces = []
    for start in range(0, CHUNK, 2 * h):
        pieces.append(k[start:start + h])
        pieces.append(q[start + h:start + 2 * h])
    return jnp.concatenate(pieces, axis=0)


def _hgrn_lb(lb_raw):
    mx = jnp.max(lb_raw, axis=0, keepdims=True)
    e = jnp.exp(lb_raw - mx)
    return e[0:1, :] / jnp.sum(e, axis=0, keepdims=True)


def _level_map():
    t = lax.broadcasted_iota(jnp.int32, (CHUNK, CHUNK), 0)
    s = lax.broadcasted_iota(jnp.int32, (CHUNK, CHUNK), 1)
    bitlen = 32 - lax.clz(t ^ s)
    return jnp.where(t > s, bitlen, jnp.where(t == s, 0, -1))


def _hgrn_prompt_kernel(q_ref, k_ref, i_ref, g_ref, lg_ref, nw_ref, mc_ref,
                        o_ref, s_out_ref, st_scr):
    c = pl.program_id(1)

    @pl.when(c == 0)
    def _():
        st_scr[...] = jnp.zeros_like(st_scr)

    n_sub = q_ref.shape[0] // CHUNK
    lev = _level_map().astype(jnp.int16)
    row = lax.broadcasted_iota(jnp.int32, (CHUNK, HGRN_DK), 0)
    heads = range(HGRN_HEADS)
    pairs = [(s, h) for s in range(n_sub) for h in heads]
    rs = {s: slice(s * CHUNK, (s + 1) * CHUNK) for s in range(n_sub)}
    cs = {h: slice(h * HGRN_DK, (h + 1) * HGRN_DK) for h in heads}

    e_sub = {s: _dot_split_lhs01(mc_ref[...], lg_ref[rs[s], :] * LOG2E) for s in range(n_sub)}
    qb = {(s, h): q_ref[rs[s], cs[h]] for s, h in pairs}
    kb = {(s, h): k_ref[rs[s], cs[h]] for s, h in pairs}
    vb = {(s, h): i_ref[rs[s], cs[h]] for s, h in pairs}
    q = {p: qb[p].astype(F32) for p in pairs}
    k = {p: kb[p].astype(F32) for p in pairs}
    b = {(s, h): e_sub[s][0:CHUNK, cs[h]] for s, h in pairs}
    b_last = {p: b[p][CHUNK - 1:CHUNK, :] for p in pairs}

    st = {h: st_scr[h] for h in heads}
    o = {}
    for s, h in pairs:
        p = (s, h)
        o[p] = _dot((q[p] * jnp.exp2(b[p])).astype(BF16), st[h].T.astype(BF16))
        ks = (k[p] * jnp.exp2(b_last[p] - b[p])).astype(BF16)
        st[h] = st[h] * jnp.exp2(b_last[p]) + _dot(vb[p].astype(F32).T.astype(BF16), ks)
    for h in heads:
        st_scr[h] = st[h]

    a = {p: jnp.where(lev == 0, _dot(qb[p], k[p].T.astype(BF16)).astype(BF16), jnp.zeros((), BF16))
         for p in pairs}
    half = CHUNK // 2
    while half >= 1:
        for p in pairs:
            if half >= SUBLANES:
                x = _mix_rows(q[p], k[p], half) * jnp.exp2(_midpoint_decay(b[p], half))
            else:
                upper = (row & half) != 0
                if half in MXU_LEVEL_HALVES:
                    blk = 1 + MXU_LEVEL_HALVES.index(half)
                    w = jnp.exp2(e_sub[p[0]][blk * CHUNK:(blk + 1) * CHUNK, cs[p[1]]])
                    x = jnp.where(upper, q[p], k[p]) * w
                else:
                    x = jnp.where(upper, q[p] * (1.0 - k[p]), k[p])
            gram = _dot(x.astype(BF16), x.T.astype(BF16))
            a[p] = jnp.where(lev == half.bit_length(), gram.astype(BF16), a[p])
        half //= 2

    for p in pairs:
        o[p] = o[p] + _dot(a[p], vb[p])
    for s, h in pairs:
        gate = g_ref[rs[s], cs[h]].astype(F32)
        o_ref[rs[s], cs[h]] = (_rms(o[(s, h)], nw_ref[...]) * gate).astype(BF16)

    @pl.when(c == pl.num_programs(1) - 1)
    def _():
        for h in range(HGRN_HEADS):
            s_out_ref[0, h] = st_scr[h].T


HGRN_SUBCHUNKS = 2


def _hgrn_prompt(mix, lg, norm_w, mconst, batch, seq):
    rows = HGRN_SUBCHUNKS * CHUNK
    assert seq % rows == 0
    nc = seq // rows

    def col(off):
        return pl.BlockSpec((rows, D_HGRN), lambda b, c: (b * nc + c, off // D_HGRN))

    return pl.pallas_call(
        _hgrn_prompt_kernel,
        grid=(batch, nc),
        in_specs=[
            col(OFF_Q), col(OFF_F), col(OFF_I), col(OFF_G), col(0),
            pl.BlockSpec((1, HGRN_DV), lambda b, c: (0, 0)),
            pl.BlockSpec(mconst.shape, lambda b, c: (0, 0)),
        ],
        out_specs=[
            pl.BlockSpec((rows, D_HGRN), lambda b, c: (b * nc + c, 0)),
            pl.BlockSpec((1, HGRN_HEADS, HGRN_DK, HGRN_DV), lambda b, c: (b, 0, 0, 0)),
        ],
        out_shape=[
            jax.ShapeDtypeStruct((batch * seq, D_HGRN), BF16),
            jax.ShapeDtypeStruct((batch, HGRN_HEADS, HGRN_DK, HGRN_DV), F32),
        ],
        scratch_shapes=[pltpu.VMEM((HGRN_HEADS, HGRN_DV, HGRN_DK), F32)],
        compiler_params=_cp(("arbitrary", "arbitrary")),
        name="hgrn_prompt",
    )(mix, mix, mix, mix, lg, norm_w, mconst)


def _hgrn_step_kernel(q_ref, i_ref, g_ref, lg_ref, nw_ref, s_ref,
                      o_ref, s_out_ref, o_scr):
    nb = q_ref.shape[0]
    qb = q_ref[...]
    f_t = jnp.exp(lg_ref[...]).T
    v = i_ref[...].astype(F32)
    lhs_rows = 2 * SUBLANES
    for t in range(nb):
        v_row = v[t:t + 1, :]
        s_new = v_row + f_t[:, t:t + 1] * (s_ref[t, 0] - v_row)
        s_out_ref[t, 0] = s_new
        q_rows = jnp.broadcast_to(qb[t:t + 1, :], (lhs_rows, HGRN_DK))
        o_scr[t:t + 1, :] = _dot(q_rows, s_new.astype(BF16))[0:1, :]
    o_ref[...] = (_rms(o_scr[...], nw_ref[...]) * g_ref[...].astype(F32)).astype(BF16)


def _hgrn_step(mix, lg, norm_w, state):
    nb = state.shape[0]
    hb = lambda off: off // HGRN_DK

    def col(off):
        return pl.BlockSpec((nb, HGRN_DK), lambda h: (0, hb(off) + h))

    st_spec = pl.BlockSpec((nb, 1, HGRN_DK, HGRN_DV), lambda h: (0, h, 0, 0))
    return pl.pallas_call(
        _hgrn_step_kernel,
        grid=(HGRN_HEADS,),
        in_specs=[
            col(OFF_Q), col(OFF_I), col(OFF_G), col(0),
            pl.BlockSpec((1, HGRN_DV), lambda h: (0, 0)),
            st_spec,
        ],
        out_specs=[pl.BlockSpec((nb, HGRN_DV), lambda h: (0, h)), st_spec],
        out_shape=[
            jax.ShapeDtypeStruct((nb, D_HGRN), BF16),
            jax.ShapeDtypeStruct(state.shape, F32),
        ],
        scratch_shapes=[pltpu.VMEM((nb, HGRN_DV), F32)],
        compiler_params=_cp(("arbitrary",)),
        name="hgrn_step",
    )(mix, mix, mix, lg, norm_w, state)


def _head_expand(width=SSM_HEAD_DIM):
    e = np.zeros((LANES, SSM_HEADS * width), np.float32)
    for h in range(SSM_HEADS):
        e[h, h * width:(h + 1) * width] = 1.0
    return e


def _softplus(x):
    return jnp.maximum(x, 0.0) + jnp.log(1.0 + jnp.exp(-jnp.abs(x)))


def _ssm_gate_norm(y, z_gate, nw):
    y = y * z_gate.astype(F32)
    parts = [_rms(y[:, g * GROUP_W:(g + 1) * GROUP_W], nw[:, g * GROUP_W:(g + 1) * GROUP_W])
             for g in range(SSM_GROUPS)]
    return jnp.concatenate(parts, axis=-1)


def _ssd_prompt_kernel(z_ref, xs_ref, bc_ref, dt_ref, cw_ref, cb_ref, alog_ref,
                       dvec_ref, nw_ref, tri_ref, exp_ref,
                       shift_ref, y_ref, st_out_ref, xprev_scr, st_scr):
    c = pl.program_id(1)
    t = CHUNK

    @pl.when(c == 0)
    def _():
        st_scr[...] = jnp.zeros_like(st_scr)
        xprev_scr[...] = jnp.zeros_like(xprev_scr)

    subs = range(xs_ref.shape[0] // t)
    rs = [slice(s * t, (s + 1) * t) for s in subs]

    x_cur = [jnp.concatenate([xs_ref[r, :], bc_ref[r, :]], axis=-1) for r in rs]
    x_prev = [xprev_scr[...]] + x_cur[:-1]
    xprev_scr[...] = x_cur[-1]
    taps = [_dot(shift_ref[...], jnp.concatenate([x_prev[s], x_cur[s]], axis=0)) for s in subs]
    xbc = []
    for s in subs:
        acc = cb_ref[...] + cw_ref[SSM_CONV - 1:SSM_CONV, :] * x_cur[s].astype(F32)
        for d in range(1, SSM_CONV):
            acc = acc + cw_ref[SSM_CONV - 1 - d:SSM_CONV - d, :] * taps[s][(d - 1) * t:d * t, :]
        xbc.append(_silu(acc))
    xs = [x[:, 0:D_SSM] for x in xbc]

    dt = [dt_ref[r, :] for r in rs]
    neg_a = -LOG2E * jnp.exp(alog_ref[...])
    cs = [_dot_exact_lhs01(tri_ref[...], dt[s] * neg_a) for s in subs]
    ex = exp_ref[...]
    dt_full = [_dot_exact_rhs01(dt[s], ex) for s in subs]
    cs_full = [_dot_exact_rhs01(cs[s], ex) for s in subs]
    cs_last_full = [x[t - 1:t, :] for x in cs_full]
    x_dt = [xs[s] * dt_full[s] for s in subs]
    x_end = [(x_dt[s] * jnp.exp2(cs_last_full[s] - cs_full[s])).astype(BF16) for s in subs]
    cs_t = [x.T for x in cs]

    causal = (lax.broadcasted_iota(jnp.int32, (t, t), 0)
              >= lax.broadcasted_iota(jnp.int32, (t, t), 1))
    lane = lax.broadcasted_iota(jnp.int32, (1, D_SSM), 1)
    odd_head = (lane & SSM_HEAD_DIM) != 0
    x_by_parity = []
    for s in subs:
        x_b = x_dt[s].astype(BF16)
        zero = jnp.zeros_like(x_b)
        x_by_parity.append((jnp.where(odd_head, zero, x_b), jnp.where(odd_head, x_b, zero)))
    heads_per_group = SSM_HEADS // SSM_GROUPS
    pair_w = 2 * SSM_HEAD_DIM
    never = -1e30

    def group_bc(s, g):
        b_g = xbc[s][:, D_SSM + g * SSM_STATE:D_SSM + (g + 1) * SSM_STATE]
        c_off = D_SSM + SSM_GROUPS * SSM_STATE + g * SSM_STATE
        return b_g.T.astype(BF16), xbc[s][:, c_off:c_off + SSM_STATE].astype(BF16)

    bc_t = {(s, g): group_bc(s, g) for s in subs for g in range(SSM_GROUPS)}

    y_diag = []
    for s in subs:
        y_parts = []
        for g in range(SSM_GROUPS):
            b_t, c_g = bc_t[(s, g)]
            gmat = _dot(c_g, b_t)
            for pp in range(heads_per_group // 2):
                h0 = g * heads_per_group + 2 * pp
                psl = slice(h0 * SSM_HEAD_DIM, h0 * SSM_HEAD_DIM + pair_w)
                yp = None
                for sub in range(2):
                    h = h0 + sub
                    diff = cs[s][:, h:h + 1] - cs_t[s][h:h + 1, :]
                    w = jnp.exp2(jnp.where(causal, diff, never)) * gmat
                    part = _dot(w.astype(BF16), x_by_parity[s][sub][:, psl])
                    yp = part if yp is None else yp + part
                y_parts.append(yp)
        y_diag.append(jnp.concatenate(y_parts, axis=-1))

    y_off = []
    for s in subs:
        offs = []
        for g in range(SSM_GROUPS):
            sl = slice(g * GROUP_W, (g + 1) * GROUP_W)
            b_t, c_g = bc_t[(s, g)]
            st_g = st_scr[:, sl]
            offs.append(_dot(c_g, st_g.astype(BF16)))
            st_scr[:, sl] = st_g * jnp.exp2(cs_last_full[s][:, sl]) + _dot(b_t, x_end[s][:, sl])
        y_off.append(jnp.concatenate(offs, axis=-1) * jnp.exp2(cs_full[s]))

    for s in subs:
        y = y_diag[s] + y_off[s] + dvec_ref[...] * xs[s]
        y_ref[rs[s], :] = _ssm_gate_norm(y, z_ref[rs[s], :], nw_ref[...]).astype(BF16)

    @pl.when(c == pl.num_programs(1) - 1)
    def _():
        for j in range(D_SSM // LANES):
            st_out_ref[0, j * LANES:(j + 1) * LANES, :] = st_scr[:, j * LANES:(j + 1) * LANES].T


SSD_SUBCHUNKS = 1


def _ssd_prompt(mix, dt, conv_w, conv_b, a_log, d_full, norm_w, tri, expand, batch, seq):
    rows = SSD_SUBCHUNKS * CHUNK
    assert seq % rows == 0
    nc = seq // rows
    const = lambda shape: pl.BlockSpec(shape, lambda b, c: (0, 0))
    return pl.pallas_call(
        _ssd_prompt_kernel,
        grid=(batch, nc),
        in_specs=[
            pl.BlockSpec((rows, D_SSM), lambda b, c: (b * nc + c, OFF_Z // D_SSM)),
            pl.BlockSpec((rows, D_SSM), lambda b, c: (b * nc + c, OFF_XS // D_SSM)),
            pl.BlockSpec((rows, 512), lambda b, c: (b * nc + c, OFF_BC // 512)),
            pl.BlockSpec((rows, LANES), lambda b, c: (b * nc + c, 0)),
            const((SSM_CONV, CONV_DIM)), const((1, CONV_DIM)),
            const((1, LANES)),
            const((1, D_SSM)), const((1, D_SSM)),
            const((CHUNK, CHUNK)), const((LANES, D_SSM)),
            const(((SSM_CONV - 1) * CHUNK, 2 * CHUNK)),
        ],
        out_specs=[
            pl.BlockSpec((rows, D_SSM), lambda b, c: (b * nc + c, 0)),
            pl.BlockSpec((1, D_SSM, SSM_STATE), lambda b, c: (b, 0, 0)),
        ],
        out_shape=[
            jax.ShapeDtypeStruct((batch * seq, D_SSM), BF16),
            jax.ShapeDtypeStruct((batch, D_SSM, SSM_STATE), F32),
        ],
        scratch_shapes=[
            pltpu.VMEM((CHUNK, CONV_DIM), BF16),
            pltpu.VMEM((SSM_STATE, D_SSM), F32),
        ],
        compiler_params=_cp(("arbitrary", "arbitrary")),
        name="ssd_prompt",
    )(mix, mix, mix, dt, conv_w, conv_b, a_log, d_full, norm_w, tri, expand,
      jnp.asarray(_conv_shifts(), BF16))


def _conv_shifts():
    m = np.zeros(((SSM_CONV - 1) * CHUNK, 2 * CHUNK), np.float32)
    for d in range(1, SSM_CONV):
        for t in range(CHUNK):
            m[(d - 1) * CHUNK + t, CHUNK + t - d] = 1.0
    return m


def _ssd_step_kernel(z_ref, xs_ref, bc_ref, dt_ref, b0_ref, b1_ref, b2_ref, cw_ref, cb_ref,
                     alog_ref, dvec_ref, nw_ref, exp_ref, exl_ref, st_ref,
                     y_ref, st_out_ref, xt_scr, at_scr, xs_scr, bc_scr, y_scr):
    p = pl.program_id(0)
    nb = z_ref.shape[0]
    pair_w = 2 * SSM_HEAD_DIM
    pairs_per_group = SSM_HEADS // SSM_GROUPS // 2

    @pl.when(p == 0)
    def _():
        x_new = jnp.concatenate([xs_ref[...], bc_ref[...]], axis=-1).astype(F32)
        acc = (cb_ref[...] + cw_ref[0:1, :] * b0_ref[...] + cw_ref[1:2, :] * b1_ref[...]
               + cw_ref[2:3, :] * b2_ref[...] + cw_ref[3:4, :] * x_new)
        xbc = _silu(acc)
        xs = xbc[:, 0:D_SSM]
        dt = dt_ref[...]
        da = dt * (-jnp.exp(alog_ref[...]))
        ex = exp_ref[...]
        x_dt = xs * _dot_exact_rhs01(dt, ex)
        decay = jnp.exp(_dot_exact_rhs01(da, exl_ref[...]))
        xs_scr[...] = xs
        bc_scr[...] = xbc[:, D_SSM:]
        for j in range(D_SSM // LANES):
            sl = slice(j * LANES, (j + 1) * LANES)
            xt_scr[sl, :] = x_dt[:, sl].T
            at_scr[j] = decay[:, 2 * j * LANES:2 * (j + 1) * LANES]

    g_is_1 = p >= pairs_per_group
    row0 = pl.multiple_of(p * pair_w, pair_w)
    x_t = xt_scr[pl.ds(row0, pair_w), :]
    a_p = at_scr[p]
    bc = bc_scr[...]
    b_all = jnp.where(g_is_1, bc[:, SSM_STATE:2 * SSM_STATE], bc[:, 0:SSM_STATE])
    c_all = jnp.where(g_is_1, bc[:, 3 * SSM_STATE:4 * SSM_STATE],
                      bc[:, 2 * SSM_STATE:3 * SSM_STATE]).astype(BF16)
    for t in range(nb):
        inject = x_t[:, t:t + 1] * b_all[t:t + 1, :]
        halves = []
        for sub in range(2):
            rows = slice(sub * SSM_HEAD_DIM, (sub + 1) * SSM_HEAD_DIM)
            half = a_p[t:t + 1, sub * LANES:(sub + 1) * LANES] * st_ref[t, sub] + inject[rows]
            st_out_ref[t, sub] = half
            halves.append(half)
        new = jnp.concatenate(halves, axis=0)
        c_rows = jnp.broadcast_to(c_all[t:t + 1, :], (SUBLANES, SSM_STATE))
        y_scr[p, t:t + 1, :] = _dot_nt(c_rows, new.astype(BF16))[0:1, :]

    @pl.when(p == pl.num_programs(0) - 1)
    def _():
        y_mix = jnp.concatenate([y_scr[j] for j in range(SSM_HEADS // 2)], axis=-1)
        y = y_mix + dvec_ref[...] * xs_scr[...]
        y_ref[...] = _ssm_gate_norm(y, z_ref[...], nw_ref[...]).astype(BF16)


def _ssd_step(mix, dt, buf, conv_w, conv_b, a_log, d_full, norm_w, expand, state):
    nb = state.shape[0]
    n_pairs = SSM_HEADS // 2
    const = lambda shape: pl.BlockSpec(shape, lambda p: (0, 0))
    st_spec = pl.BlockSpec((nb, 2, SSM_HEAD_DIM, SSM_STATE), lambda p: (0, p, 0, 0))
    return pl.pallas_call(
        _ssd_step_kernel,
        grid=(n_pairs,),
        in_specs=[
            pl.BlockSpec((nb, D_SSM), lambda p: (0, OFF_Z // D_SSM)),
            pl.BlockSpec((nb, D_SSM), lambda p: (0, OFF_XS // D_SSM)),
            pl.BlockSpec((nb, 512), lambda p: (0, OFF_BC // 512)),
            const((nb, LANES)),
            const((nb, CONV_DIM)), const((nb, CONV_DIM)), const((nb, CONV_DIM)),
            const((SSM_CONV, CONV_DIM)), const((1, CONV_DIM)),
            const((1, LANES)),
            const((1, D_SSM)), const((1, D_SSM)),
            const((LANES, D_SSM)), const((LANES, SSM_HEADS * LANES)),
            st_spec,
        ],
        out_specs=[const((nb, D_SSM)), st_spec],
        out_shape=[
            jax.ShapeDtypeStruct((nb, D_SSM), BF16),
            jax.ShapeDtypeStruct(state.shape, F32),
        ],
        scratch_shapes=[
            pltpu.VMEM((D_SSM, nb), F32),
            pltpu.VMEM((n_pairs, nb, 2 * LANES), F32),
            pltpu.VMEM((nb, D_SSM), F32),
            pltpu.VMEM((nb, 2 * SSM_GROUPS * SSM_STATE), F32),
            pltpu.VMEM((n_pairs, nb, 2 * SSM_HEAD_DIM), F32),
        ],
        compiler_params=_cp(("arbitrary",)),
        name="ssd_step",
    )(mix, mix, mix, dt, buf[:, 0], buf[:, 1], buf[:, 2], conv_w, conv_b, a_log,
      d_full, norm_w, expand, jnp.asarray(_head_expand(LANES), BF16), state)


def _ffn_prompt_kernel(oa_ref, ys_ref, x_ref, wo_ref, n2_ref, wg_ref, wv_ref, wd_ref, cw_ref, cb_ref,
                       fnw_ref, y_ref, tail_ref, ge_scr, *, tiles_per_seq):
    i = pl.program_id(0)
    tm = x_ref.shape[0]
    pad = SUBLANES
    seq_start = lax.rem(i, tiles_per_seq) == 0

    @pl.when(seq_start)
    def _():
        ge_scr[0:pad, :] = jnp.zeros((pad, D_FF), F32)

    @pl.when(jnp.logical_not(seq_start))
    def _():
        ge_scr[0:pad, :] = ge_scr[tm:tm + pad, :]

    n_sub = max(1, tm // FFN_SUB_ROWS)
    sub = tm // n_sub
    rs = [slice(s * sub, (s + 1) * sub) for s in range(n_sub)]
    x1 = [x_ref[r, :] + _dot(oa_ref[r, :], wo_ref[0:D_HGRN, :])
          + _dot(ys_ref[r, :], wo_ref[D_HGRN:D_HGRN + D_SSM, :]) for r in rs]
    h2 = [_rms(x, n2_ref[...]).astype(BF16) for x in x1]

    acc = [None] * n_sub
    bounds = np.cumsum((0,) + FFN_COL_BLOCKS)
    for c0, c1 in zip(bounds[:-1].tolist(), bounds[1:].tolist()):
        for s, r in enumerate(rs):
            gate = _dot(h2[s], wg_ref[:, c0:c1])
            val = _dot(h2[s], wv_ref[:, c0:c1])
            ge_scr[pad + r.start:pad + r.stop, c0:c1] = gate
            if s == n_sub - 1:
                tail_ref[0, :, c0:c1] = gate[sub - pad:, :]
            conv = (cb_ref[:, c0:c1] + cw_ref[2:3, c0:c1] * gate
                    + cw_ref[1:2, c0:c1] * ge_scr[pad - 1 + r.start:pad - 1 + r.stop, c0:c1]
                    + cw_ref[0:1, c0:c1] * ge_scr[pad - 2 + r.start:pad - 2 + r.stop, c0:c1])
            act = (_silu(conv) * val).astype(BF16)
            part = _dot(act, wd_ref[c0:c1, :])
            acc[s] = part if acc[s] is None else acc[s] + part
    for s, r in enumerate(rs):
        y_ref[r, :] = _rms(x1[s] + acc[s], fnw_ref[...])


FFN_ROW_TILE = 512
FFN_SUB_ROWS = 256
FFN_COL_BLOCKS = (1024, 1024, 768)
assert sum(FFN_COL_BLOCKS) == D_FF and all(c % LANES == 0 for c in FFN_COL_BLOCKS)


def _ffn_prompt(o_a, y_s, x2d, w_o, norm2_w, w_gate, w_val, w_down, conv_w, conv_b, fnorm_w, seq):
    n = x2d.shape[0]
    tm = FFN_ROW_TILE
    assert seq % tm == 0
    kern = functools.partial(_ffn_prompt_kernel, tiles_per_seq=seq // tm)
    row = lambda w: pl.BlockSpec((tm, w), lambda i: (i, 0))
    resident = lambda shape: pl.BlockSpec(shape, lambda i: (0, 0), pipeline_mode=pl.Buffered(1))
    return pl.pallas_call(
        kern,
        grid=(n // tm,),
        in_specs=[
            row(D_HGRN), row(D_SSM), row(D_MODEL),
            resident((D_HGRN + D_SSM, D_MODEL)), resident((1, D_MODEL)),
            resident((D_MODEL, D_FF)), resident((D_MODEL, D_FF)), resident((D_FF, D_MODEL)),
            resident((FFN_CONV, D_FF)), resident((1, D_FF)), resident((1, D_MODEL)),
        ],
        out_specs=[
            row(D_MODEL),
            pl.BlockSpec((1, SUBLANES, D_FF), lambda i: (i, 0, 0)),
        ],
        out_shape=[
            jax.ShapeDtypeStruct((n, D_MODEL), F32),
            jax.ShapeDtypeStruct((n // tm, SUBLANES, D_FF), F32),
        ],
        scratch_shapes=[pltpu.VMEM((tm + SUBLANES, D_FF), F32)],
        compiler_params=_cp(("arbitrary",)),
        name="ffn_prompt",
    )(o_a, y_s, x2d, w_o, norm2_w, w_gate, w_val, w_down, conv_w, conv_b, fnorm_w)


FF_CAST_BLOCK = 256


def _ffn_step_kernel(oa_ref, ys_ref, x_ref, wo_ref, n2_ref, wg_ref, wv_ref, wd_ref, cw_ref, cb_ref,
                     fnw_ref, b0_ref, b1_ref,
                     y_ref, gate_ref, wob_ref, wgb_ref, wvb_ref, wdb_ref, x1_scr, h2_scr, acc_scr):
    j = pl.program_id(0)

    @pl.when(j == 0)
    def _():
        wo = wo_ref[...].astype(BF16)
        wob_ref[...] = wo
        x1 = (x_ref[...] + _dot(oa_ref[...], wo[0:D_HGRN, :]) + _dot(ys_ref[...], wo[D_HGRN:, :]))
        x1_scr[...] = x1
        h2_scr[...] = _rms(x1, n2_ref[...]).astype(BF16)
        acc_scr[...] = jnp.zeros_like(acc_scr)

    wg = wg_ref[...].astype(BF16)
    wv = wv_ref[...].astype(BF16)
    wd = wd_ref[...].astype(BF16)
    wgb_ref[...] = wg
    wvb_ref[...] = wv
    wdb_ref[...] = wd
    h2 = h2_scr[...]
    gate = _dot(h2, wg)
    val = _dot(h2, wv)
    gate_ref[...] = gate
    conv = (cb_ref[...] + cw_ref[2:3, :] * gate + cw_ref[1:2, :] * b1_ref[...]
            + cw_ref[0:1, :] * b0_ref[...])
    act = (_silu(conv) * val).astype(BF16)
    acc_scr[...] = acc_scr[...] + _dot(act, wd)

    @pl.when(j == pl.num_programs(0) - 1)
    def _():
        y_ref[...] = _rms(x1_scr[...] + acc_scr[...], fnw_ref[...])


def _ffn_step(o_a, y_s, x2d, w_out, norm2_w, w_up, w_down, conv_w, conv_b, fnorm_w, buf):
    n = x2d.shape[0]
    blk = FF_CAST_BLOCK
    nj = D_FF // blk
    const = lambda shape: pl.BlockSpec(shape, lambda j: (0, 0))
    col = lambda rows: pl.BlockSpec((rows, blk), lambda j: (0, j))
    return pl.pallas_call(
        _ffn_step_kernel,
        grid=(nj,),
        in_specs=[
            const((n, D_HGRN)), const((n, D_SSM)), const((n, D_MODEL)),
            const((D_HGRN + D_SSM, D_MODEL)), const((1, D_MODEL)),
            col(D_MODEL), pl.BlockSpec((D_MODEL, blk), lambda j: (0, nj + j)),
            pl.BlockSpec((blk, D_MODEL), lambda j: (j, 0)),
            col(FFN_CONV), col(1), const((1, D_MODEL)), col(n), col(n),
        ],
        out_specs=[
            const((n, D_MODEL)), col(n),
            const((D_HGRN + D_SSM, D_MODEL)), col(D_MODEL), col(D_MODEL),
            pl.BlockSpec((blk, D_MODEL), lambda j: (j, 0)),
        ],
        out_shape=[
            jax.ShapeDtypeStruct((n, D_MODEL), F32),
            jax.ShapeDtypeStruct((n, D_FF), F32),
            jax.ShapeDtypeStruct((D_HGRN + D_SSM, D_MODEL), BF16),
            jax.ShapeDtypeStruct((D_MODEL, D_FF), BF16),
            jax.ShapeDtypeStruct((D_MODEL, D_FF), BF16),
            jax.ShapeDtypeStruct((D_FF, D_MODEL), BF16),
        ],
        scratch_shapes=[
            pltpu.VMEM((n, D_MODEL), F32),
            pltpu.VMEM((n, D_MODEL), BF16),
            pltpu.VMEM((n, D_MODEL), F32),
        ],
        compiler_params=_cp(("arbitrary",)),
        name="ffn_step",
    )(o_a, y_s, x2d, w_out, norm2_w, w_up, w_up, w_down, conv_w, conv_b, fnorm_w,
      buf[:, 0], buf[:, 1])


def _row(v):
    return v.reshape(1, -1).astype(F32)


def _pad_lanes(v):
    return jnp.pad(v.astype(F32), (0, LANES - v.shape[0])).reshape(1, LANES)


def kernel(x_prompt, x_sample, state_hgrn, state_ssm, state_conv_ssm, state_conv_ffn, norm1_w, w_in, hgrn_lb, hgrn_norm_w, ssm_conv_w, ssm_conv_b, ssm_dt_bias, ssm_a_log, ssm_d, ssm_norm_w, w_out, norm2_w, w_up, ffn_conv_w, ffn_conv_b, w_down, final_norm_w):
    depth = w_in.shape[0]
    assert depth == 1, "single-layer trunk"
    l = 0
    batch, seq, _ = x_prompt.shape
    dec_batch, dec_seq, _ = x_sample.shape
    assert dec_seq == 1 and seq % CHUNK == 0 and seq >= SSM_CONV

    w_in_t = w_in[l].T
    d_full = jnp.repeat(ssm_d[l].astype(F32), SSM_HEAD_DIM).reshape(1, D_SSM)
    dt_bias = _pad_lanes(ssm_dt_bias[l])
    a_log = _pad_lanes(ssm_a_log[l])
    mconst = jnp.asarray(_hgrn_const(), BF16)
    tri = jnp.asarray(np.tril(np.ones((CHUNK, CHUNK), np.float32)), BF16)
    expand = jnp.asarray(_head_expand(), BF16)
    lb_raw = hgrn_lb.astype(F32)

    xs_ = x_sample.reshape(dec_batch, D_MODEL)
    w_main, proj_s, lg_s, dt_s = _inproj_cast(xs_, _row(norm1_w[l]), w_in_t, lb_raw, dt_bias)
    oa_s, hgrn_s = _hgrn_step(proj_s, lg_s, _row(hgrn_norm_w[l]), state_hgrn[l])
    ys_s, ssm_s = _ssd_step(proj_s, dt_s, state_conv_ssm[l], ssm_conv_w[l], _row(ssm_conv_b[l]),
                            a_log, d_full, _row(ssm_norm_w[l]), expand, state_ssm[l])
    y_s, gate_s, w_ob, w_gb, w_vb, w_db = _ffn_step(
        oa_s, ys_s, xs_, w_out[l], _row(norm2_w[l]), w_up[l], w_down[l],
        ffn_conv_w[l], _row(ffn_conv_b[l]), _row(final_norm_w), state_conv_ffn[l])
    cs_s = jnp.concatenate([state_conv_ssm[l][:, 1:], proj_s[:, None, OFF_XS:OFF_XS + CONV_DIM]],
                           axis=1)
    cf_s = jnp.concatenate([state_conv_ffn[l][:, 1:], gate_s[:, None, :]], axis=1)

    xp = x_prompt.reshape(batch * seq, D_MODEL)
    proj_p, lg_p, dt_p = _inproj(xp, _row(norm1_w[l]), w_main, w_in_t, lb_raw, dt_bias)
    oa_p, hgrn_p = _hgrn_prompt(proj_p, lg_p, _row(hgrn_norm_w[l]), mconst, batch, seq)
    ys_p, ssm_p = _ssd_prompt(proj_p, dt_p, ssm_conv_w[l], _row(ssm_conv_b[l]), a_log,
                              d_full, _row(ssm_norm_w[l]), tri, expand, batch, seq)
    y_p, tail_p = _ffn_prompt(oa_p, ys_p, xp, w_ob, _row(norm2_w[l]), w_gb, w_vb, w_db,
                              ffn_conv_w[l], _row(ffn_conv_b[l]), _row(final_norm_w), seq)
    proj_p3 = proj_p.reshape(batch, seq, D_MAIN)
    cs_p = proj_p3[:, seq - (SSM_CONV - 1):, OFF_XS:OFF_XS + CONV_DIM]
    tails = tail_p.reshape(batch, seq // FFN_ROW_TILE, SUBLANES, D_FF)
    cf_p = tails[:, -1, SUBLANES - (FFN_CONV - 1):, :]

    dt_ = x_prompt.dtype
    return (y_p.reshape(batch, seq, D_MODEL).astype(dt_),
            y_s.reshape(dec_batch, 1, D_MODEL).astype(dt_),
            hgrn_p[None].astype(dt_),
            hgrn_s[None].astype(dt_),
            ssm_p.reshape(1, batch, SSM_HEADS, SSM_HEAD_DIM, SSM_STATE).astype(dt_),
            ssm_s[None].astype(dt_),
            cs_p[None].astype(dt_),
            cs_s[None].astype(dt_),
            cf_p[None].astype(dt_),
            cf_s[None].astype(dt_))
```

```python
import functools

import numpy as np
import jax
import jax.numpy as jnp
from jax import lax
from jax.experimental import pallas as pl
from jax.experimental.pallas import tpu as pltpu

F32 = jnp.float32
BF16 = jnp.bfloat16
EPS = 1e-6

LANES = 128
SUBLANES = 8

D_MODEL = 1024
HGRN_HEADS = 8
HGRN_DK = 128
HGRN_DV = 128
D_HGRN = HGRN_HEADS * HGRN_DV
SSM_HEADS = 16
SSM_HEAD_DIM = 64
D_SSM = SSM_HEADS * SSM_HEAD_DIM
SSM_STATE = 128
SSM_GROUPS = 2
SSM_CONV = 4
CONV_DIM = D_SSM + 2 * SSM_GROUPS * SSM_STATE
D_FF = 2816
FFN_CONV = 3
D_MAIN = 4 * D_HGRN + D_SSM + CONV_DIM
OFF_Q, OFF_F, OFF_I, OFF_G = 0, 1024, 2048, 3072
OFF_Z, OFF_XS, OFF_BC = 4096, 5120, 6144

CHUNK = 128
GROUP_W = D_SSM // SSM_GROUPS
VMEM_LIMIT = 56 * 1024 * 1024


def _cp(sem):
    return pltpu.CompilerParams(dimension_semantics=sem, vmem_limit_bytes=VMEM_LIMIT)


def _dot(a, b):
    return jnp.dot(a, b, preferred_element_type=F32)


def _dot_nt(a, b):
    return lax.dot_general(a, b, (((1,), (1,)), ((), ())), preferred_element_type=F32)


def _split3(x):
    h = x.astype(BF16)
    r = x - h.astype(F32)
    m = r.astype(BF16)
    lo = (r - m.astype(F32)).astype(BF16)
    return h, m, lo


def _dot_exact_lhs01(m01, x):
    h, m, lo = _split3(x)
    return _dot(m01, h) + _dot(m01, m) + _dot(m01, lo)


def _dot_exact_rhs01(x, m01):
    h, m, lo = _split3(x)
    return _dot(h, m01) + _dot(m, m01) + _dot(lo, m01)


def _dot_split_lhs01(m01, x):
    h = x.astype(BF16)
    lo = (x - h.astype(F32)).astype(BF16)
    return _dot(m01, h) + _dot(m01, lo)


def _sigmoid(x):
    return 1.0 / (1.0 + jnp.exp(-x))


def _silu(x):
    return x * _sigmoid(x)


def _rms(x, w):
    ms = jnp.mean(x * x, axis=-1, keepdims=True)
    return x * lax.rsqrt(ms + EPS) * w


def _dt_proj(hb, wdt_ref):
    rows = lax.broadcasted_iota(jnp.int32, wdt_ref.shape, 0)
    wdt = jnp.where(rows < SSM_HEADS, wdt_ref[...], 0.0).astype(BF16)
    return _dot_nt(hb, wdt)

def _inproj_kernel(x_ref, nw_ref, w_ref, wdt_ref, lb_ref, dtb_ref, mix_ref, lg_ref, dt_ref):
    n_sub = max(1, x_ref.shape[0] // INPROJ_SUB_ROWS)
    sub_rows = x_ref.shape[0] // n_sub
    rs = [slice(s * sub_rows, (s + 1) * sub_rows) for s in range(n_sub)]
    hb = [_rms(x_ref[r, :], nw_ref[...]).astype(BF16) for r in rs]

    def put(r, off, val):
        mix_ref[r, off:off + val.shape[1]] = val.astype(BF16)

    w = D_HGRN
    lb = _hgrn_lb(lb_ref[...])
    for s, r in enumerate(rs):
        qf = _dot(hb[s], w_ref[:, OFF_Q:OFF_Q + 2 * w])
        f = lb + (1.0 - lb) * _sigmoid(qf[:, w:])
        lg_ref[r, :] = jnp.log(f)
        put(r, OFF_F, 1.0 - f)
        put(r, OFF_Q, _silu(qf[:, :w]))
    for s, r in enumerate(rs):
        ig = _dot(hb[s], w_ref[:, OFF_I:OFF_I + 2 * w])
        put(r, OFF_I, ig[:, :w])
        put(r, OFF_G, _silu(ig[:, w:]))
    for s, r in enumerate(rs):
        zx = _dot(hb[s], w_ref[:, OFF_Z:OFF_Z + D_SSM + CONV_DIM])
        put(r, OFF_Z, _silu(zx[:, :D_SSM]))
        put(r, OFF_XS, zx[:, D_SSM:])
    for s, r in enumerate(rs):
        dt_ref[r, :] = _softplus(_dt_proj(hb[s], wdt_ref) + dtb_ref[...])


INPROJ_ROW_TILE = 512
INPROJ_SUB_ROWS = 256


def _inproj(x2d, norm_w, w_main, w_t, lb_raw, dt_bias):
    n = x2d.shape[0]
    tm = min(INPROJ_ROW_TILE, n)
    assert n % tm == 0
    row = lambda w: pl.BlockSpec((tm, w), lambda i: (i, 0))
    resident = lambda shape: pl.BlockSpec(shape, lambda i: (0, 0), pipeline_mode=pl.Buffered(1))
    return pl.pallas_call(
        _inproj_kernel,
        grid=(n // tm,),
        in_specs=[
            row(D_MODEL), resident((1, D_MODEL)),
            resident((D_MODEL, D_MAIN)),
            pl.BlockSpec((LANES, D_MODEL), lambda i: (D_MAIN // LANES, 0), pipeline_mode=pl.Buffered(1)),
            resident(lb_raw.shape), resident((1, LANES)),
        ],
        out_specs=[row(D_MAIN), row(D_HGRN), row(LANES)],
        out_shape=[
            jax.ShapeDtypeStruct((n, D_MAIN), BF16),
            jax.ShapeDtypeStruct((n, D_HGRN), F32),
            jax.ShapeDtypeStruct((n, LANES), F32),
        ],
        compiler_params=_cp(("arbitrary",)),
        name="inproj",
    )(x2d, norm_w, w_main, w_t, lb_raw, dt_bias)


CAST_BLOCK = 512


def _inproj_cast_kernel(x_ref, nw_ref, w_ref, wdt_ref, lb_ref, dtb_ref,
                        wb_ref, mix_ref, lg_ref, dt_ref, h_scr):
    j = pl.program_id(0)
    blk = CAST_BLOCK
    q0, f0, i0, g0, z0, x0 = (off // blk for off in (OFF_Q, OFF_F, OFF_I, OFF_G, OFF_Z, OFF_XS))

    @pl.when(j == 0)
    def _():
        hb = _rms(x_ref[...], nw_ref[...]).astype(BF16)
        h_scr[...] = hb
        dt_ref[...] = _softplus(_dt_proj(hb, wdt_ref) + dtb_ref[...])

    wb = w_ref[...].T.astype(BF16)
    wb_ref[...] = wb
    p = _dot(h_scr[...], wb)

    @pl.when(((j >= q0) & (j < f0)) | ((j >= g0) & (j < x0)))
    def _():
        mix_ref[...] = _silu(p).astype(BF16)

    @pl.when(((j >= i0) & (j < g0)) | (j >= x0))
    def _():
        mix_ref[...] = p.astype(BF16)

    for fj in range(f0, i0):
        @pl.when(j == fj)
        def _(fj=fj):
            cols = slice((fj - f0) * blk, (fj - f0 + 1) * blk)
            lb = _hgrn_lb(lb_ref[:, cols])
            f = lb + (1.0 - lb) * _sigmoid(p)
            lg_ref[:, cols] = jnp.log(f)
            mix_ref[...] = (1.0 - f).astype(BF16)


def _inproj_cast(x2d, norm_w, w_t, lb_raw, dt_bias):
    n = x2d.shape[0]
    blk = CAST_BLOCK
    const = lambda shape: pl.BlockSpec(shape, lambda j: (0, 0))
    return pl.pallas_call(
        _inproj_cast_kernel,
        grid=(D_MAIN // blk,),
        in_specs=[
            const((n, D_MODEL)), const((1, D_MODEL)),
            pl.BlockSpec((blk, D_MODEL), lambda j: (j, 0)),
            pl.BlockSpec((LANES, D_MODEL), lambda j: (D_MAIN // LANES, 0)),
            const(lb_raw.shape), const((1, LANES)),
        ],
        out_specs=[
            pl.BlockSpec((D_MODEL, blk), lambda j: (0, j)),
            pl.BlockSpec((n, blk), lambda j: (0, j)),
            const((n, D_HGRN)), const((n, LANES)),
        ],
        out_shape=[
            jax.ShapeDtypeStruct((D_MODEL, D_MAIN), BF16),
            jax.ShapeDtypeStruct((n, D_MAIN), BF16),
            jax.ShapeDtypeStruct((n, D_HGRN), F32),
            jax.ShapeDtypeStruct((n, LANES), F32),
        ],
        scratch_shapes=[pltpu.VMEM((n, D_MODEL), BF16)],
        compiler_params=_cp(("arbitrary",)),
        name="inproj_cast",
    )(x2d, norm_w, w_t, w_t, lb_raw, dt_bias)


LOG2E = 1.4426950408889634
N_LEVELS = 7
MXU_LEVEL_HALVES = (4, 2)


def _hgrn_const():
    c = CHUNK
    t = np.arange(c)[:, None]
    j = np.arange(c)[None, :]
    blocks = [(j <= t)]
    for h in MXU_LEVEL_HALVES:
        mid = (t // (2 * h)) * (2 * h) + h
        upper = (t >= mid) & (j >= mid) & (j <= t)
        lower = (t < mid) & (j > t) & (j < mid)
        blocks.append(upper | lower)
    return np.concatenate(blocks, axis=0).astype(np.float32)


def _midpoint_decay(b, h):
    pieces = []
    for start in range(0, CHUNK, 2 * h):
        mid = start + h
        m = b[mid - 1:mid, :]
        pieces.append(m - b[start:mid])
        pieces.append(b[mid:mid + h] - m)
    return jnp.concatenate(pieces, axis=0)


def _mix_rows(q, k, h):
    pieces = []
    for start in range(0, CHUNK, 2 * h):
        pieces.append(k[start:start + h])
        pieces.append(q[start + h:start + 2 * h])
    return jnp.concatenate(pieces, axis=0)


def _hgrn_lb(lb_raw):
    mx = jnp.max(lb_raw, axis=0, keepdims=True)
    e = jnp.exp(lb_raw - mx)
    return e[0:1, :] / jnp.sum(e, axis=0, keepdims=True)


def _level_map():
    t = lax.broadcasted_iota(jnp.int32, (CHUNK, CHUNK), 0)
    s = lax.broadcasted_iota(jnp.int32, (CHUNK, CHUNK), 1)
    bitlen = 32 - lax.clz(t ^ s)
    return jnp.where(t > s, bitlen, jnp.where(t == s, 0, -1))


def _hgrn_prompt_kernel(q_ref, k_ref, i_ref, g_ref, lg_ref, nw_ref, mc_ref,
                        o_ref, s_out_ref, st_scr):
    c = pl.program_id(1)

    @pl.when(c == 0)
    def _():
        st_scr[...] = jnp.zeros_like(st_scr)

    n_sub = q_ref.shape[0] // CHUNK
    lev = _level_map().astype(jnp.int16)
    row = lax.broadcasted_iota(jnp.int32, (CHUNK, HGRN_DK), 0)
    heads = range(HGRN_HEADS)
    pairs = [(s, h) for s in range(n_sub) for h in heads]
    rs = {s: slice(s * CHUNK, (s + 1) * CHUNK) for s in range(n_sub)}
    cs = {h: slice(h * HGRN_DK, (h + 1) * HGRN_DK) for h in heads}

    e_sub = {s: _dot_split_lhs01(mc_ref[...], lg_ref[rs[s], :] * LOG2E) for s in range(n_sub)}
    qb = {(s, h): q_ref[rs[s], cs[h]] for s, h in pairs}
    kb = {(s, h): k_ref[rs[s], cs[h]] for s, h in pairs}
    vb = {(s, h): i_ref[rs[s], cs[h]] for s, h in pairs}
    q = {p: qb[p].astype(F32) for p in pairs}
    k = {p: kb[p].astype(F32) for p in pairs}
    b = {(s, h): e_sub[s][0:CHUNK, cs[h]] for s, h in pairs}
    b_last = {p: b[p][CHUNK - 1:CHUNK, :] for p in pairs}

    st = {h: st_scr[h] for h in heads}
    o = {}
    for s, h in pairs:
        p = (s, h)
        o[p] = _dot((q[p] * jnp.exp2(b[p])).astype(BF16), st[h].T.astype(BF16))
        ks = (k[p] * jnp.exp2(b_last[p] - b[p])).astype(BF16)
        st[h] = st[h] * jnp.exp2(b_last[p]) + _dot(vb[p].astype(F32).T.astype(BF16), ks)
    for h in heads:
        st_scr[h] = st[h]

    a = {p: jnp.where(lev == 0, _dot(qb[p], k[p].T.astype(BF16)).astype(BF16), jnp.zeros((), BF16))
         for p in pairs}
    half = CHUNK // 2
    while half >= 1:
        for p in pairs:
            if half >= SUBLANES:
                x = _mix_rows(q[p], k[p], half) * jnp.exp2(_midpoint_decay(b[p], half))
            else:
                upper = (row & half) != 0
                if half in MXU_LEVEL_HALVES:
                    blk = 1 + MXU_LEVEL_HALVES.index(half)
                    w = jnp.exp2(e_sub[p[0]][blk * CHUNK:(blk + 1) * CHUNK, cs[p[1]]])
                    x = jnp.where(upper, q[p], k[p]) * w
                else:
                    x = jnp.where(upper, q[p] * (1.0 - k[p]), k[p])
            gram = _dot(x.astype(BF16), x.T.astype(BF16))
            a[p] = jnp.where(lev == half.bit_length(), gram.astype(BF16), a[p])
        half //= 2

    for p in pairs:
        o[p] = o[p] + _dot(a[p], vb[p])
    for s, h in pairs:
        gate = g_ref[rs[s], cs[h]].astype(F32)
        o_ref[rs[s], cs[h]] = (_rms(o[(s, h)], nw_ref[...]) * gate).astype(BF16)

    @pl.when(c == pl.num_programs(1) - 1)
    def _():
        for h in range(HGRN_HEADS):
            s_out_ref[0, h] = st_scr[h].T


HGRN_SUBCHUNKS = 4


def _hgrn_prompt(mix, lg, norm_w, mconst, batch, seq):
    rows = HGRN_SUBCHUNKS * CHUNK
    assert seq % rows == 0
    nc = seq // rows

    def col(off):
        return pl.BlockSpec((rows, D_HGRN), lambda b, c: (b * nc + c, off // D_HGRN))

    return pl.pallas_call(
        _hgrn_prompt_kernel,
        grid=(batch, nc),
        in_specs=[
            col(OFF_Q), col(OFF_F), col(OFF_I), col(OFF_G), col(0),
            pl.BlockSpec((1, HGRN_DV), lambda b, c: (0, 0)),
            pl.BlockSpec(mconst.shape, lambda b, c: (0, 0)),
        ],
        out_specs=[
            pl.BlockSpec((rows, D_HGRN), lambda b, c: (b * nc + c, 0)),
            pl.BlockSpec((1, HGRN_HEADS, HGRN_DK, HGRN_DV), lambda b, c: (b, 0, 0, 0)),
        ],
        out_shape=[
            jax.ShapeDtypeStruct((batch * seq, D_HGRN), BF16),
            jax.ShapeDtypeStruct((batch, HGRN_HEADS, HGRN_DK, HGRN_DV), F32),
        ],
        scratch_shapes=[pltpu.VMEM((HGRN_HEADS, HGRN_DV, HGRN_DK), F32)],
        compiler_params=_cp(("arbitrary", "arbitrary")),
        name="hgrn_prompt",
    )(mix, mix, mix, mix, lg, norm_w, mconst)


def _hgrn_step_kernel(q_ref, i_ref, g_ref, lg_ref, nw_ref, s_ref,
                      o_ref, s_out_ref, o_scr):
    nb = q_ref.shape[0]
    qb = q_ref[...]
    f_t = jnp.exp(lg_ref[...]).T
    v = i_ref[...].astype(F32)
    lhs_rows = 2 * SUBLANES
    for t in range(nb):
        v_row = v[t:t + 1, :]
        s_new = v_row + f_t[:, t:t + 1] * (s_ref[t, 0] - v_row)
        s_out_ref[t, 0] = s_new
        q_rows = jnp.broadcast_to(qb[t:t + 1, :], (lhs_rows, HGRN_DK))
        o_scr[t:t + 1, :] = _dot(q_rows, s_new.astype(BF16))[0:1, :]
    o_ref[...] = (_rms(o_scr[...], nw_ref[...]) * g_ref[...].astype(F32)).astype(BF16)


def _hgrn_step(mix, lg, norm_w, state):
    nb = state.shape[0]
    hb = lambda off: off // HGRN_DK

    def col(off):
        return pl.BlockSpec((nb, HGRN_DK), lambda h: (0, hb(off) + h))

    st_spec = pl.BlockSpec((nb, 1, HGRN_DK, HGRN_DV), lambda h: (0, h, 0, 0))
    return pl.pallas_call(
        _hgrn_step_kernel,
        grid=(HGRN_HEADS,),
        in_specs=[
            col(OFF_Q), col(OFF_I), col(OFF_G), col(0),
            pl.BlockSpec((1, HGRN_DV), lambda h: (0, 0)),
            st_spec,
        ],
        out_specs=[pl.BlockSpec((nb, HGRN_DV), lambda h: (0, h)), st_spec],
        out_shape=[
            jax.ShapeDtypeStruct((nb, D_HGRN), BF16),
            jax.ShapeDtypeStruct(state.shape, F32),
        ],
        scratch_shapes=[pltpu.VMEM((nb, HGRN_DV), F32)],
        compiler_params=_cp(("arbitrary",)),
        name="hgrn_step",
    )(mix, mix, mix, lg, norm_w, state)


def _head_expand(width=SSM_HEAD_DIM):
    e = np.zeros((LANES, SSM_HEADS * width), np.float32)
    for h in range(SSM_HEADS):
        e[h, h * width:(h + 1) * width] = 1.0
    return e


def _softplus(x):
    return jnp.maximum(x, 0.0) + jnp.log(1.0 + jnp.exp(-jnp.abs(x)))


def _ssm_gate_norm(y, z_gate, nw):
    y = y * z_gate.astype(F32)
    parts = [_rms(y[:, g * GROUP_W:(g + 1) * GROUP_W], nw[:, g * GROUP_W:(g + 1) * GROUP_W])
             for g in range(SSM_GROUPS)]
    return jnp.concatenate(parts, axis=-1)


def _ssd_prompt_kernel(z_ref, xs_ref, bc_ref, dt_ref, cw_ref, cb_ref, alog_ref,
                       dvec_ref, nw_ref, tri_ref, exp_ref,
                       shift_ref, y_ref, st_out_ref, xprev_scr, st_scr):
    c = pl.program_id(1)
    t = CHUNK

    @pl.when(c == 0)
    def _():
        st_scr[...] = jnp.zeros_like(st_scr)
        xprev_scr[...] = jnp.zeros_like(xprev_scr)

    subs = range(xs_ref.shape[0] // t)
    rs = [slice(s * t, (s + 1) * t) for s in subs]

    x_cur = [jnp.concatenate([xs_ref[r, :], bc_ref[r, :]], axis=-1) for r in rs]
    x_prev = [xprev_scr[...]] + x_cur[:-1]
    xprev_scr[...] = x_cur[-1]
    taps = [_dot(shift_ref[...], jnp.concatenate([x_prev[s], x_cur[s]], axis=0)) for s in subs]
    xbc = []
    for s in subs:
        acc = cb_ref[...] + cw_ref[SSM_CONV - 1:SSM_CONV, :] * x_cur[s].astype(F32)
        for d in range(1, SSM_CONV):
            acc = acc + cw_ref[SSM_CONV - 1 - d:SSM_CONV - d, :] * taps[s][(d - 1) * t:d * t, :]
        xbc.append(_silu(acc))
    xs = [x[:, 0:D_SSM] for x in xbc]

    dt = [dt_ref[r, :] for r in rs]
    neg_a = -LOG2E * jnp.exp(alog_ref[...])
    cs = [_dot_exact_lhs01(tri_ref[...], dt[s] * neg_a) for s in subs]
    ex = exp_ref[...]
    dt_full = [_dot_exact_rhs01(dt[s], ex) for s in subs]
    cs_full = [_dot_exact_rhs01(cs[s], ex) for s in subs]
    cs_last_full = [x[t - 1:t, :] for x in cs_full]
    x_dt = [xs[s] * dt_full[s] for s in subs]
    x_end = [(x_dt[s] * jnp.exp2(cs_last_full[s] - cs_full[s])).astype(BF16) for s in subs]
    cs_t = [x.T for x in cs]

    causal = (lax.broadcasted_iota(jnp.int32, (t, t), 0)
              >= lax.broadcasted_iota(jnp.int32, (t, t), 1))
    lane = lax.broadcasted_iota(jnp.int32, (1, D_SSM), 1)
    odd_head = (lane & SSM_HEAD_DIM) != 0
    x_by_parity = []
    for s in subs:
        x_b = x_dt[s].astype(BF16)
        zero = jnp.zeros_like(x_b)
        x_by_parity.append((jnp.where(odd_head, zero, x_b), jnp.where(odd_head, x_b, zero)))
    heads_per_group = SSM_HEADS // SSM_GROUPS
    pair_w = 2 * SSM_HEAD_DIM
    never = -1e30

    def group_bc(s, g):
        b_g = xbc[s][:, D_SSM + g * SSM_STATE:D_SSM + (g + 1) * SSM_STATE]
        c_off = D_SSM + SSM_GROUPS * SSM_STATE + g * SSM_STATE
        return b_g.T.astype(BF16), xbc[s][:, c_off:c_off + SSM_STATE].astype(BF16)

    bc_t = {(s, g): group_bc(s, g) for s in subs for g in range(SSM_GROUPS)}

    y_diag = []
    for s in subs:
        y_parts = []
        for g in range(SSM_GROUPS):
            b_t, c_g = bc_t[(s, g)]
            gmat = _dot(c_g, b_t)
            for pp in range(heads_per_group // 2):
                h0 = g * heads_per_group + 2 * pp
                psl = slice(h0 * SSM_HEAD_DIM, h0 * SSM_HEAD_DIM + pair_w)
                yp = None
                for sub in range(2):
                    h = h0 + sub
                    diff = cs[s][:, h:h + 1] - cs_t[s][h:h + 1, :]
                    w = jnp.exp2(jnp.where(causal, diff, never)) * gmat
                    part = _dot(w.astype(BF16), x_by_parity[s][sub][:, psl])
                    yp = part if yp is None else yp + part
                y_parts.append(yp)
        y_diag.append(jnp.concatenate(y_parts, axis=-1))

    y_off = []
    for s in subs:
        offs = []
        for g in range(SSM_GROUPS):
            sl = slice(g * GROUP_W, (g + 1) * GROUP_W)
            b_t, c_g = bc_t[(s, g)]
            st_g = st_scr[:, sl]
            offs.append(_dot(c_g, st_g.astype(BF16)))
            st_scr[:, sl] = st_g * jnp.exp2(cs_last_full[s][:, sl]) + _dot(b_t, x_end[s][:, sl])
        y_off.append(jnp.concatenate(offs, axis=-1) * jnp.exp2(cs_full[s]))

    for s in subs:
        y = y_diag[s] + y_off[s] + dvec_ref[...] * xs[s]
        y_ref[rs[s], :] = _ssm_gate_norm(y, z_ref[rs[s], :], nw_ref[...]).astype(BF16)

    @pl.when(c == pl.num_programs(1) - 1)
    def _():
        for j in range(D_SSM // LANES):
            st_out_ref[0, j * LANES:(j + 1) * LANES, :] = st_scr[:, j * LANES:(j + 1) * LANES].T


SSD_SUBCHUNKS = 2


def _ssd_prompt(mix, dt, conv_w, conv_b, a_log, d_full, norm_w, tri, expand, batch, seq):
    rows = SSD_SUBCHUNKS * CHUNK
    assert seq % rows == 0
    nc = seq // rows
    const = lambda shape: pl.BlockSpec(shape, lambda b, c: (0, 0))
    return pl.pallas_call(
        _ssd_prompt_kernel,
        grid=(batch, nc),
        in_specs=[
            pl.BlockSpec((rows, D_SSM), lambda b, c: (b * nc + c, OFF_Z // D_SSM)),
            pl.BlockSpec((rows, D_SSM), lambda b, c: (b * nc + c, OFF_XS // D_SSM)),
            pl.BlockSpec((rows, 512), lambda b, c: (b * nc + c, OFF_BC // 512)),
            pl.BlockSpec((rows, LANES), lambda b, c: (b * nc + c, 0)),
            const((SSM_CONV, CONV_DIM)), const((1, CONV_DIM)),
            const((1, LANES)),
            const((1, D_SSM)), const((1, D_SSM)),
            const((CHUNK, CHUNK)), const((LANES, D_SSM)),
            const(((SSM_CONV - 1) * CHUNK, 2 * CHUNK)),
        ],
        out_specs=[
            pl.BlockSpec((rows, D_SSM), lambda b, c: (b * nc + c, 0)),
            pl.BlockSpec((1, D_SSM, SSM_STATE), lambda b, c: (b, 0, 0)),
        ],
        out_shape=[
            jax.ShapeDtypeStruct((batch * seq, D_SSM), BF16),
            jax.ShapeDtypeStruct((batch, D_SSM, SSM_STATE), F32),
        ],
        scratch_shapes=[
            pltpu.VMEM((CHUNK, CONV_DIM), BF16),
            pltpu.VMEM((SSM_STATE, D_SSM), F32),
        ],
        compiler_params=_cp(("arbitrary", "arbitrary")),
        name="ssd_prompt",
    )(mix, mix, mix, dt, conv_w, conv_b, a_log, d_full, norm_w, tri, expand,
      jnp.asarray(_conv_shifts(), BF16))


def _conv_shifts():
    m = np.zeros(((SSM_CONV - 1) * CHUNK, 2 * CHUNK), np.float32)
    for d in range(1, SSM_CONV):
        for t in range(CHUNK):
            m[(d - 1) * CHUNK + t, CHUNK + t - d] = 1.0
    return m


def _ssd_step_kernel(z_ref, xs_ref, bc_ref, dt_ref, b0_ref, b1_ref, b2_ref, cw_ref, cb_ref,
                     alog_ref, dvec_ref, nw_ref, exp_ref, exl_ref, st_ref,
                     y_ref, st_out_ref, xt_scr, at_scr, xs_scr, bc_scr, y_scr):
    p = pl.program_id(0)
    nb = z_ref.shape[0]
    pair_w = 2 * SSM_HEAD_DIM
    pairs_per_group = SSM_HEADS // SSM_GROUPS // 2

    @pl.when(p == 0)
    def _():
        x_new = jnp.concatenate([xs_ref[...], bc_ref[...]], axis=-1).astype(F32)
        acc = (cb_ref[...] + cw_ref[0:1, :] * b0_ref[...] + cw_ref[1:2, :] * b1_ref[...]
               + cw_ref[2:3, :] * b2_ref[...] + cw_ref[3:4, :] * x_new)
        xbc = _silu(acc)
        xs = xbc[:, 0:D_SSM]
        dt = dt_ref[...]
        da = dt * (-jnp.exp(alog_ref[...]))
        ex = exp_ref[...]
        x_dt = xs * _dot_exact_rhs01(dt, ex)
        decay = jnp.exp(_dot_exact_rhs01(da, exl_ref[...]))
        xs_scr[...] = xs
        bc_scr[...] = xbc[:, D_SSM:]
        for j in range(D_SSM // LANES):
            sl = slice(j * LANES, (j + 1) * LANES)
            xt_scr[sl, :] = x_dt[:, sl].T
            at_scr[j] = decay[:, 2 * j * LANES:2 * (j + 1) * LANES]

    g_is_1 = p >= pairs_per_group
    row0 = pl.multiple_of(p * pair_w, pair_w)
    x_t = xt_scr[pl.ds(row0, pair_w), :]
    a_p = at_scr[p]
    bc = bc_scr[...]
    b_all = jnp.where(g_is_1, bc[:, SSM_STATE:2 * SSM_STATE], bc[:, 0:SSM_STATE])
    c_all = jnp.where(g_is_1, bc[:, 3 * SSM_STATE:4 * SSM_STATE],
                      bc[:, 2 * SSM_STATE:3 * SSM_STATE]).astype(BF16)
    for t in range(nb):
        inject = x_t[:, t:t + 1] * b_all[t:t + 1, :]
        halves = []
        for sub in range(2):
            rows = slice(sub * SSM_HEAD_DIM, (sub + 1) * SSM_HEAD_DIM)
            half = a_p[t:t + 1, sub * LANES:(sub + 1) * LANES] * st_ref[t, sub] + inject[rows]
            st_out_ref[t, sub] = half
            halves.append(half)
        new = jnp.concatenate(halves, axis=0)
        c_rows = jnp.broadcast_to(c_all[t:t + 1, :], (SUBLANES, SSM_STATE))
        y_scr[p, t:t + 1, :] = _dot_nt(c_rows, new.astype(BF16))[0:1, :]

    @pl.when(p == pl.num_programs(0) - 1)
    def _():
        y_mix = jnp.concatenate([y_scr[j] for j in range(SSM_HEADS // 2)], axis=-1)
        y = y_mix + dvec_ref[...] * xs_scr[...]
        y_ref[...] = _ssm_gate_norm(y, z_ref[...], nw_ref[...]).astype(BF16)


def _ssd_step(mix, dt, buf, conv_w, conv_b, a_log, d_full, norm_w, expand, state):
    nb = state.shape[0]
    n_pairs = SSM_HEADS // 2
    const = lambda shape: pl.BlockSpec(shape, lambda p: (0, 0))
    st_spec = pl.BlockSpec((nb, 2, SSM_HEAD_DIM, SSM_STATE), lambda p: (0, p, 0, 0))
    return pl.pallas_call(
        _ssd_step_kernel,
        grid=(n_pairs,),
        in_specs=[
            pl.BlockSpec((nb, D_SSM), lambda p: (0, OFF_Z // D_SSM)),
            pl.BlockSpec((nb, D_SSM), lambda p: (0, OFF_XS // D_SSM)),
            pl.BlockSpec((nb, 512), lambda p: (0, OFF_BC // 512)),
            const((nb, LANES)),
            const((nb, CONV_DIM)), const((nb, CONV_DIM)), const((nb, CONV_DIM)),
            const((SSM_CONV, CONV_DIM)), const((1, CONV_DIM)),
            const((1, LANES)),
            const((1, D_SSM)), const((1, D_SSM)),
            const((LANES, D_SSM)), const((LANES, SSM_HEADS * LANES)),
            st_spec,
        ],
        out_specs=[const((nb, D_SSM)), st_spec],
        out_shape=[
            jax.ShapeDtypeStruct((nb, D_SSM), BF16),
            jax.ShapeDtypeStruct(state.shape, F32),
        ],
        scratch_shapes=[
            pltpu.VMEM((D_SSM, nb), F32),
            pltpu.VMEM((n_pairs, nb, 2 * LANES), F32),
            pltpu.VMEM((nb, D_SSM), F32),
            pltpu.VMEM((nb, 2 * SSM_GROUPS * SSM_STATE), F32),
            pltpu.VMEM((n_pairs, nb, 2 * SSM_HEAD_DIM), F32),
        ],
        compiler_params=_cp(("arbitrary",)),
        name="ssd_step",
    )(mix, mix, mix, dt, buf[:, 0], buf[:, 1], buf[:, 2], conv_w, conv_b, a_log,
      d_full, norm_w, expand, jnp.asarray(_head_expand(LANES), BF16), state)


def _ffn_prompt_kernel(oa_ref, ys_ref, x_ref, wo_ref, n2_ref, wg_ref, wv_ref, wd_ref, cw_ref, cb_ref,
                       fnw_ref, y_ref, tail_ref, ge_scr, *, tiles_per_seq):
    i = pl.program_id(0)
    tm = x_ref.shape[0]
    pad = SUBLANES
    seq_start = lax.rem(i, tiles_per_seq) == 0

    @pl.when(seq_start)
    def _():
        ge_scr[0:pad, :] = jnp.zeros((pad, D_FF), F32)

    @pl.when(jnp.logical_not(seq_start))
    def _():
        ge_scr[0:pad, :] = ge_scr[tm:tm + pad, :]

    n_sub = max(1, tm // FFN_SUB_ROWS)
    sub = tm // n_sub
    rs = [slice(s * sub, (s + 1) * sub) for s in range(n_sub)]
    x1 = [x_ref[r, :] + _dot(oa_ref[r, :], wo_ref[0:D_HGRN, :])
          + _dot(ys_ref[r, :], wo_ref[D_HGRN:D_HGRN + D_SSM, :]) for r in rs]
    h2 = [_rms(x, n2_ref[...]).astype(BF16) for x in x1]

    acc = [None] * n_sub
    bounds = np.cumsum((0,) + FFN_COL_BLOCKS)
    for c0, c1 in zip(bounds[:-1].tolist(), bounds[1:].tolist()):
        for s, r in enumerate(rs):
            gate = _dot(h2[s], wg_ref[:, c0:c1])
            val = _dot(h2[s], wv_ref[:, c0:c1])
            ge_scr[pad + r.start:pad + r.stop, c0:c1] = gate
            if s == n_sub - 1:
                tail_ref[0, :, c0:c1] = gate[sub - pad:, :]
            conv = (cb_ref[:, c0:c1] + cw_ref[2:3, c0:c1] * gate
                    + cw_ref[1:2, c0:c1] * ge_scr[pad - 1 + r.start:pad - 1 + r.stop, c0:c1]
                    + cw_ref[0:1, c0:c1] * ge_scr[pad - 2 + r.start:pad - 2 + r.stop, c0:c1])
            act = (_silu(conv) * val).astype(BF16)
            part = _dot(act, wd_ref[c0:c1, :])
            acc[s] = part if acc[s] is None else acc[s] + part
    for s, r in enumerate(rs):
        y_ref[r, :] = _rms(x1[s] + acc[s], fnw_ref[...])


FFN_ROW_TILE = 512
FFN_SUB_ROWS = 256
FFN_COL_BLOCKS = (1024, 1024, 768)
assert sum(FFN_COL_BLOCKS) == D_FF and all(c % LANES == 0 for c in FFN_COL_BLOCKS)


def _ffn_prompt(o_a, y_s, x2d, w_o, norm2_w, w_gate, w_val, w_down, conv_w, conv_b, fnorm_w, seq):
    n = x2d.shape[0]
    tm = FFN_ROW_TILE
    assert seq % tm == 0
    kern = functools.partial(_ffn_prompt_kernel, tiles_per_seq=seq // tm)
    row = lambda w: pl.BlockSpec((tm, w), lambda i: (i, 0))
    resident = lambda shape: pl.BlockSpec(shape, lambda i: (0, 0), pipeline_mode=pl.Buffered(1))
    return pl.pallas_call(
        kern,
        grid=(n // tm,),
        in_specs=[
            row(D_HGRN), row(D_SSM), row(D_MODEL),
            resident((D_HGRN + D_SSM, D_MODEL)), resident((1, D_MODEL)),
            resident((D_MODEL, D_FF)), resident((D_MODEL, D_FF)), resident((D_FF, D_MODEL)),
            resident((FFN_CONV, D_FF)), resident((1, D_FF)), resident((1, D_MODEL)),
        ],
        out_specs=[
            row(D_MODEL),
            pl.BlockSpec((1, SUBLANES, D_FF), lambda i: (i, 0, 0)),
        ],
        out_shape=[
            jax.ShapeDtypeStruct((n, D_MODEL), F32),
            jax.ShapeDtypeStruct((n // tm, SUBLANES, D_FF), F32),
        ],
        scratch_shapes=[pltpu.VMEM((tm + SUBLANES, D_FF), F32)],
        compiler_params=_cp(("arbitrary",)),
        name="ffn_prompt",
    )(o_a, y_s, x2d, w_o, norm2_w, w_gate, w_val, w_down, conv_w, conv_b, fnorm_w)


FF_CAST_BLOCK = 256


def _ffn_step_kernel(oa_ref, ys_ref, x_ref, wo_ref, n2_ref, wg_ref, wv_ref, wd_ref, cw_ref, cb_ref,
                     fnw_ref, b0_ref, b1_ref,
                     y_ref, gate_ref, wob_ref, wgb_ref, wvb_ref, wdb_ref, x1_scr, h2_scr, acc_scr):
    j = pl.program_id(0)

    @pl.when(j == 0)
    def _():
        wo = wo_ref[...].astype(BF16)
        wob_ref[...] = wo
        x1 = (x_ref[...] + _dot(oa_ref[...], wo[0:D_HGRN, :]) + _dot(ys_ref[...], wo[D_HGRN:, :]))
        x1_scr[...] = x1
        h2_scr[...] = _rms(x1, n2_ref[...]).astype(BF16)
        acc_scr[...] = jnp.zeros_like(acc_scr)

    wg = wg_ref[...].astype(BF16)
    wv = wv_ref[...].astype(BF16)
    wd = wd_ref[...].astype(BF16)
    wgb_ref[...] = wg
    wvb_ref[...] = wv
    wdb_ref[...] = wd
    h2 = h2_scr[...]
    gate = _dot(h2, wg)
    val = _dot(h2, wv)
    gate_ref[...] = gate
    conv = (cb_ref[...] + cw_ref[2:3, :] * gate + cw_ref[1:2, :] * b1_ref[...]
            + cw_ref[0:1, :] * b0_ref[...])
    act = (_silu(conv) * val).astype(BF16)
    acc_scr[...] = acc_scr[...] + _dot(act, wd)

    @pl.when(j == pl.num_programs(0) - 1)
    def _():
        y_ref[...] = _rms(x1_scr[...] + acc_scr[...], fnw_ref[...])


def _ffn_step(o_a, y_s, x2d, w_out, norm2_w, w_up, w_down, conv_w, conv_b, fnorm_w, buf):
    n = x2d.shape[0]
    blk = FF_CAST_BLOCK
    nj = D_FF // blk
    const = lambda shape: pl.BlockSpec(shape, lambda j: (0, 0))
    col = lambda rows: pl.BlockSpec((rows, blk), lambda j: (0, j))
    return pl.pallas_call(
        _ffn_step_kernel,
        grid=(nj,),
        in_specs=[
            const((n, D_HGRN)), const((n, D_SSM)), const((n, D_MODEL)),
            const((D_HGRN + D_SSM, D_MODEL)), const((1, D_MODEL)),
            col(D_MODEL), pl.BlockSpec((D_MODEL, blk), lambda j: (0, nj + j)),
            pl.BlockSpec((blk, D_MODEL), lambda j: (j, 0)),
            col(FFN_CONV), col(1), const((1, D_MODEL)), col(n), col(n),
        ],
        out_specs=[
            const((n, D_MODEL)), col(n),
            const((D_HGRN + D_SSM, D_MODEL)), col(D_MODEL), col(D_MODEL),
            pl.BlockSpec((blk, D_MODEL), lambda j: (j, 0)),
        ],
        out_shape=[
            jax.ShapeDtypeStruct((n, D_MODEL), F32),
            jax.ShapeDtypeStruct((n, D_FF), F32),
            jax.ShapeDtypeStruct((D_HGRN + D_SSM, D_MODEL), BF16),
            jax.ShapeDtypeStruct((D_MODEL, D_FF), BF16),
            jax.ShapeDtypeStruct((D_MODEL, D_FF), BF16),
            jax.ShapeDtypeStruct((D_FF, D_MODEL), BF16),
        ],
        scratch_shapes=[
            pltpu.VMEM((n, D_MODEL), F32),
            pltpu.VMEM((n, D_MODEL), BF16),
            pltpu.VMEM((n, D_MODEL), F32),
        ],
        compiler_params=_cp(("arbitrary",)),
        name="ffn_step",
    )(o_a, y_s, x2d, w_out, norm2_w, w_up, w_up, w_down, conv_w, conv_b, fnorm_w,
      buf[:, 0], buf[:, 1])


def _row(v):
    return v.reshape(1, -1).astype(F32)


def _pad_lanes(v):
    return jnp.pad(v.astype(F32), (0, LANES - v.shape[0])).reshape(1, LANES)


def kernel(x_prompt, x_sample, state_hgrn, state_ssm, state_conv_ssm, state_conv_ffn, norm1_w, w_in, hgrn_lb, hgrn_norm_w, ssm_conv_w, ssm_conv_b, ssm_dt_bias, ssm_a_log, ssm_d, ssm_norm_w, w_out, norm2_w, w_up, ffn_conv_w, ffn_conv_b, w_down, final_norm_w):
    depth = w_in.shape[0]
    assert depth == 1, "single-layer trunk"
    l = 0
    batch, seq, _ = x_prompt.shape
    dec_batch, dec_seq, _ = x_sample.shape
    assert dec_seq == 1 and seq % CHUNK == 0 and seq >= SSM_CONV

    w_in_t = w_in[l].T
    d_full = jnp.repeat(ssm_d[l].astype(F32), SSM_HEAD_DIM).reshape(1, D_SSM)
    dt_bias = _pad_lanes(ssm_dt_bias[l])
    a_log = _pad_lanes(ssm_a_log[l])
    mconst = jnp.asarray(_hgrn_const(), BF16)
    tri = jnp.asarray(np.tril(np.ones((CHUNK, CHUNK), np.float32)), BF16)
    expand = jnp.asarray(_head_expand(), BF16)
    lb_raw = hgrn_lb.astype(F32)

    xs_ = x_sample.reshape(dec_batch, D_MODEL)
    w_main, proj_s, lg_s, dt_s = _inproj_cast(xs_, _row(norm1_w[l]), w_in_t, lb_raw, dt_bias)
    oa_s, hgrn_s = _hgrn_step(proj_s, lg_s, _row(hgrn_norm_w[l]), state_hgrn[l])
    ys_s, ssm_s = _ssd_step(proj_s, dt_s, state_conv_ssm[l], ssm_conv_w[l], _row(ssm_conv_b[l]),
                            a_log, d_full, _row(ssm_norm_w[l]), expand, state_ssm[l])
    y_s, gate_s, w_ob, w_gb, w_vb, w_db = _ffn_step(
        oa_s, ys_s, xs_, w_out[l], _row(norm2_w[l]), w_up[l], w_down[l],
        ffn_conv_w[l], _row(ffn_conv_b[l]), _row(final_norm_w), state_conv_ffn[l])
    cs_s = jnp.concatenate([state_conv_ssm[l][:, 1:], proj_s[:, None, OFF_XS:OFF_XS + CONV_DIM]],
                           axis=1)
    cf_s = jnp.concatenate([state_conv_ffn[l][:, 1:], gate_s[:, None, :]], axis=1)

    xp = x_prompt.reshape(batch * seq, D_MODEL)
    proj_p, lg_p, dt_p = _inproj(xp, _row(norm1_w[l]), w_main, w_in_t, lb_raw, dt_bias)
    oa_p, hgrn_p = _hgrn_prompt(proj_p, lg_p, _row(hgrn_norm_w[l]), mconst, batch, seq)
    ys_p, ssm_p = _ssd_prompt(proj_p, dt_p, ssm_conv_w[l], _row(ssm_conv_b[l]), a_log,
                              d_full, _row(ssm_norm_w[l]), tri, expand, batch, seq)
    y_p, tail_p = _ffn_prompt(oa_p, ys_p, xp, w_ob, _row(norm2_w[l]), w_gb, w_vb, w_db,
                              ffn_conv_w[l], _row(ffn_conv_b[l]), _row(final_norm_w), seq)
    proj_p3 = proj_p.reshape(batch, seq, D_MAIN)
    cs_p = proj_p3[:, seq - (SSM_CONV - 1):, OFF_XS:OFF_XS + CONV_DIM]
    tails = tail_p.reshape(batch, seq // FFN_ROW_TILE, SUBLANES, D_FF)
    cf_p = tails[:, -1, SUBLANES - (FFN_CONV - 1):, :]

    dt_ = x_prompt.dtype
    return (y_p.reshape(batch, seq, D_MODEL).astype(dt_),
            y_s.reshape(dec_batch, 1, D_MODEL).astype(dt_),
            hgrn_p[None].astype(dt_),
            hgrn_s[None].astype(dt_),
            ssm_p.reshape(1, batch, SSM_HEADS, SSM_HEAD_DIM, SSM_STATE).astype(dt_),
            ssm_s[None].astype(dt_),
            cs_p[None].astype(dt_),
            cs_s[None].astype(dt_),
            cf_p[None].astype(dt_),
            cf_s[None].astype(dt_))
```

```python
import functools

import numpy as np
import jax
import jax.numpy as jnp
from jax import lax
from jax.experimental import pallas as pl
from jax.experimental.pallas import tpu as pltpu

F32 = jnp.float32
BF16 = jnp.bfloat16
EPS = 1e-6

LANES = 128
SUBLANES = 8

D_MODEL = 1024
HGRN_HEADS = 8
HGRN_DK = 128
HGRN_DV = 128
D_HGRN = HGRN_HEADS * HGRN_DV
SSM_HEADS = 16
SSM_HEAD_DIM = 64
D_SSM = SSM_HEADS * SSM_HEAD_DIM
SSM_STATE = 128
SSM_GROUPS = 2
SSM_CONV = 4
CONV_DIM = D_SSM + 2 * SSM_GROUPS * SSM_STATE
D_FF = 2816
FFN_CONV = 3
D_MAIN = 4 * D_HGRN + D_SSM + CONV_DIM
OFF_Q, OFF_F, OFF_I, OFF_G = 0, 1024, 2048, 3072
OFF_Z, OFF_XS, OFF_BC = 4096, 5120, 6144

CHUNK = 128
GROUP_W = D_SSM // SSM_GROUPS
VMEM_LIMIT = 56 * 1024 * 1024


def _cp(sem):
    return pltpu.CompilerParams(dimension_semantics=sem, vmem_limit_bytes=VMEM_LIMIT)


def _dot(a, b):
    return jnp.dot(a, b, preferred_element_type=F32)


def _dot_nt(a, b):
    return lax.dot_general(a, b, (((1,), (1,)), ((), ())), preferred_element_type=F32)


def _split3(x):
    h = x.astype(BF16)
    r = x - h.astype(F32)
    m = r.astype(BF16)
    lo = (r - m.astype(F32)).astype(BF16)
    return h, m, lo


def _dot_exact_lhs01(m01, x):
    h, m, lo = _split3(x)
    return _dot(m01, h) + _dot(m01, m) + _dot(m01, lo)


def _dot_exact_rhs01(x, m01):
    h, m, lo = _split3(x)
    return _dot(h, m01) + _dot(m, m01) + _dot(lo, m01)


def _dot_split_lhs01(m01, x):
    h = x.astype(BF16)
    lo = (x - h.astype(F32)).astype(BF16)
    return _dot(m01, h) + _dot(m01, lo)


def _sigmoid(x):
    return 1.0 / (1.0 + jnp.exp(-x))


def _silu(x):
    return x * _sigmoid(x)


def _rms(x, w):
    ms = jnp.mean(x * x, axis=-1, keepdims=True)
    return x * lax.rsqrt(ms + EPS) * w


def _dt_proj(hb, wdt_ref):
    rows = lax.broadcasted_iota(jnp.int32, wdt_ref.shape, 0)
    wdt = jnp.where(rows < SSM_HEADS, wdt_ref[...], 0.0).astype(BF16)
    return _dot_nt(hb, wdt)

def _inproj_kernel(x_ref, nw_ref, w_ref, wdt_ref, lb_ref, dtb_ref, mix_ref, lg_ref, dt_ref):
    n_sub = max(1, x_ref.shape[0] // INPROJ_SUB_ROWS)
    sub_rows = x_ref.shape[0] // n_sub
    rs = [slice(s * sub_rows, (s + 1) * sub_rows) for s in range(n_sub)]
    hb = [_rms(x_ref[r, :], nw_ref[...]).astype(BF16) for r in rs]

    def put(r, off, val):
        mix_ref[r, off:off + val.shape[1]] = val.astype(BF16)

    w = D_HGRN
    lb = _hgrn_lb(lb_ref[...])
    for s, r in enumerate(rs):
        qf = _dot(hb[s], w_ref[:, OFF_Q:OFF_Q + 2 * w])
        f = lb + (1.0 - lb) * _sigmoid(qf[:, w:])
        lg_ref[r, :] = jnp.log(f)
        put(r, OFF_F, 1.0 - f)
        put(r, OFF_Q, _silu(qf[:, :w]))
    for s, r in enumerate(rs):
        ig = _dot(hb[s], w_ref[:, OFF_I:OFF_I + 2 * w])
        put(r, OFF_I, ig[:, :w])
        put(r, OFF_G, _silu(ig[:, w:]))
    for s, r in enumerate(rs):
        zx = _dot(hb[s], w_ref[:, OFF_Z:OFF_Z + D_SSM + CONV_DIM])
        put(r, OFF_Z, _silu(zx[:, :D_SSM]))
        put(r, OFF_XS, zx[:, D_SSM:])
    for s, r in enumerate(rs):
        dt_ref[r, :] = _softplus(_dt_proj(hb[s], wdt_ref) + dtb_ref[...])


INPROJ_ROW_TILE = 512
INPROJ_SUB_ROWS = 256


def _inproj(x2d, norm_w, w_main, w_t, lb_raw, dt_bias):
    n = x2d.shape[0]
    tm = min(INPROJ_ROW_TILE, n)
    assert n % tm == 0
    row = lambda w: pl.BlockSpec((tm, w), lambda i: (i, 0))
    resident = lambda shape: pl.BlockSpec(shape, lambda i: (0, 0), pipeline_mode=pl.Buffered(1))
    return pl.pallas_call(
        _inproj_kernel,
        grid=(n // tm,),
        in_specs=[
            row(D_MODEL), resident((1, D_MODEL)),
            resident((D_MODEL, D_MAIN)),
            pl.BlockSpec((LANES, D_MODEL), lambda i: (D_MAIN // LANES, 0), pipeline_mode=pl.Buffered(1)),
            resident(lb_raw.shape), resident((1, LANES)),
        ],
        out_specs=[row(D_MAIN), row(D_HGRN), row(LANES)],
        out_shape=[
            jax.ShapeDtypeStruct((n, D_MAIN), BF16),
            jax.ShapeDtypeStruct((n, D_HGRN), F32),
            jax.ShapeDtypeStruct((n, LANES), F32),
        ],
        compiler_params=_cp(("arbitrary",)),
        name="inproj",
    )(x2d, norm_w, w_main, w_t, lb_raw, dt_bias)


CAST_BLOCK = 512


def _inproj_cast_kernel(x_ref, nw_ref, w_ref, wdt_ref, lb_ref, dtb_ref,
                        wb_ref, mix_ref, lg_ref, dt_ref, h_scr):
    j = pl.program_id(0)
    blk = CAST_BLOCK
    q0, f0, i0, g0, z0, x0 = (off // blk for off in (OFF_Q, OFF_F, OFF_I, OFF_G, OFF_Z, OFF_XS))

    @pl.when(j == 0)
    def _():
        hb = _rms(x_ref[...], nw_ref[...]).astype(BF16)
        h_scr[...] = hb
        dt_ref[...] = _softplus(_dt_proj(hb, wdt_ref) + dtb_ref[...])

    wb = w_ref[...].T.astype(BF16)
    wb_ref[...] = wb
    p = _dot(h_scr[...], wb)

    @pl.when(((j >= q0) & (j < f0)) | ((j >= g0) & (j < x0)))
    def _():
        mix_ref[...] = _silu(p).astype(BF16)

    @pl.when(((j >= i0) & (j < g0)) | (j >= x0))
    def _():
        mix_ref[...] = p.astype(BF16)

    for fj in range(f0, i0):
        @pl.when(j == fj)
        def _(fj=fj):
            cols = slice((fj - f0) * blk, (fj - f0 + 1) * blk)
            lb = _hgrn_lb(lb_ref[:, cols])
            f = lb + (1.0 - lb) * _sigmoid(p)
            lg_ref[:, cols] = jnp.log(f)
            mix_ref[...] = (1.0 - f).astype(BF16)


def _inproj_cast(x2d, norm_w, w_t, lb_raw, dt_bias):
    n = x2d.shape[0]
    blk = CAST_BLOCK
    const = lambda shape: pl.BlockSpec(shape, lambda j: (0, 0))
    return pl.pallas_call(
        _inproj_cast_kernel,
        grid=(D_MAIN // blk,),
        in_specs=[
            const((n, D_MODEL)), const((1, D_MODEL)),
            pl.BlockSpec((blk, D_MODEL), lambda j: (j, 0)),
            pl.BlockSpec((LANES, D_MODEL), lambda j: (D_MAIN // LANES, 0)),
            const(lb_raw.shape), const((1, LANES)),
        ],
        out_specs=[
            pl.BlockSpec((D_MODEL, blk), lambda j: (0, j)),
            pl.BlockSpec((n, blk), lambda j: (0, j)),
            const((n, D_HGRN)), const((n, LANES)),
        ],
        out_shape=[
            jax.ShapeDtypeStruct((D_MODEL, D_MAIN), BF16),
            jax.ShapeDtypeStruct((n, D_MAIN), BF16),
            jax.ShapeDtypeStruct((n, D_HGRN), F32),
            jax.ShapeDtypeStruct((n, LANES), F32),
        ],
        scratch_shapes=[pltpu.VMEM((n, D_MODEL), BF16)],
        compiler_params=_cp(("arbitrary",)),
        name="inproj_cast",
    )(x2d, norm_w, w_t, w_t, lb_raw, dt_bias)


LOG2E = 1.4426950408889634
N_LEVELS = 7
MXU_LEVEL_HALVES = (4, 2)


def _hgrn_const():
    c = CHUNK
    t = np.arange(c)[:, None]
    j = np.arange(c)[None, :]
    blocks = [(j <= t)]
    for h in MXU_LEVEL_HALVES:
        mid = (t // (2 * h)) * (2 * h) + h
        upper = (t >= mid) & (j >= mid) & (j <= t)
        lower = (t < mid) & (j > t) & (j < mid)
        blocks.append(upper | lower)
    return np.concatenate(blocks, axis=0).astype(np.float32)


def _midpoint_decay(b, h):
    pieces = []
    for start in range(0, CHUNK, 2 * h):
        mid = start + h
        m = b[mid - 1:mid, :]
        pieces.append(m - b[start:mid])
        pieces.append(b[mid:mid + h] - m)
    return jnp.concatenate(pieces, axis=0)


def _mix_rows(q, k, h):
    pieces = []
    for start in range(0, CHUNK, 2 * h):
        pieces.append(k[start:start + h])
        pieces.append(q[start + h:start + 2 * h])
    return jnp.concatenate(pieces, axis=0)


def _hgrn_lb(lb_raw):
    mx = jnp.max(lb_raw, axis=0, keepdims=True)
    e = jnp.exp(lb_raw - mx)
    return e[0:1, :] / jnp.sum(e, axis=0, keepdims=True)


def _level_map():
    t = lax.broadcasted_iota(jnp.int32, (CHUNK, CHUNK), 0)
    s = lax.broadcasted_iota(jnp.int32, (CHUNK, CHUNK), 1)
    bitlen = 32 - lax.clz(t ^ s)
    return jnp.where(t > s, bitlen, jnp.where(t == s, 0, -1))


def _hgrn_prompt_kernel(q_ref, k_ref, i_ref, g_ref, lg_ref, nw_ref, mc_ref,
                        o_ref, s_out_ref, st_scr):
    c = pl.program_id(1)

    @pl.when(c == 0)
    def _():
        st_scr[...] = jnp.zeros_like(st_scr)

    n_sub = q_ref.shape[0] // CHUNK
    lev = _level_map().astype(jnp.int16)
    row = lax.broadcasted_iota(jnp.int32, (CHUNK, HGRN_DK), 0)
    heads = range(HGRN_HEADS)
    pairs = [(s, h) for s in range(n_sub) for h in heads]
    rs = {s: slice(s * CHUNK, (s + 1) * CHUNK) for s in range(n_sub)}
    cs = {h: slice(h * HGRN_DK, (h + 1) * HGRN_DK) for h in heads}

    e_sub = {s: _dot_split_lhs01(mc_ref[...], lg_ref[rs[s], :] * LOG2E) for s in range(n_sub)}
    qb = {(s, h): q_ref[rs[s], cs[h]] for s, h in pairs}
    kb = {(s, h): k_ref[rs[s], cs[h]] for s, h in pairs}
    vb = {(s, h): i_ref[rs[s], cs[h]] for s, h in pairs}
    q = {p: qb[p].astype(F32) for p in pairs}
    k = {p: kb[p].astype(F32) for p in pairs}
    b = {(s, h): e_sub[s][0:CHUNK, cs[h]] for s, h in pairs}
    b_last = {p: b[p][CHUNK - 1:CHUNK, :] for p in pairs}

    st = {h: st_scr[h] for h in heads}
    o = {}
    for s, h in pairs:
        p = (s, h)
        o[p] = _dot((q[p] * jnp.exp2(b[p])).astype(BF16), st[h].T.astype(BF16))
        ks = (k[p] * jnp.exp2(b_last[p] - b[p])).astype(BF16)
        st[h] = st[h] * jnp.exp2(b_last[p]) + _dot(vb[p].astype(F32).T.astype(BF16), ks)
    for h in heads:
        st_scr[h] = st[h]

    a = {p: jnp.where(lev == 0, _dot(qb[p], k[p].T.astype(BF16)).astype(BF16), jnp.zeros((), BF16))
         for p in pairs}
    half = CHUNK // 2
    while half >= 1:
        for p in pairs:
            if half >= SUBLANES:
                x = _mix_rows(q[p], k[p], half) * jnp.exp2(_midpoint_decay(b[p], half))
            else:
                upper = (row & half) != 0
                if half in MXU_LEVEL_HALVES:
                    blk = 1 + MXU_LEVEL_HALVES.index(half)
                    w = jnp.exp2(e_sub[p[0]][blk * CHUNK:(blk + 1) * CHUNK, cs[p[1]]])
                    x = jnp.where(upper, q[p], k[p]) * w
                else:
                    x = jnp.where(upper, q[p] * (1.0 - k[p]), k[p])
            gram = _dot(x.astype(BF16), x.T.astype(BF16))
            a[p] = jnp.where(lev == half.bit_length(), gram.astype(BF16), a[p])
        half //= 2

    for p in pairs:
        o[p] = o[p] + _dot(a[p], vb[p])
    for s, h in pairs:
        gate = g_ref[rs[s], cs[h]].astype(F32)
        o_ref[rs[s], cs[h]] = (_rms(o[(s, h)], nw_ref[...]) * gate).astype(BF16)

    @pl.when(c == pl.num_programs(1) - 1)
    def _():
        for h in range(HGRN_HEADS):
            s_out_ref[0, h] = st_scr[h].T


HGRN_SUBCHUNKS = 4


def _hgrn_prompt(mix, lg, norm_w, mconst, batch, seq):
    rows = HGRN_SUBCHUNKS * CHUNK
    assert seq % rows == 0
    nc = seq // rows

    def col(off):
        return pl.BlockSpec((rows, D_HGRN), lambda b, c: (b * nc + c, off // D_HGRN))

    return pl.pallas_call(
        _hgrn_prompt_kernel,
        grid=(batch, nc),
        in_specs=[
            col(OFF_Q), col(OFF_F), col(OFF_I), col(OFF_G), col(0),
            pl.BlockSpec((1, HGRN_DV), lambda b, c: (0, 0)),
            pl.BlockSpec(mconst.shape, lambda b, c: (0, 0)),
        ],
        out_specs=[
            pl.BlockSpec((rows, D_HGRN), lambda b, c: (b * nc + c, 0)),
            pl.BlockSpec((1, HGRN_HEADS, HGRN_DK, HGRN_DV), lambda b, c: (b, 0, 0, 0)),
        ],
        out_shape=[
            jax.ShapeDtypeStruct((batch * seq, D_HGRN), BF16),
            jax.ShapeDtypeStruct((batch, HGRN_HEADS, HGRN_DK, HGRN_DV), F32),
        ],
        scratch_shapes=[pltpu.VMEM((HGRN_HEADS, HGRN_DV, HGRN_DK), F32)],
        compiler_params=_cp(("arbitrary", "arbitrary")),
        name="hgrn_prompt",
    )(mix, mix, mix, mix, lg, norm_w, mconst)


def _hgrn_step_kernel(q_ref, i_ref, g_ref, lg_ref, nw_ref, s_ref,
                      o_ref, s_out_ref, o_scr):
    nb = q_ref.shape[0]
    qb = q_ref[...]
    f_t = jnp.exp(lg_ref[...]).T
    v = i_ref[...].astype(F32)
    lhs_rows = 2 * SUBLANES
    for t in range(nb):
        v_row = v[t:t + 1, :]
        s_new = v_row + f_t[:, t:t + 1] * (s_ref[t, 0] - v_row)
        s_out_ref[t, 0] = s_new
        q_rows = jnp.broadcast_to(qb[t:t + 1, :], (lhs_rows, HGRN_DK))
        o_scr[t:t + 1, :] = _dot(q_rows, s_new.astype(BF16))[0:1, :]
    o_ref[...] = (_rms(o_scr[...], nw_ref[...]) * g_ref[...].astype(F32)).astype(BF16)


def _hgrn_step(mix, lg, norm_w, state):
    nb = state.shape[0]
    hb = lambda off: off // HGRN_DK

    def col(off):
        return pl.BlockSpec((nb, HGRN_DK), lambda h: (0, hb(off) + h))

    st_spec = pl.BlockSpec((nb, 1, HGRN_DK, HGRN_DV), lambda h: (0, h, 0, 0))
    return pl.pallas_call(
        _hgrn_step_kernel,
        grid=(HGRN_HEADS,),
        in_specs=[
            col(OFF_Q), col(OFF_I), col(OFF_G), col(0),
            pl.BlockSpec((1, HGRN_DV), lambda h: (0, 0)),
            st_spec,
        ],
        out_specs=[pl.BlockSpec((nb, HGRN_DV), lambda h: (0, h)), st_spec],
        out_shape=[
            jax.ShapeDtypeStruct((nb, D_HGRN), BF16),
            jax.ShapeDtypeStruct(state.shape, F32),
        ],
        scratch_shapes=[pltpu.VMEM((nb, HGRN_DV), F32)],
        compiler_params=_cp(("arbitrary",)),
        name="hgrn_step",
    )(mix, mix, mix, lg, norm_w, state)


def _head_expand(width=SSM_HEAD_DIM):
    e = np.zeros((LANES, SSM_HEADS * width), np.float32)
    for h in range(SSM_HEADS):
        e[h, h * width:(h + 1) * width] = 1.0
    return e


def _softplus(x):
    return jnp.maximum(x, 0.0) + jnp.log(1.0 + jnp.exp(-jnp.abs(x)))


def _ssm_gate_norm(y, z_gate, nw):
    y = y * z_gate.astype(F32)
    parts = [_rms(y[:, g * GROUP_W:(g + 1) * GROUP_W], nw[:, g * GROUP_W:(g + 1) * GROUP_W])
             for g in range(SSM_GROUPS)]
    return jnp.concatenate(parts, axis=-1)


def _ssd_prompt_kernel(z_ref, xs_ref, bc_ref, dt_ref, cw_ref, cb_ref, alog_ref,
                       dvec_ref, nw_ref, tri_ref, exp_ref,
                       shift_ref, y_ref, st_out_ref, xprev_scr, st_scr):
    c = pl.program_id(1)
    t = CHUNK

    @pl.when(c == 0)
    def _():
        st_scr[...] = jnp.zeros_like(st_scr)
        xprev_scr[...] = jnp.zeros_like(xprev_scr)

    subs = range(xs_ref.shape[0] // t)
    rs = [slice(s * t, (s + 1) * t) for s in subs]

    x_cur = [jnp.concatenate([xs_ref[r, :], bc_ref[r, :]], axis=-1) for r in rs]
    x_prev = [xprev_scr[...]] + x_cur[:-1]
    xprev_scr[...] = x_cur[-1]
    taps = [_dot(shift_ref[...], jnp.concatenate([x_prev[s], x_cur[s]], axis=0)) for s in subs]
    xbc = []
    for s in subs:
        acc = cb_ref[...] + cw_ref[SSM_CONV - 1:SSM_CONV, :] * x_cur[s].astype(F32)
        for d in range(1, SSM_CONV):
            acc = acc + cw_ref[SSM_CONV - 1 - d:SSM_CONV - d, :] * taps[s][(d - 1) * t:d * t, :]
        xbc.append(_silu(acc))
    xs = [x[:, 0:D_SSM] for x in xbc]

    dt = [dt_ref[r, :] for r in rs]
    neg_a = -LOG2E * jnp.exp(alog_ref[...])
    cs = [_dot_exact_lhs01(tri_ref[...], dt[s] * neg_a) for s in subs]
    ex = exp_ref[...]
    dt_full = [_dot_exact_rhs01(dt[s], ex) for s in subs]
    cs_full = [_dot_exact_rhs01(cs[s], ex) for s in subs]
    cs_last_full = [x[t - 1:t, :] for x in cs_full]
    x_dt = [xs[s] * dt_full[s] for s in subs]
    x_end = [(x_dt[s] * jnp.exp2(cs_last_full[s] - cs_full[s])).astype(BF16) for s in subs]
    cs_t = [x.T for x in cs]

    causal = (lax.broadcasted_iota(jnp.int32, (t, t), 0)
              >= lax.broadcasted_iota(jnp.int32, (t, t), 1))
    lane = lax.broadcasted_iota(jnp.int32, (1, D_SSM), 1)
    odd_head = (lane & SSM_HEAD_DIM) != 0
    x_by_parity = []
    for s in subs:
        x_b = x_dt[s].astype(BF16)
        zero = jnp.zeros_like(x_b)
        x_by_parity.append((jnp.where(odd_head, zero, x_b), jnp.where(odd_head, x_b, zero)))
    heads_per_group = SSM_HEADS // SSM_GROUPS
    pair_w = 2 * SSM_HEAD_DIM
    never = -1e30

    def group_bc(s, g):
        b_g = xbc[s][:, D_SSM + g * SSM_STATE:D_SSM + (g + 1) * SSM_STATE]
        c_off = D_SSM + SSM_GROUPS * SSM_STATE + g * SSM_STATE
        return b_g.T.astype(BF16), xbc[s][:, c_off:c_off + SSM_STATE].astype(BF16)

    bc_t = {(s, g): group_bc(s, g) for s in subs for g in range(SSM_GROUPS)}

    y_diag = []
    for s in subs:
        y_parts = []
        for g in range(SSM_GROUPS):
            b_t, c_g = bc_t[(s, g)]
            gmat = _dot(c_g, b_t)
            for pp in range(heads_per_group // 2):
                h0 = g * heads_per_group + 2 * pp
                psl = slice(h0 * SSM_HEAD_DIM, h0 * SSM_HEAD_DIM + pair_w)
                ws = []
                for sub in range(2):
                    h = h0 + sub
                    diff = cs[s][:, h:h + 1] - cs_t[s][h:h + 1, :]
                    ws.append((jnp.exp2(jnp.where(causal, diff, never)) * gmat).astype(BF16))
                y_parts.append(_dot(jnp.concatenate(ws, axis=1),
                                    jnp.concatenate([x_by_parity[s][0][:, psl],
                                                     x_by_parity[s][1][:, psl]], axis=0)))
        y_diag.append(jnp.concatenate(y_parts, axis=-1))

    y_off = []
    for s in subs:
        offs = []
        for g in range(SSM_GROUPS):
            sl = slice(g * GROUP_W, (g + 1) * GROUP_W)
            b_t, c_g = bc_t[(s, g)]
            st_g = st_scr[:, sl]
            offs.append(_dot(c_g, st_g.astype(BF16)))
            st_scr[:, sl] = st_g * jnp.exp2(cs_last_full[s][:, sl]) + _dot(b_t, x_end[s][:, sl])
        y_off.append(jnp.concatenate(offs, axis=-1) * jnp.exp2(cs_full[s]))

    for s in subs:
        y = y_diag[s] + y_off[s] + dvec_ref[...] * xs[s]
        y_ref[rs[s], :] = _ssm_gate_norm(y, z_ref[rs[s], :], nw_ref[...]).astype(BF16)

    @pl.when(c == pl.num_programs(1) - 1)
    def _():
        for j in range(D_SSM // LANES):
            st_out_ref[0, j * LANES:(j + 1) * LANES, :] = st_scr[:, j * LANES:(j + 1) * LANES].T


SSD_SUBCHUNKS = 1


def _ssd_prompt(mix, dt, conv_w, conv_b, a_log, d_full, norm_w, tri, expand, batch, seq):
    rows = SSD_SUBCHUNKS * CHUNK
    assert seq % rows == 0
    nc = seq // rows
    const = lambda shape: pl.BlockSpec(shape, lambda b, c: (0, 0))
    return pl.pallas_call(
        _ssd_prompt_kernel,
        grid=(batch, nc),
        in_specs=[
            pl.BlockSpec((rows, D_SSM), lambda b, c: (b * nc + c, OFF_Z // D_SSM)),
            pl.BlockSpec((rows, D_SSM), lambda b, c: (b * nc + c, OFF_XS // D_SSM)),
            pl.BlockSpec((rows, 512), lambda b, c: (b * nc + c, OFF_BC // 512)),
            pl.BlockSpec((rows, LANES), lambda b, c: (b * nc + c, 0)),
            const((SSM_CONV, CONV_DIM)), const((1, CONV_DIM)),
            const((1, LANES)),
            const((1, D_SSM)), const((1, D_SSM)),
            const((CHUNK, CHUNK)), const((LANES, D_SSM)),
            const(((SSM_CONV - 1) * CHUNK, 2 * CHUNK)),
        ],
        out_specs=[
            pl.BlockSpec((rows, D_SSM), lambda b, c: (b * nc + c, 0)),
            pl.BlockSpec((1, D_SSM, SSM_STATE), lambda b, c: (b, 0, 0)),
        ],
        out_shape=[
            jax.ShapeDtypeStruct((batch * seq, D_SSM), BF16),
            jax.ShapeDtypeStruct((batch, D_SSM, SSM_STATE), F32),
        ],
        scratch_shapes=[
            pltpu.VMEM((CHUNK, CONV_DIM), BF16),
            pltpu.VMEM((SSM_STATE, D_SSM), F32),
        ],
        compiler_params=_cp(("arbitrary", "arbitrary")),
        name="ssd_prompt",
    )(mix, mix, mix, dt, conv_w, conv_b, a_log, d_full, norm_w, tri, expand,
      jnp.asarray(_conv_shifts(), BF16))


def _conv_shifts():
    m = np.zeros(((SSM_CONV - 1) * CHUNK, 2 * CHUNK), np.float32)
    for d in range(1, SSM_CONV):
        for t in range(CHUNK):
            m[(d - 1) * CHUNK + t, CHUNK + t - d] = 1.0
    return m


def _ssd_step_kernel(z_ref, xs_ref, bc_ref, dt_ref, b0_ref, b1_ref, b2_ref, cw_ref, cb_ref,
                     alog_ref, dvec_ref, nw_ref, exp_ref, exl_ref, st_ref,
                     y_ref, st_out_ref, xt_scr, at_scr, xs_scr, bc_scr, y_scr):
    p = pl.program_id(0)
    nb = z_ref.shape[0]
    pair_w = 2 * SSM_HEAD_DIM
    pairs_per_group = SSM_HEADS // SSM_GROUPS // 2

    @pl.when(p == 0)
    def _():
        x_new = jnp.concatenate([xs_ref[...], bc_ref[...]], axis=-1).astype(F32)
        acc = (cb_ref[...] + cw_ref[0:1, :] * b0_ref[...] + cw_ref[1:2, :] * b1_ref[...]
               + cw_ref[2:3, :] * b2_ref[...] + cw_ref[3:4, :] * x_new)
        xbc = _silu(acc)
        xs = xbc[:, 0:D_SSM]
        dt = dt_ref[...]
        da = dt * (-jnp.exp(alog_ref[...]))
        ex = exp_ref[...]
        x_dt = xs * _dot_exact_rhs01(dt, ex)
        decay = jnp.exp(_dot_exact_rhs01(da, exl_ref[...]))
        xs_scr[...] = xs
        bc_scr[...] = xbc[:, D_SSM:]
        for j in range(D_SSM // LANES):
            sl = slice(j * LANES, (j + 1) * LANES)
            xt_scr[sl, :] = x_dt[:, sl].T
            at_scr[j] = decay[:, 2 * j * LANES:2 * (j + 1) * LANES]

    g_is_1 = p >= pairs_per_group
    row0 = pl.multiple_of(p * pair_w, pair_w)
    x_t = xt_scr[pl.ds(row0, pair_w), :]
    a_p = at_scr[p]
    bc = bc_scr[...]
    b_all = jnp.where(g_is_1, bc[:, SSM_STATE:2 * SSM_STATE], bc[:, 0:SSM_STATE])
    c_all = jnp.where(g_is_1, bc[:, 3 * SSM_STATE:4 * SSM_STATE],
                      bc[:, 2 * SSM_STATE:3 * SSM_STATE]).astype(BF16)
    for t in range(nb):
        inject = x_t[:, t:t + 1] * b_all[t:t + 1, :]
        halves = []
        for sub in range(2):
            rows = slice(sub * SSM_HEAD_DIM, (sub + 1) * SSM_HEAD_DIM)
            half = a_p[t:t + 1, sub * LANES:(sub + 1) * LANES] * st_ref[t, sub] + inject[rows]
            st_out_ref[t, sub] = half
            halves.append(half)
        new = jnp.concatenate(halves, axis=0)
        c_rows = jnp.broadcast_to(c_all[t:t + 1, :], (SUBLANES, SSM_STATE))
        y_scr[p, t:t + 1, :] = _dot_nt(c_rows, new.astype(BF16))[0:1, :]

    @pl.when(p == pl.num_programs(0) - 1)
    def _():
        y_mix = jnp.concatenate([y_scr[j] for j in range(SSM_HEADS // 2)], axis=-1)
        y = y_mix + dvec_ref[...] * xs_scr[...]
        y_ref[...] = _ssm_gate_norm(y, z_ref[...], nw_ref[...]).astype(BF16)


def _ssd_step(mix, dt, buf, conv_w, conv_b, a_log, d_full, norm_w, expand, state):
    nb = state.shape[0]
    n_pairs = SSM_HEADS // 2
    const = lambda shape: pl.BlockSpec(shape, lambda p: (0, 0))
    st_spec = pl.BlockSpec((nb, 2, SSM_HEAD_DIM, SSM_STATE), lambda p: (0, p, 0, 0))
    return pl.pallas_call(
        _ssd_step_kernel,
        grid=(n_pairs,),
        in_specs=[
            pl.BlockSpec((nb, D_SSM), lambda p: (0, OFF_Z // D_SSM)),
            pl.BlockSpec((nb, D_SSM), lambda p: (0, OFF_XS // D_SSM)),
            pl.BlockSpec((nb, 512), lambda p: (0, OFF_BC // 512)),
            const((nb, LANES)),
            const((nb, CONV_DIM)), const((nb, CONV_DIM)), const((nb, CONV_DIM)),
            const((SSM_CONV, CONV_DIM)), const((1, CONV_DIM)),
            const((1, LANES)),
            const((1, D_SSM)), const((1, D_SSM)),
            const((LANES, D_SSM)), const((LANES, SSM_HEADS * LANES)),
            st_spec,
        ],
        out_specs=[const((nb, D_SSM)), st_spec],
        out_shape=[
            jax.ShapeDtypeStruct((nb, D_SSM), BF16),
            jax.ShapeDtypeStruct(state.shape, F32),
        ],
        scratch_shapes=[
            pltpu.VMEM((D_SSM, nb), F32),
            pltpu.VMEM((n_pairs, nb, 2 * LANES), F32),
            pltpu.VMEM((nb, D_SSM), F32),
            pltpu.VMEM((nb, 2 * SSM_GROUPS * SSM_STATE), F32),
            pltpu.VMEM((n_pairs, nb, 2 * SSM_HEAD_DIM), F32),
        ],
        compiler_params=_cp(("arbitrary",)),
        name="ssd_step",
    )(mix, mix, mix, dt, buf[:, 0], buf[:, 1], buf[:, 2], conv_w, conv_b, a_log,
      d_full, norm_w, expand, jnp.asarray(_head_expand(LANES), BF16), state)


def _ffn_prompt_kernel(oa_ref, ys_ref, x_ref, wo_ref, n2_ref, wg_ref, wv_ref, wd_ref, cw_ref, cb_ref,
                       fnw_ref, y_ref, tail_ref, ge_scr, *, tiles_per_seq):
    i = pl.program_id(0)
    tm = x_ref.shape[0]
    pad = SUBLANES
    seq_start = lax.rem(i, tiles_per_seq) == 0

    @pl.when(seq_start)
    def _():
        ge_scr[0:pad, :] = jnp.zeros((pad, D_FF), F32)

    @pl.when(jnp.logical_not(seq_start))
    def _():
        ge_scr[0:pad, :] = ge_scr[tm:tm + pad, :]

    n_sub = max(1, tm // FFN_SUB_ROWS)
    sub = tm // n_sub
    rs = [slice(s * sub, (s + 1) * sub) for s in range(n_sub)]
    x1 = [x_ref[r, :] + _dot(oa_ref[r, :], wo_ref[0:D_HGRN, :])
          + _dot(ys_ref[r, :], wo_ref[D_HGRN:D_HGRN + D_SSM, :]) for r in rs]
    h2 = [_rms(x, n2_ref[...]).astype(BF16) for x in x1]

    acc = [None] * n_sub
    bounds = np.cumsum((0,) + FFN_COL_BLOCKS)
    for c0, c1 in zip(bounds[:-1].tolist(), bounds[1:].tolist()):
        for s, r in enumerate(rs):
            gate = _dot(h2[s], wg_ref[:, c0:c1])
            val = _dot(h2[s], wv_ref[:, c0:c1])
            ge_scr[pad + r.start:pad + r.stop, c0:c1] = gate
            if s == n_sub - 1:
                tail_ref[0, :, c0:c1] = gate[sub - pad:, :]
            conv = (cb_ref[:, c0:c1] + cw_ref[2:3, c0:c1] * gate
                    + cw_ref[1:2, c0:c1] * ge_scr[pad - 1 + r.start:pad - 1 + r.stop, c0:c1]
                    + cw_ref[0:1, c0:c1] * ge_scr[pad - 2 + r.start:pad - 2 + r.stop, c0:c1])
            act = (_silu(conv) * val).astype(BF16)
            part = _dot(act, wd_ref[c0:c1, :])
            acc[s] = part if acc[s] is None else acc[s] + part
    for s, r in enumerate(rs):
        y_ref[r, :] = _rms(x1[s] + acc[s], fnw_ref[...])


FFN_ROW_TILE = 512
FFN_SUB_ROWS = 256
FFN_COL_BLOCKS = (1024, 1024, 768)
assert sum(FFN_COL_BLOCKS) == D_FF and all(c % LANES == 0 for c in FFN_COL_BLOCKS)


def _ffn_prompt(o_a, y_s, x2d, w_o, norm2_w, w_gate, w_val, w_down, conv_w, conv_b, fnorm_w, seq):
    n = x2d.shape[0]
    tm = FFN_ROW_TILE
    assert seq % tm == 0
    kern = functools.partial(_ffn_prompt_kernel, tiles_per_seq=seq // tm)
    row = lambda w: pl.BlockSpec((tm, w), lambda i: (i, 0))
    resident = lambda shape: pl.BlockSpec(shape, lambda i: (0, 0), pipeline_mode=pl.Buffered(1))
    return pl.pallas_call(
        kern,
        grid=(n // tm,),
        in_specs=[
            row(D_HGRN), row(D_SSM), row(D_MODEL),
            resident((D_HGRN + D_SSM, D_MODEL)), resident((1, D_MODEL)),
            resident((D_MODEL, D_FF)), resident((D_MODEL, D_FF)), resident((D_FF, D_MODEL)),
            resident((FFN_CONV, D_FF)), resident((1, D_FF)), resident((1, D_MODEL)),
        ],
        out_specs=[
            row(D_MODEL),
            pl.BlockSpec((1, SUBLANES, D_FF), lambda i: (i, 0, 0)),
        ],
        out_shape=[
            jax.ShapeDtypeStruct((n, D_MODEL), F32),
            jax.ShapeDtypeStruct((n // tm, SUBLANES, D_FF), F32),
        ],
        scratch_shapes=[pltpu.VMEM((tm + SUBLANES, D_FF), F32)],
        compiler_params=_cp(("arbitrary",)),
        name="ffn_prompt",
    )(o_a, y_s, x2d, w_o, norm2_w, w_gate, w_val, w_down, conv_w, conv_b, fnorm_w)


FF_CAST_BLOCK = 256


def _ffn_step_kernel(oa_ref, ys_ref, x_ref, wo_ref, n2_ref, wg_ref, wv_ref, wd_ref, cw_ref, cb_ref,
                     fnw_ref, b0_ref, b1_ref,
                     y_ref, gate_ref, wob_ref, wgb_ref, wvb_ref, wdb_ref, x1_scr, h2_scr, acc_scr):
    j = pl.program_id(0)

    @pl.when(j == 0)
    def _():
        wo = wo_ref[...].astype(BF16)
        wob_ref[...] = wo
        x1 = (x_ref[...] + _dot(oa_ref[...], wo[0:D_HGRN, :]) + _dot(ys_ref[...], wo[D_HGRN:, :]))
        x1_scr[...] = x1
        h2_scr[...] = _rms(x1, n2_ref[...]).astype(BF16)
        acc_scr[...] = jnp.zeros_like(acc_scr)

    wg = wg_ref[...].astype(BF16)
    wv = wv_ref[...].astype(BF16)
    wd = wd_ref[...].astype(BF16)
    wgb_ref[...] = wg
    wvb_ref[...] = wv
    wdb_ref[...] = wd
    h2 = h2_scr[...]
    gate = _dot(h2, wg)
    val = _dot(h2, wv)
    gate_ref[...] = gate
    conv = (cb_ref[...] + cw_ref[2:3, :] * gate + cw_ref[1:2, :] * b1_ref[...]
            + cw_ref[0:1, :] * b0_ref[...])
    act = (_silu(conv) * val).astype(BF16)
    acc_scr[...] = acc_scr[...] + _dot(act, wd)

    @pl.when(j == pl.num_programs(0) - 1)
    def _():
        y_ref[...] = _rms(x1_scr[...] + acc_scr[...], fnw_ref[...])


def _ffn_step(o_a, y_s, x2d, w_out, norm2_w, w_up, w_down, conv_w, conv_b, fnorm_w, buf):
    n = x2d.shape[0]
    blk = FF_CAST_BLOCK
    nj = D_FF // blk
    const = lambda shape: pl.BlockSpec(shape, lambda j: (0, 0))
    col = lambda rows: pl.BlockSpec((rows, blk), lambda j: (0, j))
    return pl.pallas_call(
        _ffn_step_kernel,
        grid=(nj,),
        in_specs=[
            const((n, D_HGRN)), const((n, D_SSM)), const((n, D_MODEL)),
            const((D_HGRN + D_SSM, D_MODEL)), const((1, D_MODEL)),
            col(D_MODEL), pl.BlockSpec((D_MODEL, blk), lambda j: (0, nj + j)),
            pl.BlockSpec((blk, D_MODEL), lambda j: (j, 0)),
            col(FFN_CONV), col(1), const((1, D_MODEL)), col(n), col(n),
        ],
        out_specs=[
            const((n, D_MODEL)), col(n),
            const((D_HGRN + D_SSM, D_MODEL)), col(D_MODEL), col(D_MODEL),
            pl.BlockSpec((blk, D_MODEL), lambda j: (j, 0)),
        ],
        out_shape=[
            jax.ShapeDtypeStruct((n, D_MODEL), F32),
            jax.ShapeDtypeStruct((n, D_FF), F32),
            jax.ShapeDtypeStruct((D_HGRN + D_SSM, D_MODEL), BF16),
            jax.ShapeDtypeStruct((D_MODEL, D_FF), BF16),
            jax.ShapeDtypeStruct((D_MODEL, D_FF), BF16),
            jax.ShapeDtypeStruct((D_FF, D_MODEL), BF16),
        ],
        scratch_shapes=[
            pltpu.VMEM((n, D_MODEL), F32),
            pltpu.VMEM((n, D_MODEL), BF16),
            pltpu.VMEM((n, D_MODEL), F32),
        ],
        compiler_params=_cp(("arbitrary",)),
        name="ffn_step",
    )(o_a, y_s, x2d, w_out, norm2_w, w_up, w_up, w_down, conv_w, conv_b, fnorm_w,
      buf[:, 0], buf[:, 1])


def _row(v):
    return v.reshape(1, -1).astype(F32)


def _pad_lanes(v):
    return jnp.pad(v.astype(F32), (0, LANES - v.shape[0])).reshape(1, LANES)


def kernel(x_prompt, x_sample, state_hgrn, state_ssm, state_conv_ssm, state_conv_ffn, norm1_w, w_in, hgrn_lb, hgrn_norm_w, ssm_conv_w, ssm_conv_b, ssm_dt_bias, ssm_a_log, ssm_d, ssm_norm_w, w_out, norm2_w, w_up, ffn_conv_w, ffn_conv_b, w_down, final_norm_w):
    depth = w_in.shape[0]
    assert depth == 1, "single-layer trunk"
    l = 0
    batch, seq, _ = x_prompt.shape
    dec_batch, dec_seq, _ = x_sample.shape
    assert dec_seq == 1 and seq % CHUNK == 0 and seq >= SSM_CONV

    w_in_t = w_in[l].T
    d_full = jnp.repeat(ssm_d[l].astype(F32), SSM_HEAD_DIM).reshape(1, D_SSM)
    dt_bias = _pad_lanes(ssm_dt_bias[l])
    a_log = _pad_lanes(ssm_a_log[l])
    mconst = jnp.asarray(_hgrn_const(), BF16)
    tri = jnp.asarray(np.tril(np.ones((CHUNK, CHUNK), np.float32)), BF16)
    expand = jnp.asarray(_head_expand(), BF16)
    lb_raw = hgrn_lb.astype(F32)

    xs_ = x_sample.reshape(dec_batch, D_MODEL)
    w_main, proj_s, lg_s, dt_s = _inproj_cast(xs_, _row(norm1_w[l]), w_in_t, lb_raw, dt_bias)
    oa_s, hgrn_s = _hgrn_step(proj_s, lg_s, _row(hgrn_norm_w[l]), state_hgrn[l])
    ys_s, ssm_s = _ssd_step(proj_s, dt_s, state_conv_ssm[l], ssm_conv_w[l], _row(ssm_conv_b[l]),
                            a_log, d_full, _row(ssm_norm_w[l]), expand, state_ssm[l])
    y_s, gate_s, w_ob, w_gb, w_vb, w_db = _ffn_step(
        oa_s, ys_s, xs_, w_out[l], _row(norm2_w[l]), w_up[l], w_down[l],
        ffn_conv_w[l], _row(ffn_conv_b[l]), _row(final_norm_w), state_conv_ffn[l])
    cs_s = jnp.concatenate([state_conv_ssm[l][:, 1:], proj_s[:, None, OFF_XS:OFF_XS + CONV_DIM]],
                           axis=1)
    cf_s = jnp.concatenate([state_conv_ffn[l][:, 1:], gate_s[:, None, :]], axis=1)

    xp = x_prompt.reshape(batch * seq, D_MODEL)
    proj_p, lg_p, dt_p = _inproj(xp, _row(norm1_w[l]), w_main, w_in_t, lb_raw, dt_bias)
    oa_p, hgrn_p = _hgrn_prompt(proj_p, lg_p, _row(hgrn_norm_w[l]), mconst, batch, seq)
    ys_p, ssm_p = _ssd_prompt(proj_p, dt_p, ssm_conv_w[l], _row(ssm_conv_b[l]), a_log,
                              d_full, _row(ssm_norm_w[l]), tri, expand, batch, seq)
    y_p, tail_p = _ffn_prompt(oa_p, ys_p, xp, w_ob, _row(norm2_w[l]), w_gb, w_vb, w_db,
                              ffn_conv_w[l], _row(ffn_conv_b[l]), _row(final_norm_w), seq)
    proj_p3 = proj_p.reshape(batch, seq, D_MAIN)
    cs_p = proj_p3[:, seq - (SSM_CONV - 1):, OFF_XS:OFF_XS + CONV_DIM]
    tails = tail_p.reshape(batch, seq // FFN_ROW_TILE, SUBLANES, D_FF)
    cf_p = tails[:, -1, SUBLANES - (FFN_CONV - 1):, :]

    dt_ = x_prompt.dtype
    return (y_p.reshape(batch, seq, D_MODEL).astype(dt_),
            y_s.reshape(dec_batch, 1, D_MODEL).astype(dt_),
            hgrn_p[None].astype(dt_),
            hgrn_s[None].astype(dt_),
            ssm_p.reshape(1, batch, SSM_HEADS, SSM_HEAD_DIM, SSM_STATE).astype(dt_),
            ssm_s[None].astype(dt_),
            cs_p[None].astype(dt_),
            cs_s[None].astype(dt_),
            cf_p[None].astype(dt_),
            cf_s[None].astype(dt_))
```

```python
import functools

import numpy as np
import jax
import jax.numpy as jnp
from jax import lax
from jax.experimental import pallas as pl
from jax.experimental.pallas import tpu as pltpu

F32 = jnp.float32
BF16 = jnp.bfloat16
EPS = 1e-6

LANES = 128
SUBLANES = 8

D_MODEL = 1024
HGRN_HEADS = 8
HGRN_DK = 128
HGRN_DV = 128
D_HGRN = HGRN_HEADS * HGRN_DV
SSM_HEADS = 16
SSM_HEAD_DIM = 64
D_SSM = SSM_HEADS * SSM_HEAD_DIM
SSM_STATE = 128
SSM_GROUPS = 2
SSM_CONV = 4
CONV_DIM = D_SSM + 2 * SSM_GROUPS * SSM_STATE
D_FF = 2816
FFN_CONV = 3
D_MAIN = 4 * D_HGRN + D_SSM + CONV_DIM
OFF_Q, OFF_F, OFF_I, OFF_G = 0, 1024, 2048, 3072
OFF_Z, OFF_XS, OFF_BC = 4096, 5120, 6144

CHUNK = 128
GROUP_W = D_SSM // SSM_GROUPS
VMEM_LIMIT = 56 * 1024 * 1024


def _cp(sem):
    return pltpu.CompilerParams(dimension_semantics=sem, vmem_limit_bytes=VMEM_LIMIT)


def _dot(a, b):
    return jnp.dot(a, b, preferred_element_type=F32)


def _dot_nt(a, b):
    return lax.dot_general(a, b, (((1,), (1,)), ((), ())), preferred_element_type=F32)


def _split3(x):
    h = x.astype(BF16)
    r = x - h.astype(F32)
    m = r.astype(BF16)
    lo = (r - m.astype(F32)).astype(BF16)
    return h, m, lo


def _dot_exact_lhs01(m01, x):
    h, m, lo = _split3(x)
    return _dot(m01, h) + _dot(m01, m) + _dot(m01, lo)


def _dot_exact_rhs01(x, m01):
    h, m, lo = _split3(x)
    return _dot(h, m01) + _dot(m, m01) + _dot(lo, m01)


def _dot_split_lhs01(m01, x):
    h = x.astype(BF16)
    lo = (x - h.astype(F32)).astype(BF16)
    return _dot(m01, h) + _dot(m01, lo)


def _sigmoid(x):
    return 1.0 / (1.0 + jnp.exp(-x))


def _silu(x):
    return x * _sigmoid(x)


def _rms(x, w):
    ms = jnp.mean(x * x, axis=-1, keepdims=True)
    return x * lax.rsqrt(ms + EPS) * w


def _dt_proj(hb, wdt_ref):
    rows = lax.broadcasted_iota(jnp.int32, wdt_ref.shape, 0)
    wdt = jnp.where(rows < SSM_HEADS, wdt_ref[...], 0.0).astype(BF16)
    return _dot_nt(hb, wdt)

def _inproj_kernel(x_ref, nw_ref, w_ref, wdt_ref, lb_ref, dtb_ref, mix_ref, lg_ref, dt_ref):
    n_sub = max(1, x_ref.shape[0] // INPROJ_SUB_ROWS)
    sub_rows = x_ref.shape[0] // n_sub
    rs = [slice(s * sub_rows, (s + 1) * sub_rows) for s in range(n_sub)]
    hb = [_rms(x_ref[r, :], nw_ref[...]).astype(BF16) for r in rs]

    def put(r, off, val):
        mix_ref[r, off:off + val.shape[1]] = val.astype(BF16)

    w = D_HGRN
    lb = _hgrn_lb(lb_ref[...])
    for s, r in enumerate(rs):
        qf = _dot(hb[s], w_ref[:, OFF_Q:OFF_Q + 2 * w])
        f = lb + (1.0 - lb) * _sigmoid(qf[:, w:])
        lg_ref[r, :] = jnp.log(f)
        put(r, OFF_F, 1.0 - f)
        put(r, OFF_Q, _silu(qf[:, :w]))
    for s, r in enumerate(rs):
        ig = _dot(hb[s], w_ref[:, OFF_I:OFF_I + 2 * w])
        put(r, OFF_I, ig[:, :w])
        put(r, OFF_G, _silu(ig[:, w:]))
    for s, r in enumerate(rs):
        zx = _dot(hb[s], w_ref[:, OFF_Z:OFF_Z + D_SSM + CONV_DIM])
        put(r, OFF_Z, _silu(zx[:, :D_SSM]))
        put(r, OFF_XS, zx[:, D_SSM:])
    for s, r in enumerate(rs):
        dt_ref[r, :] = _softplus(_dt_proj(hb[s], wdt_ref) + dtb_ref[...])


INPROJ_ROW_TILE = 512
INPROJ_SUB_ROWS = 256


def _inproj(x2d, norm_w, w_main, w_t, lb_raw, dt_bias):
    n = x2d.shape[0]
    tm = min(INPROJ_ROW_TILE, n)
    assert n % tm == 0
    row = lambda w: pl.BlockSpec((tm, w), lambda i: (i, 0))
    resident = lambda shape: pl.BlockSpec(shape, lambda i: (0, 0), pipeline_mode=pl.Buffered(1))
    return pl.pallas_call(
        _inproj_kernel,
        grid=(n // tm,),
        in_specs=[
            row(D_MODEL), resident((1, D_MODEL)),
            resident((D_MODEL, D_MAIN)),
            pl.BlockSpec((LANES, D_MODEL), lambda i: (D_MAIN // LANES, 0), pipeline_mode=pl.Buffered(1)),
            resident(lb_raw.shape), resident((1, LANES)),
        ],
        out_specs=[row(D_MAIN), row(D_HGRN), row(LANES)],
        out_shape=[
            jax.ShapeDtypeStruct((n, D_MAIN), BF16),
            jax.ShapeDtypeStruct((n, D_HGRN), F32),
            jax.ShapeDtypeStruct((n, LANES), F32),
        ],
        compiler_params=_cp(("arbitrary",)),
        name="inproj",
    )(x2d, norm_w, w_main, w_t, lb_raw, dt_bias)


CAST_BLOCK = 512


def _inproj_cast_kernel(x_ref, nw_ref, w_ref, wdt_ref, lb_ref, dtb_ref,
                        wb_ref, mix_ref, lg_ref, dt_ref, h_scr):
    j = pl.program_id(0)
    blk = CAST_BLOCK
    q0, f0, i0, g0, z0, x0 = (off // blk for off in (OFF_Q, OFF_F, OFF_I, OFF_G, OFF_Z, OFF_XS))

    @pl.when(j == 0)
    def _():
        hb = _rms(x_ref[...], nw_ref[...]).astype(BF16)
        h_scr[...] = hb
        dt_ref[...] = _softplus(_dt_proj(hb, wdt_ref) + dtb_ref[...])

    wb = w_ref[...].T.astype(BF16)
    wb_ref[...] = wb
    p = _dot(h_scr[...], wb)

    @pl.when(((j >= q0) & (j < f0)) | ((j >= g0) & (j < x0)))
    def _():
        mix_ref[...] = _silu(p).astype(BF16)

    @pl.when(((j >= i0) & (j < g0)) | (j >= x0))
    def _():
        mix_ref[...] = p.astype(BF16)

    for fj in range(f0, i0):
        @pl.when(j == fj)
        def _(fj=fj):
            cols = slice((fj - f0) * blk, (fj - f0 + 1) * blk)
            lb = _hgrn_lb(lb_ref[:, cols])
            f = lb + (1.0 - lb) * _sigmoid(p)
            lg_ref[:, cols] = jnp.log(f)
            mix_ref[...] = (1.0 - f).astype(BF16)


def _inproj_cast(x2d, norm_w, w_t, lb_raw, dt_bias):
    n = x2d.shape[0]
    blk = CAST_BLOCK
    const = lambda shape: pl.BlockSpec(shape, lambda j: (0, 0))
    return pl.pallas_call(
        _inproj_cast_kernel,
        grid=(D_MAIN // blk,),
        in_specs=[
            const((n, D_MODEL)), const((1, D_MODEL)),
            pl.BlockSpec((blk, D_MODEL), lambda j: (j, 0)),
            pl.BlockSpec((LANES, D_MODEL), lambda j: (D_MAIN // LANES, 0)),
            const(lb_raw.shape), const((1, LANES)),
        ],
        out_specs=[
            pl.BlockSpec((D_MODEL, blk), lambda j: (0, j)),
            pl.BlockSpec((n, blk), lambda j: (0, j)),
            const((n, D_HGRN)), const((n, LANES)),
        ],
        out_shape=[
            jax.ShapeDtypeStruct((D_MODEL, D_MAIN), BF16),
            jax.ShapeDtypeStruct((n, D_MAIN), BF16),
            jax.ShapeDtypeStruct((n, D_HGRN), F32),
            jax.ShapeDtypeStruct((n, LANES), F32),
        ],
        scratch_shapes=[pltpu.VMEM((n, D_MODEL), BF16)],
        compiler_params=_cp(("arbitrary",)),
        name="inproj_cast",
    )(x2d, norm_w, w_t, w_t, lb_raw, dt_bias)


LOG2E = 1.4426950408889634
N_LEVELS = 7
MXU_LEVEL_HALVES = (4, 2)


def _hgrn_const():
    c = CHUNK
    t = np.arange(c)[:, None]
    j = np.arange(c)[None, :]
    blocks = [(j <= t)]
    for h in MXU_LEVEL_HALVES:
        mid = (t // (2 * h)) * (2 * h) + h
        upper = (t >= mid) & (j >= mid) & (j <= t)
        lower = (t < mid) & (j > t) & (j < mid)
        blocks.append(upper | lower)
    return np.concatenate(blocks, axis=0).astype(np.float32)


def _midpoint_decay(b, h):
    pieces = []
    for start in range(0, CHUNK, 2 * h):
        mid = start + h
        m = b[mid - 1:mid, :]
        pieces.append(m - b[start:mid])
        pieces.append(b[mid:mid + h] - m)
    return jnp.concatenate(pieces, axis=0)


def _mix_rows(q, k, h):
    pieces = []
    for start in range(0, CHUNK, 2 * h):
        pieces.append(k[start:start + h])
        pieces.append(q[start + h:start + 2 * h])
    return jnp.concatenate(pieces, axis=0)


def _hgrn_lb(lb_raw):
    mx = jnp.max(lb_raw, axis=0, keepdims=True)
    e = jnp.exp(lb_raw - mx)
    return e[0:1, :] / jnp.sum(e, axis=0, keepdims=True)


def _level_map():
    t = lax.broadcasted_iota(jnp.int32, (CHUNK, CHUNK), 0)
    s = lax.broadcasted_iota(jnp.int32, (CHUNK, CHUNK), 1)
    bitlen = 32 - lax.clz(t ^ s)
    return jnp.where(t > s, bitlen, jnp.where(t == s, 0, -1))


def _hgrn_prompt_kernel(q_ref, k_ref, i_ref, g_ref, lg_ref, nw_ref, mc_ref,
                        o_ref, s_out_ref, st_scr):
    c = pl.program_id(1)

    @pl.when(c == 0)
    def _():
        st_scr[...] = jnp.zeros_like(st_scr)

    n_sub = q_ref.shape[0] // CHUNK
    lev = _level_map().astype(jnp.int16)
    row = lax.broadcasted_iota(jnp.int32, (CHUNK, HGRN_DK), 0)
    heads = range(HGRN_HEADS)
    pairs = [(s, h) for s in range(n_sub) for h in heads]
    rs = {s: slice(s * CHUNK, (s + 1) * CHUNK) for s in range(n_sub)}
    cs = {h: slice(h * HGRN_DK, (h + 1) * HGRN_DK) for h in heads}

    e_sub = {s: _dot_split_lhs01(mc_ref[...], lg_ref[rs[s], :] * LOG2E) for s in range(n_sub)}
    qb = {(s, h): q_ref[rs[s], cs[h]] for s, h in pairs}
    kb = {(s, h): k_ref[rs[s], cs[h]] for s, h in pairs}
    vb = {(s, h): i_ref[rs[s], cs[h]] for s, h in pairs}
    q = {p: qb[p].astype(F32) for p in pairs}
    k = {p: kb[p].astype(F32) for p in pairs}
    b = {(s, h): e_sub[s][0:CHUNK, cs[h]] for s, h in pairs}
    b_last = {p: b[p][CHUNK - 1:CHUNK, :] for p in pairs}

    st = {h: st_scr[h] for h in heads}
    o = {}
    for s, h in pairs:
        p = (s, h)
        o[p] = _dot((q[p] * jnp.exp2(b[p])).astype(BF16), st[h].T.astype(BF16))
        ks = (k[p] * jnp.exp2(b_last[p] - b[p])).astype(BF16)
        st[h] = st[h] * jnp.exp2(b_last[p]) + _dot(vb[p].astype(F32).T.astype(BF16), ks)
    for h in heads:
        st_scr[h] = st[h]

    a = {p: jnp.where(lev == 0, _dot(qb[p], k[p].T.astype(BF16)).astype(BF16), jnp.zeros((), BF16))
         for p in pairs}
    half = CHUNK // 2
    while half >= 1:
        for p in pairs:
            if half >= SUBLANES:
                x = _mix_rows(q[p], k[p], half) * jnp.exp2(_midpoint_decay(b[p], half))
            else:
                upper = (row & half) != 0
                if half in MXU_LEVEL_HALVES:
                    blk = 1 + MXU_LEVEL_HALVES.index(half)
                    w = jnp.exp2(e_sub[p[0]][blk * CHUNK:(blk + 1) * CHUNK, cs[p[1]]])
                    x = jnp.where(upper, q[p], k[p]) * w
                else:
                    x = jnp.where(upper, q[p] * (1.0 - k[p]), k[p])
            gram = _dot(x.astype(BF16), x.T.astype(BF16))
            a[p] = jnp.where(lev == half.bit_length(), gram.astype(BF16), a[p])
        half //= 2

    for p in pairs:
        o[p] = o[p] + _dot(a[p], vb[p])
    for s, h in pairs:
        gate = g_ref[rs[s], cs[h]].astype(F32)
        o_ref[rs[s], cs[h]] = (_rms(o[(s, h)], nw_ref[...]) * gate).astype(BF16)

    @pl.when(c == pl.num_programs(1) - 1)
    def _():
        for h in range(HGRN_HEADS):
            s_out_ref[0, h] = st_scr[h].T


HGRN_SUBCHUNKS = 4


def _hgrn_prompt(mix, lg, norm_w, mconst, batch, seq):
    rows = HGRN_SUBCHUNKS * CHUNK
    assert seq % rows == 0
    nc = seq // rows

    def col(off):
        return pl.BlockSpec((rows, D_HGRN), lambda b, c: (b * nc + c, off // D_HGRN))

    return pl.pallas_call(
        _hgrn_prompt_kernel,
        grid=(batch, nc),
        in_specs=[
            col(OFF_Q), col(OFF_F), col(OFF_I), col(OFF_G), col(0),
            pl.BlockSpec((1, HGRN_DV), lambda b, c: (0, 0)),
            pl.BlockSpec(mconst.shape, lambda b, c: (0, 0)),
        ],
        out_specs=[
            pl.BlockSpec((rows, D_HGRN), lambda b, c: (b * nc + c, 0)),
            pl.BlockSpec((1, HGRN_HEADS, HGRN_DK, HGRN_DV), lambda b, c: (b, 0, 0, 0)),
        ],
        out_shape=[
            jax.ShapeDtypeStruct((batch * seq, D_HGRN), BF16),
            jax.ShapeDtypeStruct((batch, HGRN_HEADS, HGRN_DK, HGRN_DV), F32),
        ],
        scratch_shapes=[pltpu.VMEM((HGRN_HEADS, HGRN_DV, HGRN_DK), F32)],
        compiler_params=_cp(("arbitrary", "arbitrary")),
        name="hgrn_prompt",
    )(mix, mix, mix, mix, lg, norm_w, mconst)


def _hgrn_step_kernel(q_ref, i_ref, g_ref, lg_ref, nw_ref, s_ref,
                      o_ref, s_out_ref, o_scr):
    nb = q_ref.shape[0]
    qb = q_ref[...]
    f_t = jnp.exp(lg_ref[...]).T
    v = i_ref[...].astype(F32)
    lhs_rows = 2 * SUBLANES
    for t in range(nb):
        v_row = v[t:t + 1, :]
        s_new = v_row + f_t[:, t:t + 1] * (s_ref[t, 0] - v_row)
        s_out_ref[t, 0] = s_new
        q_rows = jnp.broadcast_to(qb[t:t + 1, :], (lhs_rows, HGRN_DK))
        o_scr[t:t + 1, :] = _dot(q_rows, s_new.astype(BF16))[0:1, :]
    o_ref[...] = (_rms(o_scr[...], nw_ref[...]) * g_ref[...].astype(F32)).astype(BF16)


def _hgrn_step(mix, lg, norm_w, state):
    nb = state.shape[0]
    hb = lambda off: off // HGRN_DK

    def col(off):
        return pl.BlockSpec((nb, HGRN_DK), lambda h: (0, hb(off) + h))

    st_spec = pl.BlockSpec((nb, 1, HGRN_DK, HGRN_DV), lambda h: (0, h, 0, 0))
    return pl.pallas_call(
        _hgrn_step_kernel,
        grid=(HGRN_HEADS,),
        in_specs=[
            col(OFF_Q), col(OFF_I), col(OFF_G), col(0),
            pl.BlockSpec((1, HGRN_DV), lambda h: (0, 0)),
            st_spec,
        ],
        out_specs=[pl.BlockSpec((nb, HGRN_DV), lambda h: (0, h)), st_spec],
        out_shape=[
            jax.ShapeDtypeStruct((nb, D_HGRN), BF16),
            jax.ShapeDtypeStruct(state.shape, F32),
        ],
        scratch_shapes=[pltpu.VMEM((nb, HGRN_DV), F32)],
        compiler_params=_cp(("arbitrary",)),
        name="hgrn_step",
    )(mix, mix, mix, lg, norm_w, state)


def _head_expand(width=SSM_HEAD_DIM):
    e = np.zeros((LANES, SSM_HEADS * width), np.float32)
    for h in range(SSM_HEADS):
        e[h, h * width:(h + 1) * width] = 1.0
    return e


def _softplus(x):
    return jnp.maximum(x, 0.0) + jnp.log(1.0 + jnp.exp(-jnp.abs(x)))


def _ssm_gate_norm(y, z_gate, nw):
    y = y * z_gate.astype(F32)
    parts = [_rms(y[:, g * GROUP_W:(g + 1) * GROUP_W], nw[:, g * GROUP_W:(g + 1) * GROUP_W])
             for g in range(SSM_GROUPS)]
    return jnp.concatenate(parts, axis=-1)


def _ssd_prompt_kernel(z_ref, xs_ref, bc_ref, dt_ref, cw_ref, cb_ref, alog_ref,
                       dvec_ref, nw_ref, tri_ref, exp_ref,
                       shift_ref, y_ref, st_out_ref, xprev_scr, st_scr):
    c = pl.program_id(1)
    t = CHUNK

    @pl.when(c == 0)
    def _():
        st_scr[...] = jnp.zeros_like(st_scr)
        xprev_scr[...] = jnp.zeros_like(xprev_scr)

    subs = range(xs_ref.shape[0] // t)
    rs = [slice(s * t, (s + 1) * t) for s in subs]

    x_cur = [jnp.concatenate([xs_ref[r, :], bc_ref[r, :]], axis=-1) for r in rs]
    x_prev = [xprev_scr[...]] + x_cur[:-1]
    xprev_scr[...] = x_cur[-1]
    taps = [_dot(shift_ref[...], jnp.concatenate([x_prev[s], x_cur[s]], axis=0)) for s in subs]
    xbc = []
    for s in subs:
        acc = cb_ref[...] + cw_ref[SSM_CONV - 1:SSM_CONV, :] * x_cur[s].astype(F32)
        for d in range(1, SSM_CONV):
            acc = acc + cw_ref[SSM_CONV - 1 - d:SSM_CONV - d, :] * taps[s][(d - 1) * t:d * t, :]
        xbc.append(_silu(acc))
    xs = [x[:, 0:D_SSM] for x in xbc]

    dt = [dt_ref[r, :] for r in rs]
    neg_a = -LOG2E * jnp.exp(alog_ref[...])
    cs = [_dot_exact_lhs01(tri_ref[...], dt[s] * neg_a) for s in subs]
    ex = exp_ref[...]
    dt_full = [_dot_exact_rhs01(dt[s], ex) for s in subs]
    cs_full = [_dot_exact_rhs01(cs[s], ex) for s in subs]
    cs_last_full = [x[t - 1:t, :] for x in cs_full]
    x_dt = [xs[s] * dt_full[s] for s in subs]
    x_end = [(x_dt[s] * jnp.exp2(cs_last_full[s] - cs_full[s])).astype(BF16) for s in subs]
    cs_t = [x.T for x in cs]

    causal = (lax.broadcasted_iota(jnp.int32, (t, t), 0)
              >= lax.broadcasted_iota(jnp.int32, (t, t), 1))
    lane = lax.broadcasted_iota(jnp.int32, (1, D_SSM), 1)
    odd_head = (lane & SSM_HEAD_DIM) != 0
    x_by_parity = []
    for s in subs:
        x_b = x_dt[s].astype(BF16)
        zero = jnp.zeros_like(x_b)
        x_by_parity.append((jnp.where(odd_head, zero, x_b), jnp.where(odd_head, x_b, zero)))
    heads_per_group = SSM_HEADS // SSM_GROUPS
    pair_w = 2 * SSM_HEAD_DIM
    never = -1e30

    def group_bc(s, g):
        b_g = xbc[s][:, D_SSM + g * SSM_STATE:D_SSM + (g + 1) * SSM_STATE]
        c_off = D_SSM + SSM_GROUPS * SSM_STATE + g * SSM_STATE
        return b_g.T.astype(BF16), xbc[s][:, c_off:c_off + SSM_STATE].astype(BF16)

    bc_t = {(s, g): group_bc(s, g) for s in subs for g in range(SSM_GROUPS)}

    y_diag = []
    for s in subs:
        y_parts = []
        for g in range(SSM_GROUPS):
            b_t, c_g = bc_t[(s, g)]
            gmat = _dot(c_g, b_t)
            for pp in range(heads_per_group // 2):
                h0 = g * heads_per_group + 2 * pp
                psl = slice(h0 * SSM_HEAD_DIM, h0 * SSM_HEAD_DIM + pair_w)
                yp = None
                for sub in range(2):
                    h = h0 + sub
                    diff = cs[s][:, h:h + 1] - cs_t[s][h:h + 1, :]
                    w = jnp.exp2(jnp.where(causal, diff, never)) * gmat
                    part = _dot(w.astype(BF16), x_by_parity[s][sub][:, psl])
                    yp = part if yp is None else yp + part
                y_parts.append(yp)
        y_diag.append(jnp.concatenate(y_parts, axis=-1))

    y_off = []
    for s in subs:
        offs = []
        for g in range(SSM_GROUPS):
            sl = slice(g * GROUP_W, (g + 1) * GROUP_W)
            b_t, c_g = bc_t[(s, g)]
            st_g = st_scr[:, sl]
            offs.append(_dot(c_g, st_g.astype(BF16)))
            st_scr[:, sl] = st_g * jnp.exp2(cs_last_full[s][:, sl]) + _dot(b_t, x_end[s][:, sl])
        y_off.append(jnp.concatenate(offs, axis=-1) * jnp.exp2(cs_full[s]))

    for s in subs:
        y = y_diag[s] + y_off[s] + dvec_ref[...] * xs[s]
        y_ref[rs[s], :] = _ssm_gate_norm(y, z_ref[rs[s], :], nw_ref[...]).astype(BF16)

    @pl.when(c == pl.num_programs(1) - 1)
    def _():
        for j in range(D_SSM // LANES):
            st_out_ref[0, j * LANES:(j + 1) * LANES, :] = st_scr[:, j * LANES:(j + 1) * LANES].T


SSD_SUBCHUNKS = 1


def _ssd_prompt(mix, dt, conv_w, conv_b, a_log, d_full, norm_w, tri, expand, batch, seq):
    rows = SSD_SUBCHUNKS * CHUNK
    assert seq % rows == 0
    nc = seq // rows
    const = lambda shape: pl.BlockSpec(shape, lambda b, c: (0, 0))
    return pl.pallas_call(
        _ssd_prompt_kernel,
        grid=(batch, nc),
        in_specs=[
            pl.BlockSpec((rows, D_SSM), lambda b, c: (b * nc + c, OFF_Z // D_SSM)),
            pl.BlockSpec((rows, D_SSM), lambda b, c: (b * nc + c, OFF_XS // D_SSM)),
            pl.BlockSpec((rows, 512), lambda b, c: (b * nc + c, OFF_BC // 512)),
            pl.BlockSpec((rows, LANES), lambda b, c: (b * nc + c, 0)),
            const((SSM_CONV, CONV_DIM)), const((1, CONV_DIM)),
            const((1, LANES)),
            const((1, D_SSM)), const((1, D_SSM)),
            const((CHUNK, CHUNK)), const((LANES, D_SSM)),
            const(((SSM_CONV - 1) * CHUNK, 2 * CHUNK)),
        ],
        out_specs=[
            pl.BlockSpec((rows, D_SSM), lambda b, c: (b * nc + c, 0)),
            pl.BlockSpec((1, D_SSM, SSM_STATE), lambda b, c: (b, 0, 0)),
        ],
        out_shape=[
            jax.ShapeDtypeStruct((batch * seq, D_SSM), BF16),
            jax.ShapeDtypeStruct((batch, D_SSM, SSM_STATE), F32),
        ],
        scratch_shapes=[
            pltpu.VMEM((CHUNK, CONV_DIM), BF16),
            pltpu.VMEM((SSM_STATE, D_SSM), F32),
        ],
        compiler_params=_cp(("arbitrary", "arbitrary")),
        name="ssd_prompt",
    )(mix, mix, mix, dt, conv_w, conv_b, a_log, d_full, norm_w, tri, expand,
      jnp.asarray(_conv_shifts(), BF16))


def _conv_shifts():
    m = np.zeros(((SSM_CONV - 1) * CHUNK, 2 * CHUNK), np.float32)
    for d in range(1, SSM_CONV):
        for t in range(CHUNK):
            m[(d - 1) * CHUNK + t, CHUNK + t - d] = 1.0
    return m


def _ssd_step_kernel(z_ref, xs_ref, bc_ref, dt_ref, b0_ref, b1_ref, b2_ref, cw_ref, cb_ref,
                     alog_ref, dvec_ref, nw_ref, exp_ref, exl_ref, st_ref,
                     y_ref, st_out_ref, xt_scr, at_scr, xs_scr, bc_scr, y_scr):
    p = pl.program_id(0)
    nb = z_ref.shape[0]
    pair_w = 2 * SSM_HEAD_DIM
    pairs_per_group = SSM_HEADS // SSM_GROUPS // 2

    @pl.when(p == 0)
    def _():
        x_new = jnp.concatenate([xs_ref[...], bc_ref[...]], axis=-1).astype(F32)
        acc = (cb_ref[...] + cw_ref[0:1, :] * b0_ref[...] + cw_ref[1:2, :] * b1_ref[...]
               + cw_ref[2:3, :] * b2_ref[...] + cw_ref[3:4, :] * x_new)
        xbc = _silu(acc)
        xs = xbc[:, 0:D_SSM]
        dt = dt_ref[...]
        da = dt * (-jnp.exp(alog_ref[...]))
        ex = exp_ref[...]
        x_dt = xs * _dot_exact_rhs01(dt, ex)
        decay = jnp.exp(_dot_exact_rhs01(da, exl_ref[...]))
        xs_scr[...] = xs
        bc_scr[...] = xbc[:, D_SSM:]
        for j in range(D_SSM // LANES):
            sl = slice(j * LANES, (j + 1) * LANES)
            xt_scr[sl, :] = x_dt[:, sl].T
            at_scr[j] = decay[:, 2 * j * LANES:2 * (j + 1) * LANES]

    g_is_1 = p >= pairs_per_group
    row0 = pl.multiple_of(p * pair_w, pair_w)
    x_t = xt_scr[pl.ds(row0, pair_w), :]
    a_p = at_scr[p]
    bc = bc_scr[...]
    b_all = jnp.where(g_is_1, bc[:, SSM_STATE:2 * SSM_STATE], bc[:, 0:SSM_STATE])
    c_all = jnp.where(g_is_1, bc[:, 3 * SSM_STATE:4 * SSM_STATE],
                      bc[:, 2 * SSM_STATE:3 * SSM_STATE]).astype(BF16)
    for t in range(nb):
        inject = x_t[:, t:t + 1] * b_all[t:t + 1, :]
        halves = []
        for sub in range(2):
            rows = slice(sub * SSM_HEAD_DIM, (sub + 1) * SSM_HEAD_DIM)
            half = a_p[t:t + 1, sub * LANES:(sub + 1) * LANES] * st_ref[t, sub] + inject[rows]
            st_out_ref[t, sub] = half
            halves.append(half)
        new = jnp.concatenate(halves, axis=0)
        c_rows = jnp.broadcast_to(c_all[t:t + 1, :], (SUBLANES, SSM_STATE))
        y_scr[p, t:t + 1, :] = _dot_nt(c_rows, new.astype(BF16))[0:1, :]

    @pl.when(p == pl.num_programs(0) - 1)
    def _():
        y_mix = jnp.concatenate([y_scr[j] for j in range(SSM_HEADS // 2)], axis=-1)
        y = y_mix + dvec_ref[...] * xs_scr[...]
        y_ref[...] = _ssm_gate_norm(y, z_ref[...], nw_ref[...]).astype(BF16)


def _ssd_step(mix, dt, buf, conv_w, conv_b, a_log, d_full, norm_w, expand, state):
    nb = state.shape[0]
    n_pairs = SSM_HEADS // 2
    const = lambda shape: pl.BlockSpec(shape, lambda p: (0, 0))
    st_spec = pl.BlockSpec((nb, 2, SSM_HEAD_DIM, SSM_STATE), lambda p: (0, p, 0, 0))
    return pl.pallas_call(
        _ssd_step_kernel,
        grid=(n_pairs,),
        in_specs=[
            pl.BlockSpec((nb, D_SSM), lambda p: (0, OFF_Z // D_SSM)),
            pl.BlockSpec((nb, D_SSM), lambda p: (0, OFF_XS // D_SSM)),
            pl.BlockSpec((nb, 512), lambda p: (0, OFF_BC // 512)),
            const((nb, LANES)),
            const((nb, CONV_DIM)), const((nb, CONV_DIM)), const((nb, CONV_DIM)),
            const((SSM_CONV, CONV_DIM)), const((1, CONV_DIM)),
            const((1, LANES)),
            const((1, D_SSM)), const((1, D_SSM)),
            const((LANES, D_SSM)), const((LANES, SSM_HEADS * LANES)),
            st_spec,
        ],
        out_specs=[const((nb, D_SSM)), st_spec],
        out_shape=[
            jax.ShapeDtypeStruct((nb, D_SSM), BF16),
            jax.ShapeDtypeStruct(state.shape, F32),
        ],
        scratch_shapes=[
            pltpu.VMEM((D_SSM, nb), F32),
            pltpu.VMEM((n_pairs, nb, 2 * LANES), F32),
            pltpu.VMEM((nb, D_SSM), F32),
            pltpu.VMEM((nb, 2 * SSM_GROUPS * SSM_STATE), F32),
            pltpu.VMEM((n_pairs, nb, 2 * SSM_HEAD_DIM), F32),
        ],
        compiler_params=_cp(("arbitrary",)),
        name="ssd_step",
    )(mix, mix, mix, dt, buf[:, 0], buf[:, 1], buf[:, 2], conv_w, conv_b, a_log,
      d_full, norm_w, expand, jnp.asarray(_head_expand(LANES), BF16), state)


def _ffn_prompt_kernel(oa_ref, ys_ref, x_ref, wo_ref, n2_ref, wg_ref, wv_ref, wd_ref, cw_ref, cb_ref,
                       fnw_ref, y_ref, tail_ref, ge_scr, *, tiles_per_seq):
    i = pl.program_id(0)
    tm = x_ref.shape[0]
    pad = SUBLANES
    seq_start = lax.rem(i, tiles_per_seq) == 0

    @pl.when(seq_start)
    def _():
        ge_scr[0:pad, :] = jnp.zeros((pad, D_FF), F32)

    @pl.when(jnp.logical_not(seq_start))
    def _():
        ge_scr[0:pad, :] = ge_scr[tm:tm + pad, :]

    n_sub = max(1, tm // FFN_SUB_ROWS)
    sub = tm // n_sub
    rs = [slice(s * sub, (s + 1) * sub) for s in range(n_sub)]
    x1 = [x_ref[r, :] + _dot(oa_ref[r, :], wo_ref[0:D_HGRN, :])
          + _dot(ys_ref[r, :], wo_ref[D_HGRN:D_HGRN + D_SSM, :]) for r in rs]
    h2 = [_rms(x, n2_ref[...]).astype(BF16) for x in x1]

    acc = [None] * n_sub
    bounds = np.cumsum((0,) + FFN_COL_BLOCKS)
    for c0, c1 in zip(bounds[:-1].tolist(), bounds[1:].tolist()):
        for s, r in enumerate(rs):
            gate = _dot(h2[s], wg_ref[:, c0:c1])
            val = _dot(h2[s], wv_ref[:, c0:c1])
            ge_scr[pad + r.start:pad + r.stop, c0:c1] = gate
            if s == n_sub - 1:
                tail_ref[0, :, c0:c1] = gate[sub - pad:, :]
            conv = (cb_ref[:, c0:c1] + cw_ref[2:3, c0:c1] * gate
                    + cw_ref[1:2, c0:c1] * ge_scr[pad - 1 + r.start:pad - 1 + r.stop, c0:c1]
                    + cw_ref[0:1, c0:c1] * ge_scr[pad - 2 + r.start:pad - 2 + r.stop, c0:c1])
            act = (_silu(conv) * val).astype(BF16)
            part = _dot(act, wd_ref[c0:c1, :])
            acc[s] = part if acc[s] is None else acc[s] + part
    for s, r in enumerate(rs):
        y_ref[r, :] = _rms(x1[s] + acc[s], fnw_ref[...])


FFN_ROW_TILE = 512
FFN_SUB_ROWS = 256
FFN_COL_BLOCKS = (1024, 1024, 768)
assert sum(FFN_COL_BLOCKS) == D_FF and all(c % LANES == 0 for c in FFN_COL_BLOCKS)


def _ffn_prompt(o_a, y_s, x2d, w_o, norm2_w, w_gate, w_val, w_down, conv_w, conv_b, fnorm_w, seq):
    n = x2d.shape[0]
    tm = FFN_ROW_TILE
    assert seq % tm == 0
    kern = functools.partial(_ffn_prompt_kernel, tiles_per_seq=seq // tm)
    row = lambda w: pl.BlockSpec((tm, w), lambda i: (i, 0))
    resident = lambda shape: pl.BlockSpec(shape, lambda i: (0, 0), pipeline_mode=pl.Buffered(1))
    return pl.pallas_call(
        kern,
        grid=(n // tm,),
        in_specs=[
            row(D_HGRN), row(D_SSM), row(D_MODEL),
            resident((D_HGRN + D_SSM, D_MODEL)), resident((1, D_MODEL)),
            resident((D_MODEL, D_FF)), resident((D_MODEL, D_FF)), resident((D_FF, D_MODEL)),
            resident((FFN_CONV, D_FF)), resident((1, D_FF)), resident((1, D_MODEL)),
        ],
        out_specs=[
            row(D_MODEL),
            pl.BlockSpec((1, SUBLANES, D_FF), lambda i: (i, 0, 0)),
        ],
        out_shape=[
            jax.ShapeDtypeStruct((n, D_MODEL), F32),
            jax.ShapeDtypeStruct((n // tm, SUBLANES, D_FF), F32),
        ],
        scratch_shapes=[pltpu.VMEM((tm + SUBLANES, D_FF), F32)],
        compiler_params=_cp(("arbitrary",)),
        name="ffn_prompt",
    )(o_a, y_s, x2d, w_o, norm2_w, w_gate, w_val, w_down, conv_w, conv_b, fnorm_w)


FF_CAST_BLOCK = 256


def _ffn_step_kernel(oa_ref, ys_ref, x_ref, wo_ref, n2_ref, wg_ref, wv_ref, wd_ref, cw_ref, cb_ref,
                     fnw_ref, b0_ref, b1_ref,
                     y_ref, gate_ref, wob_ref, wgb_ref, wvb_ref, wdb_ref, x1_scr, h2_scr, acc_scr):
    j = pl.program_id(0)

    @pl.when(j == 0)
    def _():
        wo = wo_ref[...].astype(BF16)
        wob_ref[...] = wo
        x1 = (x_ref[...] + _dot(oa_ref[...], wo[0:D_HGRN, :]) + _dot(ys_ref[...], wo[D_HGRN:, :]))
        x1_scr[...] = x1
        h2_scr[...] = _rms(x1, n2_ref[...]).astype(BF16)
        acc_scr[...] = jnp.zeros_like(acc_scr)

    wg = wg_ref[...].astype(BF16)
    wv = wv_ref[...].astype(BF16)
    wd = wd_ref[...].astype(BF16)
    wgb_ref[...] = wg
    wvb_ref[...] = wv
    wdb_ref[...] = wd
    h2 = h2_scr[...]
    gate = _dot(h2, wg)
    val = _dot(h2, wv)
    gate_ref[...] = gate
    conv = (cb_ref[...] + cw_ref[2:3, :] * gate + cw_ref[1:2, :] * b1_ref[...]
            + cw_ref[0:1, :] * b0_ref[...])
    act = (_silu(conv) * val).astype(BF16)
    acc_scr[...] = acc_scr[...] + _dot(act, wd)

    @pl.when(j == pl.num_programs(0) - 1)
    def _():
        y_ref[...] = _rms(x1_scr[...] + acc_scr[...], fnw_ref[...])


def _ffn_step(o_a, y_s, x2d, w_out, norm2_w, w_up, w_down, conv_w, conv_b, fnorm_w, buf):
    n = x2d.shape[0]
    blk = FF_CAST_BLOCK
    nj = D_FF // blk
    const = lambda shape: pl.BlockSpec(shape, lambda j: (0, 0))
    col = lambda rows: pl.BlockSpec((rows, blk), lambda j: (0, j))
    return pl.pallas_call(
        _ffn_step_kernel,
        grid=(nj,),
        in_specs=[
            const((n, D_HGRN)), const((n, D_SSM)), const((n, D_MODEL)),
            const((D_HGRN + D_SSM, D_MODEL)), const((1, D_MODEL)),
            col(D_MODEL), pl.BlockSpec((D_MODEL, blk), lambda j: (0, nj + j)),
            pl.BlockSpec((blk, D_MODEL), lambda j: (j, 0)),
            col(FFN_CONV), col(1), const((1, D_MODEL)), col(n), col(n),
        ],
        out_specs=[
            const((n, D_MODEL)), col(n),
            const((D_HGRN + D_SSM, D_MODEL)), col(D_MODEL), col(D_MODEL),
            pl.BlockSpec((blk, D_MODEL), lambda j: (j, 0)),
        ],
        out_shape=[
            jax.ShapeDtypeStruct((n, D_MODEL), F32),
            jax.ShapeDtypeStruct((n, D_FF), F32),
            jax.ShapeDtypeStruct((D_HGRN + D_SSM, D_MODEL), BF16),
            jax.ShapeDtypeStruct((D_MODEL, D_FF), BF16),
            jax.ShapeDtypeStruct((D_MODEL, D_FF), BF16),
            jax.ShapeDtypeStruct((D_FF, D_MODEL), BF16),
        ],
        scratch_shapes=[
            pltpu.VMEM((n, D_MODEL), F32),
            pltpu.VMEM((n, D_MODEL), BF16),
            pltpu.VMEM((n, D_MODEL), F32),
        ],
        compiler_params=_cp(("arbitrary",)),
        name="ffn_step",
    )(o_a, y_s, x2d, w_out, norm2_w, w_up, w_up, w_down, conv_w, conv_b, fnorm_w,
      buf[:, 0], buf[:, 1])


def _row(v):
    return v.reshape(1, -1).astype(F32)


def _pad_lanes(v):
    return jnp.pad(v.astype(F32), (0, LANES - v.shape[0])).reshape(1, LANES)


def kernel(x_prompt, x_sample, state_hgrn, state_ssm, state_conv_ssm, state_conv_ffn, norm1_w, w_in, hgrn_lb, hgrn_norm_w, ssm_conv_w, ssm_conv_b, ssm_dt_bias, ssm_a_log, ssm_d, ssm_norm_w, w_out, norm2_w, w_up, ffn_conv_w, ffn_conv_b, w_down, final_norm_w):
    depth = w_in.shape[0]
    assert depth == 1, "single-layer trunk"
    l = 0
    batch, seq, _ = x_prompt.shape
    dec_batch, dec_seq, _ = x_sample.shape
    assert dec_seq == 1 and seq % CHUNK == 0 and seq >= SSM_CONV

    w_in_t = w_in[l].T
    d_full = jnp.repeat(ssm_d[l].astype(F32), SSM_HEAD_DIM).reshape(1, D_SSM)
    dt_bias = _pad_lanes(ssm_dt_bias[l])
    a_log = _pad_lanes(ssm_a_log[l])
    mconst = jnp.asarray(_hgrn_const(), BF16)
    tri = jnp.asarray(np.tril(np.ones((CHUNK, CHUNK), np.float32)), BF16)
    expand = jnp.asarray(_head_expand(), BF16)
    lb_raw = hgrn_lb.astype(F32)

    xs_ = x_sample.reshape(dec_batch, D_MODEL)
    w_main, proj_s, lg_s, dt_s = _inproj_cast(xs_, _row(norm1_w[l]), w_in_t, lb_raw, dt_bias)
    oa_s, hgrn_s = _hgrn_step(proj_s, lg_s, _row(hgrn_norm_w[l]), state_hgrn[l])
    ys_s, ssm_s = _ssd_step(proj_s, dt_s, state_conv_ssm[l], ssm_conv_w[l], _row(ssm_conv_b[l]),
                            a_log, d_full, _row(ssm_norm_w[l]), expand, state_ssm[l])
    y_s, gate_s, w_ob, w_gb, w_vb, w_db = _ffn_step(
        oa_s, ys_s, xs_, w_out[l], _row(norm2_w[l]), w_up[l], w_down[l],
        ffn_conv_w[l], _row(ffn_conv_b[l]), _row(final_norm_w), state_conv_ffn[l])
    cs_s = jnp.concatenate([state_conv_ssm[l][:, 1:], proj_s[:, None, OFF_XS:OFF_XS + CONV_DIM]],
                           axis=1)
    cf_s = jnp.concatenate([state_conv_ffn[l][:, 1:], gate_s[:, None, :]], axis=1)

    xp = x_prompt.reshape(batch * seq, D_MODEL)
    proj_p, lg_p, dt_p = _inproj(xp, _row(norm1_w[l]), w_main, w_in_t, lb_raw, dt_bias)
    oa_p, hgrn_p = _hgrn_prompt(proj_p, lg_p, _row(hgrn_norm_w[l]), mconst, batch, seq)
    ys_p, ssm_p = _ssd_prompt(proj_p, dt_p, ssm_conv_w[l], _row(ssm_conv_b[l]), a_log,
                              d_full, _row(ssm_norm_w[l]), tri, expand, batch, seq)
    y_p, tail_p = _ffn_prompt(oa_p, ys_p, xp, w_ob, _row(norm2_w[l]), w_gb, w_vb, w_db,
                              ffn_conv_w[l], _row(ffn_conv_b[l]), _row(final_norm_w), seq)
    proj_p3 = proj_p.reshape(batch, seq, D_MAIN)
    cs_p = proj_p3[:, seq - (SSM_CONV - 1):, OFF_XS:OFF_XS + CONV_DIM]
    tails = tail_p.reshape(batch, seq // FFN_ROW_TILE, SUBLANES, D_FF)
    cf_p = tails[:, -1, SUBLANES - (FFN_CONV - 1):, :]

    dt_ = x_prompt.dtype
    return (y_p.reshape(batch, seq, D_MODEL).astype(dt_),
            y_s.reshape(dec_batch, 1, D_MODEL).astype(dt_),
            hgrn_p[None].astype(dt_),
            hgrn_s[None].astype(dt_),
            ssm_p.reshape(1, batch, SSM_HEADS, SSM_HEAD_DIM, SSM_STATE).astype(dt_),
            ssm_s[None].astype(dt_),
            cs_p[None].astype(dt_),
            cs_s[None].astype(dt_),
            cf_p[None].astype(dt_),
            cf_s[None].astype(dt_))
```

```python
import functools

import numpy as np
import jax
import jax.numpy as jnp
from jax import lax
from jax.experimental import pallas as pl
from jax.experimental.pallas import tpu as pltpu

F32 = jnp.float32
BF16 = jnp.bfloat16
EPS = 1e-6

LANES = 128
SUBLANES = 8

D_MODEL = 1024
HGRN_HEADS = 8
HGRN_DK = 128
HGRN_DV = 128
D_HGRN = HGRN_HEADS * HGRN_DV
SSM_HEADS = 16
SSM_HEAD_DIM = 64
D_SSM = SSM_HEADS * SSM_HEAD_DIM
SSM_STATE = 128
SSM_GROUPS = 2
SSM_CONV = 4
CONV_DIM = D_SSM + 2 * SSM_GROUPS * SSM_STATE
D_FF = 2816
FFN_CONV = 3
D_MAIN = 4 * D_HGRN + D_SSM + CONV_DIM
OFF_Q, OFF_F, OFF_I, OFF_G = 0, 1024, 2048, 3072
OFF_Z, OFF_XS, OFF_BC = 4096, 5120, 6144

CHUNK = 128
GROUP_W = D_SSM // SSM_GROUPS
VMEM_LIMIT = 56 * 1024 * 1024


def _cp(sem):
    return pltpu.CompilerParams(dimension_semantics=sem, vmem_limit_bytes=VMEM_LIMIT)


def _dot(a, b):
    return jnp.dot(a, b, preferred_element_type=F32)


def _dot_nt(a, b):
    return lax.dot_general(a, b, (((1,), (1,)), ((), ())), preferred_element_type=F32)


def _split3(x):
    h = x.astype(BF16)
    r = x - h.astype(F32)
    m = r.astype(BF16)
    lo = (r - m.astype(F32)).astype(BF16)
    return h, m, lo


def _dot_exact_lhs01(m01, x):
    h, m, lo = _split3(x)
    return _dot(m01, h) + _dot(m01, m) + _dot(m01, lo)


def _dot_exact_rhs01(x, m01):
    h, m, lo = _split3(x)
    return _dot(h, m01) + _dot(m, m01) + _dot(lo, m01)


def _dot_split_lhs01(m01, x):
    h = x.astype(BF16)
    lo = (x - h.astype(F32)).astype(BF16)
    return _dot(m01, h) + _dot(m01, lo)


def _sigmoid(x):
    return 1.0 / (1.0 + jnp.exp(-x))


def _silu(x):
    return x * _sigmoid(x)


def _rms(x, w):
    ms = jnp.mean(x * x, axis=-1, keepdims=True)
    return x * lax.rsqrt(ms + EPS) * w


def _dt_proj(hb, wdt_ref):
    rows = lax.broadcasted_iota(jnp.int32, wdt_ref.shape, 0)
    wdt = jnp.where(rows < SSM_HEADS, wdt_ref[...], 0.0).astype(BF16)
    return _dot_nt(hb, wdt)

def _inproj_kernel(x_ref, nw_ref, w_ref, wdt_ref, lb_ref, dtb_ref, mix_ref, lg_ref, dt_ref):
    n_sub = max(1, x_ref.shape[0] // INPROJ_SUB_ROWS)
    sub_rows = x_ref.shape[0] // n_sub
    rs = [slice(s * sub_rows, (s + 1) * sub_rows) for s in range(n_sub)]
    hb = [_rms(x_ref[r, :], nw_ref[...]).astype(BF16) for r in rs]

    def put(r, off, val):
        mix_ref[r, off:off + val.shape[1]] = val.astype(BF16)

    w = D_HGRN
    lb = _hgrn_lb(lb_ref[...])
    for s, r in enumerate(rs):
        qf = _dot(hb[s], w_ref[:, OFF_Q:OFF_Q + 2 * w])
        f = lb + (1.0 - lb) * _sigmoid(qf[:, w:])
        lg_ref[r, :] = jnp.log(f)
        put(r, OFF_F, 1.0 - f)
        put(r, OFF_Q, _silu(qf[:, :w]))
    for s, r in enumerate(rs):
        ig = _dot(hb[s], w_ref[:, OFF_I:OFF_I + 2 * w])
        put(r, OFF_I, ig[:, :w])
        put(r, OFF_G, _silu(ig[:, w:]))
    for s, r in enumerate(rs):
        zx = _dot(hb[s], w_ref[:, OFF_Z:OFF_Z + D_SSM + CONV_DIM])
        put(r, OFF_Z, _silu(zx[:, :D_SSM]))
        put(r, OFF_XS, zx[:, D_SSM:])
    for s, r in enumerate(rs):
        dt_ref[r, :] = _softplus(_dt_proj(hb[s], wdt_ref) + dtb_ref[...])


INPROJ_ROW_TILE = 512
INPROJ_SUB_ROWS = 256


def _inproj(x2d, norm_w, w_main, w_t, lb_raw, dt_bias):
    n = x2d.shape[0]
    tm = min(INPROJ_ROW_TILE, n)
    assert n % tm == 0
    row = lambda w: pl.BlockSpec((tm, w), lambda i: (i, 0))
    resident = lambda shape: pl.BlockSpec(shape, lambda i: (0, 0), pipeline_mode=pl.Buffered(1))
    return pl.pallas_call(
        _inproj_kernel,
        grid=(n // tm,),
        in_specs=[
            row(D_MODEL), resident((1, D_MODEL)),
            resident((D_MODEL, D_MAIN)),
            pl.BlockSpec((LANES, D_MODEL), lambda i: (D_MAIN // LANES, 0), pipeline_mode=pl.Buffered(1)),
            resident(lb_raw.shape), resident((1, LANES)),
        ],
        out_specs=[row(D_MAIN), row(D_HGRN), row(LANES)],
        out_shape=[
            jax.ShapeDtypeStruct((n, D_MAIN), BF16),
            jax.ShapeDtypeStruct((n, D_HGRN), F32),
            jax.ShapeDtypeStruct((n, LANES), F32),
        ],
        compiler_params=_cp(("arbitrary",)),
        name="inproj",
    )(x2d, norm_w, w_main, w_t, lb_raw, dt_bias)


CAST_BLOCK = 512


def _inproj_cast_kernel(x_ref, nw_ref, w_ref, wdt_ref, lb_ref, dtb_ref,
                        wb_ref, mix_ref, lg_ref, dt_ref, h_scr):
    j = pl.program_id(0)
    blk = CAST_BLOCK
    q0, f0, i0, g0, z0, x0 = (off // blk for off in (OFF_Q, OFF_F, OFF_I, OFF_G, OFF_Z, OFF_XS))

    @pl.when(j == 0)
    def _():
        hb = _rms(x_ref[...], nw_ref[...]).astype(BF16)
        h_scr[...] = hb
        dt_ref[...] = _softplus(_dt_proj(hb, wdt_ref) + dtb_ref[...])

    wb = w_ref[...].T.astype(BF16)
    wb_ref[...] = wb
    p = _dot(h_scr[...], wb)

    @pl.when(((j >= q0) & (j < f0)) | ((j >= g0) & (j < x0)))
    def _():
        mix_ref[...] = _silu(p).astype(BF16)

    @pl.when(((j >= i0) & (j < g0)) | (j >= x0))
    def _():
        mix_ref[...] = p.astype(BF16)

    for fj in range(f0, i0):
        @pl.when(j == fj)
        def _(fj=fj):
            cols = slice((fj - f0) * blk, (fj - f0 + 1) * blk)
            lb = _hgrn_lb(lb_ref[:, cols])
            f = lb + (1.0 - lb) * _sigmoid(p)
            lg_ref[:, cols] = jnp.log(f)
            mix_ref[...] = (1.0 - f).astype(BF16)


def _inproj_cast(x2d, norm_w, w_t, lb_raw, dt_bias):
    n = x2d.shape[0]
    blk = CAST_BLOCK
    const = lambda shape: pl.BlockSpec(shape, lambda j: (0, 0))
    return pl.pallas_call(
        _inproj_cast_kernel,
        grid=(D_MAIN // blk,),
        in_specs=[
            const((n, D_MODEL)), const((1, D_MODEL)),
            pl.BlockSpec((blk, D_MODEL), lambda j: (j, 0)),
            pl.BlockSpec((LANES, D_MODEL), lambda j: (D_MAIN // LANES, 0)),
            const(lb_raw.shape), const((1, LANES)),
        ],
        out_specs=[
            pl.BlockSpec((D_MODEL, blk), lambda j: (0, j)),
            pl.BlockSpec((n, blk), lambda j: (0, j)),
            const((n, D_HGRN)), const((n, LANES)),
        ],
        out_shape=[
            jax.ShapeDtypeStruct((D_MODEL, D_MAIN), BF16),
            jax.ShapeDtypeStruct((n, D_MAIN), BF16),
            jax.ShapeDtypeStruct((n, D_HGRN), F32),
            jax.ShapeDtypeStruct((n, LANES), F32),
        ],
        scratch_shapes=[pltpu.VMEM((n, D_MODEL), BF16)],
        compiler_params=_cp(("arbitrary",)),
        name="inproj_cast",
    )(x2d, norm_w, w_t, w_t, lb_raw, dt_bias)


LOG2E = 1.4426950408889634
MXU_LEVEL_HALVES = (4, 2)


def _hgrn_const():
    c = CHUNK
    t = np.arange(c)[:, None]
    j = np.arange(c)[None, :]
    blocks = [(j <= t)]
    for h in MXU_LEVEL_HALVES:
        mid = (t // (2 * h)) * (2 * h) + h
        upper = (t >= mid) & (j >= mid) & (j <= t)
        lower = (t < mid) & (j > t) & (j < mid)
        blocks.append(upper | lower)
    return np.concatenate(blocks, axis=0).astype(np.float32)


def _midpoint_decay(b, h):
    pieces = []
    for start in range(0, CHUNK, 2 * h):
        mid = start + h
        m = b[mid - 1:mid, :]
        pieces.append(m - b[start:mid])
        pieces.append(b[mid:mid + h] - m)
    return jnp.concatenate(pieces, axis=0)


def _mix_rows(q, k, h):
    pieces = []
    for start in range(0, CHUNK, 2 * h):
        pieces.append(k[start:start + h])
        pieces.append(q[start + h:start + 2 * h])
    return jnp.concatenate(pieces, axis=0)


def _hgrn_lb(lb_raw):
    mx = jnp.max(lb_raw, axis=0, keepdims=True)
    e = jnp.exp(lb_raw - mx)
    return e[0:1, :] / jnp.sum(e, axis=0, keepdims=True)


def _level_map():
    t = lax.broadcasted_iota(jnp.int32, (CHUNK, CHUNK), 0)
    s = lax.broadcasted_iota(jnp.int32, (CHUNK, CHUNK), 1)
    bitlen = 32 - lax.clz(t ^ s)
    return jnp.where(t > s, bitlen, jnp.where(t == s, 0, -1))


def _hgrn_prompt_kernel(q_ref, k_ref, i_ref, g_ref, lg_ref, nw_ref, mc_ref,
                        o_ref, s_out_ref, st_scr):
    c = pl.program_id(1)

    @pl.when(c == 0)
    def _():
        st_scr[...] = jnp.zeros_like(st_scr)

    n_sub = q_ref.shape[0] // CHUNK
    lev = _level_map().astype(jnp.int16)
    row = lax.broadcasted_iota(jnp.int32, (CHUNK, HGRN_DK), 0)
    heads = range(HGRN_HEADS)
    pairs = [(s, h) for s in range(n_sub) for h in heads]
    rs = {s: slice(s * CHUNK, (s + 1) * CHUNK) for s in range(n_sub)}
    cs = {h: slice(h * HGRN_DK, (h + 1) * HGRN_DK) for h in heads}

    e_sub = {s: _dot_split_lhs01(mc_ref[...], lg_ref[rs[s], :] * LOG2E) for s in range(n_sub)}
    qb = {(s, h): q_ref[rs[s], cs[h]] for s, h in pairs}
    kb = {(s, h): k_ref[rs[s], cs[h]] for s, h in pairs}
    vb = {(s, h): i_ref[rs[s], cs[h]] for s, h in pairs}
    q = {p: qb[p].astype(F32) for p in pairs}
    k = {p: kb[p].astype(F32) for p in pairs}
    b = {(s, h): e_sub[s][0:CHUNK, cs[h]] for s, h in pairs}
    b_last = {p: b[p][CHUNK - 1:CHUNK, :] for p in pairs}

    st = {h: st_scr[h] for h in heads}
    o = {}
    for s, h in pairs:
        p = (s, h)
        o[p] = _dot((q[p] * jnp.exp2(b[p])).astype(BF16), st[h].T.astype(BF16))
        ks = (k[p] * jnp.exp2(b_last[p] - b[p])).astype(BF16)
        st[h] = st[h] * jnp.exp2(b_last[p]) + _dot(vb[p].astype(F32).T.astype(BF16), ks)
    for h in heads:
        st_scr[h] = st[h]

    a = {p: jnp.where(lev == 0, _dot(qb[p], k[p].T.astype(BF16)).astype(BF16), jnp.zeros((), BF16))
         for p in pairs}
    half = CHUNK // 2
    while half >= 1:
        for p in pairs:
            if half >= SUBLANES:
                x = _mix_rows(q[p], k[p], half) * jnp.exp2(_midpoint_decay(b[p], half))
            else:
                upper = (row & half) != 0
                if half in MXU_LEVEL_HALVES:
                    blk = 1 + MXU_LEVEL_HALVES.index(half)
                    w = jnp.exp2(e_sub[p[0]][blk * CHUNK:(blk + 1) * CHUNK, cs[p[1]]])
                    x = jnp.where(upper, q[p], k[p]) * w
                else:
                    x = jnp.where(upper, q[p] * (1.0 - k[p]), k[p])
            gram = _dot(x.astype(BF16), x.T.astype(BF16))
            a[p] = jnp.where(lev == half.bit_length(), gram.astype(BF16), a[p])
        half //= 2

    for p in pairs:
        o[p] = o[p] + _dot(a[p], vb[p])
    for s, h in pairs:
        gate = g_ref[rs[s], cs[h]].astype(F32)
        o_ref[rs[s], cs[h]] = (_rms(o[(s, h)], nw_ref[...]) * gate).astype(BF16)

    @pl.when(c == pl.num_programs(1) - 1)
    def _():
        for h in range(HGRN_HEADS):
            s_out_ref[0, h] = st_scr[h].T


HGRN_SUBCHUNKS = 4


def _hgrn_prompt(mix, lg, norm_w, mconst, batch, seq):
    rows = HGRN_SUBCHUNKS * CHUNK
    assert seq % rows == 0
    nc = seq // rows

    def col(off):
        return pl.BlockSpec((rows, D_HGRN), lambda b, c: (b * nc + c, off // D_HGRN))

    return pl.pallas_call(
        _hgrn_prompt_kernel,
        grid=(batch, nc),
        in_specs=[
            col(OFF_Q), col(OFF_F), col(OFF_I), col(OFF_G), col(0),
            pl.BlockSpec((1, HGRN_DV), lambda b, c: (0, 0)),
            pl.BlockSpec(mconst.shape, lambda b, c: (0, 0)),
        ],
        out_specs=[
            pl.BlockSpec((rows, D_HGRN), lambda b, c: (b * nc + c, 0)),
            pl.BlockSpec((1, HGRN_HEADS, HGRN_DK, HGRN_DV), lambda b, c: (b, 0, 0, 0)),
        ],
        out_shape=[
            jax.ShapeDtypeStruct((batch * seq, D_HGRN), BF16),
            jax.ShapeDtypeStruct((batch, HGRN_HEADS, HGRN_DK, HGRN_DV), F32),
        ],
        scratch_shapes=[pltpu.VMEM((HGRN_HEADS, HGRN_DV, HGRN_DK), F32)],
        compiler_params=_cp(("arbitrary", "arbitrary")),
        name="hgrn_prompt",
    )(mix, mix, mix, mix, lg, norm_w, mconst)


def _hgrn_step_kernel(q_ref, i_ref, g_ref, lg_ref, nw_ref, s_ref,
                      o_ref, s_out_ref, o_scr):
    nb = q_ref.shape[0]
    qb = q_ref[...]
    f_t = jnp.exp(lg_ref[...]).T
    v = i_ref[...].astype(F32)
    lhs_rows = 2 * SUBLANES
    for t in range(nb):
        v_row = v[t:t + 1, :]
        s_new = v_row + f_t[:, t:t + 1] * (s_ref[t, 0] - v_row)
        s_out_ref[t, 0] = s_new
        q_rows = jnp.broadcast_to(qb[t:t + 1, :], (lhs_rows, HGRN_DK))
        o_scr[t:t + 1, :] = _dot(q_rows, s_new.astype(BF16))[0:1, :]
    o_ref[...] = (_rms(o_scr[...], nw_ref[...]) * g_ref[...].astype(F32)).astype(BF16)


def _hgrn_step(mix, lg, norm_w, state):
    nb = state.shape[0]
    hb = lambda off: off // HGRN_DK

    def col(off):
        return pl.BlockSpec((nb, HGRN_DK), lambda h: (0, hb(off) + h))

    st_spec = pl.BlockSpec((nb, 1, HGRN_DK, HGRN_DV), lambda h: (0, h, 0, 0))
    return pl.pallas_call(
        _hgrn_step_kernel,
        grid=(HGRN_HEADS,),
        in_specs=[
            col(OFF_Q), col(OFF_I), col(OFF_G), col(0),
            pl.BlockSpec((1, HGRN_DV), lambda h: (0, 0)),
            st_spec,
        ],
        out_specs=[pl.BlockSpec((nb, HGRN_DV), lambda h: (0, h)), st_spec],
        out_shape=[
            jax.ShapeDtypeStruct((nb, D_HGRN), BF16),
            jax.ShapeDtypeStruct(state.shape, F32),
        ],
        scratch_shapes=[pltpu.VMEM((nb, HGRN_DV), F32)],
        compiler_params=_cp(("arbitrary",)),
        name="hgrn_step",
    )(mix, mix, mix, lg, norm_w, state)


def _head_expand(width=SSM_HEAD_DIM):
    e = np.zeros((LANES, SSM_HEADS * width), np.float32)
    for h in range(SSM_HEADS):
        e[h, h * width:(h + 1) * width] = 1.0
    return e


def _softplus(x):
    return jnp.maximum(x, 0.0) + jnp.log(1.0 + jnp.exp(-jnp.abs(x)))


def _ssm_gate_norm(y, z_gate, nw):
    y = y * z_gate.astype(F32)
    parts = [_rms(y[:, g * GROUP_W:(g + 1) * GROUP_W], nw[:, g * GROUP_W:(g + 1) * GROUP_W])
             for g in range(SSM_GROUPS)]
    return jnp.concatenate(parts, axis=-1)


def _ssd_prompt_kernel(z_ref, xs_ref, bc_ref, dt_ref, cw_ref, cb_ref, alog_ref,
                       dvec_ref, nw_ref, tri_ref, exp_ref,
                       shift_ref, y_ref, st_out_ref, xprev_scr, st_scr):
    c = pl.program_id(1)
    t = CHUNK

    @pl.when(c == 0)
    def _():
        st_scr[...] = jnp.zeros_like(st_scr)
        xprev_scr[...] = jnp.zeros_like(xprev_scr)

    subs = range(xs_ref.shape[0] // t)
    rs = [slice(s * t, (s + 1) * t) for s in subs]

    x_cur = [jnp.concatenate([xs_ref[r, :], bc_ref[r, :]], axis=-1) for r in rs]
    x_prev = [xprev_scr[...]] + x_cur[:-1]
    xprev_scr[...] = x_cur[-1]
    taps = [_dot(shift_ref[...], jnp.concatenate([x_prev[s], x_cur[s]], axis=0)) for s in subs]
    xbc = []
    for s in subs:
        acc = cb_ref[...] + cw_ref[SSM_CONV - 1:SSM_CONV, :] * x_cur[s].astype(F32)
        for d in range(1, SSM_CONV):
            acc = acc + cw_ref[SSM_CONV - 1 - d:SSM_CONV - d, :] * taps[s][(d - 1) * t:d * t, :]
        xbc.append(_silu(acc))
    xs = [x[:, 0:D_SSM] for x in xbc]

    dt = [dt_ref[r, :] for r in rs]
    neg_a = -LOG2E * jnp.exp(alog_ref[...])
    cs = [_dot_exact_lhs01(tri_ref[...], dt[s] * neg_a) for s in subs]
    ex = exp_ref[...]
    dt_full = [_dot_exact_rhs01(dt[s], ex) for s in subs]
    cs_full = [_dot_exact_rhs01(cs[s], ex) for s in subs]
    cs_last_full = [x[t - 1:t, :] for x in cs_full]
    x_dt = [xs[s] * dt_full[s] for s in subs]
    x_end = [(x_dt[s] * jnp.exp2(cs_last_full[s] - cs_full[s])).astype(BF16) for s in subs]
    cs_t = [x.T for x in cs]

    causal = (lax.broadcasted_iota(jnp.int32, (t, t), 0)
              >= lax.broadcasted_iota(jnp.int32, (t, t), 1))
    lane = lax.broadcasted_iota(jnp.int32, (1, D_SSM), 1)
    odd_head = (lane & SSM_HEAD_DIM) != 0
    x_by_parity = []
    for s in subs:
        x_b = x_dt[s].astype(BF16)
        zero = jnp.zeros_like(x_b)
        x_by_parity.append((jnp.where(odd_head, zero, x_b), jnp.where(odd_head, x_b, zero)))
    heads_per_group = SSM_HEADS // SSM_GROUPS
    pair_w = 2 * SSM_HEAD_DIM
    never = -1e30

    def group_bc(s, g):
        b_g = xbc[s][:, D_SSM + g * SSM_STATE:D_SSM + (g + 1) * SSM_STATE]
        c_off = D_SSM + SSM_GROUPS * SSM_STATE + g * SSM_STATE
        return b_g.T.astype(BF16), xbc[s][:, c_off:c_off + SSM_STATE].astype(BF16)

    bc_t = {(s, g): group_bc(s, g) for s in subs for g in range(SSM_GROUPS)}

    y_diag = []
    for s in subs:
        y_parts = []
        for g in range(SSM_GROUPS):
            b_t, c_g = bc_t[(s, g)]
            gmat = _dot(c_g, b_t)
            for pp in range(heads_per_group // 2):
                h0 = g * heads_per_group + 2 * pp
                psl = slice(h0 * SSM_HEAD_DIM, h0 * SSM_HEAD_DIM + pair_w)
                yp = None
                for sub in range(2):
                    h = h0 + sub
                    diff = cs[s][:, h:h + 1] - cs_t[s][h:h + 1, :]
                    w = jnp.exp2(jnp.where(causal, diff, never)) * gmat
                    part = _dot(w.astype(BF16), x_by_parity[s][sub][:, psl])
                    yp = part if yp is None else yp + part
                y_parts.append(yp)
        y_diag.append(jnp.concatenate(y_parts, axis=-1))

    y_off = []
    for s in subs:
        offs = []
        for g in range(SSM_GROUPS):
            sl = slice(g * GROUP_W, (g + 1) * GROUP_W)
            b_t, c_g = bc_t[(s, g)]
            st_g = st_scr[:, sl]
            offs.append(_dot(c_g, st_g.astype(BF16)))
            st_scr[:, sl] = st_g * jnp.exp2(cs_last_full[s][:, sl]) + _dot(b_t, x_end[s][:, sl])
        y_off.append(jnp.concatenate(offs, axis=-1) * jnp.exp2(cs_full[s]))

    for s in subs:
        y = y_diag[s] + y_off[s] + dvec_ref[...] * xs[s]
        y_ref[rs[s], :] = _ssm_gate_norm(y, z_ref[rs[s], :], nw_ref[...]).astype(BF16)

    @pl.when(c == pl.num_programs(1) - 1)
    def _():
        for j in range(D_SSM // LANES):
            st_out_ref[0, j * LANES:(j + 1) * LANES, :] = st_scr[:, j * LANES:(j + 1) * LANES].T


SSD_SUBCHUNKS = 1


def _ssd_prompt(mix, dt, conv_w, conv_b, a_log, d_full, norm_w, tri, expand, batch, seq):
    rows = SSD_SUBCHUNKS * CHUNK
    assert seq % rows == 0
    nc = seq // rows
    const = lambda shape: pl.BlockSpec(shape, lambda b, c: (0, 0))
    return pl.pallas_call(
        _ssd_prompt_kernel,
        grid=(batch, nc),
        in_specs=[
            pl.BlockSpec((rows, D_SSM), lambda b, c: (b * nc + c, OFF_Z // D_SSM)),
            pl.BlockSpec((rows, D_SSM), lambda b, c: (b * nc + c, OFF_XS // D_SSM)),
            pl.BlockSpec((rows, 512), lambda b, c: (b * nc + c, OFF_BC // 512)),
            pl.BlockSpec((rows, LANES), lambda b, c: (b * nc + c, 0)),
            const((SSM_CONV, CONV_DIM)), const((1, CONV_DIM)),
            const((1, LANES)),
            const((1, D_SSM)), const((1, D_SSM)),
            const((CHUNK, CHUNK)), const((LANES, D_SSM)),
            const(((SSM_CONV - 1) * CHUNK, 2 * CHUNK)),
        ],
        out_specs=[
            pl.BlockSpec((rows, D_SSM), lambda b, c: (b * nc + c, 0)),
            pl.BlockSpec((1, D_SSM, SSM_STATE), lambda b, c: (b, 0, 0)),
        ],
        out_shape=[
            jax.ShapeDtypeStruct((batch * seq, D_SSM), BF16),
            jax.ShapeDtypeStruct((batch, D_SSM, SSM_STATE), F32),
        ],
        scratch_shapes=[
            pltpu.VMEM((CHUNK, CONV_DIM), BF16),
            pltpu.VMEM((SSM_STATE, D_SSM), F32),
        ],
        compiler_params=_cp(("arbitrary", "arbitrary")),
        name="ssd_prompt",
    )(mix, mix, mix, dt, conv_w, conv_b, a_log, d_full, norm_w, tri, expand,
      jnp.asarray(_conv_shifts(), BF16))


def _conv_shifts():
    m = np.zeros(((SSM_CONV - 1) * CHUNK, 2 * CHUNK), np.float32)
    for d in range(1, SSM_CONV):
        for t in range(CHUNK):
            m[(d - 1) * CHUNK + t, CHUNK + t - d] = 1.0
    return m


def _ssd_step_kernel(z_ref, xs_ref, bc_ref, dt_ref, b0_ref, b1_ref, b2_ref, cw_ref, cb_ref,
                     alog_ref, dvec_ref, nw_ref, exp_ref, exl_ref, st_ref,
                     y_ref, st_out_ref, xt_scr, at_scr, xs_scr, bc_scr, y_scr):
    p = pl.program_id(0)
    nb = z_ref.shape[0]
    pair_w = 2 * SSM_HEAD_DIM
    pairs_per_group = SSM_HEADS // SSM_GROUPS // 2

    @pl.when(p == 0)
    def _():
        x_new = jnp.concatenate([xs_ref[...], bc_ref[...]], axis=-1).astype(F32)
        acc = (cb_ref[...] + cw_ref[0:1, :] * b0_ref[...] + cw_ref[1:2, :] * b1_ref[...]
               + cw_ref[2:3, :] * b2_ref[...] + cw_ref[3:4, :] * x_new)
        xbc = _silu(acc)
        xs = xbc[:, 0:D_SSM]
        dt = dt_ref[...]
        da = dt * (-jnp.exp(alog_ref[...]))
        ex = exp_ref[...]
        x_dt = xs * _dot_exact_rhs01(dt, ex)
        decay = jnp.exp(_dot_exact_rhs01(da, exl_ref[...]))
        xs_scr[...] = xs
        bc_scr[...] = xbc[:, D_SSM:]
        for j in range(D_SSM // LANES):
            sl = slice(j * LANES, (j + 1) * LANES)
            xt_scr[sl, :] = x_dt[:, sl].T
            at_scr[j] = decay[:, 2 * j * LANES:2 * (j + 1) * LANES]

    g_is_1 = p >= pairs_per_group
    row0 = pl.multiple_of(p * pair_w, pair_w)
    x_t = xt_scr[pl.ds(row0, pair_w), :]
    a_p = at_scr[p]
    bc = bc_scr[...]
    b_all = jnp.where(g_is_1, bc[:, SSM_STATE:2 * SSM_STATE], bc[:, 0:SSM_STATE])
    c_all = jnp.where(g_is_1, bc[:, 3 * SSM_STATE:4 * SSM_STATE],
                      bc[:, 2 * SSM_STATE:3 * SSM_STATE]).astype(BF16)
    for t in range(nb):
        inject = x_t[:, t:t + 1] * b_all[t:t + 1, :]
        halves = []
        for sub in range(2):
            rows = slice(sub * SSM_HEAD_DIM, (sub + 1) * SSM_HEAD_DIM)
            half = a_p[t:t + 1, sub * LANES:(sub + 1) * LANES] * st_ref[t, sub] + inject[rows]
            st_out_ref[t, sub] = half
            halves.append(half)
        new = jnp.concatenate(halves, axis=0)
        c_rows = jnp.broadcast_to(c_all[t:t + 1, :], (SUBLANES, SSM_STATE))
        y_scr[p, t:t + 1, :] = _dot_nt(c_rows, new.astype(BF16))[0:1, :]

    @pl.when(p == pl.num_programs(0) - 1)
    def _():
        y_mix = jnp.concatenate([y_scr[j] for j in range(SSM_HEADS // 2)], axis=-1)
        y = y_mix + dvec_ref[...] * xs_scr[...]
        y_ref[...] = _ssm_gate_norm(y, z_ref[...], nw_ref[...]).astype(BF16)


def _ssd_step(mix, dt, buf, conv_w, conv_b, a_log, d_full, norm_w, expand, state):
    nb = state.shape[0]
    n_pairs = SSM_HEADS // 2
    const = lambda shape: pl.BlockSpec(shape, lambda p: (0, 0))
    st_spec = pl.BlockSpec((nb, 2, SSM_HEAD_DIM, SSM_STATE), lambda p: (0, p, 0, 0))
    return pl.pallas_call(
        _ssd_step_kernel,
        grid=(n_pairs,),
        in_specs=[
            pl.BlockSpec((nb, D_SSM), lambda p: (0, OFF_Z // D_SSM)),
            pl.BlockSpec((nb, D_SSM), lambda p: (0, OFF_XS // D_SSM)),
            pl.BlockSpec((nb, 512), lambda p: (0, OFF_BC // 512)),
            const((nb, LANES)),
            const((nb, CONV_DIM)), const((nb, CONV_DIM)), const((nb, CONV_DIM)),
            const((SSM_CONV, CONV_DIM)), const((1, CONV_DIM)),
            const((1, LANES)),
            const((1, D_SSM)), const((1, D_SSM)),
            const((LANES, D_SSM)), const((LANES, SSM_HEADS * LANES)),
            st_spec,
        ],
        out_specs=[const((nb, D_SSM)), st_spec],
        out_shape=[
            jax.ShapeDtypeStruct((nb, D_SSM), BF16),
            jax.ShapeDtypeStruct(state.shape, F32),
        ],
        scratch_shapes=[
            pltpu.VMEM((D_SSM, nb), F32),
            pltpu.VMEM((n_pairs, nb, 2 * LANES), F32),
            pltpu.VMEM((nb, D_SSM), F32),
            pltpu.VMEM((nb, 2 * SSM_GROUPS * SSM_STATE), F32),
            pltpu.VMEM((n_pairs, nb, 2 * SSM_HEAD_DIM), F32),
        ],
        compiler_params=_cp(("arbitrary",)),
        name="ssd_step",
    )(mix, mix, mix, dt, buf[:, 0], buf[:, 1], buf[:, 2], conv_w, conv_b, a_log,
      d_full, norm_w, expand, jnp.asarray(_head_expand(LANES), BF16), state)


def _ffn_prompt_kernel(oa_ref, ys_ref, x_ref, wo_ref, n2_ref, wg_ref, wv_ref, wd_ref, cw_ref, cb_ref,
                       fnw_ref, y_ref, tail_ref, ge_scr, *, tiles_per_seq):
    i = pl.program_id(0)
    tm = x_ref.shape[0]
    pad = SUBLANES
    seq_start = lax.rem(i, tiles_per_seq) == 0

    @pl.when(seq_start)
    def _():
        ge_scr[0:pad, :] = jnp.zeros((pad, D_FF), F32)

    @pl.when(jnp.logical_not(seq_start))
    def _():
        ge_scr[0:pad, :] = ge_scr[tm:tm + pad, :]

    n_sub = max(1, tm // FFN_SUB_ROWS)
    sub = tm // n_sub
    rs = [slice(s * sub, (s + 1) * sub) for s in range(n_sub)]
    x1 = [x_ref[r, :] + _dot(oa_ref[r, :], wo_ref[0:D_HGRN, :])
          + _dot(ys_ref[r, :], wo_ref[D_HGRN:D_HGRN + D_SSM, :]) for r in rs]
    h2 = [_rms(x, n2_ref[...]).astype(BF16) for x in x1]

    acc = [None] * n_sub
    bounds = np.cumsum((0,) + FFN_COL_BLOCKS)
    for c0, c1 in zip(bounds[:-1].tolist(), bounds[1:].tolist()):
        for s, r in enumerate(rs):
            gate = _dot(h2[s], wg_ref[:, c0:c1])
            val = _dot(h2[s], wv_ref[:, c0:c1])
            ge_scr[pad + r.start:pad + r.stop, c0:c1] = gate
            if s == n_sub - 1:
                tail_ref[0, :, c0:c1] = gate[sub - pad:, :]
            conv = (cb_ref[:, c0:c1] + cw_ref[2:3, c0:c1] * gate
                    + cw_ref[1:2, c0:c1] * ge_scr[pad - 1 + r.start:pad - 1 + r.stop, c0:c1]
                    + cw_ref[0:1, c0:c1] * ge_scr[pad - 2 + r.start:pad - 2 + r.stop, c0:c1])
            act = (_silu(conv) * val).astype(BF16)
            part = _dot(act, wd_ref[c0:c1, :])
            acc[s] = part if acc[s] is None else acc[s] + part
    for s, r in enumerate(rs):
        y_ref[r, :] = _rms(x1[s] + acc[s], fnw_ref[...])


FFN_ROW_TILE = 512
FFN_SUB_ROWS = 256
FFN_COL_BLOCKS = (1024, 1024, 768)
assert sum(FFN_COL_BLOCKS) == D_FF and all(c % LANES == 0 for c in FFN_COL_BLOCKS)


def _ffn_prompt(o_a, y_s, x2d, w_o, norm2_w, w_gate, w_val, w_down, conv_w, conv_b, fnorm_w, seq):
    n = x2d.shape[0]
    tm = FFN_ROW_TILE
    assert seq % tm == 0
    kern = functools.partial(_ffn_prompt_kernel, tiles_per_seq=seq // tm)
    row = lambda w: pl.BlockSpec((tm, w), lambda i: (i, 0))
    resident = lambda shape: pl.BlockSpec(shape, lambda i: (0, 0), pipeline_mode=pl.Buffered(1))
    return pl.pallas_call(
        kern,
        grid=(n // tm,),
        in_specs=[
            row(D_HGRN), row(D_SSM), row(D_MODEL),
            resident((D_HGRN + D_SSM, D_MODEL)), resident((1, D_MODEL)),
            resident((D_MODEL, D_FF)), resident((D_MODEL, D_FF)), resident((D_FF, D_MODEL)),
            resident((FFN_CONV, D_FF)), resident((1, D_FF)), resident((1, D_MODEL)),
        ],
        out_specs=[
            row(D_MODEL),
            pl.BlockSpec((1, SUBLANES, D_FF), lambda i: (i, 0, 0)),
        ],
        out_shape=[
            jax.ShapeDtypeStruct((n, D_MODEL), F32),
            jax.ShapeDtypeStruct((n // tm, SUBLANES, D_FF), F32),
        ],
        scratch_shapes=[pltpu.VMEM((tm + SUBLANES, D_FF), F32)],
        compiler_params=_cp(("arbitrary",)),
        name="ffn_prompt",
    )(o_a, y_s, x2d, w_o, norm2_w, w_gate, w_val, w_down, conv_w, conv_b, fnorm_w)


FF_CAST_BLOCK = 256


def _ffn_step_kernel(oa_ref, ys_ref, x_ref, wo_ref, n2_ref, wg_ref, wv_ref, wd_ref, cw_ref, cb_ref,
                     fnw_ref, b0_ref, b1_ref,
                     y_ref, gate_ref, wob_ref, wgb_ref, wvb_ref, wdb_ref, x1_scr, h2_scr, acc_scr):
    j = pl.program_id(0)

    @pl.when(j == 0)
    def _():
        wo = wo_ref[...].astype(BF16)
        wob_ref[...] = wo
        x1 = (x_ref[...] + _dot(oa_ref[...], wo[0:D_HGRN, :]) + _dot(ys_ref[...], wo[D_HGRN:, :]))
        x1_scr[...] = x1
        h2_scr[...] = _rms(x1, n2_ref[...]).astype(BF16)
        acc_scr[...] = jnp.zeros_like(acc_scr)

    wg = wg_ref[...].astype(BF16)
    wv = wv_ref[...].astype(BF16)
    wd = wd_ref[...].astype(BF16)
    wgb_ref[...] = wg
    wvb_ref[...] = wv
    wdb_ref[...] = wd
    h2 = h2_scr[...]
    gate = _dot(h2, wg)
    val = _dot(h2, wv)
    gate_ref[...] = gate
    conv = (cb_ref[...] + cw_ref[2:3, :] * gate + cw_ref[1:2, :] * b1_ref[...]
            + cw_ref[0:1, :] * b0_ref[...])
    act = (_silu(conv) * val).astype(BF16)
    acc_scr[...] = acc_scr[...] + _dot(act, wd)

    @pl.when(j == pl.num_programs(0) - 1)
    def _():
        y_ref[...] = _rms(x1_scr[...] + acc_scr[...], fnw_ref[...])


def _ffn_step(o_a, y_s, x2d, w_out, norm2_w, w_up, w_down, conv_w, conv_b, fnorm_w, buf):
    n = x2d.shape[0]
    blk = FF_CAST_BLOCK
    nj = D_FF // blk
    const = lambda shape: pl.BlockSpec(shape, lambda j: (0, 0))
    col = lambda rows: pl.BlockSpec((rows, blk), lambda j: (0, j))
    return pl.pallas_call(
        _ffn_step_kernel,
        grid=(nj,),
        in_specs=[
            const((n, D_HGRN)), const((n, D_SSM)), const((n, D_MODEL)),
            const((D_HGRN + D_SSM, D_MODEL)), const((1, D_MODEL)),
            col(D_MODEL), pl.BlockSpec((D_MODEL, blk), lambda j: (0, nj + j)),
            pl.BlockSpec((blk, D_MODEL), lambda j: (j, 0)),
            col(FFN_CONV), col(1), const((1, D_MODEL)), col(n), col(n),
        ],
        out_specs=[
            const((n, D_MODEL)), col(n),
            const((D_HGRN + D_SSM, D_MODEL)), col(D_MODEL), col(D_MODEL),
            pl.BlockSpec((blk, D_MODEL), lambda j: (j, 0)),
        ],
        out_shape=[
            jax.ShapeDtypeStruct((n, D_MODEL), F32),
            jax.ShapeDtypeStruct((n, D_FF), F32),
            jax.ShapeDtypeStruct((D_HGRN + D_SSM, D_MODEL), BF16),
            jax.ShapeDtypeStruct((D_MODEL, D_FF), BF16),
            jax.ShapeDtypeStruct((D_MODEL, D_FF), BF16),
            jax.ShapeDtypeStruct((D_FF, D_MODEL), BF16),
        ],
        scratch_shapes=[
            pltpu.VMEM((n, D_MODEL), F32),
            pltpu.VMEM((n, D_MODEL), BF16),
            pltpu.VMEM((n, D_MODEL), F32),
        ],
        compiler_params=_cp(("arbitrary",)),
        name="ffn_step",
    )(o_a, y_s, x2d, w_out, norm2_w, w_up, w_up, w_down, conv_w, conv_b, fnorm_w,
      buf[:, 0], buf[:, 1])


def _row(v):
    return v.reshape(1, -1).astype(F32)


def _pad_lanes(v):
    return jnp.pad(v.astype(F32), (0, LANES - v.shape[0])).reshape(1, LANES)


def kernel(x_prompt, x_sample, state_hgrn, state_ssm, state_conv_ssm, state_conv_ffn, norm1_w, w_in, hgrn_lb, hgrn_norm_w, ssm_conv_w, ssm_conv_b, ssm_dt_bias, ssm_a_log, ssm_d, ssm_norm_w, w_out, norm2_w, w_up, ffn_conv_w, ffn_conv_b, w_down, final_norm_w):
    depth = w_in.shape[0]
    assert depth == 1, "single-layer trunk"
    l = 0
    batch, seq, _ = x_prompt.shape
    dec_batch, dec_seq, _ = x_sample.shape
    assert dec_seq == 1 and seq % CHUNK == 0 and seq >= SSM_CONV

    w_in_t = w_in[l].T
    d_full = jnp.repeat(ssm_d[l].astype(F32), SSM_HEAD_DIM).reshape(1, D_SSM)
    dt_bias = _pad_lanes(ssm_dt_bias[l])
    a_log = _pad_lanes(ssm_a_log[l])
    mconst = jnp.asarray(_hgrn_const(), BF16)
    tri = jnp.asarray(np.tril(np.ones((CHUNK, CHUNK), np.float32)), BF16)
    expand = jnp.asarray(_head_expand(), BF16)
    lb_raw = hgrn_lb.astype(F32)

    xs_ = x_sample.reshape(dec_batch, D_MODEL)
    w_main, proj_s, lg_s, dt_s = _inproj_cast(xs_, _row(norm1_w[l]), w_in_t, lb_raw, dt_bias)
    oa_s, hgrn_s = _hgrn_step(proj_s, lg_s, _row(hgrn_norm_w[l]), state_hgrn[l])
    ys_s, ssm_s = _ssd_step(proj_s, dt_s, state_conv_ssm[l], ssm_conv_w[l], _row(ssm_conv_b[l]),
                            a_log, d_full, _row(ssm_norm_w[l]), expand, state_ssm[l])
    y_s, gate_s, w_ob, w_gb, w_vb, w_db = _ffn_step(
        oa_s, ys_s, xs_, w_out[l], _row(norm2_w[l]), w_up[l], w_down[l],
        ffn_conv_w[l], _row(ffn_conv_b[l]), _row(final_norm_w), state_conv_ffn[l])
    cs_s = jnp.concatenate([state_conv_ssm[l][:, 1:], proj_s[:, None, OFF_XS:OFF_XS + CONV_DIM]],
                           axis=1)
    cf_s = jnp.concatenate([state_conv_ffn[l][:, 1:], gate_s[:, None, :]], axis=1)

    xp = x_prompt.reshape(batch * seq, D_MODEL)
    proj_p, lg_p, dt_p = _inproj(xp, _row(norm1_w[l]), w_main, w_in_t, lb_raw, dt_bias)
    oa_p, hgrn_p = _hgrn_prompt(proj_p, lg_p, _row(hgrn_norm_w[l]), mconst, batch, seq)
    ys_p, ssm_p = _ssd_prompt(proj_p, dt_p, ssm_conv_w[l], _row(ssm_conv_b[l]), a_log,
                              d_full, _row(ssm_norm_w[l]), tri, expand, batch, seq)
    y_p, tail_p = _ffn_prompt(oa_p, ys_p, xp, w_ob, _row(norm2_w[l]), w_gb, w_vb, w_db,
                              ffn_conv_w[l], _row(ffn_conv_b[l]), _row(final_norm_w), seq)
    proj_p3 = proj_p.reshape(batch, seq, D_MAIN)
    cs_p = proj_p3[:, seq - (SSM_CONV - 1):, OFF_XS:OFF_XS + CONV_DIM]
    tails = tail_p.reshape(batch, seq // FFN_ROW_TILE, SUBLANES, D_FF)
    cf_p = tails[:, -1, SUBLANES - (FFN_CONV - 1):, :]

    dt_ = x_prompt.dtype
    return (y_p.reshape(batch, seq, D_MODEL).astype(dt_),
            y_s.reshape(dec_batch, 1, D_MODEL).astype(dt_),
            hgrn_p[None].astype(dt_),
            hgrn_s[None].astype(dt_),
            ssm_p.reshape(1, batch, SSM_HEADS, SSM_HEAD_DIM, SSM_STATE).astype(dt_),
            ssm_s[None].astype(dt_),
            cs_p[None].astype(dt_),
            cs_s[None].astype(dt_),
            cf_p[None].astype(dt_),
            cf_s[None].astype(dt_))
```

```python
import functools

import numpy as np
import jax
import jax.numpy as jnp
from jax import lax
from jax.experimental import pallas as pl
from jax.experimental.pallas import tpu as pltpu

F32 = jnp.float32
BF16 = jnp.bfloat16
EPS = 1e-6

LANES = 128
SUBLANES = 8

D_MODEL = 1024
HGRN_HEADS = 8
HGRN_DK = 128
HGRN_DV = 128
D_HGRN = HGRN_HEADS * HGRN_DV
SSM_HEADS = 16
SSM_HEAD_DIM = 64
D_SSM = SSM_HEADS * SSM_HEAD_DIM
SSM_STATE = 128
SSM_GROUPS = 2
SSM_CONV = 4
CONV_DIM = D_SSM + 2 * SSM_GROUPS * SSM_STATE
D_FF = 2816
FFN_CONV = 3
D_MAIN = 4 * D_HGRN + D_SSM + CONV_DIM
OFF_Q, OFF_F, OFF_I, OFF_G = 0, 1024, 2048, 3072
OFF_Z, OFF_XS, OFF_BC = 4096, 5120, 6144

CHUNK = 128
GROUP_W = D_SSM // SSM_GROUPS
VMEM_LIMIT = 56 * 1024 * 1024


def _cp(sem):
    return pltpu.CompilerParams(dimension_semantics=sem, vmem_limit_bytes=VMEM_LIMIT)


def _dot(a, b):
    return jnp.dot(a, b, preferred_element_type=F32)


def _dot_nt(a, b):
    return lax.dot_general(a, b, (((1,), (1,)), ((), ())), preferred_element_type=F32)


def _split3(x):
    h = x.astype(BF16)
    r = x - h.astype(F32)
    m = r.astype(BF16)
    lo = (r - m.astype(F32)).astype(BF16)
    return h, m, lo


def _dot_exact_lhs01(m01, x):
    h, m, lo = _split3(x)
    return _dot(m01, h) + _dot(m01, m) + _dot(m01, lo)


def _dot_exact_rhs01(x, m01):
    h, m, lo = _split3(x)
    return _dot(h, m01) + _dot(m, m01) + _dot(lo, m01)


def _dot_split_rhs01(x, m01):
    h = x.astype(BF16)
    lo = (x - h.astype(F32)).astype(BF16)
    return _dot(h, m01) + _dot(lo, m01)


def _dot_split_lhs01(m01, x):
    h = x.astype(BF16)
    lo = (x - h.astype(F32)).astype(BF16)
    return _dot(m01, h) + _dot(m01, lo)


def _sigmoid(x):
    return 1.0 / (1.0 + jnp.exp(-x))


def _silu(x):
    return x * _sigmoid(x)


def _rms(x, w):
    ms = jnp.mean(x * x, axis=-1, keepdims=True)
    return x * lax.rsqrt(ms + EPS) * w


def _dt_proj(hb, wdt_ref):
    rows = lax.broadcasted_iota(jnp.int32, wdt_ref.shape, 0)
    wdt = jnp.where(rows < SSM_HEADS, wdt_ref[...], 0.0).astype(BF16)
    return _dot_nt(hb, wdt)

def _inproj_kernel(x_ref, nw_ref, w_ref, wdt_ref, lb_ref, dtb_ref, mix_ref, lg_ref, dt_ref):
    n_sub = max(1, x_ref.shape[0] // INPROJ_SUB_ROWS)
    sub_rows = x_ref.shape[0] // n_sub
    rs = [slice(s * sub_rows, (s + 1) * sub_rows) for s in range(n_sub)]
    hb = [_rms(x_ref[r, :], nw_ref[...]).astype(BF16) for r in rs]

    def put(r, off, val):
        mix_ref[r, off:off + val.shape[1]] = val.astype(BF16)

    w = D_HGRN
    lb = _hgrn_lb(lb_ref[...])
    for s, r in enumerate(rs):
        qf = _dot(hb[s], w_ref[:, OFF_Q:OFF_Q + 2 * w])
        f = lb + (1.0 - lb) * _sigmoid(qf[:, w:])
        lg_ref[r, :] = jnp.log(f)
        put(r, OFF_F, 1.0 - f)
        put(r, OFF_Q, _silu(qf[:, :w]))
    for s, r in enumerate(rs):
        ig = _dot(hb[s], w_ref[:, OFF_I:OFF_I + 2 * w])
        put(r, OFF_I, ig[:, :w])
        put(r, OFF_G, _silu(ig[:, w:]))
    for s, r in enumerate(rs):
        zx = _dot(hb[s], w_ref[:, OFF_Z:OFF_Z + D_SSM + CONV_DIM])
        put(r, OFF_Z, _silu(zx[:, :D_SSM]))
        put(r, OFF_XS, zx[:, D_SSM:])
    for s, r in enumerate(rs):
        dt_ref[r, :] = _softplus(_dt_proj(hb[s], wdt_ref) + dtb_ref[...])


INPROJ_ROW_TILE = 512
INPROJ_SUB_ROWS = 128


def _inproj(x2d, norm_w, w_main, w_t, lb_raw, dt_bias):
    n = x2d.shape[0]
    tm = min(INPROJ_ROW_TILE, n)
    assert n % tm == 0
    row = lambda w: pl.BlockSpec((tm, w), lambda i: (i, 0))
    resident = lambda shape: pl.BlockSpec(shape, lambda i: (0, 0), pipeline_mode=pl.Buffered(1))
    return pl.pallas_call(
        _inproj_kernel,
        grid=(n // tm,),
        in_specs=[
            row(D_MODEL), resident((1, D_MODEL)),
            resident((D_MODEL, D_MAIN)),
            pl.BlockSpec((LANES, D_MODEL), lambda i: (D_MAIN // LANES, 0), pipeline_mode=pl.Buffered(1)),
            resident(lb_raw.shape), resident((1, LANES)),
        ],
        out_specs=[row(D_MAIN), row(D_HGRN), row(LANES)],
        out_shape=[
            jax.ShapeDtypeStruct((n, D_MAIN), BF16),
            jax.ShapeDtypeStruct((n, D_HGRN), F32),
            jax.ShapeDtypeStruct((n, LANES), F32),
        ],
        compiler_params=_cp(("arbitrary",)),
        name="inproj",
    )(x2d, norm_w, w_main, w_t, lb_raw, dt_bias)


CAST_BLOCK = 512


def _inproj_cast_kernel(x_ref, nw_ref, w_ref, wdt_ref, lb_ref, dtb_ref,
                        wb_ref, mix_ref, lg_ref, dt_ref, h_scr):
    j = pl.program_id(0)
    blk = CAST_BLOCK
    q0, f0, i0, g0, z0, x0 = (off // blk for off in (OFF_Q, OFF_F, OFF_I, OFF_G, OFF_Z, OFF_XS))

    @pl.when(j == 0)
    def _():
        hb = _rms(x_ref[...], nw_ref[...]).astype(BF16)
        h_scr[...] = hb
        dt_ref[...] = _softplus(_dt_proj(hb, wdt_ref) + dtb_ref[...])

    wb = w_ref[...].T.astype(BF16)
    wb_ref[...] = wb
    p = _dot(h_scr[...], wb)

    @pl.when(((j >= q0) & (j < f0)) | ((j >= g0) & (j < x0)))
    def _():
        mix_ref[...] = _silu(p).astype(BF16)

    @pl.when(((j >= i0) & (j < g0)) | (j >= x0))
    def _():
        mix_ref[...] = p.astype(BF16)

    for fj in range(f0, i0):
        @pl.when(j == fj)
        def _(fj=fj):
            cols = slice((fj - f0) * blk, (fj - f0 + 1) * blk)
            lb = _hgrn_lb(lb_ref[:, cols])
            f = lb + (1.0 - lb) * _sigmoid(p)
            lg_ref[:, cols] = jnp.log(f)
            mix_ref[...] = (1.0 - f).astype(BF16)


def _inproj_cast(x2d, norm_w, w_t, lb_raw, dt_bias):
    n = x2d.shape[0]
    blk = CAST_BLOCK
    const = lambda shape: pl.BlockSpec(shape, lambda j: (0, 0))
    return pl.pallas_call(
        _inproj_cast_kernel,
        grid=(D_MAIN // blk,),
        in_specs=[
            const((n, D_MODEL)), const((1, D_MODEL)),
            pl.BlockSpec((blk, D_MODEL), lambda j: (j, 0)),
            pl.BlockSpec((LANES, D_MODEL), lambda j: (D_MAIN // LANES, 0)),
            const(lb_raw.shape), const((1, LANES)),
        ],
        out_specs=[
            pl.BlockSpec((D_MODEL, blk), lambda j: (0, j)),
            pl.BlockSpec((n, blk), lambda j: (0, j)),
            const((n, D_HGRN)), const((n, LANES)),
        ],
        out_shape=[
            jax.ShapeDtypeStruct((D_MODEL, D_MAIN), BF16),
            jax.ShapeDtypeStruct((n, D_MAIN), BF16),
            jax.ShapeDtypeStruct((n, D_HGRN), F32),
            jax.ShapeDtypeStruct((n, LANES), F32),
        ],
        scratch_shapes=[pltpu.VMEM((n, D_MODEL), BF16)],
        compiler_params=_cp(("arbitrary",)),
        name="inproj_cast",
    )(x2d, norm_w, w_t, w_t, lb_raw, dt_bias)


LOG2E = 1.4426950408889634
MXU_LEVEL_HALVES = (4, 2)


def _hgrn_const():
    c = CHUNK
    t = np.arange(c)[:, None]
    j = np.arange(c)[None, :]
    blocks = [(j <= t)]
    for h in MXU_LEVEL_HALVES:
        mid = (t // (2 * h)) * (2 * h) + h
        upper = (t >= mid) & (j >= mid) & (j <= t)
        lower = (t < mid) & (j > t) & (j < mid)
        blocks.append(upper | lower)
    return np.concatenate(blocks, axis=0).astype(np.float32)


def _midpoint_decay(b, h):
    pieces = []
    for start in range(0, CHUNK, 2 * h):
        mid = start + h
        m = b[mid - 1:mid, :]
        pieces.append(m - b[start:mid])
        pieces.append(b[mid:mid + h] - m)
    return jnp.concatenate(pieces, axis=0)


def _mix_rows(q, k, h):
    pieces = []
    for start in range(0, CHUNK, 2 * h):
        pieces.append(k[start:start + h])
        pieces.append(q[start + h:start + 2 * h])
    return jnp.concatenate(pieces, axis=0)


def _hgrn_lb(lb_raw):
    mx = jnp.max(lb_raw, axis=0, keepdims=True)
    e = jnp.exp(lb_raw - mx)
    return e[0:1, :] / jnp.sum(e, axis=0, keepdims=True)


def _level_map():
    t = lax.broadcasted_iota(jnp.int32, (CHUNK, CHUNK), 0)
    s = lax.broadcasted_iota(jnp.int32, (CHUNK, CHUNK), 1)
    bitlen = 32 - lax.clz(t ^ s)
    return jnp.where(t > s, bitlen, jnp.where(t == s, 0, -1))


def _hgrn_prompt_kernel(q_ref, k_ref, i_ref, g_ref, lg_ref, nw_ref, mc_ref,
                        o_ref, s_out_ref, st_scr):
    c = pl.program_id(1)

    @pl.when(c == 0)
    def _():
        st_scr[...] = jnp.zeros_like(st_scr)

    n_sub = q_ref.shape[0] // CHUNK
    lev = _level_map().astype(jnp.int16)
    row = lax.broadcasted_iota(jnp.int32, (CHUNK, HGRN_DK), 0)
    heads = range(HGRN_HEADS)
    pairs = [(s, h) for s in range(n_sub) for h in heads]
    rs = {s: slice(s * CHUNK, (s + 1) * CHUNK) for s in range(n_sub)}
    cs = {h: slice(h * HGRN_DK, (h + 1) * HGRN_DK) for h in heads}

    e_sub = {s: _dot_split_lhs01(mc_ref[...], lg_ref[rs[s], :] * LOG2E) for s in range(n_sub)}
    qb = {(s, h): q_ref[rs[s], cs[h]] for s, h in pairs}
    kb = {(s, h): k_ref[rs[s], cs[h]] for s, h in pairs}
    vb = {(s, h): i_ref[rs[s], cs[h]] for s, h in pairs}
    q = {p: qb[p].astype(F32) for p in pairs}
    k = {p: kb[p].astype(F32) for p in pairs}
    b = {(s, h): e_sub[s][0:CHUNK, cs[h]] for s, h in pairs}
    b_last = {p: b[p][CHUNK - 1:CHUNK, :] for p in pairs}

    st = {h: st_scr[h] for h in heads}
    o = {}
    for s, h in pairs:
        p = (s, h)
        o[p] = _dot((q[p] * jnp.exp2(b[p])).astype(BF16), st[h].T.astype(BF16))
        ks = (k[p] * jnp.exp2(b_last[p] - b[p])).astype(BF16)
        st[h] = st[h] * jnp.exp2(b_last[p]) + _dot(vb[p].astype(F32).T.astype(BF16), ks)
    for h in heads:
        st_scr[h] = st[h]

    a = {p: jnp.where(lev == 0, _dot(qb[p], k[p].T.astype(BF16)).astype(BF16), jnp.zeros((), BF16))
         for p in pairs}
    half = CHUNK // 2
    while half >= 1:
        for p in pairs:
            if half >= SUBLANES:
                x = _mix_rows(q[p], k[p], half) * jnp.exp2(_midpoint_decay(b[p], half))
            else:
                upper = (row & half) != 0
                if half in MXU_LEVEL_HALVES:
                    blk = 1 + MXU_LEVEL_HALVES.index(half)
                    w = jnp.exp2(e_sub[p[0]][blk * CHUNK:(blk + 1) * CHUNK, cs[p[1]]])
                    x = jnp.where(upper, q[p], k[p]) * w
                else:
                    x = jnp.where(upper, q[p] * (1.0 - k[p]), k[p])
            gram = _dot(x.astype(BF16), x.T.astype(BF16))
            a[p] = jnp.where(lev == half.bit_length(), gram.astype(BF16), a[p])
        half //= 2

    for p in pairs:
        o[p] = o[p] + _dot(a[p], vb[p])
    for s, h in pairs:
        gate = g_ref[rs[s], cs[h]].astype(F32)
        o_ref[rs[s], cs[h]] = (_rms(o[(s, h)], nw_ref[...]) * gate).astype(BF16)

    @pl.when(c == pl.num_programs(1) - 1)
    def _():
        for h in range(HGRN_HEADS):
            s_out_ref[0, h] = st_scr[h].T


HGRN_SUBCHUNKS = 4


def _hgrn_prompt(mix, lg, norm_w, mconst, batch, seq):
    rows = HGRN_SUBCHUNKS * CHUNK
    assert seq % rows == 0
    nc = seq // rows

    def col(off):
        return pl.BlockSpec((rows, D_HGRN), lambda b, c: (b * nc + c, off // D_HGRN))

    return pl.pallas_call(
        _hgrn_prompt_kernel,
        grid=(batch, nc),
        in_specs=[
            col(OFF_Q), col(OFF_F), col(OFF_I), col(OFF_G), col(0),
            pl.BlockSpec((1, HGRN_DV), lambda b, c: (0, 0)),
            pl.BlockSpec(mconst.shape, lambda b, c: (0, 0)),
        ],
        out_specs=[
            pl.BlockSpec((rows, D_HGRN), lambda b, c: (b * nc + c, 0)),
            pl.BlockSpec((1, HGRN_HEADS, HGRN_DK, HGRN_DV), lambda b, c: (b, 0, 0, 0)),
        ],
        out_shape=[
            jax.ShapeDtypeStruct((batch * seq, D_HGRN), BF16),
            jax.ShapeDtypeStruct((batch, HGRN_HEADS, HGRN_DK, HGRN_DV), F32),
        ],
        scratch_shapes=[pltpu.VMEM((HGRN_HEADS, HGRN_DV, HGRN_DK), F32)],
        compiler_params=_cp(("arbitrary", "arbitrary")),
        name="hgrn_prompt",
    )(mix, mix, mix, mix, lg, norm_w, mconst)


def _hgrn_step_kernel(q_ref, i_ref, g_ref, lg_ref, nw_ref, s_ref,
                      o_ref, s_out_ref, o_scr):
    nb = q_ref.shape[0]
    qb = q_ref[...]
    f_t = jnp.exp(lg_ref[...]).T
    v = i_ref[...].astype(F32)
    lhs_rows = 2 * SUBLANES
    for t in range(nb):
        v_row = v[t:t + 1, :]
        s_new = v_row + f_t[:, t:t + 1] * (s_ref[t, 0] - v_row)
        s_out_ref[t, 0] = s_new
        q_rows = jnp.broadcast_to(qb[t:t + 1, :], (lhs_rows, HGRN_DK))
        o_scr[t:t + 1, :] = _dot(q_rows, s_new.astype(BF16))[0:1, :]
    o_ref[...] = (_rms(o_scr[...], nw_ref[...]) * g_ref[...].astype(F32)).astype(BF16)


def _hgrn_step(mix, lg, norm_w, state):
    nb = state.shape[0]
    hb = lambda off: off // HGRN_DK

    def col(off):
        return pl.BlockSpec((nb, HGRN_DK), lambda h: (0, hb(off) + h))

    st_spec = pl.BlockSpec((nb, 1, HGRN_DK, HGRN_DV), lambda h: (0, h, 0, 0))
    return pl.pallas_call(
        _hgrn_step_kernel,
        grid=(HGRN_HEADS,),
        in_specs=[
            col(OFF_Q), col(OFF_I), col(OFF_G), col(0),
            pl.BlockSpec((1, HGRN_DV), lambda h: (0, 0)),
            st_spec,
        ],
        out_specs=[pl.BlockSpec((nb, HGRN_DV), lambda h: (0, h)), st_spec],
        out_shape=[
            jax.ShapeDtypeStruct((nb, D_HGRN), BF16),
            jax.ShapeDtypeStruct(state.shape, F32),
        ],
        scratch_shapes=[pltpu.VMEM((nb, HGRN_DV), F32)],
        compiler_params=_cp(("arbitrary",)),
        name="hgrn_step",
    )(mix, mix, mix, lg, norm_w, state)


def _head_expand(width=SSM_HEAD_DIM):
    e = np.zeros((LANES, SSM_HEADS * width), np.float32)
    for h in range(SSM_HEADS):
        e[h, h * width:(h + 1) * width] = 1.0
    return e


def _softplus(x):
    return jnp.maximum(x, 0.0) + jnp.log(1.0 + jnp.exp(-jnp.abs(x)))


def _ssm_gate_norm(y, z_gate, nw):
    y = y * z_gate.astype(F32)
    parts = [_rms(y[:, g * GROUP_W:(g + 1) * GROUP_W], nw[:, g * GROUP_W:(g + 1) * GROUP_W])
             for g in range(SSM_GROUPS)]
    return jnp.concatenate(parts, axis=-1)


def _ssd_prompt_kernel(z_ref, xs_ref, bc_ref, dt_ref, cw_ref, cb_ref, alog_ref,
                       dvec_ref, nw_ref, tri_ref, exp_ref,
                       shift_ref, y_ref, st_out_ref, xprev_scr, st_scr):
    c = pl.program_id(1)
    t = CHUNK

    @pl.when(c == 0)
    def _():
        st_scr[...] = jnp.zeros_like(st_scr)
        xprev_scr[...] = jnp.zeros_like(xprev_scr)

    subs = range(xs_ref.shape[0] // t)
    rs = [slice(s * t, (s + 1) * t) for s in subs]

    x_cur = [jnp.concatenate([xs_ref[r, :], bc_ref[r, :]], axis=-1) for r in rs]
    x_prev = [xprev_scr[...]] + x_cur[:-1]
    xprev_scr[...] = x_cur[-1]
    taps = [_dot(shift_ref[...], jnp.concatenate([x_prev[s], x_cur[s]], axis=0)) for s in subs]
    xbc = []
    for s in subs:
        acc = cb_ref[...] + cw_ref[SSM_CONV - 1:SSM_CONV, :] * x_cur[s].astype(F32)
        for d in range(1, SSM_CONV):
            acc = acc + cw_ref[SSM_CONV - 1 - d:SSM_CONV - d, :] * taps[s][(d - 1) * t:d * t, :]
        xbc.append(_silu(acc))
    xs = [x[:, 0:D_SSM] for x in xbc]

    dt = [dt_ref[r, :] for r in rs]
    neg_a = -LOG2E * jnp.exp(alog_ref[...])
    cs = [_dot_exact_lhs01(tri_ref[...], dt[s] * neg_a) for s in subs]
    ex = exp_ref[...]
    dt_full = [_dot_split_rhs01(dt[s], ex) for s in subs]
    cs_full = [_dot_exact_rhs01(cs[s], ex) for s in subs]
    cs_last_full = [x[t - 1:t, :] for x in cs_full]
    x_dt = [xs[s] * dt_full[s] for s in subs]
    x_end = [(x_dt[s] * jnp.exp2(cs_last_full[s] - cs_full[s])).astype(BF16) for s in subs]
    cs_t = [x.T for x in cs]

    causal = (lax.broadcasted_iota(jnp.int32, (t, t), 0)
              >= lax.broadcasted_iota(jnp.int32, (t, t), 1))
    lane = lax.broadcasted_iota(jnp.int32, (1, D_SSM), 1)
    odd_head = (lane & SSM_HEAD_DIM) != 0
    x_by_parity = []
    for s in subs:
        x_b = x_dt[s].astype(BF16)
        zero = jnp.zeros_like(x_b)
        x_by_parity.append((jnp.where(odd_head, zero, x_b), jnp.where(odd_head, x_b, zero)))
    heads_per_group = SSM_HEADS // SSM_GROUPS
    pair_w = 2 * SSM_HEAD_DIM
    never = -1e30

    def group_bc(s, g):
        b_g = xbc[s][:, D_SSM + g * SSM_STATE:D_SSM + (g + 1) * SSM_STATE]
        c_off = D_SSM + SSM_GROUPS * SSM_STATE + g * SSM_STATE
        return b_g.T.astype(BF16), xbc[s][:, c_off:c_off + SSM_STATE].astype(BF16)

    bc_t = {(s, g): group_bc(s, g) for s in subs for g in range(SSM_GROUPS)}

    y_diag = []
    for s in subs:
        y_parts = []
        for g in range(SSM_GROUPS):
            b_t, c_g = bc_t[(s, g)]
            gmat = _dot(c_g, b_t)
            for pp in range(heads_per_group // 2):
                h0 = g * heads_per_group + 2 * pp
                psl = slice(h0 * SSM_HEAD_DIM, h0 * SSM_HEAD_DIM + pair_w)
                yp = None
                for sub in range(2):
                    h = h0 + sub
                    diff = cs[s][:, h:h + 1] - cs_t[s][h:h + 1, :]
                    w = jnp.exp2(jnp.where(causal, diff, never)) * gmat
                    part = _dot(w.astype(BF16), x_by_parity[s][sub][:, psl])
                    yp = part if yp is None else yp + part
                y_parts.append(yp)
        y_diag.append(jnp.concatenate(y_parts, axis=-1))

    y_off = []
    for s in subs:
        offs = []
        for g in range(SSM_GROUPS):
            sl = slice(g * GROUP_W, (g + 1) * GROUP_W)
            b_t, c_g = bc_t[(s, g)]
            st_g = st_scr[:, sl]
            offs.append(_dot(c_g, st_g.astype(BF16)))
            st_scr[:, sl] = st_g * jnp.exp2(cs_last_full[s][:, sl]) + _dot(b_t, x_end[s][:, sl])
        y_off.append(jnp.concatenate(offs, axis=-1) * jnp.exp2(cs_full[s]))

    for s in subs:
        y = y_diag[s] + y_off[s] + dvec_ref[...] * xs[s]
        y_ref[rs[s], :] = _ssm_gate_norm(y, z_ref[rs[s], :], nw_ref[...]).astype(BF16)

    @pl.when(c == pl.num_programs(1) - 1)
    def _():
        for j in range(D_SSM // LANES):
            st_out_ref[0, j * LANES:(j + 1) * LANES, :] = st_scr[:, j * LANES:(j + 1) * LANES].T


SSD_SUBCHUNKS = 1


def _ssd_prompt(mix, dt, conv_w, conv_b, a_log, d_full, norm_w, tri, expand, batch, seq):
    rows = SSD_SUBCHUNKS * CHUNK
    assert seq % rows == 0
    nc = seq // rows
    const = lambda shape: pl.BlockSpec(shape, lambda b, c: (0, 0))
    return pl.pallas_call(
        _ssd_prompt_kernel,
        grid=(batch, nc),
        in_specs=[
            pl.BlockSpec((rows, D_SSM), lambda b, c: (b * nc + c, OFF_Z // D_SSM)),
            pl.BlockSpec((rows, D_SSM), lambda b, c: (b * nc + c, OFF_XS // D_SSM)),
            pl.BlockSpec((rows, 512), lambda b, c: (b * nc + c, OFF_BC // 512)),
            pl.BlockSpec((rows, LANES), lambda b, c: (b * nc + c, 0)),
            const((SSM_CONV, CONV_DIM)), const((1, CONV_DIM)),
            const((1, LANES)),
            const((1, D_SSM)), const((1, D_SSM)),
            const((CHUNK, CHUNK)), const((LANES, D_SSM)),
            const(((SSM_CONV - 1) * CHUNK, 2 * CHUNK)),
        ],
        out_specs=[
            pl.BlockSpec((rows, D_SSM), lambda b, c: (b * nc + c, 0)),
            pl.BlockSpec((1, D_SSM, SSM_STATE), lambda b, c: (b, 0, 0)),
        ],
        out_shape=[
            jax.ShapeDtypeStruct((batch * seq, D_SSM), BF16),
            jax.ShapeDtypeStruct((batch, D_SSM, SSM_STATE), F32),
        ],
        scratch_shapes=[
            pltpu.VMEM((CHUNK, CONV_DIM), BF16),
            pltpu.VMEM((SSM_STATE, D_SSM), F32),
        ],
        compiler_params=_cp(("arbitrary", "arbitrary")),
        name="ssd_prompt",
    )(mix, mix, mix, dt, conv_w, conv_b, a_log, d_full, norm_w, tri, expand,
      jnp.asarray(_conv_shifts(), BF16))


def _conv_shifts():
    m = np.zeros(((SSM_CONV - 1) * CHUNK, 2 * CHUNK), np.float32)
    for d in range(1, SSM_CONV):
        for t in range(CHUNK):
            m[(d - 1) * CHUNK + t, CHUNK + t - d] = 1.0
    return m


def _ssd_step_kernel(z_ref, xs_ref, bc_ref, dt_ref, b0_ref, b1_ref, b2_ref, cw_ref, cb_ref,
                     alog_ref, dvec_ref, nw_ref, exp_ref, exl_ref, st_ref,
                     y_ref, st_out_ref, xt_scr, at_scr, xs_scr, bc_scr, y_scr):
    p = pl.program_id(0)
    nb = z_ref.shape[0]
    pair_w = 2 * SSM_HEAD_DIM
    pairs_per_group = SSM_HEADS // SSM_GROUPS // 2

    @pl.when(p == 0)
    def _():
        x_new = jnp.concatenate([xs_ref[...], bc_ref[...]], axis=-1).astype(F32)
        acc = (cb_ref[...] + cw_ref[0:1, :] * b0_ref[...] + cw_ref[1:2, :] * b1_ref[...]
               + cw_ref[2:3, :] * b2_ref[...] + cw_ref[3:4, :] * x_new)
        xbc = _silu(acc)
        xs = xbc[:, 0:D_SSM]
        dt = dt_ref[...]
        da = dt * (-jnp.exp(alog_ref[...]))
        ex = exp_ref[...]
        x_dt = xs * _dot_exact_rhs01(dt, ex)
        decay = jnp.exp(_dot_exact_rhs01(da, exl_ref[...]))
        xs_scr[...] = xs
        bc_scr[...] = xbc[:, D_SSM:]
        for j in range(D_SSM // LANES):
            sl = slice(j * LANES, (j + 1) * LANES)
            xt_scr[sl, :] = x_dt[:, sl].T
            at_scr[j] = decay[:, 2 * j * LANES:2 * (j + 1) * LANES]

    g_is_1 = p >= pairs_per_group
    row0 = pl.multiple_of(p * pair_w, pair_w)
    x_t = xt_scr[pl.ds(row0, pair_w), :]
    a_p = at_scr[p]
    bc = bc_scr[...]
    b_all = jnp.where(g_is_1, bc[:, SSM_STATE:2 * SSM_STATE], bc[:, 0:SSM_STATE])
    c_all = jnp.where(g_is_1, bc[:, 3 * SSM_STATE:4 * SSM_STATE],
                      bc[:, 2 * SSM_STATE:3 * SSM_STATE]).astype(BF16)
    for t in range(nb):
        inject = x_t[:, t:t + 1] * b_all[t:t + 1, :]
        halves = []
        for sub in range(2):
            rows = slice(sub * SSM_HEAD_DIM, (sub + 1) * SSM_HEAD_DIM)
            half = a_p[t:t + 1, sub * LANES:(sub + 1) * LANES] * st_ref[t, sub] + inject[rows]
            st_out_ref[t, sub] = half
            halves.append(half)
        new = jnp.concatenate(halves, axis=0)
        c_rows = jnp.broadcast_to(c_all[t:t + 1, :], (SUBLANES, SSM_STATE))
        y_scr[p, t:t + 1, :] = _dot_nt(c_rows, new.astype(BF16))[0:1, :]

    @pl.when(p == pl.num_programs(0) - 1)
    def _():
        y_mix = jnp.concatenate([y_scr[j] for j in range(SSM_HEADS // 2)], axis=-1)
        y = y_mix + dvec_ref[...] * xs_scr[...]
        y_ref[...] = _ssm_gate_norm(y, z_ref[...], nw_ref[...]).astype(BF16)


def _ssd_step(mix, dt, buf, conv_w, conv_b, a_log, d_full, norm_w, expand, state):
    nb = state.shape[0]
    n_pairs = SSM_HEADS // 2
    const = lambda shape: pl.BlockSpec(shape, lambda p: (0, 0))
    st_spec = pl.BlockSpec((nb, 2, SSM_HEAD_DIM, SSM_STATE), lambda p: (0, p, 0, 0))
    return pl.pallas_call(
        _ssd_step_kernel,
        grid=(n_pairs,),
        in_specs=[
            pl.BlockSpec((nb, D_SSM), lambda p: (0, OFF_Z // D_SSM)),
            pl.BlockSpec((nb, D_SSM), lambda p: (0, OFF_XS // D_SSM)),
            pl.BlockSpec((nb, 512), lambda p: (0, OFF_BC // 512)),
            const((nb, LANES)),
            const((nb, CONV_DIM)), const((nb, CONV_DIM)), const((nb, CONV_DIM)),
            const((SSM_CONV, CONV_DIM)), const((1, CONV_DIM)),
            const((1, LANES)),
            const((1, D_SSM)), const((1, D_SSM)),
            const((LANES, D_SSM)), const((LANES, SSM_HEADS * LANES)),
            st_spec,
        ],
        out_specs=[const((nb, D_SSM)), st_spec],
        out_shape=[
            jax.ShapeDtypeStruct((nb, D_SSM), BF16),
            jax.ShapeDtypeStruct(state.shape, F32),
        ],
        scratch_shapes=[
            pltpu.VMEM((D_SSM, nb), F32),
            pltpu.VMEM((n_pairs, nb, 2 * LANES), F32),
            pltpu.VMEM((nb, D_SSM), F32),
            pltpu.VMEM((nb, 2 * SSM_GROUPS * SSM_STATE), F32),
            pltpu.VMEM((n_pairs, nb, 2 * SSM_HEAD_DIM), F32),
        ],
        compiler_params=_cp(("arbitrary",)),
        name="ssd_step",
    )(mix, mix, mix, dt, buf[:, 0], buf[:, 1], buf[:, 2], conv_w, conv_b, a_log,
      d_full, norm_w, expand, jnp.asarray(_head_expand(LANES), BF16), state)


def _ffn_prompt_kernel(oa_ref, ys_ref, x_ref, wo_ref, n2_ref, wg_ref, wv_ref, wd_ref, cw_ref, cb_ref,
                       fnw_ref, y_ref, tail_ref, ge_scr, *, tiles_per_seq):
    i = pl.program_id(0)
    tm = x_ref.shape[0]
    pad = SUBLANES
    seq_start = lax.rem(i, tiles_per_seq) == 0

    @pl.when(seq_start)
    def _():
        ge_scr[0:pad, :] = jnp.zeros((pad, D_FF), F32)

    @pl.when(jnp.logical_not(seq_start))
    def _():
        ge_scr[0:pad, :] = ge_scr[tm:tm + pad, :]

    n_sub = max(1, tm // FFN_SUB_ROWS)
    sub = tm // n_sub
    rs = [slice(s * sub, (s + 1) * sub) for s in range(n_sub)]
    x1 = [x_ref[r, :] + _dot(oa_ref[r, :], wo_ref[0:D_HGRN, :])
          + _dot(ys_ref[r, :], wo_ref[D_HGRN:D_HGRN + D_SSM, :]) for r in rs]
    h2 = [_rms(x, n2_ref[...]).astype(BF16) for x in x1]

    acc = [None] * n_sub
    bounds = np.cumsum((0,) + FFN_COL_BLOCKS)
    for c0, c1 in zip(bounds[:-1].tolist(), bounds[1:].tolist()):
        for s, r in enumerate(rs):
            gate = _dot(h2[s], wg_ref[:, c0:c1])
            val = _dot(h2[s], wv_ref[:, c0:c1])
            ge_scr[pad + r.start:pad + r.stop, c0:c1] = gate
            if s == n_sub - 1:
                tail_ref[0, :, c0:c1] = gate[sub - pad:, :]
            conv = (cb_ref[:, c0:c1] + cw_ref[2:3, c0:c1] * gate
                    + cw_ref[1:2, c0:c1] * ge_scr[pad - 1 + r.start:pad - 1 + r.stop, c0:c1]
                    + cw_ref[0:1, c0:c1] * ge_scr[pad - 2 + r.start:pad - 2 + r.stop, c0:c1])
            act = (_silu(conv) * val).astype(BF16)
            part = _dot(act, wd_ref[c0:c1, :])
            acc[s] = part if acc[s] is None else acc[s] + part
    for s, r in enumerate(rs):
        y_ref[r, :] = _rms(x1[s] + acc[s], fnw_ref[...])


FFN_ROW_TILE = 512
FFN_SUB_ROWS = 256
FFN_COL_BLOCKS = (1024, 1024, 768)
assert sum(FFN_COL_BLOCKS) == D_FF and all(c % LANES == 0 for c in FFN_COL_BLOCKS)


def _ffn_prompt(o_a, y_s, x2d, w_o, norm2_w, w_gate, w_val, w_down, conv_w, conv_b, fnorm_w, seq):
    n = x2d.shape[0]
    tm = FFN_ROW_TILE
    assert seq % tm == 0
    kern = functools.partial(_ffn_prompt_kernel, tiles_per_seq=seq // tm)
    row = lambda w: pl.BlockSpec((tm, w), lambda i: (i, 0))
    resident = lambda shape: pl.BlockSpec(shape, lambda i: (0, 0), pipeline_mode=pl.Buffered(1))
    return pl.pallas_call(
        kern,
        grid=(n // tm,),
        in_specs=[
            row(D_HGRN), row(D_SSM), row(D_MODEL),
            resident((D_HGRN + D_SSM, D_MODEL)), resident((1, D_MODEL)),
            resident((D_MODEL, D_FF)), resident((D_MODEL, D_FF)), resident((D_FF, D_MODEL)),
            resident((FFN_CONV, D_FF)), resident((1, D_FF)), resident((1, D_MODEL)),
        ],
        out_specs=[
            row(D_MODEL),
            pl.BlockSpec((1, SUBLANES, D_FF), lambda i: (i, 0, 0)),
        ],
        out_shape=[
            jax.ShapeDtypeStruct((n, D_MODEL), F32),
            jax.ShapeDtypeStruct((n // tm, SUBLANES, D_FF), F32),
        ],
        scratch_shapes=[pltpu.VMEM((tm + SUBLANES, D_FF), F32)],
        compiler_params=_cp(("arbitrary",)),
        name="ffn_prompt",
    )(o_a, y_s, x2d, w_o, norm2_w, w_gate, w_val, w_down, conv_w, conv_b, fnorm_w)


FF_CAST_BLOCK = 256


def _ffn_step_kernel(oa_ref, ys_ref, x_ref, wo_ref, n2_ref, wg_ref, wv_ref, wd_ref, cw_ref, cb_ref,
                     fnw_ref, b0_ref, b1_ref,
                     y_ref, gate_ref, wob_ref, wgb_ref, wvb_ref, wdb_ref, x1_scr, h2_scr, acc_scr):
    j = pl.program_id(0)

    @pl.when(j == 0)
    def _():
        wo = wo_ref[...].astype(BF16)
        wob_ref[...] = wo
        x1 = (x_ref[...] + _dot(oa_ref[...], wo[0:D_HGRN, :]) + _dot(ys_ref[...], wo[D_HGRN:, :]))
        x1_scr[...] = x1
        h2_scr[...] = _rms(x1, n2_ref[...]).astype(BF16)
        acc_scr[...] = jnp.zeros_like(acc_scr)

    wg = wg_ref[...].astype(BF16)
    wv = wv_ref[...].astype(BF16)
    wd = wd_ref[...].astype(BF16)
    wgb_ref[...] = wg
    wvb_ref[...] = wv
    wdb_ref[...] = wd
    h2 = h2_scr[...]
    gate = _dot(h2, wg)
    val = _dot(h2, wv)
    gate_ref[...] = gate
    conv = (cb_ref[...] + cw_ref[2:3, :] * gate + cw_ref[1:2, :] * b1_ref[...]
            + cw_ref[0:1, :] * b0_ref[...])
    act = (_silu(conv) * val).astype(BF16)
    acc_scr[...] = acc_scr[...] + _dot(act, wd)

    @pl.when(j == pl.num_programs(0) - 1)
    def _():
        y_ref[...] = _rms(x1_scr[...] + acc_scr[...], fnw_ref[...])


def _ffn_step(o_a, y_s, x2d, w_out, norm2_w, w_up, w_down, conv_w, conv_b, fnorm_w, buf):
    n = x2d.shape[0]
    blk = FF_CAST_BLOCK
    nj = D_FF // blk
    const = lambda shape: pl.BlockSpec(shape, lambda j: (0, 0))
    col = lambda rows: pl.BlockSpec((rows, blk), lambda j: (0, j))
    return pl.pallas_call(
        _ffn_step_kernel,
        grid=(nj,),
        in_specs=[
            const((n, D_HGRN)), const((n, D_SSM)), const((n, D_MODEL)),
            const((D_HGRN + D_SSM, D_MODEL)), const((1, D_MODEL)),
            col(D_MODEL), pl.BlockSpec((D_MODEL, blk), lambda j: (0, nj + j)),
            pl.BlockSpec((blk, D_MODEL), lambda j: (j, 0)),
            col(FFN_CONV), col(1), const((1, D_MODEL)), col(n), col(n),
        ],
        out_specs=[
            const((n, D_MODEL)), col(n),
            const((D_HGRN + D_SSM, D_MODEL)), col(D_MODEL), col(D_MODEL),
            pl.BlockSpec((blk, D_MODEL), lambda j: (j, 0)),
        ],
        out_shape=[
            jax.ShapeDtypeStruct((n, D_MODEL), F32),
            jax.ShapeDtypeStruct((n, D_FF), F32),
            jax.ShapeDtypeStruct((D_HGRN + D_SSM, D_MODEL), BF16),
            jax.ShapeDtypeStruct((D_MODEL, D_FF), BF16),
            jax.ShapeDtypeStruct((D_MODEL, D_FF), BF16),
            jax.ShapeDtypeStruct((D_FF, D_MODEL), BF16),
        ],
        scratch_shapes=[
            pltpu.VMEM((n, D_MODEL), F32),
            pltpu.VMEM((n, D_MODEL), BF16),
            pltpu.VMEM((n, D_MODEL), F32),
        ],
        compiler_params=_cp(("arbitrary",)),
        name="ffn_step",
    )(o_a, y_s, x2d, w_out, norm2_w, w_up, w_up, w_down, conv_w, conv_b, fnorm_w,
      buf[:, 0], buf[:, 1])


def _row(v):
    return v.reshape(1, -1).astype(F32)


def _pad_lanes(v):
    return jnp.pad(v.astype(F32), (0, LANES - v.shape[0])).reshape(1, LANES)


def kernel(x_prompt, x_sample, state_hgrn, state_ssm, state_conv_ssm, state_conv_ffn, norm1_w, w_in, hgrn_lb, hgrn_norm_w, ssm_conv_w, ssm_conv_b, ssm_dt_bias, ssm_a_log, ssm_d, ssm_norm_w, w_out, norm2_w, w_up, ffn_conv_w, ffn_conv_b, w_down, final_norm_w):
    depth = w_in.shape[0]
    assert depth == 1, "single-layer trunk"
    l = 0
    batch, seq, _ = x_prompt.shape
    dec_batch, dec_seq, _ = x_sample.shape
    assert dec_seq == 1 and seq % CHUNK == 0 and seq >= SSM_CONV

    w_in_t = w_in[l].T
    d_full = jnp.repeat(ssm_d[l].astype(F32), SSM_HEAD_DIM).reshape(1, D_SSM)
    dt_bias = _pad_lanes(ssm_dt_bias[l])
    a_log = _pad_lanes(ssm_a_log[l])
    mconst = jnp.asarray(_hgrn_const(), BF16)
    tri = jnp.asarray(np.tril(np.ones((CHUNK, CHUNK), np.float32)), BF16)
    expand = jnp.asarray(_head_expand(), BF16)
    lb_raw = hgrn_lb.astype(F32)

    xs_ = x_sample.reshape(dec_batch, D_MODEL)
    w_main, proj_s, lg_s, dt_s = _inproj_cast(xs_, _row(norm1_w[l]), w_in_t, lb_raw, dt_bias)
    oa_s, hgrn_s = _hgrn_step(proj_s, lg_s, _row(hgrn_norm_w[l]), state_hgrn[l])
    ys_s, ssm_s = _ssd_step(proj_s, dt_s, state_conv_ssm[l], ssm_conv_w[l], _row(ssm_conv_b[l]),
                            a_log, d_full, _row(ssm_norm_w[l]), expand, state_ssm[l])
    y_s, gate_s, w_ob, w_gb, w_vb, w_db = _ffn_step(
        oa_s, ys_s, xs_, w_out[l], _row(norm2_w[l]), w_up[l], w_down[l],
        ffn_conv_w[l], _row(ffn_conv_b[l]), _row(final_norm_w), state_conv_ffn[l])
    cs_s = jnp.concatenate([state_conv_ssm[l][:, 1:], proj_s[:, None, OFF_XS:OFF_XS + CONV_DIM]],
                           axis=1)
    cf_s = jnp.concatenate([state_conv_ffn[l][:, 1:], gate_s[:, None, :]], axis=1)

    xp = x_prompt.reshape(batch * seq, D_MODEL)
    proj_p, lg_p, dt_p = _inproj(xp, _row(norm1_w[l]), w_main, w_in_t, lb_raw, dt_bias)
    oa_p, hgrn_p = _hgrn_prompt(proj_p, lg_p, _row(hgrn_norm_w[l]), mconst, batch, seq)
    ys_p, ssm_p = _ssd_prompt(proj_p, dt_p, ssm_conv_w[l], _row(ssm_conv_b[l]), a_log,
                              d_full, _row(ssm_norm_w[l]), tri, expand, batch, seq)
    y_p, tail_p = _ffn_prompt(oa_p, ys_p, xp, w_ob, _row(norm2_w[l]), w_gb, w_vb, w_db,
                              ffn_conv_w[l], _row(ffn_conv_b[l]), _row(final_norm_w), seq)
    proj_p3 = proj_p.reshape(batch, seq, D_MAIN)
    cs_p = proj_p3[:, seq - (SSM_CONV - 1):, OFF_XS:OFF_XS + CONV_DIM]
    tails = tail_p.reshape(batch, seq // FFN_ROW_TILE, SUBLANES, D_FF)
    cf_p = tails[:, -1, SUBLANES - (FFN_CONV - 1):, :]

    dt_ = x_prompt.dtype
    return (y_p.reshape(batch, seq, D_MODEL).astype(dt_),
            y_s.reshape(dec_batch, 1, D_MODEL).astype(dt_),
            hgrn_p[None].astype(dt_),
            hgrn_s[None].astype(dt_),
            ssm_p.reshape(1, batch, SSM_HEADS, SSM_HEAD_DIM, SSM_STATE).astype(dt_),
            ssm_s[None].astype(dt_),
            cs_p[None].astype(dt_),
            cs_s[None].astype(dt_),
            cf_p[None].astype(dt_),
            cf_s[None].astype(dt_))
```

```python
import functools

import numpy as np
import jax
import jax.numpy as jnp
from jax import lax
from jax.experimental import pallas as pl
from jax.experimental.pallas import tpu as pltpu

F32 = jnp.float32
BF16 = jnp.bfloat16
EPS = 1e-6

LANES = 128
SUBLANES = 8

D_MODEL = 1024
HGRN_HEADS = 8
HGRN_DK = 128
HGRN_DV = 128
D_HGRN = HGRN_HEADS * HGRN_DV
SSM_HEADS = 16
SSM_HEAD_DIM = 64
D_SSM = SSM_HEADS * SSM_HEAD_DIM
SSM_STATE = 128
SSM_GROUPS = 2
SSM_CONV = 4
CONV_DIM = D_SSM + 2 * SSM_GROUPS * SSM_STATE
D_FF = 2816
FFN_CONV = 3
D_MAIN = 4 * D_HGRN + D_SSM + CONV_DIM
OFF_Q, OFF_F, OFF_I, OFF_G = 0, 1024, 2048, 3072
OFF_Z, OFF_XS, OFF_BC = 4096, 5120, 6144

CHUNK = 128
GROUP_W = D_SSM // SSM_GROUPS
VMEM_LIMIT = 56 * 1024 * 1024


def _cp(sem):
    return pltpu.CompilerParams(dimension_semantics=sem, vmem_limit_bytes=VMEM_LIMIT)


def _dot(a, b):
    return jnp.dot(a, b, preferred_element_type=F32)


def _dot_nt(a, b):
    return lax.dot_general(a, b, (((1,), (1,)), ((), ())), preferred_element_type=F32)


def _split3(x):
    h = x.astype(BF16)
    r = x - h.astype(F32)
    m = r.astype(BF16)
    lo = (r - m.astype(F32)).astype(BF16)
    return h, m, lo


def _dot_exact_lhs01(m01, x):
    h, m, lo = _split3(x)
    return _dot(m01, h) + _dot(m01, m) + _dot(m01, lo)


def _dot_exact_rhs01(x, m01):
    n = x.shape[0]
    wide = _dot(jnp.concatenate(_split3(x), axis=0), m01)
    return wide[0:n] + wide[n:2 * n] + wide[2 * n:3 * n]


def _dot_split_lhs01(m01, x):
    h = x.astype(BF16)
    lo = (x - h.astype(F32)).astype(BF16)
    return _dot(m01, h) + _dot(m01, lo)


def _sigmoid(x):
    return 1.0 / (1.0 + jnp.exp(-x))


def _silu(x):
    return x * _sigmoid(x)


def _rms(x, w):
    ms = jnp.mean(x * x, axis=-1, keepdims=True)
    return x * lax.rsqrt(ms + EPS) * w


def _dt_proj(hb, wdt_ref):
    rows = lax.broadcasted_iota(jnp.int32, wdt_ref.shape, 0)
    wdt = jnp.where(rows < SSM_HEADS, wdt_ref[...], 0.0).astype(BF16)
    return _dot_nt(hb, wdt)

def _inproj_kernel(x_ref, nw_ref, w_ref, wdt_ref, lb_ref, dtb_ref, mix_ref, lg_ref, dt_ref):
    n_sub = max(1, x_ref.shape[0] // INPROJ_SUB_ROWS)
    sub_rows = x_ref.shape[0] // n_sub
    rs = [slice(s * sub_rows, (s + 1) * sub_rows) for s in range(n_sub)]
    hb = [_rms(x_ref[r, :], nw_ref[...]).astype(BF16) for r in rs]

    def put(r, off, val):
        mix_ref[r, off:off + val.shape[1]] = val.astype(BF16)

    w = D_HGRN
    lb = _hgrn_lb(lb_ref[...])
    for s, r in enumerate(rs):
        qf = _dot(hb[s], w_ref[:, OFF_Q:OFF_Q + 2 * w])
        f = lb + (1.0 - lb) * _sigmoid(qf[:, w:])
        lg_ref[r, :] = jnp.log(f)
        put(r, OFF_F, 1.0 - f)
        put(r, OFF_Q, _silu(qf[:, :w]))
    for s, r in enumerate(rs):
        ig = _dot(hb[s], w_ref[:, OFF_I:OFF_I + 2 * w])
        put(r, OFF_I, ig[:, :w])
        put(r, OFF_G, _silu(ig[:, w:]))
    for s, r in enumerate(rs):
        zx = _dot(hb[s], w_ref[:, OFF_Z:OFF_Z + D_SSM + CONV_DIM])
        put(r, OFF_Z, _silu(zx[:, :D_SSM]))
        put(r, OFF_XS, zx[:, D_SSM:])
    for s, r in enumerate(rs):
        dt_ref[r, :] = _softplus(_dt_proj(hb[s], wdt_ref) + dtb_ref[...])


INPROJ_ROW_TILE = 512
INPROJ_SUB_ROWS = 256


def _inproj(x2d, norm_w, w_main, w_t, lb_raw, dt_bias):
    n = x2d.shape[0]
    tm = min(INPROJ_ROW_TILE, n)
    assert n % tm == 0
    row = lambda w: pl.BlockSpec((tm, w), lambda i: (i, 0))
    resident = lambda shape: pl.BlockSpec(shape, lambda i: (0, 0), pipeline_mode=pl.Buffered(1))
    return pl.pallas_call(
        _inproj_kernel,
        grid=(n // tm,),
        in_specs=[
            row(D_MODEL), resident((1, D_MODEL)),
            resident((D_MODEL, D_MAIN)),
            pl.BlockSpec((LANES, D_MODEL), lambda i: (D_MAIN // LANES, 0), pipeline_mode=pl.Buffered(1)),
            resident(lb_raw.shape), resident((1, LANES)),
        ],
        out_specs=[row(D_MAIN), row(D_HGRN), row(LANES)],
        out_shape=[
            jax.ShapeDtypeStruct((n, D_MAIN), BF16),
            jax.ShapeDtypeStruct((n, D_HGRN), F32),
            jax.ShapeDtypeStruct((n, LANES), F32),
        ],
        compiler_params=_cp(("arbitrary",)),
        name="inproj",
    )(x2d, norm_w, w_main, w_t, lb_raw, dt_bias)


CAST_BLOCK = 512


def _inproj_cast_kernel(x_ref, nw_ref, w_ref, wdt_ref, lb_ref, dtb_ref,
                        wb_ref, mix_ref, lg_ref, dt_ref, h_scr):
    j = pl.program_id(0)
    blk = CAST_BLOCK
    q0, f0, i0, g0, z0, x0 = (off // blk for off in (OFF_Q, OFF_F, OFF_I, OFF_G, OFF_Z, OFF_XS))

    @pl.when(j == 0)
    def _():
        hb = _rms(x_ref[...], nw_ref[...]).astype(BF16)
        h_scr[...] = hb
        dt_ref[...] = _softplus(_dt_proj(hb, wdt_ref) + dtb_ref[...])

    wb = w_ref[...].T.astype(BF16)
    wb_ref[...] = wb
    p = _dot(h_scr[...], wb)

    @pl.when(((j >= q0) & (j < f0)) | ((j >= g0) & (j < x0)))
    def _():
        mix_ref[...] = _silu(p).astype(BF16)

    @pl.when(((j >= i0) & (j < g0)) | (j >= x0))
    def _():
        mix_ref[...] = p.astype(BF16)

    for fj in range(f0, i0):
        @pl.when(j == fj)
        def _(fj=fj):
            cols = slice((fj - f0) * blk, (fj - f0 + 1) * blk)
            lb = _hgrn_lb(lb_ref[:, cols])
            f = lb + (1.0 - lb) * _sigmoid(p)
            lg_ref[:, cols] = jnp.log(f)
            mix_ref[...] = (1.0 - f).astype(BF16)


def _inproj_cast(x2d, norm_w, w_t, lb_raw, dt_bias):
    n = x2d.shape[0]
    blk = CAST_BLOCK
    const = lambda shape: pl.BlockSpec(shape, lambda j: (0, 0))
    return pl.pallas_call(
        _inproj_cast_kernel,
        grid=(D_MAIN // blk,),
        in_specs=[
            const((n, D_MODEL)), const((1, D_MODEL)),
            pl.BlockSpec((blk, D_MODEL), lambda j: (j, 0)),
            pl.BlockSpec((LANES, D_MODEL), lambda j: (D_MAIN // LANES, 0)),
            const(lb_raw.shape), const((1, LANES)),
        ],
        out_specs=[
            pl.BlockSpec((D_MODEL, blk), lambda j: (0, j)),
            pl.BlockSpec((n, blk), lambda j: (0, j)),
            const((n, D_HGRN)), const((n, LANES)),
        ],
        out_shape=[
            jax.ShapeDtypeStruct((D_MODEL, D_MAIN), BF16),
            jax.ShapeDtypeStruct((n, D_MAIN), BF16),
            jax.ShapeDtypeStruct((n, D_HGRN), F32),
            jax.ShapeDtypeStruct((n, LANES), F32),
        ],
        scratch_shapes=[pltpu.VMEM((n, D_MODEL), BF16)],
        compiler_params=_cp(("arbitrary",)),
        name="inproj_cast",
    )(x2d, norm_w, w_t, w_t, lb_raw, dt_bias)


LOG2E = 1.4426950408889634
MXU_LEVEL_HALVES = (4, 2)


def _hgrn_const():
    c = CHUNK
    t = np.arange(c)[:, None]
    j = np.arange(c)[None, :]
    blocks = [(j <= t)]
    for h in MXU_LEVEL_HALVES:
        mid = (t // (2 * h)) * (2 * h) + h
        upper = (t >= mid) & (j >= mid) & (j <= t)
        lower = (t < mid) & (j > t) & (j < mid)
        blocks.append(upper | lower)
    return np.concatenate(blocks, axis=0).astype(np.float32)


def _midpoint_decay(b, h):
    pieces = []
    for start in range(0, CHUNK, 2 * h):
        mid = start + h
        m = b[mid - 1:mid, :]
        pieces.append(m - b[start:mid])
        pieces.append(b[mid:mid + h] - m)
    return jnp.concatenate(pieces, axis=0)


def _mix_rows(q, k, h):
    pieces = []
    for start in range(0, CHUNK, 2 * h):
        pieces.append(k[start:start + h])
        pieces.append(q[start + h:start + 2 * h])
    return jnp.concatenate(pieces, axis=0)


def _hgrn_lb(lb_raw):
    mx = jnp.max(lb_raw, axis=0, keepdims=True)
    e = jnp.exp(lb_raw - mx)
    return e[0:1, :] / jnp.sum(e, axis=0, keepdims=True)


def _level_map():
    t = lax.broadcasted_iota(jnp.int32, (CHUNK, CHUNK), 0)
    s = lax.broadcasted_iota(jnp.int32, (CHUNK, CHUNK), 1)
    bitlen = 32 - lax.clz(t ^ s)
    return jnp.where(t > s, bitlen, jnp.where(t == s, 0, -1))


def _hgrn_prompt_kernel(q_ref, k_ref, i_ref, g_ref, lg_ref, nw_ref, mc_ref,
                        o_ref, s_out_ref, st_scr):
    c = pl.program_id(1)

    @pl.when(c == 0)
    def _():
        st_scr[...] = jnp.zeros_like(st_scr)

    n_sub = q_ref.shape[0] // CHUNK
    lev = _level_map().astype(jnp.int16)
    row = lax.broadcasted_iota(jnp.int32, (CHUNK, HGRN_DK), 0)
    heads = range(HGRN_HEADS)
    pairs = [(s, h) for s in range(n_sub) for h in heads]
    rs = {s: slice(s * CHUNK, (s + 1) * CHUNK) for s in range(n_sub)}
    cs = {h: slice(h * HGRN_DK, (h + 1) * HGRN_DK) for h in heads}

    e_sub = {s: _dot_split_lhs01(mc_ref[...], lg_ref[rs[s], :] * LOG2E) for s in range(n_sub)}
    qb = {(s, h): q_ref[rs[s], cs[h]] for s, h in pairs}
    kb = {(s, h): k_ref[rs[s], cs[h]] for s, h in pairs}
    vb = {(s, h): i_ref[rs[s], cs[h]] for s, h in pairs}
    q = {p: qb[p].astype(F32) for p in pairs}
    k = {p: kb[p].astype(F32) for p in pairs}
    b = {(s, h): e_sub[s][0:CHUNK, cs[h]] for s, h in pairs}
    b_last = {p: b[p][CHUNK - 1:CHUNK, :] for p in pairs}

    st = {h: st_scr[h] for h in heads}
    o = {}
    for s, h in pairs:
        p = (s, h)
        o[p] = _dot((q[p] * jnp.exp2(b[p])).astype(BF16), st[h].T.astype(BF16))
        ks = (k[p] * jnp.exp2(b_last[p] - b[p])).astype(BF16)
        st[h] = st[h] * jnp.exp2(b_last[p]) + _dot(vb[p].astype(F32).T.astype(BF16), ks)
    for h in heads:
        st_scr[h] = st[h]

    a = {p: jnp.where(lev == 0, _dot(qb[p], k[p].T.astype(BF16)).astype(BF16), jnp.zeros((), BF16))
         for p in pairs}
    half = CHUNK // 2
    while half >= 1:
        for p in pairs:
            if half >= SUBLANES:
                x = _mix_rows(q[p], k[p], half) * jnp.exp2(_midpoint_decay(b[p], half))
            else:
                upper = (row & half) != 0
                if half in MXU_LEVEL_HALVES:
                    blk = 1 + MXU_LEVEL_HALVES.index(half)
                    w = jnp.exp2(e_sub[p[0]][blk * CHUNK:(blk + 1) * CHUNK, cs[p[1]]])
                    x = jnp.where(upper, q[p], k[p]) * w
                else:
                    x = jnp.where(upper, q[p] * (1.0 - k[p]), k[p])
            gram = _dot(x.astype(BF16), x.T.astype(BF16))
            a[p] = jnp.where(lev == half.bit_length(), gram.astype(BF16), a[p])
        half //= 2

    for p in pairs:
        o[p] = o[p] + _dot(a[p], vb[p])
    for s, h in pairs:
        gate = g_ref[rs[s], cs[h]].astype(F32)
        o_ref[rs[s], cs[h]] = (_rms(o[(s, h)], nw_ref[...]) * gate).astype(BF16)

    @pl.when(c == pl.num_programs(1) - 1)
    def _():
        for h in range(HGRN_HEADS):
            s_out_ref[0, h] = st_scr[h].T


HGRN_SUBCHUNKS = 4


def _hgrn_prompt(mix, lg, norm_w, mconst, batch, seq):
    rows = HGRN_SUBCHUNKS * CHUNK
    assert seq % rows == 0
    nc = seq // rows

    def col(off):
        return pl.BlockSpec((rows, D_HGRN), lambda b, c: (b * nc + c, off // D_HGRN))

    return pl.pallas_call(
        _hgrn_prompt_kernel,
        grid=(batch, nc),
        in_specs=[
            col(OFF_Q), col(OFF_F), col(OFF_I), col(OFF_G), col(0),
            pl.BlockSpec((1, HGRN_DV), lambda b, c: (0, 0)),
            pl.BlockSpec(mconst.shape, lambda b, c: (0, 0)),
        ],
        out_specs=[
            pl.BlockSpec((rows, D_HGRN), lambda b, c: (b * nc + c, 0)),
            pl.BlockSpec((1, HGRN_HEADS, HGRN_DK, HGRN_DV), lambda b, c: (b, 0, 0, 0)),
        ],
        out_shape=[
            jax.ShapeDtypeStruct((batch * seq, D_HGRN), BF16),
            jax.ShapeDtypeStruct((batch, HGRN_HEADS, HGRN_DK, HGRN_DV), F32),
        ],
        scratch_shapes=[pltpu.VMEM((HGRN_HEADS, HGRN_DV, HGRN_DK), F32)],
        compiler_params=_cp(("arbitrary", "arbitrary")),
        name="hgrn_prompt",
    )(mix, mix, mix, mix, lg, norm_w, mconst)


def _hgrn_step_kernel(q_ref, i_ref, g_ref, lg_ref, nw_ref, s_ref,
                      o_ref, s_out_ref, o_scr):
    nb = q_ref.shape[0]
    qb = q_ref[...]
    f_t = jnp.exp(lg_ref[...]).T
    v = i_ref[...].astype(F32)
    lhs_rows = 2 * SUBLANES
    for t in range(nb):
        v_row = v[t:t + 1, :]
        s_new = v_row + f_t[:, t:t + 1] * (s_ref[t, 0] - v_row)
        s_out_ref[t, 0] = s_new
        q_rows = jnp.broadcast_to(qb[t:t + 1, :], (lhs_rows, HGRN_DK))
        o_scr[t:t + 1, :] = _dot(q_rows, s_new.astype(BF16))[0:1, :]
    o_ref[...] = (_rms(o_scr[...], nw_ref[...]) * g_ref[...].astype(F32)).astype(BF16)


def _hgrn_step(mix, lg, norm_w, state):
    nb = state.shape[0]
    hb = lambda off: off // HGRN_DK

    def col(off):
        return pl.BlockSpec((nb, HGRN_DK), lambda h: (0, hb(off) + h))

    st_spec = pl.BlockSpec((nb, 1, HGRN_DK, HGRN_DV), lambda h: (0, h, 0, 0))
    return pl.pallas_call(
        _hgrn_step_kernel,
        grid=(HGRN_HEADS,),
        in_specs=[
            col(OFF_Q), col(OFF_I), col(OFF_G), col(0),
            pl.BlockSpec((1, HGRN_DV), lambda h: (0, 0)),
            st_spec,
        ],
        out_specs=[pl.BlockSpec((nb, HGRN_DV), lambda h: (0, h)), st_spec],
        out_shape=[
            jax.ShapeDtypeStruct((nb, D_HGRN), BF16),
            jax.ShapeDtypeStruct(state.shape, F32),
        ],
        scratch_shapes=[pltpu.VMEM((nb, HGRN_DV), F32)],
        compiler_params=_cp(("arbitrary",)),
        name="hgrn_step",
    )(mix, mix, mix, lg, norm_w, state)


def _head_expand(width=SSM_HEAD_DIM):
    e = np.zeros((LANES, SSM_HEADS * width), np.float32)
    for h in range(SSM_HEADS):
        e[h, h * width:(h + 1) * width] = 1.0
    return e


def _softplus(x):
    return jnp.maximum(x, 0.0) + jnp.log(1.0 + jnp.exp(-jnp.abs(x)))


def _ssm_gate_norm(y, z_gate, nw):
    y = y * z_gate.astype(F32)
    parts = [_rms(y[:, g * GROUP_W:(g + 1) * GROUP_W], nw[:, g * GROUP_W:(g + 1) * GROUP_W])
             for g in range(SSM_GROUPS)]
    return jnp.concatenate(parts, axis=-1)


def _ssd_prompt_kernel(z_ref, xs_ref, bc_ref, dt_ref, cw_ref, cb_ref, alog_ref,
                       dvec_ref, nw_ref, tri_ref, exp_ref,
                       shift_ref, y_ref, st_out_ref, xprev_scr, st_scr):
    c = pl.program_id(1)
    t = CHUNK

    @pl.when(c == 0)
    def _():
        st_scr[...] = jnp.zeros_like(st_scr)
        xprev_scr[...] = jnp.zeros_like(xprev_scr)

    subs = range(xs_ref.shape[0] // t)
    rs = [slice(s * t, (s + 1) * t) for s in subs]

    x_cur = [jnp.concatenate([xs_ref[r, :], bc_ref[r, :]], axis=-1) for r in rs]
    x_prev = [xprev_scr[...]] + x_cur[:-1]
    xprev_scr[...] = x_cur[-1]
    taps = [_dot(shift_ref[...], jnp.concatenate([x_prev[s], x_cur[s]], axis=0)) for s in subs]
    xbc = []
    for s in subs:
        acc = cb_ref[...] + cw_ref[SSM_CONV - 1:SSM_CONV, :] * x_cur[s].astype(F32)
        for d in range(1, SSM_CONV):
            acc = acc + cw_ref[SSM_CONV - 1 - d:SSM_CONV - d, :] * taps[s][(d - 1) * t:d * t, :]
        xbc.append(_silu(acc))
    xs = [x[:, 0:D_SSM] for x in xbc]

    dt = [dt_ref[r, :] for r in rs]
    neg_a = -LOG2E * jnp.exp(alog_ref[...])
    cs = [_dot_exact_lhs01(tri_ref[...], dt[s] * neg_a) for s in subs]
    ex = exp_ref[...]
    dt_full = [_dot_exact_rhs01(dt[s], ex) for s in subs]
    cs_full = [_dot_exact_rhs01(cs[s], ex) for s in subs]
    cs_last_full = [x[t - 1:t, :] for x in cs_full]
    x_dt = [xs[s] * dt_full[s] for s in subs]
    x_end = [(x_dt[s] * jnp.exp2(cs_last_full[s] - cs_full[s])).astype(BF16) for s in subs]
    cs_t = [x.T for x in cs]

    causal = (lax.broadcasted_iota(jnp.int32, (t, t), 0)
              >= lax.broadcasted_iota(jnp.int32, (t, t), 1))
    lane = lax.broadcasted_iota(jnp.int32, (1, D_SSM), 1)
    odd_head = (lane & SSM_HEAD_DIM) != 0
    x_by_parity = []
    for s in subs:
        x_b = x_dt[s].astype(BF16)
        zero = jnp.zeros_like(x_b)
        x_by_parity.append((jnp.where(odd_head, zero, x_b), jnp.where(odd_head, x_b, zero)))
    heads_per_group = SSM_HEADS // SSM_GROUPS
    pair_w = 2 * SSM_HEAD_DIM
    never = -1e30

    def group_bc(s, g):
        b_g = xbc[s][:, D_SSM + g * SSM_STATE:D_SSM + (g + 1) * SSM_STATE]
        c_off = D_SSM + SSM_GROUPS * SSM_STATE + g * SSM_STATE
        return b_g.T.astype(BF16), xbc[s][:, c_off:c_off + SSM_STATE].astype(BF16)

    bc_t = {(s, g): group_bc(s, g) for s in subs for g in range(SSM_GROUPS)}

    y_diag = []
    for s in subs:
        y_parts = []
        for g in range(SSM_GROUPS):
            b_t, c_g = bc_t[(s, g)]
            gmat = _dot(c_g, b_t)
            for pp in range(heads_per_group // 2):
                h0 = g * heads_per_group + 2 * pp
                psl = slice(h0 * SSM_HEAD_DIM, h0 * SSM_HEAD_DIM + pair_w)
                yp = None
                for sub in range(2):
                    h = h0 + sub
                    diff = cs[s][:, h:h + 1] - cs_t[s][h:h + 1, :]
                    w = jnp.exp2(jnp.where(causal, diff, never)) * gmat
                    part = _dot(w.astype(BF16), x_by_parity[s][sub][:, psl])
                    yp = part if yp is None else yp + part
                y_parts.append(yp)
        y_diag.append(jnp.concatenate(y_parts, axis=-1))

    y_off = []
    for s in subs:
        offs = []
        for g in range(SSM_GROUPS):
            sl = slice(g * GROUP_W, (g + 1) * GROUP_W)
            b_t, c_g = bc_t[(s, g)]
            st_g = st_scr[:, sl]
            offs.append(_dot(c_g, st_g.astype(BF16)))
            st_scr[:, sl] = st_g * jnp.exp2(cs_last_full[s][:, sl]) + _dot(b_t, x_end[s][:, sl])
        y_off.append(jnp.concatenate(offs, axis=-1) * jnp.exp2(cs_full[s]))

    for s in subs:
        y = y_diag[s] + y_off[s] + dvec_ref[...] * xs[s]
        y_ref[rs[s], :] = _ssm_gate_norm(y, z_ref[rs[s], :], nw_ref[...]).astype(BF16)

    @pl.when(c == pl.num_programs(1) - 1)
    def _():
        for j in range(D_SSM // LANES):
            st_out_ref[0, j * LANES:(j + 1) * LANES, :] = st_scr[:, j * LANES:(j + 1) * LANES].T


SSD_SUBCHUNKS = 1


def _ssd_prompt(mix, dt, conv_w, conv_b, a_log, d_full, norm_w, tri, expand, batch, seq):
    rows = SSD_SUBCHUNKS * CHUNK
    assert seq % rows == 0
    nc = seq // rows
    const = lambda shape: pl.BlockSpec(shape, lambda b, c: (0, 0))
    return pl.pallas_call(
        _ssd_prompt_kernel,
        grid=(batch, nc),
        in_specs=[
            pl.BlockSpec((rows, D_SSM), lambda b, c: (b * nc + c, OFF_Z // D_SSM)),
            pl.BlockSpec((rows, D_SSM), lambda b, c: (b * nc + c, OFF_XS // D_SSM)),
            pl.BlockSpec((rows, 512), lambda b, c: (b * nc + c, OFF_BC // 512)),
            pl.BlockSpec((rows, LANES), lambda b, c: (b * nc + c, 0)),
            const((SSM_CONV, CONV_DIM)), const((1, CONV_DIM)),
            const((1, LANES)),
            const((1, D_SSM)), const((1, D_SSM)),
            const((CHUNK, CHUNK)), const((LANES, D_SSM)),
            const(((SSM_CONV - 1) * CHUNK, 2 * CHUNK)),
        ],
        out_specs=[
            pl.BlockSpec((rows, D_SSM), lambda b, c: (b * nc + c, 0)),
            pl.BlockSpec((1, D_SSM, SSM_STATE), lambda b, c: (b, 0, 0)),
        ],
        out_shape=[
            jax.ShapeDtypeStruct((batch * seq, D_SSM), BF16),
            jax.ShapeDtypeStruct((batch, D_SSM, SSM_STATE), F32),
        ],
        scratch_shapes=[
            pltpu.VMEM((CHUNK, CONV_DIM), BF16),
            pltpu.VMEM((SSM_STATE, D_SSM), F32),
        ],
        compiler_params=_cp(("arbitrary", "arbitrary")),
        name="ssd_prompt",
    )(mix, mix, mix, dt, conv_w, conv_b, a_log, d_full, norm_w, tri, expand,
      jnp.asarray(_conv_shifts(), BF16))


def _conv_shifts():
    m = np.zeros(((SSM_CONV - 1) * CHUNK, 2 * CHUNK), np.float32)
    for d in range(1, SSM_CONV):
        for t in range(CHUNK):
            m[(d - 1) * CHUNK + t, CHUNK + t - d] = 1.0
    return m


def _ssd_step_kernel(z_ref, xs_ref, bc_ref, dt_ref, b0_ref, b1_ref, b2_ref, cw_ref, cb_ref,
                     alog_ref, dvec_ref, nw_ref, exp_ref, exl_ref, st_ref,
                     y_ref, st_out_ref, xt_scr, at_scr, xs_scr, bc_scr, y_scr):
    p = pl.program_id(0)
    nb = z_ref.shape[0]
    pair_w = 2 * SSM_HEAD_DIM
    pairs_per_group = SSM_HEADS // SSM_GROUPS // 2

    @pl.when(p == 0)
    def _():
        x_new = jnp.concatenate([xs_ref[...], bc_ref[...]], axis=-1).astype(F32)
        acc = (cb_ref[...] + cw_ref[0:1, :] * b0_ref[...] + cw_ref[1:2, :] * b1_ref[...]
               + cw_ref[2:3, :] * b2_ref[...] + cw_ref[3:4, :] * x_new)
        xbc = _silu(acc)
        xs = xbc[:, 0:D_SSM]
        dt = dt_ref[...]
        da = dt * (-jnp.exp(alog_ref[...]))
        ex = exp_ref[...]
        x_dt = xs * _dot_exact_rhs01(dt, ex)
        decay = jnp.exp(_dot_exact_rhs01(da, exl_ref[...]))
        xs_scr[...] = xs
        bc_scr[...] = xbc[:, D_SSM:]
        for j in range(D_SSM // LANES):
            sl = slice(j * LANES, (j + 1) * LANES)
            xt_scr[sl, :] = x_dt[:, sl].T
            at_scr[j] = decay[:, 2 * j * LANES:2 * (j + 1) * LANES]

    g_is_1 = p >= pairs_per_group
    row0 = pl.multiple_of(p * pair_w, pair_w)
    x_t = xt_scr[pl.ds(row0, pair_w), :]
    a_p = at_scr[p]
    bc = bc_scr[...]
    b_all = jnp.where(g_is_1, bc[:, SSM_STATE:2 * SSM_STATE], bc[:, 0:SSM_STATE])
    c_all = jnp.where(g_is_1, bc[:, 3 * SSM_STATE:4 * SSM_STATE],
                      bc[:, 2 * SSM_STATE:3 * SSM_STATE]).astype(BF16)
    for t in range(nb):
        inject = x_t[:, t:t + 1] * b_all[t:t + 1, :]
        halves = []
        for sub in range(2):
            rows = slice(sub * SSM_HEAD_DIM, (sub + 1) * SSM_HEAD_DIM)
            half = a_p[t:t + 1, sub * LANES:(sub + 1) * LANES] * st_ref[t, sub] + inject[rows]
            st_out_ref[t, sub] = half
            halves.append(half)
        new = jnp.concatenate(halves, axis=0)
        c_rows = jnp.broadcast_to(c_all[t:t + 1, :], (SUBLANES, SSM_STATE))
        y_scr[p, t:t + 1, :] = _dot_nt(c_rows, new.astype(BF16))[0:1, :]

    @pl.when(p == pl.num_programs(0) - 1)
    def _():
        y_mix = jnp.concatenate([y_scr[j] for j in range(SSM_HEADS // 2)], axis=-1)
        y = y_mix + dvec_ref[...] * xs_scr[...]
        y_ref[...] = _ssm_gate_norm(y, z_ref[...], nw_ref[...]).astype(BF16)


def _ssd_step(mix, dt, buf, conv_w, conv_b, a_log, d_full, norm_w, expand, state):
    nb = state.shape[0]
    n_pairs = SSM_HEADS // 2
    const = lambda shape: pl.BlockSpec(shape, lambda p: (0, 0))
    st_spec = pl.BlockSpec((nb, 2, SSM_HEAD_DIM, SSM_STATE), lambda p: (0, p, 0, 0))
    return pl.pallas_call(
        _ssd_step_kernel,
        grid=(n_pairs,),
        in_specs=[
            pl.BlockSpec((nb, D_SSM), lambda p: (0, OFF_Z // D_SSM)),
            pl.BlockSpec((nb, D_SSM), lambda p: (0, OFF_XS // D_SSM)),
            pl.BlockSpec((nb, 512), lambda p: (0, OFF_BC // 512)),
            const((nb, LANES)),
            const((nb, CONV_DIM)), const((nb, CONV_DIM)), const((nb, CONV_DIM)),
            const((SSM_CONV, CONV_DIM)), const((1, CONV_DIM)),
            const((1, LANES)),
            const((1, D_SSM)), const((1, D_SSM)),
            const((LANES, D_SSM)), const((LANES, SSM_HEADS * LANES)),
            st_spec,
        ],
        out_specs=[const((nb, D_SSM)), st_spec],
        out_shape=[
            jax.ShapeDtypeStruct((nb, D_SSM), BF16),
            jax.ShapeDtypeStruct(state.shape, F32),
        ],
        scratch_shapes=[
            pltpu.VMEM((D_SSM, nb), F32),
            pltpu.VMEM((n_pairs, nb, 2 * LANES), F32),
            pltpu.VMEM((nb, D_SSM), F32),
            pltpu.VMEM((nb, 2 * SSM_GROUPS * SSM_STATE), F32),
            pltpu.VMEM((n_pairs, nb, 2 * SSM_HEAD_DIM), F32),
        ],
        compiler_params=_cp(("arbitrary",)),
        name="ssd_step",
    )(mix, mix, mix, dt, buf[:, 0], buf[:, 1], buf[:, 2], conv_w, conv_b, a_log,
      d_full, norm_w, expand, jnp.asarray(_head_expand(LANES), BF16), state)


def _ffn_prompt_kernel(oa_ref, ys_ref, x_ref, wo_ref, n2_ref, wg_ref, wv_ref, wd_ref, cw_ref, cb_ref,
                       fnw_ref, y_ref, tail_ref, ge_scr, *, tiles_per_seq):
    i = pl.program_id(0)
    tm = x_ref.shape[0]
    pad = SUBLANES
    seq_start = lax.rem(i, tiles_per_seq) == 0

    @pl.when(seq_start)
    def _():
        ge_scr[0:pad, :] = jnp.zeros((pad, D_FF), F32)

    @pl.when(jnp.logical_not(seq_start))
    def _():
        ge_scr[0:pad, :] = ge_scr[tm:tm + pad, :]

    n_sub = max(1, tm // FFN_SUB_ROWS)
    sub = tm // n_sub
    rs = [slice(s * sub, (s + 1) * sub) for s in range(n_sub)]
    x1 = [x_ref[r, :] + _dot(oa_ref[r, :], wo_ref[0:D_HGRN, :])
          + _dot(ys_ref[r, :], wo_ref[D_HGRN:D_HGRN + D_SSM, :]) for r in rs]
    h2 = [_rms(x, n2_ref[...]).astype(BF16) for x in x1]

    acc = [None] * n_sub
    bounds = np.cumsum((0,) + FFN_COL_BLOCKS)
    for c0, c1 in zip(bounds[:-1].tolist(), bounds[1:].tolist()):
        for s, r in enumerate(rs):
            gate = _dot(h2[s], wg_ref[:, c0:c1])
            val = _dot(h2[s], wv_ref[:, c0:c1])
            ge_scr[pad + r.start:pad + r.stop, c0:c1] = gate
            if s == n_sub - 1:
                tail_ref[0, :, c0:c1] = gate[sub - pad:, :]
            conv = (cb_ref[:, c0:c1] + cw_ref[2:3, c0:c1] * gate
                    + cw_ref[1:2, c0:c1] * ge_scr[pad - 1 + r.start:pad - 1 + r.stop, c0:c1]
                    + cw_ref[0:1, c0:c1] * ge_scr[pad - 2 + r.start:pad - 2 + r.stop, c0:c1])
            act = (_silu(conv) * val).astype(BF16)
            part = _dot(act, wd_ref[c0:c1, :])
            acc[s] = part if acc[s] is None else acc[s] + part
    for s, r in enumerate(rs):
        y_ref[r, :] = _rms(x1[s] + acc[s], fnw_ref[...])


FFN_ROW_TILE = 512
FFN_SUB_ROWS = 256
FFN_COL_BLOCKS = (1024, 1024, 768)
assert sum(FFN_COL_BLOCKS) == D_FF and all(c % LANES == 0 for c in FFN_COL_BLOCKS)


def _ffn_prompt(o_a, y_s, x2d, w_o, norm2_w, w_gate, w_val, w_down, conv_w, conv_b, fnorm_w, seq):
    n = x2d.shape[0]
    tm = FFN_ROW_TILE
    assert seq % tm == 0
    kern = functools.partial(_ffn_prompt_kernel, tiles_per_seq=seq // tm)
    row = lambda w: pl.BlockSpec((tm, w), lambda i: (i, 0))
    resident = lambda shape: pl.BlockSpec(shape, lambda i: (0, 0), pipeline_mode=pl.Buffered(1))
    return pl.pallas_call(
        kern,
        grid=(n // tm,),
        in_specs=[
            row(D_HGRN), row(D_SSM), row(D_MODEL),
            resident((D_HGRN + D_SSM, D_MODEL)), resident((1, D_MODEL)),
            resident((D_MODEL, D_FF)), resident((D_MODEL, D_FF)), resident((D_FF, D_MODEL)),
            resident((FFN_CONV, D_FF)), resident((1, D_FF)), resident((1, D_MODEL)),
        ],
        out_specs=[
            row(D_MODEL),
            pl.BlockSpec((1, SUBLANES, D_FF), lambda i: (i, 0, 0)),
        ],
        out_shape=[
            jax.ShapeDtypeStruct((n, D_MODEL), F32),
            jax.ShapeDtypeStruct((n // tm, SUBLANES, D_FF), F32),
        ],
        scratch_shapes=[pltpu.VMEM((tm + SUBLANES, D_FF), F32)],
        compiler_params=_cp(("arbitrary",)),
        name="ffn_prompt",
    )(o_a, y_s, x2d, w_o, norm2_w, w_gate, w_val, w_down, conv_w, conv_b, fnorm_w)


FF_CAST_BLOCK = 256


def _ffn_step_kernel(oa_ref, ys_ref, x_ref, wo_ref, n2_ref, wg_ref, wv_ref, wd_ref, cw_ref, cb_ref,
                     fnw_ref, b0_ref, b1_ref,
                     y_ref, gate_ref, wob_ref, wgb_ref, wvb_ref, wdb_ref, x1_scr, h2_scr, acc_scr):
    j = pl.program_id(0)

    @pl.when(j == 0)
    def _():
        wo = wo_ref[...].astype(BF16)
        wob_ref[...] = wo
        x1 = (x_ref[...] + _dot(oa_ref[...], wo[0:D_HGRN, :]) + _dot(ys_ref[...], wo[D_HGRN:, :]))
        x1_scr[...] = x1
        h2_scr[...] = _rms(x1, n2_ref[...]).astype(BF16)
        acc_scr[...] = jnp.zeros_like(acc_scr)

    wg = wg_ref[...].astype(BF16)
    wv = wv_ref[...].astype(BF16)
    wd = wd_ref[...].astype(BF16)
    wgb_ref[...] = wg
    wvb_ref[...] = wv
    wdb_ref[...] = wd
    h2 = h2_scr[...]
    gate = _dot(h2, wg)
    val = _dot(h2, wv)
    gate_ref[...] = gate
    conv = (cb_ref[...] + cw_ref[2:3, :] * gate + cw_ref[1:2, :] * b1_ref[...]
            + cw_ref[0:1, :] * b0_ref[...])
    act = (_silu(conv) * val).astype(BF16)
    acc_scr[...] = acc_scr[...] + _dot(act, wd)

    @pl.when(j == pl.num_programs(0) - 1)
    def _():
        y_ref[...] = _rms(x1_scr[...] + acc_scr[...], fnw_ref[...])


def _ffn_step(o_a, y_s, x2d, w_out, norm2_w, w_up, w_down, conv_w, conv_b, fnorm_w, buf):
    n = x2d.shape[0]
    blk = FF_CAST_BLOCK
    nj = D_FF // blk
    const = lambda shape: pl.BlockSpec(shape, lambda j: (0, 0))
    col = lambda rows: pl.BlockSpec((rows, blk), lambda j: (0, j))
    return pl.pallas_call(
        _ffn_step_kernel,
        grid=(nj,),
        in_specs=[
            const((n, D_HGRN)), const((n, D_SSM)), const((n, D_MODEL)),
            const((D_HGRN + D_SSM, D_MODEL)), const((1, D_MODEL)),
            col(D_MODEL), pl.BlockSpec((D_MODEL, blk), lambda j: (0, nj + j)),
            pl.BlockSpec((blk, D_MODEL), lambda j: (j, 0)),
            col(FFN_CONV), col(1), const((1, D_MODEL)), col(n), col(n),
        ],
        out_specs=[
            const((n, D_MODEL)), col(n),
            const((D_HGRN + D_SSM, D_MODEL)), col(D_MODEL), col(D_MODEL),
            pl.BlockSpec((blk, D_MODEL), lambda j: (j, 0)),
        ],
        out_shape=[
            jax.ShapeDtypeStruct((n, D_MODEL), F32),
            jax.ShapeDtypeStruct((n, D_FF), F32),
            jax.ShapeDtypeStruct((D_HGRN + D_SSM, D_MODEL), BF16),
            jax.ShapeDtypeStruct((D_MODEL, D_FF), BF16),
            jax.ShapeDtypeStruct((D_MODEL, D_FF), BF16),
            jax.ShapeDtypeStruct((D_FF, D_MODEL), BF16),
        ],
        scratch_shapes=[
            pltpu.VMEM((n, D_MODEL), F32),
            pltpu.VMEM((n, D_MODEL), BF16),
            pltpu.VMEM((n, D_MODEL), F32),
        ],
        compiler_params=_cp(("arbitrary",)),
        name="ffn_step",
    )(o_a, y_s, x2d, w_out, norm2_w, w_up, w_up, w_down, conv_w, conv_b, fnorm_w,
      buf[:, 0], buf[:, 1])


def _row(v):
    return v.reshape(1, -1).astype(F32)


def _pad_lanes(v):
    return jnp.pad(v.astype(F32), (0, LANES - v.shape[0])).reshape(1, LANES)


def kernel(x_prompt, x_sample, state_hgrn, state_ssm, state_conv_ssm, state_conv_ffn, norm1_w, w_in, hgrn_lb, hgrn_norm_w, ssm_conv_w, ssm_conv_b, ssm_dt_bias, ssm_a_log, ssm_d, ssm_norm_w, w_out, norm2_w, w_up, ffn_conv_w, ffn_conv_b, w_down, final_norm_w):
    depth = w_in.shape[0]
    assert depth == 1, "single-layer trunk"
    l = 0
    batch, seq, _ = x_prompt.shape
    dec_batch, dec_seq, _ = x_sample.shape
    assert dec_seq == 1 and seq % CHUNK == 0 and seq >= SSM_CONV

    w_in_t = w_in[l].T
    d_full = jnp.repeat(ssm_d[l].astype(F32), SSM_HEAD_DIM).reshape(1, D_SSM)
    dt_bias = _pad_lanes(ssm_dt_bias[l])
    a_log = _pad_lanes(ssm_a_log[l])
    mconst = jnp.asarray(_hgrn_const(), BF16)
    tri = jnp.asarray(np.tril(np.ones((CHUNK, CHUNK), np.float32)), BF16)
    expand = jnp.asarray(_head_expand(), BF16)
    lb_raw = hgrn_lb.astype(F32)

    xs_ = x_sample.reshape(dec_batch, D_MODEL)
    w_main, proj_s, lg_s, dt_s = _inproj_cast(xs_, _row(norm1_w[l]), w_in_t, lb_raw, dt_bias)
    oa_s, hgrn_s = _hgrn_step(proj_s, lg_s, _row(hgrn_norm_w[l]), state_hgrn[l])
    ys_s, ssm_s = _ssd_step(proj_s, dt_s, state_conv_ssm[l], ssm_conv_w[l], _row(ssm_conv_b[l]),
                            a_log, d_full, _row(ssm_norm_w[l]), expand, state_ssm[l])
    y_s, gate_s, w_ob, w_gb, w_vb, w_db = _ffn_step(
        oa_s, ys_s, xs_, w_out[l], _row(norm2_w[l]), w_up[l], w_down[l],
        ffn_conv_w[l], _row(ffn_conv_b[l]), _row(final_norm_w), state_conv_ffn[l])
    cs_s = jnp.concatenate([state_conv_ssm[l][:, 1:], proj_s[:, None, OFF_XS:OFF_XS + CONV_DIM]],
                           axis=1)
    cf_s = jnp.concatenate([state_conv_ffn[l][:, 1:], gate_s[:, None, :]], axis=1)

    xp = x_prompt.reshape(batch * seq, D_MODEL)
    proj_p, lg_p, dt_p = _inproj(xp, _row(norm1_w[l]), w_main, w_in_t, lb_raw, dt_bias)
    oa_p, hgrn_p = _hgrn_prompt(proj_p, lg_p, _row(hgrn_norm_w[l]), mconst, batch, seq)
    ys_p, ssm_p = _ssd_prompt(proj_p, dt_p, ssm_conv_w[l], _row(ssm_conv_b[l]), a_log,
                              d_full, _row(ssm_norm_w[l]), tri, expand, batch, seq)
    y_p, tail_p = _ffn_prompt(oa_p, ys_p, xp, w_ob, _row(norm2_w[l]), w_gb, w_vb, w_db,
                              ffn_conv_w[l], _row(ffn_conv_b[l]), _row(final_norm_w), seq)
    proj_p3 = proj_p.reshape(batch, seq, D_MAIN)
    cs_p = proj_p3[:, seq - (SSM_CONV - 1):, OFF_XS:OFF_XS + CONV_DIM]
    tails = tail_p.reshape(batch, seq // FFN_ROW_TILE, SUBLANES, D_FF)
    cf_p = tails[:, -1, SUBLANES - (FFN_CONV - 1):, :]

    dt_ = x_prompt.dtype
    return (y_p.reshape(batch, seq, D_MODEL).astype(dt_),
            y_s.reshape(dec_batch, 1, D_MODEL).astype(dt_),
            hgrn_p[None].astype(dt_),
            hgrn_s[None].astype(dt_),
            ssm_p.reshape(1, batch, SSM_HEADS, SSM_HEAD_DIM, SSM_STATE).astype(dt_),
            ssm_s[None].astype(dt_),
            cs_p[None].astype(dt_),
            cs_s[None].astype(dt_),
            cf_p[None].astype(dt_),
            cf_s[None].astype(dt_))
```
